```python
import jax
import jax.numpy as jnp
from jax import lax
import numpy as np

D_MODEL = 1024
BATCH = 32
SEQ = 256
DEPTH = 2
DEC_BATCH = 2
DEC_SEQ = 2048
PAST_LEN = 256

GRID_W = 64
N_HEADS = 8
N_KV_HEADS = 2
HEAD_DIM = 64
ATTN_W = N_HEADS * HEAD_DIM
KV_W = N_KV_HEADS * HEAD_DIM
ROPE_FREQ = HEAD_DIM // 4
ROPE_THETA = 10000.0
Q_BLOCK = 128
LRU_W = 256
LRU_BLOCKS = 4
LRU_BLOCK_W = LRU_W // LRU_BLOCKS
LRU_C = 8.0
CONV_W = 4
CONV_LEFT = 2
GLA_H = 4
GLA_DK = 64
GLA_DV = 64
GLA_W = GLA_H * GLA_DV
GLA_RANK = 16
GLA_TAU = 16.0
GLA_CHUNK = 64
MIX_W = ATTN_W + LRU_W + GLA_W
IN_SPLITS = (ATTN_W, KV_W, KV_W, LRU_W, LRU_W, GLA_H * GLA_DK, GLA_H * GLA_DK, GLA_W, GLA_W, 2 * GLA_RANK)
IN_COLS = ATTN_W + 2 * KV_W + 2 * LRU_W + 2 * GLA_H * GLA_DK + 2 * GLA_W + 2 * GLA_RANK
N_EXPERTS = 256
TOP_K = 8
EXPERT_FF = 256
SHARED_FF = 256
ROUTED_SCALE = 2.5
MOE_BLOCK = 128
EPS = 1e-6

kernel_name = 'hybrid_dit_attn_rglru_gla_moe_step'


def rms_norm(x, g):
    xf = x.astype(jnp.float32)
    y = xf * lax.rsqrt(jnp.mean(xf * xf, axis=-1, keepdims=True) + EPS)
    return (y * g.astype(jnp.float32)).astype(x.dtype)


def axial_rope_tables(n_tok):
    rows = n_tok // GRID_W
    r, col = jnp.meshgrid(jnp.arange(rows), jnp.arange(GRID_W), indexing='ij')
    r = r.reshape(-1).astype(jnp.float32)
    col = col.reshape(-1).astype(jnp.float32)
    inv = ROPE_THETA ** (-jnp.arange(ROPE_FREQ, dtype=jnp.float32) / ROPE_FREQ)
    ang = jnp.concatenate([r[:, None] * inv, col[:, None] * inv], axis=-1)
    return jnp.cos(ang), jnp.sin(ang)


def apply_axial_rope(x, cos, sin):
    b, t, h, d = x.shape
    xf = x.astype(jnp.float32).reshape(b, t, h, 2, 2, ROPE_FREQ)
    x1, x2 = xf[..., 0, :], xf[..., 1, :]
    cs = cos.reshape(1, t, 1, 2, ROPE_FREQ)
    sn = sin.reshape(1, t, 1, 2, ROPE_FREQ)
    out = jnp.stack([x1 * cs - x2 * sn, x2 * cs + x1 * sn], axis=-2)
    return out.reshape(b, t, h, d).astype(x.dtype)


def block_attention(q, k, v):
    b, t, _, d = q.shape
    grp = N_HEADS // N_KV_HEADS
    nb = t // Q_BLOCK
    qb = q.reshape(b, nb, Q_BLOCK, N_KV_HEADS, grp, d).transpose(1, 0, 2, 3, 4, 5)
    scale = HEAD_DIM ** -0.5

    def one_block(qblk):
        s = jnp.einsum('bqhgd,bkhd->bhgqk', qblk, k).astype(jnp.float32) * scale
        p = jax.nn.softmax(s, axis=-1).astype(v.dtype)
        return jnp.einsum('bhgqk,bkhd->bqhgd', p, v)

    o = lax.map(one_block, qb)
    return o.transpose(1, 0, 2, 3, 4, 5).reshape(b, t, N_HEADS * d)


def centred_dwconv(x, w, bias):
    t = x.shape[1]
    xp = jnp.pad(x, ((0, 0), (CONV_LEFT, CONV_W - 1 - CONV_LEFT), (0, 0)))
    out = bias
    for j in range(CONV_W):
        out = out + xp[:, j:j + t] * w[j]
    return out


def linear_scan(a, u, h0):
    def combine(e1, e2):
        a1, b1 = e1
        a2, b2 = e2
        return a1 * a2, a2 * b1 + b2
    a_cum, b_cum = lax.associative_scan(combine, (a, u), axis=1)
    h = a_cum * h0[:, None].astype(jnp.float32) + b_cum
    return h, h[:, -1]


def rglru_bidir(xc, wa, ba, wi, bi, lam, h0):
    bsz, t, _ = xc.shape
    xb = xc.reshape(bsz, t, LRU_BLOCKS, LRU_BLOCK_W)
    outs, finals = [], []
    for d in range(2):
        r = jax.nn.sigmoid(jnp.einsum('btnc,ncd->btnd', xb, wa[d]).reshape(bsz, t, LRU_W) + ba[d])
        i = jax.nn.sigmoid(jnp.einsum('btnc,ncd->btnd', xb, wi[d]).reshape(bsz, t, LRU_W) + bi[d])
        log_a = -LRU_C * r * jax.nn.softplus(-lam[d])
        a = jnp.exp(log_a)
        u = jnp.sqrt(-jnp.expm1(2.0 * log_a)) * (i * xc)
        if d == 1:
            a, u = a[:, ::-1], u[:, ::-1]
        h, h_last = linear_scan(a, u, h0[:, d])
        if d == 1:
            h = h[:, ::-1]
        outs.append(h)
        finals.append(h_last)
    return outs[0] + outs[1], jnp.stack(finals, axis=1)


def gla_chunk_scan(q, k, v, log_a, s0):
    bsz, t = q.shape[:2]
    nc = t // GLA_CHUNK

    def to_chunks(z):
        return z.reshape(bsz, nc, GLA_CHUNK, *z.shape[2:]).swapaxes(0, 1)

    causal = jnp.tril(jnp.ones((GLA_CHUNK, GLA_CHUNK), dtype=bool))[None, :, :, None, None]

    def step(s, inp):
        qc, kc, vc, gc = inp
        bcum = jnp.cumsum(gc, axis=1)
        inter = jnp.einsum('bthk,bhkv->bthv', qc * jnp.exp(bcum), s)
        rel = jnp.where(causal, bcum[:, :, None] - bcum[:, None, :], -jnp.inf)
        att = jnp.einsum('bthk,bshk,btshk->bths', qc, kc, jnp.exp(rel))
        intra = jnp.einsum('bths,bshv->bthv', att, vc)
        b_last = bcum[:, -1]
        s_new = jnp.exp(b_last)[..., None] * s + jnp.einsum('bshk,bshv->bhkv', kc * jnp.exp(b_last[:, None] - bcum), vc)
        return s_new, inter + intra

    s_final, o = lax.scan(step, s0.astype(jnp.float32), (to_chunks(q), to_chunks(k), to_chunks(v), to_chunks(log_a)))
    return o.swapaxes(0, 1).reshape(bsz, t, GLA_H, GLA_DV), s_final


def gla_bidir(q, k, v, la_f, la_b, s0):
    o_f, s_f = gla_chunk_scan(q, k, v, la_f, s0[:, 0])
    o_b, s_b = gla_chunk_scan(q[:, ::-1], k[:, ::-1], v[:, ::-1], la_b[:, ::-1], s0[:, 1])
    return o_f + o_b[:, ::-1], jnp.stack([s_f, s_b], axis=1)


def token_mixer(h, w_in, q_g, k_g, attn_g, conv_w, conv_b, lru_wa, lru_ba, lru_wi, lru_bi, lru_lam, lru_g,
                gla_wa2, gla_ba, gla_g, w_out, ctx_k, ctx_v, lru_h0, gla_s0, rope):
    f32 = jnp.float32
    bsz, t, _ = h.shape
    proj = h @ w_in
    offs = np.cumsum(IN_SPLITS)[:-1].tolist()
    q, k, v, lx, ly, gq, gk, gv, gg, ga = jnp.split(proj, offs, axis=-1)
    q = rms_norm(q.reshape(bsz, t, N_HEADS, HEAD_DIM), q_g)
    k = rms_norm(k.reshape(bsz, t, N_KV_HEADS, HEAD_DIM), k_g)
    v = v.reshape(bsz, t, N_KV_HEADS, HEAD_DIM)
    k_out, v_out = k, v
    if rope is not None:
        q = apply_axial_rope(q, rope[0], rope[1])
        k = apply_axial_rope(k, rope[0], rope[1])
    if ctx_k is not None:
        k = jnp.concatenate([ctx_k.astype(k.dtype), k], axis=1)
        v = jnp.concatenate([ctx_v.astype(v.dtype), v], axis=1)
    attn = rms_norm(block_attention(q, k, v), attn_g)
    xconv = centred_dwconv(lx.astype(f32), conv_w, conv_b)
    lru_h, lru_fin = rglru_bidir(xconv, lru_wa, lru_ba, lru_wi, lru_bi, lru_lam, lru_h0)
    lru = rms_norm(lru_h * jax.nn.gelu(ly.astype(f32)), lru_g)
    gq = gq.reshape(bsz, t, GLA_H, GLA_DK).astype(f32) * (GLA_DK ** -0.5)
    gk = gk.reshape(bsz, t, GLA_H, GLA_DK).astype(f32)
    gv = gv.reshape(bsz, t, GLA_H, GLA_DV).astype(f32)
    ga = ga.reshape(bsz, t, 2, GLA_RANK).astype(f32)
    la_f = (jax.nn.log_sigmoid(ga[:, :, 0] @ gla_wa2[0] + gla_ba[0]) / GLA_TAU).reshape(bsz, t, GLA_H, GLA_DK)
    la_b = (jax.nn.log_sigmoid(ga[:, :, 1] @ gla_wa2[1] + gla_ba[1]) / GLA_TAU).reshape(bsz, t, GLA_H, GLA_DK)
    gla_o, gla_fin = gla_bidir(gq, gk, gv, la_f, la_b, gla_s0)
    gla = rms_norm(gla_o, gla_g).reshape(bsz, t, GLA_W) * jax.nn.silu(gg.astype(f32))
    merged = jnp.concatenate([attn.astype(f32), lru, gla], axis=-1).astype(h.dtype)
    return merged @ w_out, (k_out, v_out, lru_fin, gla_fin)


def moe_ffn(h, router_w, router_b, w1, w3, w2, sw1, sw3, sw2):
    n, d = h.shape
    scores = jax.nn.sigmoid((h @ router_w).astype(jnp.float32))
    _, idx = lax.top_k(scores + router_b.astype(jnp.float32), TOP_K)
    gates = jnp.take_along_axis(scores, idx, axis=-1)
    gates = gates / jnp.sum(gates, axis=-1, keepdims=True) * ROUTED_SCALE
    n_assign = n * TOP_K
    flat_e = idx.reshape(-1)
    order = jnp.argsort(flat_e)
    e_sorted = flat_e[order]
    tok_sorted = order // TOP_K
    counts = jnp.bincount(flat_e, length=N_EXPERTS)
    padded = (counts + MOE_BLOCK - 1) // MOE_BLOCK * MOE_BLOCK
    start = jnp.cumsum(counts) - counts
    padded_end = jnp.cumsum(padded)
    padded_start = padded_end - padded
    dest = padded_start[e_sorted] + jnp.arange(n_assign) - start[e_sorted]
    n_blocks = (n_assign + N_EXPERTS * (MOE_BLOCK - 1) + MOE_BLOCK - 1) // MOE_BLOCK
    slot_tok = jnp.zeros((n_blocks * MOE_BLOCK,), jnp.int32).at[dest].set(tok_sorted)
    blk_e = jnp.minimum(jnp.searchsorted(padded_end, jnp.arange(n_blocks) * MOE_BLOCK, side='right'), N_EXPERTS - 1)
    xb = h[slot_tok].reshape(n_blocks, MOE_BLOCK, d)

    def expert_block(args):
        xblk, e = args
        return (jax.nn.silu(xblk @ w1[e]) * (xblk @ w3[e])) @ w2[e]

    yb = lax.map(expert_block, (xb, blk_e)).reshape(-1, d)
    contrib = yb[dest] * gates.reshape(-1)[order][:, None].astype(h.dtype)
    routed = jax.ops.segment_sum(contrib, tok_sorted, num_segments=n)
    shared = (jax.nn.silu(h @ sw1) * (h @ sw3)) @ sw2
    return routed + shared


def setup_inputs(seed: int = 0) -> dict:
    key = jax.random.key(seed)
    ks = iter(jax.random.split(key, 48))
    f32 = jnp.float32

    def nrm(shape, s):
        return jax.random.normal(next(ks), shape, f32) * s

    lam_u = jax.random.uniform(next(ks), (DEPTH, 2, LRU_W), f32, 0.9, 0.999)
    return {
        'x_prompt': nrm((BATCH, SEQ, D_MODEL), 1.0),
        'x_sample': nrm((DEC_BATCH, DEC_SEQ, D_MODEL), 1.0),
        'cache_k': nrm((DEC_BATCH, DEPTH, PAST_LEN, N_KV_HEADS, HEAD_DIM), 1.0),
        'cache_v': nrm((DEC_BATCH, DEPTH, PAST_LEN, N_KV_HEADS, HEAD_DIM), 1.0),
        'state_lru': nrm((DEC_BATCH, DEPTH, 2, LRU_W), 0.5),
        'state_gla': nrm((DEC_BATCH, DEPTH, 2, GLA_H, GLA_DK, GLA_DV), 0.5),
        'c': nrm((DEC_BATCH, D_MODEL), 1.0),
        'c_ctx': nrm((D_MODEL,), 1.0),
        'ada_w': nrm((DEPTH, D_MODEL, 6 * D_MODEL), 0.5 * D_MODEL ** -0.5),
        'ada_b': nrm((DEPTH, 6 * D_MODEL), 0.02),
        'norm1_g': 1.0 + nrm((DEPTH, D_MODEL), 0.02),
        'norm2_g': 1.0 + nrm((DEPTH, D_MODEL), 0.02),
        'w_in': nrm((DEPTH, D_MODEL, IN_COLS), D_MODEL ** -0.5),
        'q_norm_g': 1.0 + nrm((DEPTH, HEAD_DIM), 0.02),
        'k_norm_g': 1.0 + nrm((DEPTH, HEAD_DIM), 0.02),
        'attn_out_g': 1.0 + nrm((DEPTH, ATTN_W), 0.02),
        'conv_w': nrm((DEPTH, CONV_W, LRU_W), CONV_W ** -0.5),
        'conv_b': nrm((DEPTH, LRU_W), 0.02),
        'lru_wa': nrm((DEPTH, 2, LRU_BLOCKS, LRU_BLOCK_W, LRU_BLOCK_W), LRU_BLOCK_W ** -0.5),
        'lru_ba': nrm((DEPTH, 2, LRU_W), 0.02),
        'lru_wi': nrm((DEPTH, 2, LRU_BLOCKS, LRU_BLOCK_W, LRU_BLOCK_W), LRU_BLOCK_W ** -0.5),
        'lru_bi': nrm((DEPTH, 2, LRU_W), 0.02),
        'lru_lambda': jnp.log(lam_u) - jnp.log1p(-lam_u),
        'lru_out_g': 1.0 + nrm((DEPTH, LRU_W), 0.02),
        'gla_wa2': nrm((DEPTH, 2, GLA_RANK, GLA_H * GLA_DK), GLA_RANK ** -0.5),
        'gla_ba': nrm((DEPTH, 2, GLA_H * GLA_DK), 0.02),
        'gla_out_g': 1.0 + nrm((DEPTH, GLA_H, GLA_DV), 0.02),
        'w_out': nrm((DEPTH, MIX_W, D_MODEL), MIX_W ** -0.5),
        'router_w': nrm((DEPTH, D_MODEL, N_EXPERTS), D_MODEL ** -0.5),
        'router_b': nrm((DEPTH, N_EXPERTS), 0.01),
        'exp_w1': nrm((DEPTH, N_EXPERTS, D_MODEL, EXPERT_FF), D_MODEL ** -0.5),
        'exp_w3': nrm((DEPTH, N_EXPERTS, D_MODEL, EXPERT_FF), D_MODEL ** -0.5),
        'exp_w2': nrm((DEPTH, N_EXPERTS, EXPERT_FF, D_MODEL), EXPERT_FF ** -0.5),
        'sh_w1': nrm((DEPTH, D_MODEL, SHARED_FF), D_MODEL ** -0.5),
        'sh_w3': nrm((DEPTH, D_MODEL, SHARED_FF), D_MODEL ** -0.5),
        'sh_w2': nrm((DEPTH, SHARED_FF, D_MODEL), SHARED_FF ** -0.5),
    }


def reference(x_prompt, x_sample, cache_k, cache_v, state_lru, state_gla, c, c_ctx, ada_w, ada_b, norm1_g, norm2_g,
              w_in, q_norm_g, k_norm_g, attn_out_g, conv_w, conv_b, lru_wa, lru_ba, lru_wi, lru_bi, lru_lambda,
              lru_out_g, gla_wa2, gla_ba, gla_out_g, w_out, router_w, router_b, exp_w1, exp_w3, exp_w2,
              sh_w1, sh_w3, sh_w2):
    f32 = jnp.float32
    xc, xl = x_prompt, x_sample
    n_ctx_tok = xc.shape[0] * xc.shape[1]
    rope = axial_rope_tables(xl.shape[1])
    ctx_lru0 = jnp.zeros((xc.shape[0], 2, LRU_W), f32)
    ctx_gla0 = jnp.zeros((xc.shape[0], 2, GLA_H, GLA_DK, GLA_DV), f32)
    ks_out, vs_out, lru_out, gla_out = [], [], [], []
    for l in range(DEPTH):
        mix_p = (w_in[l], q_norm_g[l], k_norm_g[l], attn_out_g[l], conv_w[l], conv_b[l], lru_wa[l], lru_ba[l],
                 lru_wi[l], lru_bi[l], lru_lambda[l], lru_out_g[l], gla_wa2[l], gla_ba[l], gla_out_g[l], w_out[l])
        mod_c = jax.nn.silu(c_ctx) @ ada_w[l] + ada_b[l]
        mod_l = (jax.nn.silu(c) @ ada_w[l] + ada_b[l])[:, None, :]
        sh1c, sc1c, g1c, sh2c, sc2c, g2c = jnp.split(mod_c, 6, axis=-1)
        sh1l, sc1l, g1l, sh2l, sc2l, g2l = jnp.split(mod_l, 6, axis=-1)
        hc = rms_norm(xc, norm1_g[l]) * (1.0 + sc1c) + sh1c
        mc, (kc, vc, lru_c, gla_c) = token_mixer(hc, *mix_p, None, None, ctx_lru0, ctx_gla0, None)
        hl = rms_norm(xl, norm1_g[l]) * (1.0 + sc1l) + sh1l
        ml, _ = token_mixer(hl, *mix_p, cache_k[:, l], cache_v[:, l], state_lru[:, l], state_gla[:, l], rope)
        xc = xc + g1c * mc
        xl = xl + g1l * ml
        ks_out.append(kc)
        vs_out.append(vc)
        lru_out.append(lru_c)
        gla_out.append(gla_c)
        hc2 = rms_norm(xc, norm2_g[l]) * (1.0 + sc2c) + sh2c
        hl2 = rms_norm(xl, norm2_g[l]) * (1.0 + sc2l) + sh2l
        tokens = jnp.concatenate([hc2.reshape(-1, D_MODEL), hl2.reshape(-1, D_MODEL)], axis=0)
        ff = moe_ffn(tokens, router_w[l], router_b[l], exp_w1[l], exp_w3[l], exp_w2[l], sh_w1[l], sh_w3[l], sh_w2[l])
        xc = xc + g2c * ff[:n_ctx_tok].reshape(xc.shape)
        xl = xl + g2l * ff[n_ctx_tok:].reshape(xl.shape)
    new_cache_k = jnp.stack(ks_out, axis=1)
    new_cache_v = jnp.stack(vs_out, axis=1)
    new_state_lru = jnp.stack(lru_out, axis=1)
    new_state_gla = jnp.stack(gla_out, axis=1)
    return (xc, xl, new_cache_k, new_cache_v, new_state_lru, new_state_gla)
```

```python
import functools

import jax
import jax.numpy as jnp
import numpy as np
from jax import lax
from jax.experimental import pallas as pl
from jax.experimental.pallas import tpu as pltpu

f32 = jnp.float32
bf16 = jnp.bfloat16
i32 = jnp.int32

D_MODEL = 1024
N_HEADS = 8
N_KV_HEADS = 2
HEAD_DIM = 64
ATTN_W = N_HEADS * HEAD_DIM
KV_W = N_KV_HEADS * HEAD_DIM
GRID_W = 64
ROPE_FREQ = HEAD_DIM // 4
ROPE_THETA = 10000.0
LRU_W = 256
LRU_BLOCKS = 4
LRU_C = 8.0
CONV_W = 4
CONV_LEFT = 2
GLA_H = 4
GLA_DK = 64
GLA_W = 256
GLA_RANK = 16
GLA_TAU = 16.0
N_EXPERTS = 256
TOP_K = 8
EXPERT_FF = 256
ROUTED_SCALE = 2.5
EPS = 1e-6

LANES = 128
SUBLANES = 8
ROW_TILE = 256
GLA_BLOCK = 256
SLOT_BLOCK = 256
COMBINE_TILE = 128
D_SLABS = D_MODEL // LANES
VMEM_LIMIT = 56 * 1024 * 1024


def _cparams(sem, vmem=VMEM_LIMIT):
    return pltpu.CompilerParams(dimension_semantics=sem, vmem_limit_bytes=vmem)


def _dot(a, b):
    return jnp.dot(a, b, preferred_element_type=f32)


def _dot_nt(a, b):
    return lax.dot_general(a, b, (((1,), (1,)), ((), ())), preferred_element_type=f32)


def _dot_tn(a, b):
    return lax.dot_general(a, b, (((0,), (0,)), ((), ())), preferred_element_type=f32)


def _split(x):
    hi = x.astype(bf16)
    lo = (x - hi.astype(f32)).astype(bf16)
    return hi, lo


def _dot_x2(x, w):
    hi, lo = _split(x)
    return _dot(hi, w) + _dot(lo, w)


def _dot_2x(m, x):
    hi, lo = _split(x)
    return _dot(m, hi) + _dot(m, lo)


def _dot3(a, b_hi, b_lo):
    a_hi, a_lo = _split(a)
    return _dot(a_hi, b_hi) + _dot(a_lo, b_hi) + _dot(a_hi, b_lo)


def _sigmoid(x):
    return 1.0 / (1.0 + jnp.exp(-x))


def _silu(x):
    return x * _sigmoid(x)


def _softplus(x):
    return jnp.maximum(x, 0.0) + jnp.log(1.0 + jnp.exp(-jnp.abs(x)))


def _gelu_tanh(x):
    return 0.5 * x * (1.0 + jnp.tanh(0.7978845608028654 * (x + 0.044715 * x * x * x)))


def _rms(x, g):
    return x * lax.rsqrt(jnp.mean(x * x, axis=-1, keepdims=True) + EPS) * g


def _head_rms(x, g, bm):
    ms = _dot_x2(x * x, bm)
    return x * lax.rsqrt(ms + EPS) * g


def _ada_kernel(c_ref, w_ref, b_ref, o_ref):
    s = _silu(c_ref[...])
    w = w_ref[0]
    w_hi, w_lo = _split(w)
    o_ref[0] = _dot3(s, w_hi, w_lo) + b_ref[0]


def _ada_call(cond, ada_w, ada_b):
    depth = ada_w.shape[0]
    nt = 1536
    return pl.pallas_call(
        _ada_kernel,
        grid=(depth, 6 * D_MODEL // nt),
        in_specs=[pl.BlockSpec((SUBLANES, D_MODEL), lambda l, j: (0, 0)),
                  pl.BlockSpec((1, D_MODEL, nt), lambda l, j: (l, 0, j)),
                  pl.BlockSpec((1, 1, nt), lambda l, j: (l, 0, j))],
        out_specs=pl.BlockSpec((1, SUBLANES, nt), lambda l, j: (l, 0, j)),
        out_shape=jax.ShapeDtypeStruct((depth, SUBLANES, 6 * D_MODEL), f32),
        compiler_params=_cparams(("parallel", "parallel")),
        name="ada_mod",
    )(cond, ada_w, ada_b.reshape(depth, 1, 6 * D_MODEL))


PROJ_A = ATTN_W + 2 * KV_W
PROJ_L = 2 * LRU_W
PROJ_G = 4 * GLA_W + LANES


def _proj_kernel(x_ref, mod_ref, g_ref, w_ref, oa_ref, ol_ref, og_ref):
    mod = mod_ref[0]
    sh = mod[:, 0:D_MODEL]
    sc = mod[:, D_MODEL:2 * D_MODEL]
    h = (_rms(x_ref[...], g_ref[...]) * (1.0 + sc) + sh).astype(bf16)
    p = _dot(h, w_ref[...])
    oa_ref[...] = p[:, 0:PROJ_A]
    ol_ref[...] = p[:, PROJ_A:PROJ_A + PROJ_L]
    og_ref[...] = p[:, PROJ_A + PROJ_L:PROJ_A + PROJ_L + PROJ_G]


def _proj_call(x, mod, g, w, seq_row):
    n = x.shape[0]
    cols = PROJ_A + PROJ_L + PROJ_G
    return pl.pallas_call(
        _proj_kernel,
        grid=(n // ROW_TILE,),
        in_specs=[pl.BlockSpec((ROW_TILE, D_MODEL), lambda i: (i, 0)),
                  pl.BlockSpec((1, 1, 6 * D_MODEL), lambda i: (seq_row(i), 0, 0)),
                  pl.BlockSpec((1, D_MODEL), lambda i: (0, 0)),
                  pl.BlockSpec((D_MODEL, cols), lambda i: (0, 0))],
        out_specs=[pl.BlockSpec((ROW_TILE, PROJ_A), lambda i: (i, 0)),
                   pl.BlockSpec((ROW_TILE, PROJ_L), lambda i: (i, 0)),
                   pl.BlockSpec((ROW_TILE, PROJ_G), lambda i: (i, 0))],
        out_shape=[jax.ShapeDtypeStruct((n, PROJ_A), f32),
                   jax.ShapeDtypeStruct((n, PROJ_L), f32),
                   jax.ShapeDtypeStruct((n, PROJ_G), f32)],
        compiler_params=_cparams(("parallel",)),
        name="in_proj",
    )(x, mod, g, w)


def _rope(x, cos_t, sin_t):
    w = x.shape[1]
    up = pltpu.roll(x, w - ROPE_FREQ, 1)
    dn = pltpu.roll(x, ROPE_FREQ, 1)
    lane = lax.broadcasted_iota(i32, x.shape, 1)
    partner = jnp.where((lane & (2 * ROPE_FREQ - 1)) < ROPE_FREQ, up, dn)
    return x * cos_t + partner * sin_t


def _attend(q, kk_ref, vv_ref, o_ref):
    tq = q.shape[0]
    gw = ATTN_W // N_KV_HEADS
    lane = lax.broadcasted_iota(i32, (tq, gw), 1)
    for g in range(N_KV_HEADS):
        qg = q[:, g * gw:(g + 1) * gw]
        kg = kk_ref[:, g * gw:(g + 1) * gw]
        vg = vv_ref[:, g * gw:(g + 1) * gw]
        acc = jnp.zeros((tq, gw), f32)
        for hh in range(N_HEADS // N_KV_HEADS):
            hm = (lane >> 6) == hh
            s = _dot_nt(jnp.where(hm, qg, 0.0).astype(bf16), kg)
            m = jnp.max(s, axis=-1, keepdims=True)
            p = jnp.exp(s - m)
            l = jnp.sum(p, axis=-1, keepdims=True)
            o = _dot(p.astype(bf16), vg) / l
            acc = jnp.where(hm, o, acc)
        o_ref[:, g * gw:(g + 1) * gw] = acc


def _attn_ctx_kernel(p_ref, qg_ref, kg_ref, og_ref, bmq_ref, bmk_ref, rep_ref,
                     o_ref, ko_ref, vo_ref, kk_s, vv_s, o_s):
    p = p_ref[...]
    q = _head_rms(p[:, 0:ATTN_W], qg_ref[...], bmq_ref[...])
    k = _head_rms(p[:, ATTN_W:ATTN_W + KV_W], kg_ref[...], bmk_ref[...])
    v = p[:, ATTN_W + KV_W:PROJ_A]
    ko_ref[...] = k
    vo_ref[...] = v
    kk_s[...] = _dot(k.astype(bf16), rep_ref[...]).astype(bf16)
    vv_s[...] = _dot(v.astype(bf16), rep_ref[...]).astype(bf16)
    _attend(q * HEAD_DIM ** -0.5, kk_s, vv_s, o_s)
    o_ref[...] = _rms(o_s[...], og_ref[...])


def _attn_ctx_call(pa, n_seq, t, qg, kg, og, bmq, bmk, rep):
    assert t == ROW_TILE
    const = lambda i: (0, 0)
    return pl.pallas_call(
        _attn_ctx_kernel,
        grid=(n_seq,),
        in_specs=[pl.BlockSpec((t, PROJ_A), lambda i: (i, 0)),
                  pl.BlockSpec((1, ATTN_W), const), pl.BlockSpec((1, KV_W), const), pl.BlockSpec((1, ATTN_W), const),
                  pl.BlockSpec((ATTN_W, ATTN_W), const), pl.BlockSpec((KV_W, KV_W), const),
                  pl.BlockSpec((KV_W, ATTN_W), const)],
        out_specs=[pl.BlockSpec((t, ATTN_W), lambda i: (i, 0)),
                   pl.BlockSpec((t, KV_W), lambda i: (i, 0)),
                   pl.BlockSpec((t, KV_W), lambda i: (i, 0))],
        out_shape=[jax.ShapeDtypeStruct((n_seq * t, ATTN_W), f32),
                   jax.ShapeDtypeStruct((n_seq * t, KV_W), f32),
                   jax.ShapeDtypeStruct((n_seq * t, KV_W), f32)],
        scratch_shapes=[pltpu.VMEM((t, ATTN_W), bf16), pltpu.VMEM((t, ATTN_W), bf16), pltpu.VMEM((t, ATTN_W), f32)],
        compiler_params=_cparams(("parallel",)),
        name="attn_ctx",
    )(pa, qg, kg, og, bmq, bmk, rep)


def _attn_lat_kernel(q_ref, kv_ref, ck_ref, cv_ref, cq_ref, sq_ref, ckk_ref, skk_ref,
                     qg_ref, kg_ref, og_ref, bmq_ref, bmk_ref, rep_ref,
                     o_ref, kk_s, vv_s, o_s, *, past):
    @pl.when(pl.program_id(1) == 0)
    def _():
        kv = kv_ref[...]
        k = _head_rms(kv[:, 0:KV_W], kg_ref[...], bmk_ref[...])
        k = _rope(k, ckk_ref[...], skk_ref[...])
        v = kv[:, KV_W:2 * KV_W]
        kk_s[0:past, :] = _dot(ck_ref[0].astype(bf16), rep_ref[...]).astype(bf16)
        vv_s[0:past, :] = _dot(cv_ref[0].astype(bf16), rep_ref[...]).astype(bf16)
        kk_s[past:, :] = _dot(k.astype(bf16), rep_ref[...]).astype(bf16)
        vv_s[past:, :] = _dot(v.astype(bf16), rep_ref[...]).astype(bf16)

    q = _head_rms(q_ref[...], qg_ref[...], bmq_ref[...])
    q = _rope(q, cq_ref[...], sq_ref[...])
    _attend(q * HEAD_DIM ** -0.5, kk_s, vv_s, o_s)
    o_ref[...] = _rms(o_s[...], og_ref[...])


def _attn_lat_call(pa, row0, n_seq, t, cache_k, cache_v, cq, sq, qg, kg, og, bmq, bmk, rep):
    past = cache_k.shape[1]
    nq = t // ROW_TILE
    assert row0 % t == 0 and t % ROW_TILE == 0
    qb0 = row0 // ROW_TILE
    sb0 = row0 // t
    const = lambda b, j: (0, 0)
    return pl.pallas_call(
        functools.partial(_attn_lat_kernel, past=past),
        grid=(n_seq, nq),
        in_specs=[pl.BlockSpec((ROW_TILE, ATTN_W), lambda b, j: (qb0 + b * nq + j, 0)),
                  pl.BlockSpec((t, 2 * KV_W), lambda b, j: (sb0 + b, ATTN_W // (2 * KV_W))),
                  pl.BlockSpec((1, past, KV_W), lambda b, j: (b, 0, 0)),
                  pl.BlockSpec((1, past, KV_W), lambda b, j: (b, 0, 0)),
                  pl.BlockSpec((ROW_TILE, ATTN_W), lambda b, j: (j, 0)),
                  pl.BlockSpec((ROW_TILE, ATTN_W), lambda b, j: (j, 0)),
                  pl.BlockSpec((t, KV_W), const), pl.BlockSpec((t, KV_W), const),
                  pl.BlockSpec((1, ATTN_W), const), pl.BlockSpec((1, KV_W), const), pl.BlockSpec((1, ATTN_W), const),
                  pl.BlockSpec((ATTN_W, ATTN_W), const), pl.BlockSpec((KV_W, KV_W), const),
                  pl.BlockSpec((KV_W, ATTN_W), const)],
        out_specs=pl.BlockSpec((ROW_TILE, ATTN_W), lambda b, j: (b * nq + j, 0)),
        out_shape=jax.ShapeDtypeStruct((n_seq * t, ATTN_W), f32),
        scratch_shapes=[pltpu.VMEM((past + t, ATTN_W), bf16), pltpu.VMEM((past + t, ATTN_W), bf16),
                        pltpu.VMEM((ROW_TILE, ATTN_W), f32)],
        compiler_params=_cparams(("parallel", "arbitrary")),
        name="attn_lat",
    )(pa, pa, cache_k, cache_v, cq, sq, cq, sq, qg, kg, og, bmq, bmk, rep)


def _lru_kernel(x_ref, h0_ref, cw_ref, cb_ref, wg_ref, bg_ref, lam_ref, g_ref,
                o_ref, fin_ref, xs, a_s, u_s, h_s, *, t):
    ng = t // SUBLANES
    lx = x_ref[:, 0:LRU_W]
    ly = x_ref[:, LRU_W:2 * LRU_W]
    xs[0:SUBLANES, :] = jnp.zeros((SUBLANES, LRU_W), f32)
    xs[SUBLANES + t:2 * SUBLANES + t, :] = jnp.zeros((SUBLANES, LRU_W), f32)
    xs[SUBLANES:SUBLANES + t, :] = lx
    xc = jnp.broadcast_to(cb_ref[...], (t, LRU_W))
    for j in range(CONV_W):
        off = SUBLANES + j - CONV_LEFT
        xc = xc + xs[off:off + t, :] * cw_ref[j:j + 1, :]
    gates = _sigmoid(_dot(xc.astype(bf16), wg_ref[...]) + bg_ref[...])
    sp = _softplus(-lam_ref[...])
    sub = lax.broadcasted_iota(i32, (t, LRU_W), 0) & (SUBLANES - 1)
    for d in range(2):
        r = gates[:, (2 * d) * LRU_W:(2 * d + 1) * LRU_W]
        gi = gates[:, (2 * d + 1) * LRU_W:(2 * d + 2) * LRU_W]
        log_a = -LRU_C * r * sp[d:d + 1, :]
        a = jnp.exp(log_a)
        th = jnp.tanh(log_a)
        u = jnp.sqrt(-2.0 * th / (1.0 - th)) * (gi * xc)
        for s in (1, 2, 4):
            if d == 0:
                a_sh = pltpu.roll(a, s, 0)
                u_sh = pltpu.roll(u, s, 0)
                ok = sub >= s
            else:
                a_sh = pltpu.roll(a, t - s, 0)
                u_sh = pltpu.roll(u, t - s, 0)
                ok = sub < SUBLANES - s
            u = jnp.where(ok, a * u_sh + u, u)
            a = jnp.where(ok, a * a_sh, a)
        a_s[...] = a
        u_s[...] = u
        h0 = jnp.broadcast_to(h0_ref[0, d:d + 1, :], (SUBLANES, LRU_W))
        edge = SUBLANES - 1 if d == 0 else 0

        def body(i, carry, d=d, edge=edge):
            g = i if d == 0 else ng - 1 - i
            rows = pl.ds(pl.multiple_of(g * SUBLANES, SUBLANES), SUBLANES)
            h = a_s[rows, :] * carry + u_s[rows, :]
            if d == 0:
                h_s[rows, :] = h
            else:
                h_s[rows, :] = h_s[rows, :] + h
            return jnp.broadcast_to(h[edge:edge + 1, :], (SUBLANES, LRU_W))

        last = lax.fori_loop(0, ng, body, h0)
        fin_ref[0, d:d + 1, :] = last[0:1, :]
    o_ref[...] = _rms(h_s[...] * _gelu_tanh(ly), g_ref[...])


def _lru_call(pl_, row0, n_seq, t, h0, cw, cb, wg, bg, lam, g):
    sb0 = row0 // t
    assert row0 % t == 0
    const = lambda b: (0, 0)
    in_specs = [pl.BlockSpec((t, PROJ_L), lambda b: (sb0 + b, 0)),
                pl.BlockSpec((1, 2, LRU_W), lambda b: (b, 0, 0)),
                pl.BlockSpec((CONV_W, LRU_W), const), pl.BlockSpec((1, LRU_W), const),
                pl.BlockSpec((LRU_W, 4 * LRU_W), const), pl.BlockSpec((1, 4 * LRU_W), const),
                pl.BlockSpec((2, LRU_W), const), pl.BlockSpec((1, LRU_W), const)]
    args = [pl_, h0, cw, cb, wg, bg, lam, g]
    return pl.pallas_call(
        functools.partial(_lru_kernel, t=t),
        grid=(n_seq,),
        in_specs=in_specs,
        out_specs=[pl.BlockSpec((t, LRU_W), lambda b: (b, 0)),
                   pl.BlockSpec((1, 2, LRU_W), lambda b: (b, 0, 0))],
        out_shape=[jax.ShapeDtypeStruct((n_seq * t, LRU_W), f32), jax.ShapeDtypeStruct((n_seq, 2, LRU_W), f32)],
        scratch_shapes=[pltpu.VMEM((t + 2 * SUBLANES, LRU_W), f32), pltpu.VMEM((t, LRU_W), f32),
                        pltpu.VMEM((t, LRU_W), f32), pltpu.VMEM((t, LRU_W), f32)],
        compiler_params=_cparams(("parallel",)),
        name="rglru_t%d" % t,
    )(*args)


def _bcast_rows(b, period, off):
    w = b.shape[1]
    return jnp.concatenate(
        [jnp.broadcast_to(b[i * period + off:i * period + off + 1, :], (period, w)) for i in range(b.shape[0] // period)],
        axis=0)


def _gla_block(q, k, v, la, reverse, st_ref):
    n = GLA_BLOCK
    row = lax.broadcasted_iota(i32, (n, n), 0)
    col = lax.broadcasted_iota(i32, (n, n), 1)
    same64 = (row >> 6) == (col >> 6)
    same32 = (row >> 5) == (col >> 5)
    same16 = (row >> 4) == (col >> 4)
    if not reverse:
        cum = (same64 & (col <= row)).astype(bf16)
        m1 = same64 & ((row & 63) >= 32) & ((col & 63) < 32)
        m2 = same32 & ((row & 31) >= 16) & ((col & 31) < 16)
        m3 = same16 & (col <= row)
        offs = (31, 15, 7, 63)
    else:
        cum = (same64 & (col >= row)).astype(bf16)
        m1 = same64 & ((row & 63) < 32) & ((col & 63) >= 32)
        m2 = same32 & ((row & 31) < 16) & ((col & 31) >= 16)
        m3 = same16 & (col >= row)
        offs = (32, 16, 8, 0)
    b = _dot_2x(cum, la)
    r1 = _bcast_rows(b, 64, offs[0])
    r2 = _bcast_rows(b, 32, offs[1])
    r3 = _bcast_rows(b, 16, offs[2])
    bl = _bcast_rows(b, 64, offs[3])
    q1 = q * jnp.exp(jnp.minimum(b - r1, 0.0))
    k1 = (k * jnp.exp(jnp.minimum(r1 - b, 0.0))).astype(bf16)
    q2 = q * jnp.exp(jnp.minimum(b - r2, 0.0))
    k2 = (k * jnp.exp(jnp.minimum(r2 - b, 0.0))).astype(bf16)
    q3 = q * jnp.exp(b - r3)
    k3 = (k * jnp.exp(r3 - b)).astype(bf16)
    qe = (q * jnp.exp(b)).astype(bf16)
    kl = (k * jnp.exp(bl - b)).astype(bf16)
    dec = jnp.exp(bl)
    vb = v.astype(bf16)
    zq = jnp.zeros((n, n), f32)
    intra = jnp.zeros((n, n), f32)
    for h in range(GLA_H):
        hm = (col >> 6) == h
        a1 = _dot_nt(jnp.where(hm, q1, zq).astype(bf16), k1)
        a2 = _dot_nt(jnp.where(hm, q2, zq).astype(bf16), k2)
        a3 = _dot_nt(jnp.where(hm, q3, zq).astype(bf16), k3)
        att = jnp.where(m1, a1, jnp.where(m2, a2, jnp.where(m3, a3, zq)))
        intra = intra + _dot(att.astype(bf16), jnp.where(hm, v, zq).astype(bf16))
    outs = [None] * 4
    for c in (range(4) if not reverse else range(3, -1, -1)):
        rs = slice(64 * c, 64 * c + 64)
        st = st_ref[...]
        inter = _dot_nt(qe[rs], st.astype(bf16))
        kv = _dot_tn(vb[rs], kl[rs])
        drow = dec[64 * c:64 * c + 1, :]
        st_ref[...] = st * drow + jnp.where(same64, kv, zq)
        outs[c] = intra[rs] + inter
    return jnp.concatenate(outs, axis=0)


def _gla_kernel(x_ref, s0_ref, wa_ref, ba_ref, g_ref, bm_ref, o_ref, fin_ref, st_s, o_s, *, t):
    nblk = t // GLA_BLOCK

    def block(j, reverse):
        rows = pl.ds(j * GLA_BLOCK if isinstance(j, int) else pl.multiple_of(j * GLA_BLOCK, GLA_BLOCK), GLA_BLOCK)
        q = x_ref[rows, 0:GLA_W] * GLA_DK ** -0.5
        k = x_ref[rows, GLA_W:2 * GLA_W]
        v = x_ref[rows, 2 * GLA_W:3 * GLA_W]
        ga = x_ref[rows, 4 * GLA_W:4 * GLA_W + LANES]
        d = 1 if reverse else 0
        z = _dot(ga.astype(bf16), wa_ref[:, d * GLA_W:(d + 1) * GLA_W]) + ba_ref[:, d * GLA_W:(d + 1) * GLA_W]
        la = -_softplus(-z) * (1.0 / GLA_TAU)
        o = _gla_block(q, k, v, la, reverse, st_s)
        if reverse:
            o_s[rows, :] = o_s[rows, :] + o
        else:
            o_s[rows, :] = o

    for d in range(2):
        if s0_ref is None:
            st_s[...] = jnp.zeros((GLA_W, GLA_W), f32)
        else:
            st_s[...] = s0_ref[0, d]
        if nblk == 1:
            block(0, d == 1)
        else:
            def body(i, carry, d=d):
                block(i if d == 0 else nblk - 1 - i, d == 1)
                return carry
            lax.fori_loop(0, nblk, body, 0)
        fin_ref[0, d] = st_s[...]
    gg = x_ref[:, 3 * GLA_W:4 * GLA_W]
    o_ref[...] = _head_rms(o_s[...], g_ref[...], bm_ref[...]) * _silu(gg)


def _gla_call(pg, row0, n_seq, t, s0, wa, ba, g, bm):
    sb0 = row0 // t
    assert row0 % t == 0 and t % GLA_BLOCK == 0
    const = lambda b: (0, 0)
    in_specs = [pl.BlockSpec((t, PROJ_G), lambda b: (sb0 + b, 0))]
    args = [pg]
    if s0 is not None:
        in_specs.append(pl.BlockSpec((1, 2, GLA_W, GLA_W), lambda b: (b, 0, 0, 0)))
        args.append(s0)
    in_specs += [pl.BlockSpec((LANES, 2 * GLA_W), const), pl.BlockSpec((1, 2 * GLA_W), const),
                 pl.BlockSpec((1, GLA_W), const), pl.BlockSpec((GLA_W, GLA_W), const)]
    args += [wa, ba, g, bm]

    def kern(*refs):
        refs = list(refs)
        x_ref = refs.pop(0)
        s0_ref = refs.pop(0) if s0 is not None else None
        wa_ref, ba_ref, g_ref, bm_ref = refs[:4]
        _gla_kernel(x_ref, s0_ref, wa_ref, ba_ref, g_ref, bm_ref, *refs[4:], t=t)

    return pl.pallas_call(
        kern,
        grid=(n_seq,),
        in_specs=in_specs,
        out_specs=[pl.BlockSpec((t, GLA_W), lambda b: (b, 0)),
                   pl.BlockSpec((1, 2, GLA_W, GLA_W), lambda b: (b, 0, 0, 0))],
        out_shape=[jax.ShapeDtypeStruct((n_seq * t, GLA_W), f32), jax.ShapeDtypeStruct((n_seq, 2, GLA_W, GLA_W), f32)],
        scratch_shapes=[pltpu.VMEM((GLA_W, GLA_W), f32), pltpu.VMEM((t, GLA_W), f32)],
        compiler_params=_cparams(("parallel",)),
        name="gla_t%d" % t,
    )(*args)


def _out_kernel(x_ref, ac_ref, lc_ref, gc_ref, al_ref, ll_ref, gl_ref, mod_ref, wo_ref, g2_ref, rwh_ref, rwl_ref,
                s1_ref, s3_ref, s2_ref, xb_ref, h2_ref, sc_ref, *, n_ctx_tiles):
    is_ctx = pl.program_id(0) < n_ctx_tiles
    pick = lambda c_ref, l_ref: jnp.where(is_ctx, c_ref[...], l_ref[...]).astype(bf16)
    mod = mod_ref[0]
    g1 = mod[:, 2 * D_MODEL:3 * D_MODEL]
    sh2 = mod[:, 3 * D_MODEL:4 * D_MODEL]
    sc2 = mod[:, 4 * D_MODEL:5 * D_MODEL]
    g2 = mod[:, 5 * D_MODEL:6 * D_MODEL]
    m = (_dot(pick(ac_ref, al_ref), wo_ref[0:ATTN_W, :])
         + _dot(pick(lc_ref, ll_ref), wo_ref[ATTN_W:ATTN_W + LRU_W, :])
         + _dot(pick(gc_ref, gl_ref), wo_ref[ATTN_W + LRU_W:ATTN_W + LRU_W + GLA_W, :]))
    x1 = x_ref[...] + g1 * m
    h2 = _rms(x1, g2_ref[...]) * (1.0 + sc2) + sh2
    for j in range(D_SLABS):
        h2_ref[:, j, :] = h2[:, j * LANES:(j + 1) * LANES]
    sc_ref[...] = _sigmoid(_dot3(h2, rwh_ref[...], rwl_ref[...]))
    hb = h2.astype(bf16)
    act = (_silu(_dot(hb, s1_ref[...])) * _dot(hb, s3_ref[...])).astype(bf16)
    xb_ref[...] = x1 + g2 * _dot(act, s2_ref[...])


def _out_call(x, mix_ctx, mix_lat, mod, wo, g2, rwh, rwl, s1, s3, s2, seq_row):
    n = x.shape[0]
    nct = mix_ctx[0].shape[0] // ROW_TILE
    const = lambda i: (0, 0)
    rowb = lambda w: pl.BlockSpec((ROW_TILE, w), lambda i: (i, 0))
    ctxb = lambda w: pl.BlockSpec((ROW_TILE, w), lambda i: (jnp.minimum(i, nct - 1), 0))
    latb = lambda w: pl.BlockSpec((ROW_TILE, w), lambda i: (jnp.maximum(i - nct, 0), 0))
    return pl.pallas_call(
        functools.partial(_out_kernel, n_ctx_tiles=nct),
        grid=(n // ROW_TILE,),
        in_specs=[rowb(D_MODEL), ctxb(ATTN_W), ctxb(LRU_W), ctxb(GLA_W), latb(ATTN_W), latb(LRU_W), latb(GLA_W),
                  pl.BlockSpec((1, 1, 6 * D_MODEL), lambda i: (seq_row(i), 0, 0)),
                  pl.BlockSpec((D_MODEL, D_MODEL), const), pl.BlockSpec((1, D_MODEL), const),
                  pl.BlockSpec((D_MODEL, N_EXPERTS), const), pl.BlockSpec((D_MODEL, N_EXPERTS), const),
                  pl.BlockSpec((D_MODEL, EXPERT_FF), const), pl.BlockSpec((D_MODEL, EXPERT_FF), const),
                  pl.BlockSpec((EXPERT_FF, D_MODEL), const)],
        out_specs=[rowb(D_MODEL),
                   pl.BlockSpec((ROW_TILE, D_SLABS, LANES), lambda i: (i, 0, 0)),
                   rowb(N_EXPERTS)],
        out_shape=[jax.ShapeDtypeStruct((n, D_MODEL), f32),
                   jax.ShapeDtypeStruct((n, D_SLABS, LANES), f32),
                   jax.ShapeDtypeStruct((n, N_EXPERTS), f32)],
        compiler_params=_cparams(("parallel",)),
        name="out_proj",
    )(x, *mix_ctx, *mix_lat, mod, wo, g2, rwh, rwl, s1, s3, s2)


def _route_kernel(sc_ref, rb_ref, idx_ref, gate_ref, rank_ref, cnt_ref, run_s):
    tm = sc_ref.shape[0]

    @pl.when(pl.program_id(0) == 0)
    def _():
        run_s[...] = jnp.zeros_like(run_s)

    scores = sc_ref[...]
    sel = scores + rb_ref[...]
    lane_f = lax.broadcasted_iota(i32, (tm, N_EXPERTS), 1).astype(f32)
    lane_o = lax.broadcasted_iota(i32, (tm, LANES), 1)
    neg = jnp.full((tm, N_EXPERTS), -jnp.inf, f32)
    hots = []
    idx_o = jnp.zeros((tm, LANES), i32)
    gate_o = jnp.zeros((tm, LANES), f32)
    gsum = jnp.zeros((tm, 1), f32)
    chosen = jnp.zeros((tm, N_EXPERTS), f32)
    for k in range(TOP_K):
        m = jnp.max(sel, axis=-1, keepdims=True)
        idx_f = jnp.min(jnp.where(sel == m, lane_f, float(N_EXPERTS)), axis=-1, keepdims=True)
        hot = lane_f == idx_f
        idx = idx_f.astype(i32)
        gk = jnp.sum(jnp.where(hot, scores, 0.0), axis=-1, keepdims=True)
        sel = jnp.where(hot, neg, sel)
        hots.append(hot)
        chosen = jnp.where(hot, 1.0, chosen)
        gsum = gsum + gk
        idx_o = jnp.where(lane_o == k, idx, idx_o)
        gate_o = jnp.where(lane_o == k, gk, gate_o)
    gate_ref[...] = gate_o / gsum * ROUTED_SCALE
    idx_ref[...] = idx_o
    r = lax.broadcasted_iota(i32, (tm, tm), 0)
    c = lax.broadcasted_iota(i32, (tm, tm), 1)
    pos = _dot((c < r).astype(bf16), chosen.astype(bf16)) + run_s[...]
    rank_o = jnp.zeros((tm, LANES), i32)
    for k in range(TOP_K):
        rk = jnp.sum(jnp.where(hots[k], pos, 0.0), axis=-1, keepdims=True)
        rank_o = jnp.where(lane_o == k, rk.astype(i32), rank_o)
    rank_ref[...] = rank_o
    run_s[...] = run_s[...] + jnp.sum(chosen, axis=0, keepdims=True)
    cnt_ref[...] = run_s[...]


def _route_call(scores, rb):
    n = scores.shape[0]
    rowb = lambda w: pl.BlockSpec((ROW_TILE, w), lambda i: (i, 0))
    return pl.pallas_call(
        _route_kernel,
        grid=(n // ROW_TILE,),
        in_specs=[rowb(N_EXPERTS), pl.BlockSpec((1, N_EXPERTS), lambda i: (0, 0))],
        out_specs=[rowb(LANES), rowb(LANES), rowb(LANES), pl.BlockSpec((1, N_EXPERTS), lambda i: (0, 0))],
        out_shape=[jax.ShapeDtypeStruct((n, LANES), i32), jax.ShapeDtypeStruct((n, LANES), f32),
                   jax.ShapeDtypeStruct((n, LANES), i32), jax.ShapeDtypeStruct((1, N_EXPERTS), f32)],
        scratch_shapes=[pltpu.VMEM((1, N_EXPERTS), f32)],
        compiler_params=_cparams(("arbitrary",)),
        name="route",
    )(scores, rb)


def _dispatch_kernel(dest_ref, h_ref, slots_in, slots_ref, sem):
    del slots_in
    tm = h_ref.shape[0]

    def issue(i, carry):
        tok = i // TOP_K
        pltpu.make_async_copy(h_ref.at[tok], slots_ref.at[dest_ref[i]], sem).start()
        return carry

    lax.fori_loop(0, tm * TOP_K, issue, 0)

    def drain(i, carry):
        pltpu.make_async_copy(h_ref.at[0], slots_ref.at[0], sem).wait()
        return carry

    lax.fori_loop(0, tm * TOP_K, drain, 0)


def _dispatch_call(dest_flat, h2, slots):
    n = h2.shape[0]
    return pl.pallas_call(
        _dispatch_kernel,
        grid=(n // ROW_TILE,),
        in_specs=[pl.BlockSpec((ROW_TILE * TOP_K,), lambda i: (i,), memory_space=pltpu.SMEM),
                  pl.BlockSpec((ROW_TILE, D_SLABS, LANES), lambda i: (i, 0, 0)),
                  pl.BlockSpec(memory_space=pl.ANY)],
        out_specs=pl.BlockSpec(memory_space=pl.ANY),
        out_shape=jax.ShapeDtypeStruct(slots.shape, f32),
        scratch_shapes=[pltpu.SemaphoreType.DMA(())],
        input_output_aliases={2: 0},
        compiler_params=_cparams(("arbitrary",)),
        name="dispatch",
    )(dest_flat, h2, slots)


def _expert_kernel(be_ref, nu_ref, x_ref, w1_ref, w3_ref, w2_ref, y_ref, w1_s, w3_s, w2_s):
    i = pl.program_id(0)

    @pl.when(i < nu_ref[0])
    def _():
        prev = be_ref[jnp.maximum(i - 1, 0)]

        @pl.when((i == 0) | (be_ref[i] != prev))
        def _():
            w1_s[...] = w1_ref[0].astype(bf16)
            w3_s[...] = w3_ref[0].astype(bf16)
            w2_s[...] = w2_ref[0].astype(bf16)

        x = jnp.concatenate([x_ref[:, j, :] for j in range(D_SLABS)], axis=1).astype(bf16)
        act = (_silu(_dot(x, w1_s[...])) * _dot(x, w3_s[...])).astype(bf16)
        y = _dot(act, w2_s[...])
        for j in range(D_SLABS):
            y_ref[:, j, :] = y[:, j * LANES:(j + 1) * LANES]


def _expert_call(blk_e, n_used, slots, w1, w3, w2):
    nb = slots.shape[0] // SLOT_BLOCK
    blk = lambda i, be, nu: (jnp.minimum(i, nu[0] - 1), 0, 0)
    wsel = lambda i, be, nu: (be[jnp.minimum(i, nu[0] - 1)], 0, 0)
    grid_spec = pltpu.PrefetchScalarGridSpec(
        num_scalar_prefetch=2,
        grid=(nb,),
        in_specs=[pl.BlockSpec((SLOT_BLOCK, D_SLABS, LANES), blk),
                  pl.BlockSpec((1, D_MODEL, EXPERT_FF), wsel),
                  pl.BlockSpec((1, D_MODEL, EXPERT_FF), wsel),
                  pl.BlockSpec((1, EXPERT_FF, D_MODEL), wsel)],
        out_specs=pl.BlockSpec((SLOT_BLOCK, D_SLABS, LANES), blk),
        scratch_shapes=[pltpu.VMEM((D_MODEL, EXPERT_FF), bf16), pltpu.VMEM((D_MODEL, EXPERT_FF), bf16),
                        pltpu.VMEM((EXPERT_FF, D_MODEL), bf16)],
    )
    return pl.pallas_call(
        _expert_kernel,
        grid_spec=grid_spec,
        out_shape=jax.ShapeDtypeStruct(slots.shape, f32),
        input_output_aliases={2: 0},
        compiler_params=_cparams(("arbitrary",)),
        name="experts",
    )(blk_e, n_used, slots, w1, w3, w2)


def _combine_kernel(dest_ref, gate_ref, xb_ref, mod_ref, y_hbm, o_ref, buf, sem):
    tm = xb_ref.shape[0]

    def issue(i, carry):
        tok = i // TOP_K
        k = i - tok * TOP_K
        pltpu.make_async_copy(y_hbm.at[dest_ref[i]], buf.at[k * tm + tok], sem).start()
        return carry

    lax.fori_loop(0, tm * TOP_K, issue, 0)

    def drain(i, carry):
        pltpu.make_async_copy(y_hbm.at[0], buf.at[0], sem).wait()
        return carry

    lax.fori_loop(0, tm * TOP_K, drain, 0)
    g2 = mod_ref[0][:, 5 * D_MODEL:6 * D_MODEL]
    gates = gate_ref[...]
    for j in range(D_SLABS):
        acc = jnp.zeros((tm, LANES), f32)
        for k in range(TOP_K):
            acc = acc + gates[:, k:k + 1] * buf[k * tm:(k + 1) * tm, j, :]
        cols = slice(j * LANES, (j + 1) * LANES)
        o_ref[:, cols] = xb_ref[:, cols] + g2[:, cols] * acc


def _combine_call(dest_flat, gates, xbase, mod, yslots, seq_row_c):
    n = xbase.shape[0]
    tm = COMBINE_TILE
    return pl.pallas_call(
        _combine_kernel,
        grid=(n // tm,),
        in_specs=[pl.BlockSpec((tm * TOP_K,), lambda i: (i,), memory_space=pltpu.SMEM),
                  pl.BlockSpec((tm, LANES), lambda i: (i, 0)),
                  pl.BlockSpec((tm, D_MODEL), lambda i: (i, 0)),
                  pl.BlockSpec((1, 1, 6 * D_MODEL), lambda i: (seq_row_c(i), 0, 0)),
                  pl.BlockSpec(memory_space=pl.ANY)],
        out_specs=pl.BlockSpec((tm, D_MODEL), lambda i: (i, 0)),
        out_shape=jax.ShapeDtypeStruct((n, D_MODEL), f32),
        scratch_shapes=[pltpu.VMEM((tm * TOP_K, D_SLABS, LANES), f32), pltpu.SemaphoreType.DMA(())],
        compiler_params=_cparams(("arbitrary",)),
        name="combine",
    )(dest_flat, gates, xbase, mod, yslots)


def _block_avg(width, group):
    r = np.arange(width)
    return jnp.asarray((r[:, None] // group == r[None, :] // group).astype(np.float32) / group, dtype=bf16)


def _kv_replicate():
    c = np.arange(ATTN_W)
    src = (c // (ATTN_W // N_KV_HEADS)) * HEAD_DIM + c % HEAD_DIM
    return jnp.asarray((np.arange(KV_W)[:, None] == src[None, :]).astype(np.float32), dtype=bf16)


def _rope_lane_tables(n_tok):
    rows = n_tok // GRID_W
    r = jnp.repeat(jnp.arange(rows, dtype=f32), GRID_W)
    col = jnp.tile(jnp.arange(GRID_W, dtype=f32), rows)
    inv = ROPE_THETA ** (-jnp.arange(ROPE_FREQ, dtype=f32) / ROPE_FREQ)
    ar = r[:, None] * inv
    ac = col[:, None] * inv
    cos_h = jnp.concatenate([jnp.cos(ar), jnp.cos(ar), jnp.cos(ac), jnp.cos(ac)], axis=-1)
    sin_h = jnp.concatenate([-jnp.sin(ar), jnp.sin(ar), -jnp.sin(ac), jnp.sin(ac)], axis=-1)
    return jnp.tile(cos_h, (1, N_HEADS)), jnp.tile(sin_h, (1, N_HEADS))


def _block_diag(w):
    nb, bw, _ = w.shape
    eye = jnp.eye(nb, dtype=w.dtype)
    return (w[:, :, None, :] * eye[:, None, :, None]).reshape(nb * bw, nb * bw)


def _gla_state_in(s):
    bsz = s.shape[0]
    eye = jnp.eye(GLA_H, dtype=s.dtype)
    st = jnp.swapaxes(s, -1, -2)
    big = st[:, :, :, :, None, :] * eye[None, None, :, None, :, None]
    return big.reshape(bsz, 2, GLA_W, GLA_W)


def _gla_state_out(st):
    bsz = st.shape[0]
    s6 = st.reshape(bsz, 2, GLA_H, GLA_DK, GLA_H, GLA_DK)
    diag = jnp.stack([s6[:, :, h, :, h, :] for h in range(GLA_H)], axis=2)
    return jnp.swapaxes(diag, -1, -2)


def kernel(x_prompt, x_sample, cache_k, cache_v, state_lru, state_gla, c, c_ctx, ada_w, ada_b, norm1_g, norm2_g, w_in, q_norm_g, k_norm_g, attn_out_g, conv_w, conv_b, lru_wa, lru_ba, lru_wi, lru_bi, lru_lambda, lru_out_g, gla_wa2, gla_ba, gla_out_g, w_out, router_w, router_b, exp_w1, exp_w3, exp_w2, sh_w1, sh_w3, sh_w2):
    bc, tc, _ = x_prompt.shape
    bl, tl, _ = x_sample.shape
    depth = w_in.shape[0]
    nc = bc * tc
    n = nc + bl * tl
    past = cache_k.shape[2]
    assert tc == ROW_TILE and tl % ROW_TILE == 0 and nc % tl == 0 and bl + 1 <= SUBLANES
    assert n % ROW_TILE == 0 and n % COMBINE_TILE == 0

    def seq_row_for(tile):
        def seq_row(i):
            return jnp.where(i < nc // tile, 0, 1 + (i - nc // tile) // (tl // tile))
        return seq_row

    seq_row = seq_row_for(ROW_TILE)
    seq_row_c = seq_row_for(COMBINE_TILE)

    x = jnp.concatenate([x_prompt.reshape(nc, D_MODEL), x_sample.reshape(bl * tl, D_MODEL)], axis=0)
    cond = jnp.zeros((SUBLANES, D_MODEL), f32).at[0].set(c_ctx).at[1:1 + bl].set(c)
    mods = _ada_call(cond, ada_w, ada_b)

    bmq = _block_avg(ATTN_W, HEAD_DIM)
    bmk = _block_avg(KV_W, HEAD_DIM)
    bmg = _block_avg(GLA_W, GLA_DK)
    rep = _kv_replicate()
    cos_t, sin_t = _rope_lane_tables(tl)
    n_slots = -(-(n * TOP_K + N_EXPERTS * (SLOT_BLOCK - 1)) // SLOT_BLOCK) * SLOT_BLOCK
    slots = jnp.zeros((n_slots, D_SLABS, LANES), f32)
    tile8 = lambda v: jnp.tile(v, N_HEADS)[None, :]

    ks, vs, lrus, glas = [], [], [], []
    for l in range(depth):
        mod = mods[l].reshape(SUBLANES, 1, 6 * D_MODEL)
        w_in_p = jnp.pad(w_in[l].astype(bf16), ((0, 0), (0, LANES - 2 * GLA_RANK)))
        pa, pl_, pg = _proj_call(x, mod, norm1_g[l][None, :], w_in_p, seq_row)

        qg, kg, og = tile8(q_norm_g[l]), jnp.tile(k_norm_g[l], N_KV_HEADS)[None, :], attn_out_g[l][None, :]
        attn_c, k_new, v_new = _attn_ctx_call(pa, bc, tc, qg, kg, og, bmq, bmk, rep)
        attn_l = _attn_lat_call(pa, nc, bl, tl, cache_k[:, l].reshape(bl, past, KV_W),
                                cache_v[:, l].reshape(bl, past, KV_W), cos_t, sin_t, qg, kg, og, bmq, bmk, rep)
        ks.append(k_new.reshape(bc, tc, N_KV_HEADS, HEAD_DIM))
        vs.append(v_new.reshape(bc, tc, N_KV_HEADS, HEAD_DIM))

        wg = jnp.concatenate([_block_diag(lru_wa[l, 0]), _block_diag(lru_wi[l, 0]),
                              _block_diag(lru_wa[l, 1]), _block_diag(lru_wi[l, 1])], axis=1).astype(bf16)
        bg = jnp.concatenate([lru_ba[l, 0], lru_bi[l, 0], lru_ba[l, 1], lru_bi[l, 1]])[None, :]
        lru_args = (conv_w[l], conv_b[l][None, :], wg, bg, lru_lambda[l], lru_out_g[l][None, :])
        lru_c, lru_fin = _lru_call(pl_, 0, bc, tc, jnp.zeros((bc, 2, LRU_W), f32), *lru_args)
        lru_l, _ = _lru_call(pl_, nc, bl, tl, state_lru[:, l], *lru_args)
        lrus.append(lru_fin)

        wa = jnp.zeros((LANES, 2 * GLA_W), f32)
        wa = wa.at[0:GLA_RANK, 0:GLA_W].set(gla_wa2[l, 0]).at[GLA_RANK:2 * GLA_RANK, GLA_W:].set(gla_wa2[l, 1])
        gla_args = (wa.astype(bf16), gla_ba[l].reshape(1, 2 * GLA_W), gla_out_g[l].reshape(1, GLA_W), bmg)
        gla_c, gla_fin = _gla_call(pg, 0, bc, tc, None, *gla_args)
        gla_l, _ = _gla_call(pg, nc, bl, tl, _gla_state_in(state_gla[:, l]), *gla_args)
        glas.append(_gla_state_out(gla_fin))

        rw_hi = router_w[l].astype(bf16)
        rw_lo = (router_w[l] - rw_hi.astype(f32)).astype(bf16)
        xbase, h2, scores = _out_call(x, (attn_c, lru_c, gla_c), (attn_l, lru_l, gla_l), mod, w_out[l].astype(bf16),
                                      norm2_g[l][None, :], rw_hi, rw_lo, sh_w1[l].astype(bf16),
                                      sh_w3[l].astype(bf16), sh_w2[l].astype(bf16), seq_row)

        idx, gates, rank, counts = _route_call(scores, router_b[l][None, :])
        counts = counts[0].astype(i32)
        padded = (counts + SLOT_BLOCK - 1) // SLOT_BLOCK * SLOT_BLOCK
        padded_end = jnp.cumsum(padded)
        padded_start = padded_end - padded
        dest = (padded_start[idx[:, :TOP_K]] + rank[:, :TOP_K]).reshape(-1)
        nb = n_slots // SLOT_BLOCK
        blk_e = jnp.minimum(jnp.searchsorted(padded_end, jnp.arange(nb, dtype=i32) * SLOT_BLOCK, side='right'),
                            N_EXPERTS - 1).astype(i32)
        n_used = (padded_end[-1:] // SLOT_BLOCK).astype(i32)
        slots = _dispatch_call(dest, h2, slots)
        slots = _expert_call(blk_e, n_used, slots, exp_w1[l], exp_w3[l], exp_w2[l])
        x = _combine_call(dest, gates, xbase, mod, slots, seq_row_c)

    y_prompt = x[:nc].reshape(bc, tc, D_MODEL)
    y_sample = x[nc:].reshape(bl, tl, D_MODEL)
    return (y_prompt, y_sample, jnp.stack(ks, axis=1), jnp.stack(vs, axis=1),
            jnp.stack(lrus, axis=1), jnp.stack(glas, axis=1))
```

```python
import functools

import jax
import jax.numpy as jnp
import numpy as np
from jax import lax
from jax.experimental import pallas as pl
from jax.experimental.pallas import tpu as pltpu
from jax.experimental.pallas import tpu_sc as plsc

f32 = jnp.float32
bf16 = jnp.bfloat16
i32 = jnp.int32

D_MODEL = 1024
N_HEADS = 8
N_KV_HEADS = 2
HEAD_DIM = 64
ATTN_W = N_HEADS * HEAD_DIM
KV_W = N_KV_HEADS * HEAD_DIM
GRID_W = 64
ROPE_FREQ = HEAD_DIM // 4
ROPE_THETA = 10000.0
LRU_W = 256
LRU_BLOCKS = 4
LRU_C = 8.0
CONV_W = 4
CONV_LEFT = 2
GLA_H = 4
GLA_DK = 64
GLA_W = 256
GLA_RANK = 16
GLA_TAU = 16.0
N_EXPERTS = 256
TOP_K = 8
EXPERT_FF = 256
ROUTED_SCALE = 2.5
EPS = 1e-6

LANES = 128
SUBLANES = 8
ROW_TILE = 256
GLA_BLOCK = 256
SLOT_BLOCK = 256
COMBINE_TILE = 128
SC_CHUNK = 32
D_SLABS = D_MODEL // LANES
VMEM_LIMIT = 56 * 1024 * 1024


def _cparams(sem, vmem=VMEM_LIMIT):
    return pltpu.CompilerParams(dimension_semantics=sem, vmem_limit_bytes=vmem)


def _dot(a, b):
    return jnp.dot(a, b, preferred_element_type=f32)


def _dot_nt(a, b):
    return lax.dot_general(a, b, (((1,), (1,)), ((), ())), preferred_element_type=f32)


def _dot_tn(a, b):
    return lax.dot_general(a, b, (((0,), (0,)), ((), ())), preferred_element_type=f32)


def _split(x):
    hi = x.astype(bf16)
    lo = (x - hi.astype(f32)).astype(bf16)
    return hi, lo


def _dot_x2(x, w):
    hi, lo = _split(x)
    return _dot(hi, w) + _dot(lo, w)


def _dot_2x(m, x):
    hi, lo = _split(x)
    return _dot(m, hi) + _dot(m, lo)


def _dot3(a, b_hi, b_lo):
    a_hi, a_lo = _split(a)
    return _dot(a_hi, b_hi) + _dot(a_lo, b_hi) + _dot(a_hi, b_lo)


def _sigmoid(x):
    return 1.0 / (1.0 + jnp.exp(-x))


def _silu(x):
    return x * _sigmoid(x)


def _softplus(x):
    return jnp.maximum(x, 0.0) + jnp.log(1.0 + jnp.exp(-jnp.abs(x)))


def _gelu_tanh(x):
    return 0.5 * x * (1.0 + jnp.tanh(0.7978845608028654 * (x + 0.044715 * x * x * x)))


def _rms(x, g):
    return x * lax.rsqrt(jnp.mean(x * x, axis=-1, keepdims=True) + EPS) * g


def _head_rms(x, g, bm):
    ms = _dot_x2(x * x, bm)
    return x * lax.rsqrt(ms + EPS) * g


def _ada_kernel(c_ref, w_ref, b_ref, o_ref):
    s = _silu(c_ref[...])
    w = w_ref[0]
    w_hi, w_lo = _split(w)
    o_ref[0] = _dot3(s, w_hi, w_lo) + b_ref[0]


def _ada_call(cond, ada_w, ada_b):
    depth = ada_w.shape[0]
    nt = 1536
    return pl.pallas_call(
        _ada_kernel,
        grid=(depth, 6 * D_MODEL // nt),
        in_specs=[pl.BlockSpec((SUBLANES, D_MODEL), lambda l, j: (0, 0)),
                  pl.BlockSpec((1, D_MODEL, nt), lambda l, j: (l, 0, j)),
                  pl.BlockSpec((1, 1, nt), lambda l, j: (l, 0, j))],
        out_specs=pl.BlockSpec((1, SUBLANES, nt), lambda l, j: (l, 0, j)),
        out_shape=jax.ShapeDtypeStruct((depth, SUBLANES, 6 * D_MODEL), f32),
        compiler_params=_cparams(("parallel", "parallel")),
        name="ada_mod",
    )(cond, ada_w, ada_b.reshape(depth, 1, 6 * D_MODEL))


PROJ_A = ATTN_W + 2 * KV_W
PROJ_L = 2 * LRU_W
PROJ_G = 4 * GLA_W + LANES


def _proj_kernel(x_ref, mod_ref, g_ref, w_ref, oa_ref, ol_ref, og_ref):
    mod = mod_ref[0]
    sh = mod[:, 0:D_MODEL]
    sc = mod[:, D_MODEL:2 * D_MODEL]
    h = (_rms(x_ref[...], g_ref[...]) * (1.0 + sc) + sh).astype(bf16)
    p = _dot(h, w_ref[...])
    oa_ref[...] = p[:, 0:PROJ_A]
    ol_ref[...] = p[:, PROJ_A:PROJ_A + PROJ_L]
    og_ref[...] = p[:, PROJ_A + PROJ_L:PROJ_A + PROJ_L + PROJ_G]


def _proj_call(x, mod, g, w, seq_row):
    n = x.shape[0]
    cols = PROJ_A + PROJ_L + PROJ_G
    return pl.pallas_call(
        _proj_kernel,
        grid=(n // ROW_TILE,),
        in_specs=[pl.BlockSpec((ROW_TILE, D_MODEL), lambda i: (i, 0)),
                  pl.BlockSpec((1, 1, 6 * D_MODEL), lambda i: (seq_row(i), 0, 0)),
                  pl.BlockSpec((1, D_MODEL), lambda i: (0, 0)),
                  pl.BlockSpec((D_MODEL, cols), lambda i: (0, 0))],
        out_specs=[pl.BlockSpec((ROW_TILE, PROJ_A), lambda i: (i, 0)),
                   pl.BlockSpec((ROW_TILE, PROJ_L), lambda i: (i, 0)),
                   pl.BlockSpec((ROW_TILE, PROJ_G), lambda i: (i, 0))],
        out_shape=[jax.ShapeDtypeStruct((n, PROJ_A), f32),
                   jax.ShapeDtypeStruct((n, PROJ_L), f32),
                   jax.ShapeDtypeStruct((n, PROJ_G), f32)],
        compiler_params=_cparams(("parallel",)),
        name="in_proj",
    )(x, mod, g, w)


def _rope(x, cos_t, sin_t):
    w = x.shape[1]
    up = pltpu.roll(x, w - ROPE_FREQ, 1)
    dn = pltpu.roll(x, ROPE_FREQ, 1)
    lane = lax.broadcasted_iota(i32, x.shape, 1)
    partner = jnp.where((lane & (2 * ROPE_FREQ - 1)) < ROPE_FREQ, up, dn)
    return x * cos_t + partner * sin_t


def _attend(q, kk_ref, vv_ref, o_ref):
    tq = q.shape[0]
    gw = ATTN_W // N_KV_HEADS
    lane = lax.broadcasted_iota(i32, (tq, gw), 1)
    for g in range(N_KV_HEADS):
        qg = q[:, g * gw:(g + 1) * gw]
        kg = kk_ref[:, g * gw:(g + 1) * gw]
        vg = vv_ref[:, g * gw:(g + 1) * gw]
        acc = jnp.zeros((tq, gw), f32)
        for hh in range(N_HEADS // N_KV_HEADS):
            hm = (lane >> 6) == hh
            s = _dot_nt(jnp.where(hm, qg, 0.0).astype(bf16), kg)
            m = jnp.max(s, axis=-1, keepdims=True)
            p = jnp.exp(s - m)
            l = jnp.sum(p, axis=-1, keepdims=True)
            o = _dot(p.astype(bf16), vg) / l
            acc = jnp.where(hm, o, acc)
        o_ref[:, g * gw:(g + 1) * gw] = acc


def _attn_ctx_kernel(p_ref, qg_ref, kg_ref, og_ref, bmq_ref, bmk_ref, rep_ref,
                     o_ref, ko_ref, vo_ref, kk_s, vv_s, o_s):
    p = p_ref[...]
    q = _head_rms(p[:, 0:ATTN_W], qg_ref[...], bmq_ref[...])
    k = _head_rms(p[:, ATTN_W:ATTN_W + KV_W], kg_ref[...], bmk_ref[...])
    v = p[:, ATTN_W + KV_W:PROJ_A]
    ko_ref[...] = k
    vo_ref[...] = v
    kk_s[...] = _dot(k.astype(bf16), rep_ref[...]).astype(bf16)
    vv_s[...] = _dot(v.astype(bf16), rep_ref[...]).astype(bf16)
    _attend(q * HEAD_DIM ** -0.5, kk_s, vv_s, o_s)
    o_ref[...] = _rms(o_s[...], og_ref[...])


def _attn_ctx_call(pa, n_seq, t, qg, kg, og, bmq, bmk, rep):
    assert t == ROW_TILE
    const = lambda i: (0, 0)
    return pl.pallas_call(
        _attn_ctx_kernel,
        grid=(n_seq,),
        in_specs=[pl.BlockSpec((t, PROJ_A), lambda i: (i, 0)),
                  pl.BlockSpec((1, ATTN_W), const), pl.BlockSpec((1, KV_W), const), pl.BlockSpec((1, ATTN_W), const),
                  pl.BlockSpec((ATTN_W, ATTN_W), const), pl.BlockSpec((KV_W, KV_W), const),
                  pl.BlockSpec((KV_W, ATTN_W), const)],
        out_specs=[pl.BlockSpec((t, ATTN_W), lambda i: (i, 0)),
                   pl.BlockSpec((t, KV_W), lambda i: (i, 0)),
                   pl.BlockSpec((t, KV_W), lambda i: (i, 0))],
        out_shape=[jax.ShapeDtypeStruct((n_seq * t, ATTN_W), f32),
                   jax.ShapeDtypeStruct((n_seq * t, KV_W), f32),
                   jax.ShapeDtypeStruct((n_seq * t, KV_W), f32)],
        scratch_shapes=[pltpu.VMEM((t, ATTN_W), bf16), pltpu.VMEM((t, ATTN_W), bf16), pltpu.VMEM((t, ATTN_W), f32)],
        compiler_params=_cparams(("parallel",)),
        name="attn_ctx",
    )(pa, qg, kg, og, bmq, bmk, rep)


def _attn_lat_kernel(q_ref, kv_ref, ck_ref, cv_ref, cq_ref, sq_ref, ckk_ref, skk_ref,
                     qg_ref, kg_ref, og_ref, bmq_ref, bmk_ref, rep_ref,
                     o_ref, kk_s, vv_s, o_s, *, past):
    @pl.when(pl.program_id(1) == 0)
    def _():
        kv = kv_ref[...]
        k = _head_rms(kv[:, 0:KV_W], kg_ref[...], bmk_ref[...])
        k = _rope(k, ckk_ref[...], skk_ref[...])
        v = kv[:, KV_W:2 * KV_W]
        kk_s[0:past, :] = _dot(ck_ref[0].astype(bf16), rep_ref[...]).astype(bf16)
        vv_s[0:past, :] = _dot(cv_ref[0].astype(bf16), rep_ref[...]).astype(bf16)
        kk_s[past:, :] = _dot(k.astype(bf16), rep_ref[...]).astype(bf16)
        vv_s[past:, :] = _dot(v.astype(bf16), rep_ref[...]).astype(bf16)

    q = _head_rms(q_ref[...], qg_ref[...], bmq_ref[...])
    q = _rope(q, cq_ref[...], sq_ref[...])
    _attend(q * HEAD_DIM ** -0.5, kk_s, vv_s, o_s)
    o_ref[...] = _rms(o_s[...], og_ref[...])


def _attn_lat_call(pa, row0, n_seq, t, cache_k, cache_v, cq, sq, qg, kg, og, bmq, bmk, rep):
    past = cache_k.shape[1]
    nq = t // ROW_TILE
    assert row0 % t == 0 and t % ROW_TILE == 0
    qb0 = row0 // ROW_TILE
    sb0 = row0 // t
    const = lambda b, j: (0, 0)
    return pl.pallas_call(
        functools.partial(_attn_lat_kernel, past=past),
        grid=(n_seq, nq),
        in_specs=[pl.BlockSpec((ROW_TILE, ATTN_W), lambda b, j: (qb0 + b * nq + j, 0)),
                  pl.BlockSpec((t, 2 * KV_W), lambda b, j: (sb0 + b, ATTN_W // (2 * KV_W))),
                  pl.BlockSpec((1, past, KV_W), lambda b, j: (b, 0, 0)),
                  pl.BlockSpec((1, past, KV_W), lambda b, j: (b, 0, 0)),
                  pl.BlockSpec((ROW_TILE, ATTN_W), lambda b, j: (j, 0)),
                  pl.BlockSpec((ROW_TILE, ATTN_W), lambda b, j: (j, 0)),
                  pl.BlockSpec((t, KV_W), const), pl.BlockSpec((t, KV_W), const),
                  pl.BlockSpec((1, ATTN_W), const), pl.BlockSpec((1, KV_W), const), pl.BlockSpec((1, ATTN_W), const),
                  pl.BlockSpec((ATTN_W, ATTN_W), const), pl.BlockSpec((KV_W, KV_W), const),
                  pl.BlockSpec((KV_W, ATTN_W), const)],
        out_specs=pl.BlockSpec((ROW_TILE, ATTN_W), lambda b, j: (b * nq + j, 0)),
        out_shape=jax.ShapeDtypeStruct((n_seq * t, ATTN_W), f32),
        scratch_shapes=[pltpu.VMEM((past + t, ATTN_W), bf16), pltpu.VMEM((past + t, ATTN_W), bf16),
                        pltpu.VMEM((ROW_TILE, ATTN_W), f32)],
        compiler_params=_cparams(("parallel", "arbitrary")),
        name="attn_lat",
    )(pa, pa, cache_k, cache_v, cq, sq, cq, sq, qg, kg, og, bmq, bmk, rep)


def _lru_kernel(x_ref, h0_ref, cw_ref, cb_ref, wg_ref, bg_ref, lam_ref, g_ref,
                o_ref, fin_ref, xs, a_s, u_s, h_s, *, t):
    ng = t // SUBLANES
    lx = x_ref[:, 0:LRU_W]
    ly = x_ref[:, LRU_W:2 * LRU_W]
    xs[0:SUBLANES, :] = jnp.zeros((SUBLANES, LRU_W), f32)
    xs[SUBLANES + t:2 * SUBLANES + t, :] = jnp.zeros((SUBLANES, LRU_W), f32)
    xs[SUBLANES:SUBLANES + t, :] = lx
    xc = jnp.broadcast_to(cb_ref[...], (t, LRU_W))
    for j in range(CONV_W):
        off = SUBLANES + j - CONV_LEFT
        xc = xc + xs[off:off + t, :] * cw_ref[j:j + 1, :]
    gates = _sigmoid(_dot(xc.astype(bf16), wg_ref[...]) + bg_ref[...])
    sp = _softplus(-lam_ref[...])
    sub = lax.broadcasted_iota(i32, (t, LRU_W), 0) & (SUBLANES - 1)
    for d in range(2):
        r = gates[:, (2 * d) * LRU_W:(2 * d + 1) * LRU_W]
        gi = gates[:, (2 * d + 1) * LRU_W:(2 * d + 2) * LRU_W]
        log_a = -LRU_C * r * sp[d:d + 1, :]
        a = jnp.exp(log_a)
        th = jnp.tanh(log_a)
        u = jnp.sqrt(-2.0 * th / (1.0 - th)) * (gi * xc)
        for s in (1, 2, 4):
            if d == 0:
                a_sh = pltpu.roll(a, s, 0)
                u_sh = pltpu.roll(u, s, 0)
                ok = sub >= s
            else:
                a_sh = pltpu.roll(a, t - s, 0)
                u_sh = pltpu.roll(u, t - s, 0)
                ok = sub < SUBLANES - s
            u = jnp.where(ok, a * u_sh + u, u)
            a = jnp.where(ok, a * a_sh, a)
        a_s[...] = a
        u_s[...] = u
        h0 = jnp.broadcast_to(h0_ref[0, d:d + 1, :], (SUBLANES, LRU_W))
        edge = SUBLANES - 1 if d == 0 else 0

        def body(i, carry, d=d, edge=edge):
            g = i if d == 0 else ng - 1 - i
            rows = pl.ds(pl.multiple_of(g * SUBLANES, SUBLANES), SUBLANES)
            h = a_s[rows, :] * carry + u_s[rows, :]
            if d == 0:
                h_s[rows, :] = h
            else:
                h_s[rows, :] = h_s[rows, :] + h
            return jnp.broadcast_to(h[edge:edge + 1, :], (SUBLANES, LRU_W))

        last = lax.fori_loop(0, ng, body, h0)
        fin_ref[0, d:d + 1, :] = last[0:1, :]
    o_ref[...] = _rms(h_s[...] * _gelu_tanh(ly), g_ref[...])


def _lru_call(pl_, row0, n_seq, t, h0, cw, cb, wg, bg, lam, g):
    sb0 = row0 // t
    assert row0 % t == 0
    const = lambda b: (0, 0)
    in_specs = [pl.BlockSpec((t, PROJ_L), lambda b: (sb0 + b, 0)),
                pl.BlockSpec((1, 2, LRU_W), lambda b: (b, 0, 0)),
                pl.BlockSpec((CONV_W, LRU_W), const), pl.BlockSpec((1, LRU_W), const),
                pl.BlockSpec((LRU_W, 4 * LRU_W), const), pl.BlockSpec((1, 4 * LRU_W), const),
                pl.BlockSpec((2, LRU_W), const), pl.BlockSpec((1, LRU_W), const)]
    args = [pl_, h0, cw, cb, wg, bg, lam, g]
    return pl.pallas_call(
        functools.partial(_lru_kernel, t=t),
        grid=(n_seq,),
        in_specs=in_specs,
        out_specs=[pl.BlockSpec((t, LRU_W), lambda b: (b, 0)),
                   pl.BlockSpec((1, 2, LRU_W), lambda b: (b, 0, 0))],
        out_shape=[jax.ShapeDtypeStruct((n_seq * t, LRU_W), f32), jax.ShapeDtypeStruct((n_seq, 2, LRU_W), f32)],
        scratch_shapes=[pltpu.VMEM((t + 2 * SUBLANES, LRU_W), f32), pltpu.VMEM((t, LRU_W), f32),
                        pltpu.VMEM((t, LRU_W), f32), pltpu.VMEM((t, LRU_W), f32)],
        compiler_params=_cparams(("parallel",)),
        name="rglru_t%d" % t,
    )(*args)


def _bcast_rows(b, period, off):
    w = b.shape[1]
    return jnp.concatenate(
        [jnp.broadcast_to(b[i * period + off:i * period + off + 1, :], (period, w)) for i in range(b.shape[0] // period)],
        axis=0)


def _gla_block(q, k, v, la, reverse, st_ref):
    n = GLA_BLOCK
    row = lax.broadcasted_iota(i32, (n, n), 0)
    col = lax.broadcasted_iota(i32, (n, n), 1)
    same64 = (row >> 6) == (col >> 6)
    same32 = (row >> 5) == (col >> 5)
    same16 = (row >> 4) == (col >> 4)
    if not reverse:
        cum = (same64 & (col <= row)).astype(bf16)
        m1 = same64 & ((row & 63) >= 32) & ((col & 63) < 32)
        m2 = same32 & ((row & 31) >= 16) & ((col & 31) < 16)
        m3 = same16 & (col <= row)
        offs = (31, 15, 7, 63)
    else:
        cum = (same64 & (col >= row)).astype(bf16)
        m1 = same64 & ((row & 63) < 32) & ((col & 63) >= 32)
        m2 = same32 & ((row & 31) < 16) & ((col & 31) >= 16)
        m3 = same16 & (col >= row)
        offs = (32, 16, 8, 0)
    b = _dot_2x(cum, la)
    r1 = _bcast_rows(b, 64, offs[0])
    r2 = _bcast_rows(b, 32, offs[1])
    r3 = _bcast_rows(b, 16, offs[2])
    bl = _bcast_rows(b, 64, offs[3])
    q1 = q * jnp.exp(jnp.minimum(b - r1, 0.0))
    k1 = (k * jnp.exp(jnp.minimum(r1 - b, 0.0))).astype(bf16)
    q2 = q * jnp.exp(jnp.minimum(b - r2, 0.0))
    k2 = (k * jnp.exp(jnp.minimum(r2 - b, 0.0))).astype(bf16)
    q3 = q * jnp.exp(b - r3)
    k3 = (k * jnp.exp(r3 - b)).astype(bf16)
    qe = (q * jnp.exp(b)).astype(bf16)
    kl = (k * jnp.exp(bl - b)).astype(bf16)
    dec = jnp.exp(bl)
    vb = v.astype(bf16)
    zq = jnp.zeros((n, n), f32)
    intra = jnp.zeros((n, n), f32)
    for h in range(GLA_H):
        hm = (col >> 6) == h
        a1 = _dot_nt(jnp.where(hm, q1, zq).astype(bf16), k1)
        a2 = _dot_nt(jnp.where(hm, q2, zq).astype(bf16), k2)
        a3 = _dot_nt(jnp.where(hm, q3, zq).astype(bf16), k3)
        att = jnp.where(m1, a1, jnp.where(m2, a2, jnp.where(m3, a3, zq)))
        intra = intra + _dot(att.astype(bf16), jnp.where(hm, v, zq).astype(bf16))
    outs = [None] * 4
    for c in (range(4) if not reverse else range(3, -1, -1)):
        rs = slice(64 * c, 64 * c + 64)
        st = st_ref[...]
        inter = _dot_nt(qe[rs], st.astype(bf16))
        kv = _dot_tn(vb[rs], kl[rs])
        drow = dec[64 * c:64 * c + 1, :]
        st_ref[...] = st * drow + jnp.where(same64, kv, zq)
        outs[c] = intra[rs] + inter
    return jnp.concatenate(outs, axis=0)


def _gla_kernel(x_ref, s0_ref, wa_ref, ba_ref, g_ref, bm_ref, o_ref, fin_ref, st_s, o_s, *, t):
    nblk = t // GLA_BLOCK

    def block(j, reverse):
        rows = pl.ds(j * GLA_BLOCK if isinstance(j, int) else pl.multiple_of(j * GLA_BLOCK, GLA_BLOCK), GLA_BLOCK)
        q = x_ref[rows, 0:GLA_W] * GLA_DK ** -0.5
        k = x_ref[rows, GLA_W:2 * GLA_W]
        v = x_ref[rows, 2 * GLA_W:3 * GLA_W]
        ga = x_ref[rows, 4 * GLA_W:4 * GLA_W + LANES]
        d = 1 if reverse else 0
        z = _dot(ga.astype(bf16), wa_ref[:, d * GLA_W:(d + 1) * GLA_W]) + ba_ref[:, d * GLA_W:(d + 1) * GLA_W]
        la = -_softplus(-z) * (1.0 / GLA_TAU)
        o = _gla_block(q, k, v, la, reverse, st_s)
        if reverse:
            o_s[rows, :] = o_s[rows, :] + o
        else:
            o_s[rows, :] = o

    for d in range(2):
        if s0_ref is None:
            st_s[...] = jnp.zeros((GLA_W, GLA_W), f32)
        else:
            st_s[...] = s0_ref[0, d]
        if nblk == 1:
            block(0, d == 1)
        else:
            def body(i, carry, d=d):
                block(i if d == 0 else nblk - 1 - i, d == 1)
                return carry
            lax.fori_loop(0, nblk, body, 0)
        fin_ref[0, d] = st_s[...]
    gg = x_ref[:, 3 * GLA_W:4 * GLA_W]
    o_ref[...] = _head_rms(o_s[...], g_ref[...], bm_ref[...]) * _silu(gg)


def _gla_call(pg, row0, n_seq, t, s0, wa, ba, g, bm):
    sb0 = row0 // t
    assert row0 % t == 0 and t % GLA_BLOCK == 0
    const = lambda b: (0, 0)
    in_specs = [pl.BlockSpec((t, PROJ_G), lambda b: (sb0 + b, 0))]
    args = [pg]
    if s0 is not None:
        in_specs.append(pl.BlockSpec((1, 2, GLA_W, GLA_W), lambda b: (b, 0, 0, 0)))
        args.append(s0)
    in_specs += [pl.BlockSpec((LANES, 2 * GLA_W), const), pl.BlockSpec((1, 2 * GLA_W), const),
                 pl.BlockSpec((1, GLA_W), const), pl.BlockSpec((GLA_W, GLA_W), const)]
    args += [wa, ba, g, bm]

    def kern(*refs):
        refs = list(refs)
        x_ref = refs.pop(0)
        s0_ref = refs.pop(0) if s0 is not None else None
        wa_ref, ba_ref, g_ref, bm_ref = refs[:4]
        _gla_kernel(x_ref, s0_ref, wa_ref, ba_ref, g_ref, bm_ref, *refs[4:], t=t)

    return pl.pallas_call(
        kern,
        grid=(n_seq,),
        in_specs=in_specs,
        out_specs=[pl.BlockSpec((t, GLA_W), lambda b: (b, 0)),
                   pl.BlockSpec((1, 2, GLA_W, GLA_W), lambda b: (b, 0, 0, 0))],
        out_shape=[jax.ShapeDtypeStruct((n_seq * t, GLA_W), f32), jax.ShapeDtypeStruct((n_seq, 2, GLA_W, GLA_W), f32)],
        scratch_shapes=[pltpu.VMEM((GLA_W, GLA_W), f32), pltpu.VMEM((t, GLA_W), f32)],
        compiler_params=_cparams(("parallel",)),
        name="gla_t%d" % t,
    )(*args)


def _out_kernel(x_ref, ac_ref, lc_ref, gc_ref, al_ref, ll_ref, gl_ref, mod_ref, wo_ref, g2_ref, rwh_ref, rwl_ref,
                s1_ref, s3_ref, s2_ref, xb_ref, h2_ref, sc_ref, *, n_ctx_tiles):
    is_ctx = pl.program_id(0) < n_ctx_tiles
    pick = lambda c_ref, l_ref: jnp.where(is_ctx, c_ref[...], l_ref[...]).astype(bf16)
    mod = mod_ref[0]
    g1 = mod[:, 2 * D_MODEL:3 * D_MODEL]
    sh2 = mod[:, 3 * D_MODEL:4 * D_MODEL]
    sc2 = mod[:, 4 * D_MODEL:5 * D_MODEL]
    g2 = mod[:, 5 * D_MODEL:6 * D_MODEL]
    m = (_dot(pick(ac_ref, al_ref), wo_ref[0:ATTN_W, :])
         + _dot(pick(lc_ref, ll_ref), wo_ref[ATTN_W:ATTN_W + LRU_W, :])
         + _dot(pick(gc_ref, gl_ref), wo_ref[ATTN_W + LRU_W:ATTN_W + LRU_W + GLA_W, :]))
    x1 = x_ref[...] + g1 * m
    h2 = _rms(x1, g2_ref[...]) * (1.0 + sc2) + sh2
    for j in range(D_SLABS):
        h2_ref[:, j, :] = h2[:, j * LANES:(j + 1) * LANES]
    sc_ref[...] = _sigmoid(_dot3(h2, rwh_ref[...], rwl_ref[...]))
    hb = h2.astype(bf16)
    act = (_silu(_dot(hb, s1_ref[...])) * _dot(hb, s3_ref[...])).astype(bf16)
    xb_ref[...] = x1 + g2 * _dot(act, s2_ref[...])


def _out_call(x, mix_ctx, mix_lat, mod, wo, g2, rwh, rwl, s1, s3, s2, seq_row):
    n = x.shape[0]
    nct = mix_ctx[0].shape[0] // ROW_TILE
    const = lambda i: (0, 0)
    rowb = lambda w: pl.BlockSpec((ROW_TILE, w), lambda i: (i, 0))
    ctxb = lambda w: pl.BlockSpec((ROW_TILE, w), lambda i: (jnp.minimum(i, nct - 1), 0))
    latb = lambda w: pl.BlockSpec((ROW_TILE, w), lambda i: (jnp.maximum(i - nct, 0), 0))
    return pl.pallas_call(
        functools.partial(_out_kernel, n_ctx_tiles=nct),
        grid=(n // ROW_TILE,),
        in_specs=[rowb(D_MODEL), ctxb(ATTN_W), ctxb(LRU_W), ctxb(GLA_W), latb(ATTN_W), latb(LRU_W), latb(GLA_W),
                  pl.BlockSpec((1, 1, 6 * D_MODEL), lambda i: (seq_row(i), 0, 0)),
                  pl.BlockSpec((D_MODEL, D_MODEL), const), pl.BlockSpec((1, D_MODEL), const),
                  pl.BlockSpec((D_MODEL, N_EXPERTS), const), pl.BlockSpec((D_MODEL, N_EXPERTS), const),
                  pl.BlockSpec((D_MODEL, EXPERT_FF), const), pl.BlockSpec((D_MODEL, EXPERT_FF), const),
                  pl.BlockSpec((EXPERT_FF, D_MODEL), const)],
        out_specs=[rowb(D_MODEL),
                   pl.BlockSpec((ROW_TILE, D_SLABS, LANES), lambda i: (i, 0, 0)),
                   rowb(N_EXPERTS)],
        out_shape=[jax.ShapeDtypeStruct((n, D_MODEL), f32),
                   jax.ShapeDtypeStruct((n, D_SLABS, LANES), f32),
                   jax.ShapeDtypeStruct((n, N_EXPERTS), f32)],
        compiler_params=_cparams(("parallel",)),
        name="out_proj",
    )(x, *mix_ctx, *mix_lat, mod, wo, g2, rwh, rwl, s1, s3, s2)


def _route_kernel(sc_ref, rb_ref, idx_ref, gate_ref, rank_ref, cnt_ref, run_s):
    tm = sc_ref.shape[0]

    @pl.when(pl.program_id(0) == 0)
    def _():
        run_s[...] = jnp.zeros_like(run_s)

    scores = sc_ref[...]
    sel = scores + rb_ref[...]
    lane_f = lax.broadcasted_iota(i32, (tm, N_EXPERTS), 1).astype(f32)
    lane_o = lax.broadcasted_iota(i32, (tm, LANES), 1)
    neg = jnp.full((tm, N_EXPERTS), -jnp.inf, f32)
    hots = []
    idx_o = jnp.zeros((tm, LANES), i32)
    gate_o = jnp.zeros((tm, LANES), f32)
    gsum = jnp.zeros((tm, 1), f32)
    chosen = jnp.zeros((tm, N_EXPERTS), f32)
    for k in range(TOP_K):
        m = jnp.max(sel, axis=-1, keepdims=True)
        idx_f = jnp.min(jnp.where(sel == m, lane_f, float(N_EXPERTS)), axis=-1, keepdims=True)
        hot = lane_f == idx_f
        idx = idx_f.astype(i32)
        gk = jnp.sum(jnp.where(hot, scores, 0.0), axis=-1, keepdims=True)
        sel = jnp.where(hot, neg, sel)
        hots.append(hot)
        chosen = jnp.where(hot, 1.0, chosen)
        gsum = gsum + gk
        idx_o = jnp.where(lane_o == k, idx, idx_o)
        gate_o = jnp.where(lane_o == k, gk, gate_o)
    gate_ref[...] = gate_o / gsum * ROUTED_SCALE
    idx_ref[...] = idx_o
    r = lax.broadcasted_iota(i32, (tm, tm), 0)
    c = lax.broadcasted_iota(i32, (tm, tm), 1)
    pos = _dot((c < r).astype(bf16), chosen.astype(bf16)) + run_s[...]
    rank_o = jnp.zeros((tm, LANES), i32)
    for k in range(TOP_K):
        rk = jnp.sum(jnp.where(hots[k], pos, 0.0), axis=-1, keepdims=True)
        rank_o = jnp.where(lane_o == k, rk.astype(i32), rank_o)
    rank_ref[...] = rank_o
    run_s[...] = run_s[...] + jnp.sum(chosen, axis=0, keepdims=True)
    cnt_ref[...] = run_s[...]


def _route_call(scores, rb):
    n = scores.shape[0]
    rowb = lambda w: pl.BlockSpec((ROW_TILE, w), lambda i: (i, 0))
    return pl.pallas_call(
        _route_kernel,
        grid=(n // ROW_TILE,),
        in_specs=[rowb(N_EXPERTS), pl.BlockSpec((1, N_EXPERTS), lambda i: (0, 0))],
        out_specs=[rowb(LANES), rowb(LANES), rowb(LANES), pl.BlockSpec((1, N_EXPERTS), lambda i: (0, 0))],
        out_shape=[jax.ShapeDtypeStruct((n, LANES), i32), jax.ShapeDtypeStruct((n, LANES), f32),
                   jax.ShapeDtypeStruct((n, LANES), i32), jax.ShapeDtypeStruct((1, N_EXPERTS), f32)],
        scratch_shapes=[pltpu.VMEM((1, N_EXPERTS), f32)],
        compiler_params=_cparams(("arbitrary",)),
        name="route",
    )(scores, rb)


def _dest_kernel(idx_ref, rank_ref, cnt_ref, dest_ref):
    tm = idx_ref.shape[0]
    cnt = jnp.broadcast_to(cnt_ref[...], (SUBLANES, N_EXPERTS)).astype(i32)
    padded = ((cnt + (SLOT_BLOCK - 1)) // SLOT_BLOCK) * SLOT_BLOCK
    lane8 = lax.broadcasted_iota(i32, (SUBLANES, N_EXPERTS), 1)
    inc = padded
    s = 1
    while s < N_EXPERTS:
        inc = inc + jnp.where(lane8 >= s, pltpu.roll(inc, s, 1), 0)
        s *= 2
    start = (inc - padded)[0:1, :].astype(f32)
    lane = lax.broadcasted_iota(i32, (tm, N_EXPERTS), 1)
    lane_o = lax.broadcasted_iota(i32, (tm, LANES), 1)
    idx = idx_ref[...]
    rank = rank_ref[...]
    dest = jnp.zeros((tm, LANES), i32)
    for k in range(TOP_K):
        hot = lane == idx[:, k:k + 1]
        sk = jnp.sum(jnp.where(hot, start, 0.0), axis=-1, keepdims=True).astype(i32)
        dest = jnp.where(lane_o == k, sk + rank[:, k:k + 1], dest)
    dest_ref[...] = dest


def _dest_call(idx, rank, counts):
    n = idx.shape[0]
    rowb = lambda w: pl.BlockSpec((ROW_TILE, w), lambda i: (i, 0))
    return pl.pallas_call(
        _dest_kernel,
        grid=(n // ROW_TILE,),
        in_specs=[rowb(LANES), rowb(LANES), pl.BlockSpec((1, N_EXPERTS), lambda i: (0, 0))],
        out_specs=rowb(LANES),
        out_shape=jax.ShapeDtypeStruct((n, LANES), i32),
        compiler_params=_cparams(("parallel",)),
        name="dest",
    )(idx, rank, counts)


def _dispatch_kernel(dest_ref, h_ref, slots_in, slots_ref, sem):
    del slots_in
    tm = h_ref.shape[0]

    def issue(t, carry):
        for k in range(TOP_K):
            pltpu.make_async_copy(h_ref.at[t], slots_ref.at[dest_ref[t * TOP_K + k]], sem).start()
        return carry

    lax.fori_loop(0, tm, issue, 0)
    for _ in range(TOP_K):
        pltpu.make_async_copy(h_ref, slots_ref.at[pl.ds(0, tm)], sem).wait()


def _dispatch_call(dest_flat, h2, slots):
    n = h2.shape[0]
    return pl.pallas_call(
        _dispatch_kernel,
        grid=(n // ROW_TILE,),
        in_specs=[pl.BlockSpec((ROW_TILE * TOP_K,), lambda i: (i,), memory_space=pltpu.SMEM),
                  pl.BlockSpec((ROW_TILE, D_SLABS, LANES), lambda i: (i, 0, 0)),
                  pl.BlockSpec(memory_space=pl.ANY)],
        out_specs=pl.BlockSpec(memory_space=pl.ANY),
        out_shape=jax.ShapeDtypeStruct(slots.shape, f32),
        scratch_shapes=[pltpu.SemaphoreType.DMA(())],
        input_output_aliases={2: 0},
        compiler_params=_cparams(("arbitrary",)),
        name="dispatch",
    )(dest_flat, h2, slots)


def _expert_kernel(be_ref, nu_ref, x_ref, w1_ref, w3_ref, w2_ref, y_ref, w1_s, w3_s, w2_s):
    i = pl.program_id(0)

    @pl.when(i < nu_ref[0])
    def _():
        prev = be_ref[jnp.maximum(i - 1, 0)]

        @pl.when((i == 0) | (be_ref[i] != prev))
        def _():
            w1_s[...] = w1_ref[0, 0].astype(bf16)
            w3_s[...] = w3_ref[0, 0].astype(bf16)
            w2_s[...] = w2_ref[0, 0].astype(bf16)

        x = jnp.concatenate([x_ref[:, j, :] for j in range(D_SLABS)], axis=1).astype(bf16)
        act = (_silu(_dot(x, w1_s[...])) * _dot(x, w3_s[...])).astype(bf16)
        y = _dot(act, w2_s[...])
        for j in range(D_SLABS):
            y_ref[:, j, :] = y[:, j * LANES:(j + 1) * LANES]


def _expert_call(blk_e, n_used, slots, w1, w3, w2, layer):
    nb = slots.shape[0] // SLOT_BLOCK
    blk = lambda i, be, nu: (jnp.minimum(i, nu[0] - 1), 0, 0)
    wsel = lambda i, be, nu: (layer, be[jnp.minimum(i, nu[0] - 1)], 0, 0)
    grid_spec = pltpu.PrefetchScalarGridSpec(
        num_scalar_prefetch=2,
        grid=(nb,),
        in_specs=[pl.BlockSpec((SLOT_BLOCK, D_SLABS, LANES), blk),
                  pl.BlockSpec((1, 1, D_MODEL, EXPERT_FF), wsel),
                  pl.BlockSpec((1, 1, D_MODEL, EXPERT_FF), wsel),
                  pl.BlockSpec((1, 1, EXPERT_FF, D_MODEL), wsel)],
        out_specs=pl.BlockSpec((SLOT_BLOCK, D_SLABS, LANES), blk),
        scratch_shapes=[pltpu.VMEM((D_MODEL, EXPERT_FF), bf16), pltpu.VMEM((D_MODEL, EXPERT_FF), bf16),
                        pltpu.VMEM((EXPERT_FF, D_MODEL), bf16)],
    )
    return pl.pallas_call(
        _expert_kernel,
        grid_spec=grid_spec,
        out_shape=jax.ShapeDtypeStruct(slots.shape, f32),
        input_output_aliases={2: 0},
        compiler_params=_cparams(("arbitrary",)),
        name="experts",
    )(blk_e, n_used, slots, w1, w3, w2)


def _gather_rows_call(table, idx):
    n_idx = idx.shape[0]
    info = plsc.get_sparse_core_info()
    n_workers = info.num_cores * info.num_subcores
    per_worker = n_idx // n_workers
    assert n_idx % (n_workers * SC_CHUNK) == 0
    mesh = plsc.VectorSubcoreMesh(core_axis_name="c", subcore_axis_name="s")

    def body(table_hbm, idx_hbm, out_hbm, idx_v, rows_v, sem):
        wid = lax.axis_index("s") * info.num_cores + lax.axis_index("c")
        base = wid * per_worker

        @pl.loop(0, per_worker // SC_CHUNK)
        def _(i):
            off = pl.multiple_of(base + i * SC_CHUNK, SC_CHUNK)
            pltpu.sync_copy(idx_hbm.at[pl.ds(off, SC_CHUNK)], idx_v)
            pltpu.async_copy(table_hbm.at[idx_v], rows_v, sem).wait()
            pltpu.sync_copy(rows_v, out_hbm.at[pl.ds(off, SC_CHUNK)])

    return pl.kernel(
        body,
        out_type=jax.ShapeDtypeStruct((n_idx, D_SLABS, LANES), f32),
        mesh=mesh,
        scratch_types=[pltpu.VMEM((SC_CHUNK,), i32), pltpu.VMEM((SC_CHUNK, D_SLABS, LANES), f32),
                       pltpu.SemaphoreType.DMA],
        name="gather_rows",
    )(table, idx)


def _combine_kernel(gate_ref, xb_ref, mod_ref, y_ref, o_ref):
    tm = xb_ref.shape[0]
    g2 = mod_ref[0][:, 5 * D_MODEL:6 * D_MODEL]
    gates = gate_ref[...]
    for j in range(D_SLABS):
        acc = jnp.zeros((tm, LANES), f32)
        for k in range(TOP_K):
            acc = acc + gates[:, k:k + 1] * y_ref[k, :, j, :]
        cols = slice(j * LANES, (j + 1) * LANES)
        o_ref[:, cols] = xb_ref[:, cols] + g2[:, cols] * acc


def _combine_call(gates, xbase, mod, ygath, seq_row_c):
    n = xbase.shape[0]
    tm = COMBINE_TILE
    return pl.pallas_call(
        _combine_kernel,
        grid=(n // tm,),
        in_specs=[pl.BlockSpec((tm, LANES), lambda i: (i, 0)),
                  pl.BlockSpec((tm, D_MODEL), lambda i: (i, 0)),
                  pl.BlockSpec((1, 1, 6 * D_MODEL), lambda i: (seq_row_c(i), 0, 0)),
                  pl.BlockSpec((TOP_K, tm, D_SLABS, LANES), lambda i: (0, i, 0, 0))],
        out_specs=pl.BlockSpec((tm, D_MODEL), lambda i: (i, 0)),
        out_shape=jax.ShapeDtypeStruct((n, D_MODEL), f32),
        compiler_params=_cparams(("parallel",)),
        name="combine",
    )(gates, xbase, mod, ygath)


def _block_avg(width, group):
    r = np.arange(width)
    return jnp.asarray((r[:, None] // group == r[None, :] // group).astype(np.float32) / group, dtype=bf16)


def _kv_replicate():
    c = np.arange(ATTN_W)
    src = (c // (ATTN_W // N_KV_HEADS)) * HEAD_DIM + c % HEAD_DIM
    return jnp.asarray((np.arange(KV_W)[:, None] == src[None, :]).astype(np.float32), dtype=bf16)


def _rope_lane_tables(n_tok):
    rows = n_tok // GRID_W
    r = jnp.repeat(jnp.arange(rows, dtype=f32), GRID_W)
    col = jnp.tile(jnp.arange(GRID_W, dtype=f32), rows)
    inv = ROPE_THETA ** (-jnp.arange(ROPE_FREQ, dtype=f32) / ROPE_FREQ)
    ar = r[:, None] * inv
    ac = col[:, None] * inv
    cos_h = jnp.concatenate([jnp.cos(ar), jnp.cos(ar), jnp.cos(ac), jnp.cos(ac)], axis=-1)
    sin_h = jnp.concatenate([-jnp.sin(ar), jnp.sin(ar), -jnp.sin(ac), jnp.sin(ac)], axis=-1)
    return jnp.tile(cos_h, (1, N_HEADS)), jnp.tile(sin_h, (1, N_HEADS))


def _block_diag(w):
    nb, bw, _ = w.shape
    eye = jnp.eye(nb, dtype=w.dtype)
    return (w[:, :, None, :] * eye[:, None, :, None]).reshape(nb * bw, nb * bw)


def _gla_state_in(s):
    bsz = s.shape[0]
    eye = jnp.eye(GLA_H, dtype=s.dtype)
    st = jnp.swapaxes(s, -1, -2)
    big = st[:, :, :, :, None, :] * eye[None, None, :, None, :, None]
    return big.reshape(bsz, 2, GLA_W, GLA_W)


def _gla_state_out(st):
    bsz = st.shape[0]
    s6 = st.reshape(bsz, 2, GLA_H, GLA_DK, GLA_H, GLA_DK)
    diag = jnp.stack([s6[:, :, h, :, h, :] for h in range(GLA_H)], axis=2)
    return jnp.swapaxes(diag, -1, -2)


def kernel(x_prompt, x_sample, cache_k, cache_v, state_lru, state_gla, c, c_ctx, ada_w, ada_b, norm1_g, norm2_g, w_in, q_norm_g, k_norm_g, attn_out_g, conv_w, conv_b, lru_wa, lru_ba, lru_wi, lru_bi, lru_lambda, lru_out_g, gla_wa2, gla_ba, gla_out_g, w_out, router_w, router_b, exp_w1, exp_w3, exp_w2, sh_w1, sh_w3, sh_w2):
    bc, tc, _ = x_prompt.shape
    bl, tl, _ = x_sample.shape
    depth = w_in.shape[0]
    nc = bc * tc
    n = nc + bl * tl
    past = cache_k.shape[2]
    assert tc == ROW_TILE and tl % ROW_TILE == 0 and nc % tl == 0 and bl + 1 <= SUBLANES
    assert n % ROW_TILE == 0 and n % COMBINE_TILE == 0

    def seq_row_for(tile):
        def seq_row(i):
            return jnp.where(i < nc // tile, 0, 1 + (i - nc // tile) // (tl // tile))
        return seq_row

    seq_row = seq_row_for(ROW_TILE)
    seq_row_c = seq_row_for(COMBINE_TILE)

    x = jnp.concatenate([x_prompt.reshape(nc, D_MODEL), x_sample.reshape(bl * tl, D_MODEL)], axis=0)
    cond = jnp.zeros((SUBLANES, D_MODEL), f32).at[0].set(c_ctx).at[1:1 + bl].set(c)
    mods = _ada_call(cond, ada_w, ada_b)

    bmq = _block_avg(ATTN_W, HEAD_DIM)
    bmk = _block_avg(KV_W, HEAD_DIM)
    bmg = _block_avg(GLA_W, GLA_DK)
    rep = _kv_replicate()
    cos_t, sin_t = _rope_lane_tables(tl)
    n_slots = -(-(n * TOP_K + N_EXPERTS * (SLOT_BLOCK - 1)) // SLOT_BLOCK) * SLOT_BLOCK
    slots = jnp.zeros((n_slots, D_SLABS, LANES), f32)
    tile8 = lambda v: jnp.tile(v, N_HEADS)[None, :]

    ks, vs, lrus, glas = [], [], [], []
    for l in range(depth):
        mod = mods[l].reshape(SUBLANES, 1, 6 * D_MODEL)
        w_in_p = jnp.pad(w_in[l].astype(bf16), ((0, 0), (0, LANES - 2 * GLA_RANK)))
        pa, pl_, pg = _proj_call(x, mod, norm1_g[l][None, :], w_in_p, seq_row)

        qg, kg, og = tile8(q_norm_g[l]), jnp.tile(k_norm_g[l], N_KV_HEADS)[None, :], attn_out_g[l][None, :]
        attn_c, k_new, v_new = _attn_ctx_call(pa, bc, tc, qg, kg, og, bmq, bmk, rep)
        attn_l = _attn_lat_call(pa, nc, bl, tl, cache_k[:, l].reshape(bl, past, KV_W),
                                cache_v[:, l].reshape(bl, past, KV_W), cos_t, sin_t, qg, kg, og, bmq, bmk, rep)
        ks.append(k_new.reshape(bc, tc, N_KV_HEADS, HEAD_DIM))
        vs.append(v_new.reshape(bc, tc, N_KV_HEADS, HEAD_DIM))

        wg = jnp.concatenate([_block_diag(lru_wa[l, 0]), _block_diag(lru_wi[l, 0]),
                              _block_diag(lru_wa[l, 1]), _block_diag(lru_wi[l, 1])], axis=1).astype(bf16)
        bg = jnp.concatenate([lru_ba[l, 0], lru_bi[l, 0], lru_ba[l, 1], lru_bi[l, 1]])[None, :]
        lru_args = (conv_w[l], conv_b[l][None, :], wg, bg, lru_lambda[l], lru_out_g[l][None, :])
        lru_c, lru_fin = _lru_call(pl_, 0, bc, tc, jnp.zeros((bc, 2, LRU_W), f32), *lru_args)
        lru_l, _ = _lru_call(pl_, nc, bl, tl, state_lru[:, l], *lru_args)
        lrus.append(lru_fin)

        wa = jnp.zeros((LANES, 2 * GLA_W), f32)
        wa = wa.at[0:GLA_RANK, 0:GLA_W].set(gla_wa2[l, 0]).at[GLA_RANK:2 * GLA_RANK, GLA_W:].set(gla_wa2[l, 1])
        gla_args = (wa.astype(bf16), gla_ba[l].reshape(1, 2 * GLA_W), gla_out_g[l].reshape(1, GLA_W), bmg)
        gla_c, gla_fin = _gla_call(pg, 0, bc, tc, None, *gla_args)
        gla_l, _ = _gla_call(pg, nc, bl, tl, _gla_state_in(state_gla[:, l]), *gla_args)
        glas.append(_gla_state_out(gla_fin))

        rw_hi = router_w[l].astype(bf16)
        rw_lo = (router_w[l] - rw_hi.astype(f32)).astype(bf16)
        xbase, h2, scores = _out_call(x, (attn_c, lru_c, gla_c), (attn_l, lru_l, gla_l), mod, w_out[l].astype(bf16),
                                      norm2_g[l][None, :], rw_hi, rw_lo, sh_w1[l].astype(bf16),
                                      sh_w3[l].astype(bf16), sh_w2[l].astype(bf16), seq_row)

        idx, gates, rank, counts = _route_call(scores, router_b[l][None, :])
        dest = _dest_call(idx, rank, counts)[:, :TOP_K]
        padded = (counts[0].astype(i32) + SLOT_BLOCK - 1) // SLOT_BLOCK * SLOT_BLOCK
        padded_end = jnp.cumsum(padded)
        nb = n_slots // SLOT_BLOCK
        blk_first = jnp.arange(nb, dtype=i32) * SLOT_BLOCK
        blk_e = jnp.minimum(jnp.sum((padded_end[None, :] <= blk_first[:, None]).astype(i32), axis=1), N_EXPERTS - 1)
        n_used = padded_end[-1:] // SLOT_BLOCK
        slots = _dispatch_call(dest.reshape(-1), h2, slots)
        slots = _expert_call(blk_e, n_used, slots, exp_w1, exp_w3, exp_w2, l)
        ygath = _gather_rows_call(slots, dest.T.reshape(-1)).reshape(TOP_K, n, D_SLABS, LANES)
        x = _combine_call(gates, xbase, mod, ygath, seq_row_c)

    y_prompt = x[:nc].reshape(bc, tc, D_MODEL)
    y_sample = x[nc:].reshape(bl, tl, D_MODEL)
    return (y_prompt, y_sample, jnp.stack(ks, axis=1), jnp.stack(vs, axis=1),
            jnp.stack(lrus, axis=1), jnp.stack(glas, axis=1))
```

```python
import functools

import jax
import jax.numpy as jnp
import numpy as np
from jax import lax
from jax.experimental import pallas as pl
from jax.experimental.pallas import tpu as pltpu
from jax.experimental.pallas import tpu_sc as plsc

f32 = jnp.float32
bf16 = jnp.bfloat16
i32 = jnp.int32

D_MODEL = 1024
N_HEADS = 8
N_KV_HEADS = 2
HEAD_DIM = 64
ATTN_W = N_HEADS * HEAD_DIM
KV_W = N_KV_HEADS * HEAD_DIM
GRID_W = 64
ROPE_FREQ = HEAD_DIM // 4
ROPE_THETA = 10000.0
LRU_W = 256
LRU_BLOCKS = 4
LRU_C = 8.0
CONV_W = 4
CONV_LEFT = 2
GLA_H = 4
GLA_DK = 64
GLA_W = 256
GLA_RANK = 16
GLA_TAU = 16.0
N_EXPERTS = 256
TOP_K = 8
EXPERT_FF = 256
ROUTED_SCALE = 2.5
EPS = 1e-6

LANES = 128
SUBLANES = 8
ROW_TILE = 256
GLA_BLOCK = 256
SLOT_BLOCK = 256
COMBINE_TILE = 128
SC_CHUNK = 32
D_SLABS = D_MODEL // LANES
VMEM_LIMIT = 56 * 1024 * 1024


def _cparams(sem, vmem=VMEM_LIMIT):
    return pltpu.CompilerParams(dimension_semantics=sem, vmem_limit_bytes=vmem)


def _dot(a, b):
    return jnp.dot(a, b, preferred_element_type=f32)


def _dot_nt(a, b):
    return lax.dot_general(a, b, (((1,), (1,)), ((), ())), preferred_element_type=f32)


def _dot_tn(a, b):
    return lax.dot_general(a, b, (((0,), (0,)), ((), ())), preferred_element_type=f32)


def _split(x):
    hi = x.astype(bf16)
    lo = (x - hi.astype(f32)).astype(bf16)
    return hi, lo


def _dot_x2(x, w):
    hi, lo = _split(x)
    return _dot(hi, w) + _dot(lo, w)


def _dot_2x(m, x):
    hi, lo = _split(x)
    return _dot(m, hi) + _dot(m, lo)


def _dot3(a, b_hi, b_lo):
    a_hi, a_lo = _split(a)
    return _dot(a_hi, b_hi) + _dot(a_lo, b_hi) + _dot(a_hi, b_lo)


def _sigmoid(x):
    return 1.0 / (1.0 + jnp.exp(-x))


def _silu(x):
    return x * _sigmoid(x)


def _softplus(x):
    return jnp.maximum(x, 0.0) + jnp.log(1.0 + jnp.exp(-jnp.abs(x)))


def _gelu_tanh(x):
    return 0.5 * x * (1.0 + jnp.tanh(0.7978845608028654 * (x + 0.044715 * x * x * x)))


def _rms(x, g):
    return x * lax.rsqrt(jnp.mean(x * x, axis=-1, keepdims=True) + EPS) * g


def _head_rms(x, g, bm):
    ms = _dot_x2(x * x, bm)
    return x * lax.rsqrt(ms + EPS) * g


def _ada_kernel(c_ref, w_ref, b_ref, o_ref):
    s = _silu(c_ref[...])
    w = w_ref[0]
    w_hi, w_lo = _split(w)
    o_ref[0] = _dot3(s, w_hi, w_lo) + b_ref[0]


def _ada_call(cond, ada_w, ada_b):
    depth = ada_w.shape[0]
    nt = 1536
    return pl.pallas_call(
        _ada_kernel,
        grid=(depth, 6 * D_MODEL // nt),
        in_specs=[pl.BlockSpec((SUBLANES, D_MODEL), lambda l, j: (0, 0)),
                  pl.BlockSpec((1, D_MODEL, nt), lambda l, j: (l, 0, j)),
                  pl.BlockSpec((1, 1, nt), lambda l, j: (l, 0, j))],
        out_specs=pl.BlockSpec((1, SUBLANES, nt), lambda l, j: (l, 0, j)),
        out_shape=jax.ShapeDtypeStruct((depth, SUBLANES, 6 * D_MODEL), f32),
        compiler_params=_cparams(("parallel", "parallel")),
        name="ada_mod",
    )(cond, ada_w, ada_b.reshape(depth, 1, 6 * D_MODEL))


PROJ_A = ATTN_W + 2 * KV_W
PROJ_L = 2 * LRU_W
PROJ_G = 4 * GLA_W + LANES


def _proj_kernel(x_ref, mod_ref, g_ref, w_ref, oa_ref, ol_ref, og_ref):
    mod = mod_ref[0]
    sh = mod[:, 0:D_MODEL]
    sc = mod[:, D_MODEL:2 * D_MODEL]
    h = (_rms(x_ref[...], g_ref[...]) * (1.0 + sc) + sh).astype(bf16)
    p = _dot(h, w_ref[...])
    oa_ref[...] = p[:, 0:PROJ_A]
    ol_ref[...] = p[:, PROJ_A:PROJ_A + PROJ_L]
    og_ref[...] = p[:, PROJ_A + PROJ_L:PROJ_A + PROJ_L + PROJ_G]


def _proj_call(x, mod, g, w, seq_row):
    n = x.shape[0]
    cols = PROJ_A + PROJ_L + PROJ_G
    return pl.pallas_call(
        _proj_kernel,
        grid=(n // ROW_TILE,),
        in_specs=[pl.BlockSpec((ROW_TILE, D_MODEL), lambda i: (i, 0)),
                  pl.BlockSpec((1, 1, 6 * D_MODEL), lambda i: (seq_row(i), 0, 0)),
                  pl.BlockSpec((1, D_MODEL), lambda i: (0, 0)),
                  pl.BlockSpec((D_MODEL, cols), lambda i: (0, 0))],
        out_specs=[pl.BlockSpec((ROW_TILE, PROJ_A), lambda i: (i, 0)),
                   pl.BlockSpec((ROW_TILE, PROJ_L), lambda i: (i, 0)),
                   pl.BlockSpec((ROW_TILE, PROJ_G), lambda i: (i, 0))],
        out_shape=[jax.ShapeDtypeStruct((n, PROJ_A), f32),
                   jax.ShapeDtypeStruct((n, PROJ_L), f32),
                   jax.ShapeDtypeStruct((n, PROJ_G), f32)],
        compiler_params=_cparams(("parallel",)),
        name="in_proj",
    )(x, mod, g, w)


def _rope(x, cos_t, sin_t):
    w = x.shape[1]
    up = pltpu.roll(x, w - ROPE_FREQ, 1)
    dn = pltpu.roll(x, ROPE_FREQ, 1)
    lane = lax.broadcasted_iota(i32, x.shape, 1)
    partner = jnp.where((lane & (2 * ROPE_FREQ - 1)) < ROPE_FREQ, up, dn)
    return x * cos_t + partner * sin_t


def _attend(q, kk_ref, vv_ref, o_ref):
    tq = q.shape[0]
    gw = ATTN_W // N_KV_HEADS
    lane = lax.broadcasted_iota(i32, (tq, gw), 1)
    for g in range(N_KV_HEADS):
        qg = q[:, g * gw:(g + 1) * gw]
        kg = kk_ref[:, g * gw:(g + 1) * gw]
        vg = vv_ref[:, g * gw:(g + 1) * gw]
        acc = jnp.zeros((tq, gw), f32)
        for hh in range(N_HEADS // N_KV_HEADS):
            hm = (lane >> 6) == hh
            s = _dot_nt(jnp.where(hm, qg, 0.0).astype(bf16), kg)
            m = jnp.max(s, axis=-1, keepdims=True)
            p = jnp.exp(s - m)
            l = jnp.sum(p, axis=-1, keepdims=True)
            o = _dot(p.astype(bf16), vg) / l
            acc = jnp.where(hm, o, acc)
        o_ref[:, g * gw:(g + 1) * gw] = acc


def _attn_ctx_kernel(p_ref, qg_ref, kg_ref, og_ref, bmq_ref, bmk_ref, rep_ref,
                     o_ref, ko_ref, vo_ref, kk_s, vv_s, o_s):
    p = p_ref[...]
    q = _head_rms(p[:, 0:ATTN_W], qg_ref[...], bmq_ref[...])
    k = _head_rms(p[:, ATTN_W:ATTN_W + KV_W], kg_ref[...], bmk_ref[...])
    v = p[:, ATTN_W + KV_W:PROJ_A]
    ko_ref[...] = k
    vo_ref[...] = v
    kk_s[...] = _dot(k.astype(bf16), rep_ref[...]).astype(bf16)
    vv_s[...] = _dot(v.astype(bf16), rep_ref[...]).astype(bf16)
    _attend(q * HEAD_DIM ** -0.5, kk_s, vv_s, o_s)
    o_ref[...] = _rms(o_s[...], og_ref[...])


def _attn_ctx_call(pa, n_seq, t, qg, kg, og, bmq, bmk, rep):
    assert t == ROW_TILE
    const = lambda i: (0, 0)
    return pl.pallas_call(
        _attn_ctx_kernel,
        grid=(n_seq,),
        in_specs=[pl.BlockSpec((t, PROJ_A), lambda i: (i, 0)),
                  pl.BlockSpec((1, ATTN_W), const), pl.BlockSpec((1, KV_W), const), pl.BlockSpec((1, ATTN_W), const),
                  pl.BlockSpec((ATTN_W, ATTN_W), const), pl.BlockSpec((KV_W, KV_W), const),
                  pl.BlockSpec((KV_W, ATTN_W), const)],
        out_specs=[pl.BlockSpec((t, ATTN_W), lambda i: (i, 0)),
                   pl.BlockSpec((t, KV_W), lambda i: (i, 0)),
                   pl.BlockSpec((t, KV_W), lambda i: (i, 0))],
        out_shape=[jax.ShapeDtypeStruct((n_seq * t, ATTN_W), f32),
                   jax.ShapeDtypeStruct((n_seq * t, KV_W), f32),
                   jax.ShapeDtypeStruct((n_seq * t, KV_W), f32)],
        scratch_shapes=[pltpu.VMEM((t, ATTN_W), bf16), pltpu.VMEM((t, ATTN_W), bf16), pltpu.VMEM((t, ATTN_W), f32)],
        compiler_params=_cparams(("parallel",)),
        name="attn_ctx",
    )(pa, qg, kg, og, bmq, bmk, rep)


def _attn_lat_kernel(q_ref, kv_ref, ck_ref, cv_ref, cq_ref, sq_ref, ckk_ref, skk_ref,
                     qg_ref, kg_ref, og_ref, bmq_ref, bmk_ref, rep_ref,
                     o_ref, kk_s, vv_s, o_s, *, past):
    @pl.when(pl.program_id(1) == 0)
    def _():
        kv = kv_ref[...]
        k = _head_rms(kv[:, 0:KV_W], kg_ref[...], bmk_ref[...])
        k = _rope(k, ckk_ref[...], skk_ref[...])
        v = kv[:, KV_W:2 * KV_W]
        kk_s[0:past, :] = _dot(ck_ref[0].astype(bf16), rep_ref[...]).astype(bf16)
        vv_s[0:past, :] = _dot(cv_ref[0].astype(bf16), rep_ref[...]).astype(bf16)
        kk_s[past:, :] = _dot(k.astype(bf16), rep_ref[...]).astype(bf16)
        vv_s[past:, :] = _dot(v.astype(bf16), rep_ref[...]).astype(bf16)

    q = _head_rms(q_ref[...], qg_ref[...], bmq_ref[...])
    q = _rope(q, cq_ref[...], sq_ref[...])
    _attend(q * HEAD_DIM ** -0.5, kk_s, vv_s, o_s)
    o_ref[...] = _rms(o_s[...], og_ref[...])


def _attn_lat_call(pa, row0, n_seq, t, cache_k, cache_v, cq, sq, qg, kg, og, bmq, bmk, rep):
    past = cache_k.shape[1]
    nq = t // ROW_TILE
    assert row0 % t == 0 and t % ROW_TILE == 0
    qb0 = row0 // ROW_TILE
    sb0 = row0 // t
    const = lambda b, j: (0, 0)
    return pl.pallas_call(
        functools.partial(_attn_lat_kernel, past=past),
        grid=(n_seq, nq),
        in_specs=[pl.BlockSpec((ROW_TILE, ATTN_W), lambda b, j: (qb0 + b * nq + j, 0)),
                  pl.BlockSpec((t, 2 * KV_W), lambda b, j: (sb0 + b, ATTN_W // (2 * KV_W))),
                  pl.BlockSpec((1, past, KV_W), lambda b, j: (b, 0, 0)),
                  pl.BlockSpec((1, past, KV_W), lambda b, j: (b, 0, 0)),
                  pl.BlockSpec((ROW_TILE, ATTN_W), lambda b, j: (j, 0)),
                  pl.BlockSpec((ROW_TILE, ATTN_W), lambda b, j: (j, 0)),
                  pl.BlockSpec((t, KV_W), const), pl.BlockSpec((t, KV_W), const),
                  pl.BlockSpec((1, ATTN_W), const), pl.BlockSpec((1, KV_W), const), pl.BlockSpec((1, ATTN_W), const),
                  pl.BlockSpec((ATTN_W, ATTN_W), const), pl.BlockSpec((KV_W, KV_W), const),
                  pl.BlockSpec((KV_W, ATTN_W), const)],
        out_specs=pl.BlockSpec((ROW_TILE, ATTN_W), lambda b, j: (b * nq + j, 0)),
        out_shape=jax.ShapeDtypeStruct((n_seq * t, ATTN_W), f32),
        scratch_shapes=[pltpu.VMEM((past + t, ATTN_W), bf16), pltpu.VMEM((past + t, ATTN_W), bf16),
                        pltpu.VMEM((ROW_TILE, ATTN_W), f32)],
        compiler_params=_cparams(("parallel", "arbitrary")),
        name="attn_lat",
    )(pa, pa, cache_k, cache_v, cq, sq, cq, sq, qg, kg, og, bmq, bmk, rep)


def _lru_kernel(x_ref, h0_ref, cw_ref, cb_ref, wg_ref, bg_ref, lam_ref, g_ref,
                o_ref, fin_ref, xs, a_s, u_s, h_s, *, t):
    ng = t // SUBLANES
    lx = x_ref[:, 0:LRU_W]
    ly = x_ref[:, LRU_W:2 * LRU_W]
    xs[0:SUBLANES, :] = jnp.zeros((SUBLANES, LRU_W), f32)
    xs[SUBLANES + t:2 * SUBLANES + t, :] = jnp.zeros((SUBLANES, LRU_W), f32)
    xs[SUBLANES:SUBLANES + t, :] = lx
    xc = jnp.broadcast_to(cb_ref[...], (t, LRU_W))
    for j in range(CONV_W):
        off = SUBLANES + j - CONV_LEFT
        xc = xc + xs[off:off + t, :] * cw_ref[j:j + 1, :]
    gates = _sigmoid(_dot(xc.astype(bf16), wg_ref[...]) + bg_ref[...])
    sp = _softplus(-lam_ref[...])
    sub = lax.broadcasted_iota(i32, (t, LRU_W), 0) & (SUBLANES - 1)
    for d in range(2):
        r = gates[:, (2 * d) * LRU_W:(2 * d + 1) * LRU_W]
        gi = gates[:, (2 * d + 1) * LRU_W:(2 * d + 2) * LRU_W]
        log_a = -LRU_C * r * sp[d:d + 1, :]
        a = jnp.exp(log_a)
        th = jnp.tanh(log_a)
        u = jnp.sqrt(-2.0 * th / (1.0 - th)) * (gi * xc)
        for s in (1, 2, 4):
            if d == 0:
                a_sh = pltpu.roll(a, s, 0)
                u_sh = pltpu.roll(u, s, 0)
                ok = sub >= s
            else:
                a_sh = pltpu.roll(a, t - s, 0)
                u_sh = pltpu.roll(u, t - s, 0)
                ok = sub < SUBLANES - s
            u = jnp.where(ok, a * u_sh + u, u)
            a = jnp.where(ok, a * a_sh, a)
        a_s[...] = a
        u_s[...] = u
        h0 = jnp.broadcast_to(h0_ref[0, d:d + 1, :], (SUBLANES, LRU_W))
        edge = SUBLANES - 1 if d == 0 else 0

        def body(i, carry, d=d, edge=edge):
            g = i if d == 0 else ng - 1 - i
            rows = pl.ds(pl.multiple_of(g * SUBLANES, SUBLANES), SUBLANES)
            h = a_s[rows, :] * carry + u_s[rows, :]
            if d == 0:
                h_s[rows, :] = h
            else:
                h_s[rows, :] = h_s[rows, :] + h
            return jnp.broadcast_to(h[edge:edge + 1, :], (SUBLANES, LRU_W))

        last = lax.fori_loop(0, ng, body, h0)
        fin_ref[0, d:d + 1, :] = last[0:1, :]
    o_ref[...] = _rms(h_s[...] * _gelu_tanh(ly), g_ref[...])


def _lru_call(pl_, row0, n_seq, t, h0, cw, cb, wg, bg, lam, g):
    sb0 = row0 // t
    assert row0 % t == 0
    const = lambda b: (0, 0)
    in_specs = [pl.BlockSpec((t, PROJ_L), lambda b: (sb0 + b, 0)),
                pl.BlockSpec((1, 2, LRU_W), lambda b: (b, 0, 0)),
                pl.BlockSpec((CONV_W, LRU_W), const), pl.BlockSpec((1, LRU_W), const),
                pl.BlockSpec((LRU_W, 4 * LRU_W), const), pl.BlockSpec((1, 4 * LRU_W), const),
                pl.BlockSpec((2, LRU_W), const), pl.BlockSpec((1, LRU_W), const)]
    args = [pl_, h0, cw, cb, wg, bg, lam, g]
    return pl.pallas_call(
        functools.partial(_lru_kernel, t=t),
        grid=(n_seq,),
        in_specs=in_specs,
        out_specs=[pl.BlockSpec((t, LRU_W), lambda b: (b, 0)),
                   pl.BlockSpec((1, 2, LRU_W), lambda b: (b, 0, 0))],
        out_shape=[jax.ShapeDtypeStruct((n_seq * t, LRU_W), f32), jax.ShapeDtypeStruct((n_seq, 2, LRU_W), f32)],
        scratch_shapes=[pltpu.VMEM((t + 2 * SUBLANES, LRU_W), f32), pltpu.VMEM((t, LRU_W), f32),
                        pltpu.VMEM((t, LRU_W), f32), pltpu.VMEM((t, LRU_W), f32)],
        compiler_params=_cparams(("parallel",)),
        name="rglru_t%d" % t,
    )(*args)


def _bcast_rows(b, period, off):
    w = b.shape[1]
    return jnp.concatenate(
        [jnp.broadcast_to(b[i * period + off:i * period + off + 1, :], (period, w)) for i in range(b.shape[0] // period)],
        axis=0)


def _gla_block(q, k, v, la, reverse, st_ref):
    n = GLA_BLOCK
    row = lax.broadcasted_iota(i32, (n, n), 0)
    col = lax.broadcasted_iota(i32, (n, n), 1)
    same64 = (row >> 6) == (col >> 6)
    same32 = (row >> 5) == (col >> 5)
    same16 = (row >> 4) == (col >> 4)
    if not reverse:
        cum = (same64 & (col <= row)).astype(bf16)
        m1 = same64 & ((row & 63) >= 32) & ((col & 63) < 32)
        m2 = same32 & ((row & 31) >= 16) & ((col & 31) < 16)
        m3 = same16 & (col <= row)
        offs = (31, 15, 7, 63)
    else:
        cum = (same64 & (col >= row)).astype(bf16)
        m1 = same64 & ((row & 63) < 32) & ((col & 63) >= 32)
        m2 = same32 & ((row & 31) < 16) & ((col & 31) >= 16)
        m3 = same16 & (col >= row)
        offs = (32, 16, 8, 0)
    b = _dot_2x(cum, la)
    r1 = _bcast_rows(b, 64, offs[0])
    r2 = _bcast_rows(b, 32, offs[1])
    r3 = _bcast_rows(b, 16, offs[2])
    bl = _bcast_rows(b, 64, offs[3])
    q1 = q * jnp.exp(jnp.minimum(b - r1, 0.0))
    k1 = (k * jnp.exp(jnp.minimum(r1 - b, 0.0))).astype(bf16)
    q2 = q * jnp.exp(jnp.minimum(b - r2, 0.0))
    k2 = (k * jnp.exp(jnp.minimum(r2 - b, 0.0))).astype(bf16)
    q3 = q * jnp.exp(b - r3)
    k3 = (k * jnp.exp(r3 - b)).astype(bf16)
    qe = (q * jnp.exp(b)).astype(bf16)
    kl = (k * jnp.exp(bl - b)).astype(bf16)
    dec = jnp.exp(bl)
    vb = v.astype(bf16)
    zq = jnp.zeros((n, n), f32)
    intra = jnp.zeros((n, n), f32)
    for h in range(GLA_H):
        hm = (col >> 6) == h
        a1 = _dot_nt(jnp.where(hm, q1, zq).astype(bf16), k1)
        a2 = _dot_nt(jnp.where(hm, q2, zq).astype(bf16), k2)
        a3 = _dot_nt(jnp.where(hm, q3, zq).astype(bf16), k3)
        att = jnp.where(m1, a1, jnp.where(m2, a2, jnp.where(m3, a3, zq)))
        intra = intra + _dot(att.astype(bf16), jnp.where(hm, v, zq).astype(bf16))
    outs = [None] * 4
    for c in (range(4) if not reverse else range(3, -1, -1)):
        rs = slice(64 * c, 64 * c + 64)
        st = st_ref[...]
        inter = _dot_nt(qe[rs], st.astype(bf16))
        kv = _dot_tn(vb[rs], kl[rs])
        drow = dec[64 * c:64 * c + 1, :]
        st_ref[...] = st * drow + jnp.where(same64, kv, zq)
        outs[c] = intra[rs] + inter
    return jnp.concatenate(outs, axis=0)


def _gla_kernel(x_ref, s0_ref, wa_ref, ba_ref, g_ref, bm_ref, o_ref, fin_ref, st_s, o_s, *, t):
    nblk = t // GLA_BLOCK

    def block(j, reverse):
        rows = pl.ds(j * GLA_BLOCK if isinstance(j, int) else pl.multiple_of(j * GLA_BLOCK, GLA_BLOCK), GLA_BLOCK)
        q = x_ref[rows, 0:GLA_W] * GLA_DK ** -0.5
        k = x_ref[rows, GLA_W:2 * GLA_W]
        v = x_ref[rows, 2 * GLA_W:3 * GLA_W]
        ga = x_ref[rows, 4 * GLA_W:4 * GLA_W + LANES]
        d = 1 if reverse else 0
        z = _dot(ga.astype(bf16), wa_ref[:, d * GLA_W:(d + 1) * GLA_W]) + ba_ref[:, d * GLA_W:(d + 1) * GLA_W]
        la = -_softplus(-z) * (1.0 / GLA_TAU)
        o = _gla_block(q, k, v, la, reverse, st_s)
        if reverse:
            o_s[rows, :] = o_s[rows, :] + o
        else:
            o_s[rows, :] = o

    for d in range(2):
        if s0_ref is None:
            st_s[...] = jnp.zeros((GLA_W, GLA_W), f32)
        else:
            st_s[...] = s0_ref[0, d]
        if nblk == 1:
            block(0, d == 1)
        else:
            def body(i, carry, d=d):
                block(i if d == 0 else nblk - 1 - i, d == 1)
                return carry
            lax.fori_loop(0, nblk, body, 0)
        fin_ref[0, d] = st_s[...]
    gg = x_ref[:, 3 * GLA_W:4 * GLA_W]
    o_ref[...] = _head_rms(o_s[...], g_ref[...], bm_ref[...]) * _silu(gg)


def _gla_call(pg, row0, n_seq, t, s0, wa, ba, g, bm):
    sb0 = row0 // t
    assert row0 % t == 0 and t % GLA_BLOCK == 0
    const = lambda b: (0, 0)
    in_specs = [pl.BlockSpec((t, PROJ_G), lambda b: (sb0 + b, 0))]
    args = [pg]
    if s0 is not None:
        in_specs.append(pl.BlockSpec((1, 2, GLA_W, GLA_W), lambda b: (b, 0, 0, 0)))
        args.append(s0)
    in_specs += [pl.BlockSpec((LANES, 2 * GLA_W), const), pl.BlockSpec((1, 2 * GLA_W), const),
                 pl.BlockSpec((1, GLA_W), const), pl.BlockSpec((GLA_W, GLA_W), const)]
    args += [wa, ba, g, bm]

    def kern(*refs):
        refs = list(refs)
        x_ref = refs.pop(0)
        s0_ref = refs.pop(0) if s0 is not None else None
        wa_ref, ba_ref, g_ref, bm_ref = refs[:4]
        _gla_kernel(x_ref, s0_ref, wa_ref, ba_ref, g_ref, bm_ref, *refs[4:], t=t)

    return pl.pallas_call(
        kern,
        grid=(n_seq,),
        in_specs=in_specs,
        out_specs=[pl.BlockSpec((t, GLA_W), lambda b: (b, 0)),
                   pl.BlockSpec((1, 2, GLA_W, GLA_W), lambda b: (b, 0, 0, 0))],
        out_shape=[jax.ShapeDtypeStruct((n_seq * t, GLA_W), f32), jax.ShapeDtypeStruct((n_seq, 2, GLA_W, GLA_W), f32)],
        scratch_shapes=[pltpu.VMEM((GLA_W, GLA_W), f32), pltpu.VMEM((t, GLA_W), f32)],
        compiler_params=_cparams(("parallel",)),
        name="gla_t%d" % t,
    )(*args)


def _out_kernel(x_ref, ac_ref, lc_ref, gc_ref, al_ref, ll_ref, gl_ref, mod_ref, wo_ref, g2_ref, rwh_ref, rwl_ref,
                s1_ref, s3_ref, s2_ref, xb_ref, h2_ref, sc_ref, *, n_ctx_tiles):
    is_ctx = pl.program_id(0) < n_ctx_tiles
    pick = lambda c_ref, l_ref: jnp.where(is_ctx, c_ref[...], l_ref[...]).astype(bf16)
    mod = mod_ref[0]
    g1 = mod[:, 2 * D_MODEL:3 * D_MODEL]
    sh2 = mod[:, 3 * D_MODEL:4 * D_MODEL]
    sc2 = mod[:, 4 * D_MODEL:5 * D_MODEL]
    g2 = mod[:, 5 * D_MODEL:6 * D_MODEL]
    m = (_dot(pick(ac_ref, al_ref), wo_ref[0:ATTN_W, :])
         + _dot(pick(lc_ref, ll_ref), wo_ref[ATTN_W:ATTN_W + LRU_W, :])
         + _dot(pick(gc_ref, gl_ref), wo_ref[ATTN_W + LRU_W:ATTN_W + LRU_W + GLA_W, :]))
    x1 = x_ref[...] + g1 * m
    h2 = _rms(x1, g2_ref[...]) * (1.0 + sc2) + sh2
    h2_ref[...] = h2
    sc_ref[...] = _sigmoid(_dot3(h2, rwh_ref[...], rwl_ref[...]))
    hb = h2.astype(bf16)
    act = (_silu(_dot(hb, s1_ref[...])) * _dot(hb, s3_ref[...])).astype(bf16)
    xb_ref[...] = x1 + g2 * _dot(act, s2_ref[...])


def _out_call(x, mix_ctx, mix_lat, mod, wo, g2, rwh, rwl, s1, s3, s2, seq_row):
    n = x.shape[0]
    nct = mix_ctx[0].shape[0] // ROW_TILE
    const = lambda i: (0, 0)
    rowb = lambda w: pl.BlockSpec((ROW_TILE, w), lambda i: (i, 0))
    ctxb = lambda w: pl.BlockSpec((ROW_TILE, w), lambda i: (jnp.minimum(i, nct - 1), 0))
    latb = lambda w: pl.BlockSpec((ROW_TILE, w), lambda i: (jnp.maximum(i - nct, 0), 0))
    return pl.pallas_call(
        functools.partial(_out_kernel, n_ctx_tiles=nct),
        grid=(n // ROW_TILE,),
        in_specs=[rowb(D_MODEL), ctxb(ATTN_W), ctxb(LRU_W), ctxb(GLA_W), latb(ATTN_W), latb(LRU_W), latb(GLA_W),
                  pl.BlockSpec((1, 1, 6 * D_MODEL), lambda i: (seq_row(i), 0, 0)),
                  pl.BlockSpec((D_MODEL, D_MODEL), const), pl.BlockSpec((1, D_MODEL), const),
                  pl.BlockSpec((D_MODEL, N_EXPERTS), const), pl.BlockSpec((D_MODEL, N_EXPERTS), const),
                  pl.BlockSpec((D_MODEL, EXPERT_FF), const), pl.BlockSpec((D_MODEL, EXPERT_FF), const),
                  pl.BlockSpec((EXPERT_FF, D_MODEL), const)],
        out_specs=[rowb(D_MODEL),
                   rowb(D_MODEL),
                   rowb(N_EXPERTS)],
        out_shape=[jax.ShapeDtypeStruct((n, D_MODEL), f32),
                   jax.ShapeDtypeStruct((n, D_MODEL), f32),
                   jax.ShapeDtypeStruct((n, N_EXPERTS), f32)],
        compiler_params=_cparams(("parallel",)),
        name="out_proj",
    )(x, *mix_ctx, *mix_lat, mod, wo, g2, rwh, rwl, s1, s3, s2)


def _route_kernel(sc_ref, rb_ref, idx_ref, gate_ref, rank_ref, cnt_ref, run_s):
    tm = sc_ref.shape[0]

    @pl.when(pl.program_id(0) == 0)
    def _():
        run_s[...] = jnp.zeros_like(run_s)

    scores = sc_ref[...]
    sel = scores + rb_ref[...]
    lane_f = lax.broadcasted_iota(i32, (tm, N_EXPERTS), 1).astype(f32)
    lane_o = lax.broadcasted_iota(i32, (tm, LANES), 1)
    neg = jnp.full((tm, N_EXPERTS), -jnp.inf, f32)
    hots = []
    idx_o = jnp.zeros((tm, LANES), i32)
    gate_o = jnp.zeros((tm, LANES), f32)
    gsum = jnp.zeros((tm, 1), f32)
    chosen = jnp.zeros((tm, N_EXPERTS), f32)
    for k in range(TOP_K):
        m = jnp.max(sel, axis=-1, keepdims=True)
        idx_f = jnp.min(jnp.where(sel == m, lane_f, float(N_EXPERTS)), axis=-1, keepdims=True)
        hot = lane_f == idx_f
        idx = idx_f.astype(i32)
        gk = jnp.sum(jnp.where(hot, scores, 0.0), axis=-1, keepdims=True)
        sel = jnp.where(hot, neg, sel)
        hots.append(hot)
        chosen = jnp.where(hot, 1.0, chosen)
        gsum = gsum + gk
        idx_o = jnp.where(lane_o == k, idx, idx_o)
        gate_o = jnp.where(lane_o == k, gk, gate_o)
    gate_ref[...] = gate_o / gsum * ROUTED_SCALE
    idx_ref[...] = idx_o
    r = lax.broadcasted_iota(i32, (tm, tm), 0)
    c = lax.broadcasted_iota(i32, (tm, tm), 1)
    pos = _dot((c < r).astype(bf16), chosen.astype(bf16)) + run_s[...]
    rank_o = jnp.zeros((tm, LANES), i32)
    for k in range(TOP_K):
        rk = jnp.sum(jnp.where(hots[k], pos, 0.0), axis=-1, keepdims=True)
        rank_o = jnp.where(lane_o == k, rk.astype(i32), rank_o)
    rank_ref[...] = rank_o
    run_s[...] = run_s[...] + jnp.sum(chosen, axis=0, keepdims=True)
    cnt_ref[...] = run_s[...]


def _route_call(scores, rb):
    n = scores.shape[0]
    rowb = lambda w: pl.BlockSpec((ROW_TILE, w), lambda i: (i, 0))
    return pl.pallas_call(
        _route_kernel,
        grid=(n // ROW_TILE,),
        in_specs=[rowb(N_EXPERTS), pl.BlockSpec((1, N_EXPERTS), lambda i: (0, 0))],
        out_specs=[rowb(LANES), rowb(LANES), rowb(LANES), pl.BlockSpec((1, N_EXPERTS), lambda i: (0, 0))],
        out_shape=[jax.ShapeDtypeStruct((n, LANES), i32), jax.ShapeDtypeStruct((n, LANES), f32),
                   jax.ShapeDtypeStruct((n, LANES), i32), jax.ShapeDtypeStruct((1, N_EXPERTS), f32)],
        scratch_shapes=[pltpu.VMEM((1, N_EXPERTS), f32)],
        compiler_params=_cparams(("arbitrary",)),
        name="route",
    )(scores, rb)


def _dest_kernel(idx_ref, rank_ref, cnt_ref, dest_ref):
    tm = idx_ref.shape[0]
    cnt = jnp.broadcast_to(cnt_ref[...], (SUBLANES, N_EXPERTS)).astype(i32)
    padded = ((cnt + (SLOT_BLOCK - 1)) // SLOT_BLOCK) * SLOT_BLOCK
    lane8 = lax.broadcasted_iota(i32, (SUBLANES, N_EXPERTS), 1)
    inc = padded
    s = 1
    while s < N_EXPERTS:
        inc = inc + jnp.where(lane8 >= s, pltpu.roll(inc, s, 1), 0)
        s *= 2
    start = (inc - padded)[0:1, :].astype(f32)
    lane = lax.broadcasted_iota(i32, (tm, N_EXPERTS), 1)
    lane_o = lax.broadcasted_iota(i32, (tm, LANES), 1)
    idx = idx_ref[...]
    rank = rank_ref[...]
    dest = jnp.zeros((tm, LANES), i32)
    for k in range(TOP_K):
        hot = lane == idx[:, k:k + 1]
        sk = jnp.sum(jnp.where(hot, start, 0.0), axis=-1, keepdims=True).astype(i32)
        dest = jnp.where(lane_o == k, sk + rank[:, k:k + 1], dest)
    dest_ref[...] = dest


def _dest_call(idx, rank, counts):
    n = idx.shape[0]
    rowb = lambda w: pl.BlockSpec((ROW_TILE, w), lambda i: (i, 0))
    return pl.pallas_call(
        _dest_kernel,
        grid=(n // ROW_TILE,),
        in_specs=[rowb(LANES), rowb(LANES), pl.BlockSpec((1, N_EXPERTS), lambda i: (0, 0))],
        out_specs=rowb(LANES),
        out_shape=jax.ShapeDtypeStruct((n, LANES), i32),
        compiler_params=_cparams(("parallel",)),
        name="dest",
    )(idx, rank, counts)


def _sc_workers():
    info = plsc.get_sparse_core_info()
    mesh = plsc.VectorSubcoreMesh(core_axis_name="c", subcore_axis_name="s")
    worker_id = lambda: lax.axis_index("s") * info.num_cores + lax.axis_index("c")
    return mesh, info.num_cores * info.num_subcores, worker_id


def _scatter_rows_call(rows, dest_km, n_slots):
    n, d = rows.shape
    mesh, n_workers, worker_id = _sc_workers()
    per_worker = n // n_workers
    assert n % (n_workers * SC_CHUNK) == 0

    def body(rows_hbm, idx_hbm, out_hbm, *scratch):
        idx_vs, rows_v, sem = scratch[:TOP_K], scratch[TOP_K], scratch[TOP_K + 1]
        base = worker_id() * per_worker

        @pl.loop(0, per_worker // SC_CHUNK)
        def _(i):
            t0 = pl.multiple_of(base + i * SC_CHUNK, SC_CHUNK)
            pltpu.sync_copy(rows_hbm.at[pl.ds(t0, SC_CHUNK)], rows_v)
            for k in range(TOP_K):
                pltpu.sync_copy(idx_hbm.at[pl.ds(k * n + t0, SC_CHUNK)], idx_vs[k])
            copies = [pltpu.async_copy(rows_v, out_hbm.at[idx_vs[k]], sem) for k in range(TOP_K)]
            for cp in copies:
                cp.wait()

    return pl.kernel(
        body,
        out_type=jax.ShapeDtypeStruct((n_slots, d), f32),
        mesh=mesh,
        scratch_types=[pltpu.VMEM((SC_CHUNK,), i32)] * TOP_K + [pltpu.VMEM((SC_CHUNK, d), f32),
                                                                pltpu.SemaphoreType.DMA],
        name="scatter_rows",
    )(rows, dest_km.reshape(-1))


def _expert_kernel(be_ref, nu_ref, bv_ref, x_ref, w1_ref, w3_ref, w2_ref, y_ref, w1_s, w3_s, w2_s):
    i = pl.program_id(0)

    @pl.when(i < nu_ref[0])
    def _():
        prev = be_ref[jnp.maximum(i - 1, 0)]

        @pl.when((i == 0) | (be_ref[i] != prev))
        def _():
            w1_s[...] = w1_ref[0, 0].astype(bf16)
            w3_s[...] = w3_ref[0, 0].astype(bf16)
            w2_s[...] = w2_ref[0, 0].astype(bf16)

        row = lax.broadcasted_iota(i32, x_ref.shape, 0)
        x = jnp.where(row < bv_ref[i], x_ref[...], 0.0).astype(bf16)
        act = (_silu(_dot(x, w1_s[...])) * _dot(x, w3_s[...])).astype(bf16)
        y_ref[...] = _dot(act, w2_s[...])


def _expert_call(blk_e, n_used, blk_valid, slots, w1, w3, w2, layer):
    nb = slots.shape[0] // SLOT_BLOCK
    blk = lambda i, be, nu, bv: (jnp.minimum(i, nu[0] - 1), 0)
    wsel = lambda i, be, nu, bv: (layer, be[jnp.minimum(i, nu[0] - 1)], 0, 0)
    grid_spec = pltpu.PrefetchScalarGridSpec(
        num_scalar_prefetch=3,
        grid=(nb,),
        in_specs=[pl.BlockSpec((SLOT_BLOCK, D_MODEL), blk),
                  pl.BlockSpec((1, 1, D_MODEL, EXPERT_FF), wsel),
                  pl.BlockSpec((1, 1, D_MODEL, EXPERT_FF), wsel),
                  pl.BlockSpec((1, 1, EXPERT_FF, D_MODEL), wsel)],
        out_specs=pl.BlockSpec((SLOT_BLOCK, D_MODEL), blk),
        scratch_shapes=[pltpu.VMEM((D_MODEL, EXPERT_FF), bf16), pltpu.VMEM((D_MODEL, EXPERT_FF), bf16),
                        pltpu.VMEM((EXPERT_FF, D_MODEL), bf16)],
    )
    return pl.pallas_call(
        _expert_kernel,
        grid_spec=grid_spec,
        out_shape=jax.ShapeDtypeStruct(slots.shape, f32),
        input_output_aliases={3: 0},
        compiler_params=_cparams(("arbitrary",)),
        name="experts",
    )(blk_e, n_used, blk_valid, slots, w1, w3, w2)


def _gather_rows_call(table, idx):
    n_idx = idx.shape[0]
    d = table.shape[1]
    mesh, n_workers, worker_id = _sc_workers()
    per_worker = n_idx // n_workers
    assert n_idx % (n_workers * SC_CHUNK) == 0

    def body(table_hbm, idx_hbm, out_hbm, idx_v, rows_v, sem):
        base = worker_id() * per_worker

        @pl.loop(0, per_worker // SC_CHUNK)
        def _(i):
            off = pl.multiple_of(base + i * SC_CHUNK, SC_CHUNK)
            pltpu.sync_copy(idx_hbm.at[pl.ds(off, SC_CHUNK)], idx_v)
            pltpu.async_copy(table_hbm.at[idx_v], rows_v, sem).wait()
            pltpu.sync_copy(rows_v, out_hbm.at[pl.ds(off, SC_CHUNK)])

    return pl.kernel(
        body,
        out_type=jax.ShapeDtypeStruct((n_idx, d), f32),
        mesh=mesh,
        scratch_types=[pltpu.VMEM((SC_CHUNK,), i32), pltpu.VMEM((SC_CHUNK, d), f32), pltpu.SemaphoreType.DMA],
        name="gather_rows",
    )(table, idx)


def _combine_kernel(gate_ref, xb_ref, mod_ref, y_ref, o_ref):
    g2 = mod_ref[0][:, 5 * D_MODEL:6 * D_MODEL]
    gates = gate_ref[...]
    acc = gates[:, 0:1] * y_ref[0]
    for k in range(1, TOP_K):
        acc = acc + gates[:, k:k + 1] * y_ref[k]
    o_ref[...] = xb_ref[...] + g2 * acc


def _combine_call(gates, xbase, mod, ygath, seq_row_c):
    n = xbase.shape[0]
    tm = COMBINE_TILE
    return pl.pallas_call(
        _combine_kernel,
        grid=(n // tm,),
        in_specs=[pl.BlockSpec((tm, LANES), lambda i: (i, 0)),
                  pl.BlockSpec((tm, D_MODEL), lambda i: (i, 0)),
                  pl.BlockSpec((1, 1, 6 * D_MODEL), lambda i: (seq_row_c(i), 0, 0)),
                  pl.BlockSpec((TOP_K, tm, D_MODEL), lambda i: (0, i, 0))],
        out_specs=pl.BlockSpec((tm, D_MODEL), lambda i: (i, 0)),
        out_shape=jax.ShapeDtypeStruct((n, D_MODEL), f32),
        compiler_params=_cparams(("parallel",)),
        name="combine",
    )(gates, xbase, mod, ygath)


def _block_avg(width, group):
    r = np.arange(width)
    return jnp.asarray((r[:, None] // group == r[None, :] // group).astype(np.float32) / group, dtype=bf16)


def _kv_replicate():
    c = np.arange(ATTN_W)
    src = (c // (ATTN_W // N_KV_HEADS)) * HEAD_DIM + c % HEAD_DIM
    return jnp.asarray((np.arange(KV_W)[:, None] == src[None, :]).astype(np.float32), dtype=bf16)


def _rope_lane_tables(n_tok):
    rows = n_tok // GRID_W
    r = jnp.repeat(jnp.arange(rows, dtype=f32), GRID_W)
    col = jnp.tile(jnp.arange(GRID_W, dtype=f32), rows)
    inv = ROPE_THETA ** (-jnp.arange(ROPE_FREQ, dtype=f32) / ROPE_FREQ)
    ar = r[:, None] * inv
    ac = col[:, None] * inv
    cos_h = jnp.concatenate([jnp.cos(ar), jnp.cos(ar), jnp.cos(ac), jnp.cos(ac)], axis=-1)
    sin_h = jnp.concatenate([-jnp.sin(ar), jnp.sin(ar), -jnp.sin(ac), jnp.sin(ac)], axis=-1)
    return jnp.tile(cos_h, (1, N_HEADS)), jnp.tile(sin_h, (1, N_HEADS))


def _block_diag(w):
    nb, bw, _ = w.shape
    eye = jnp.eye(nb, dtype=w.dtype)
    return (w[:, :, None, :] * eye[:, None, :, None]).reshape(nb * bw, nb * bw)


def _gla_state_in(s):
    bsz = s.shape[0]
    eye = jnp.eye(GLA_H, dtype=s.dtype)
    st = jnp.swapaxes(s, -1, -2)
    big = st[:, :, :, :, None, :] * eye[None, None, :, None, :, None]
    return big.reshape(bsz, 2, GLA_W, GLA_W)


def _gla_state_out(st):
    bsz = st.shape[0]
    s6 = st.reshape(bsz, 2, GLA_H, GLA_DK, GLA_H, GLA_DK)
    diag = jnp.stack([s6[:, :, h, :, h, :] for h in range(GLA_H)], axis=2)
    return jnp.swapaxes(diag, -1, -2)


def kernel(x_prompt, x_sample, cache_k, cache_v, state_lru, state_gla, c, c_ctx, ada_w, ada_b, norm1_g, norm2_g, w_in, q_norm_g, k_norm_g, attn_out_g, conv_w, conv_b, lru_wa, lru_ba, lru_wi, lru_bi, lru_lambda, lru_out_g, gla_wa2, gla_ba, gla_out_g, w_out, router_w, router_b, exp_w1, exp_w3, exp_w2, sh_w1, sh_w3, sh_w2):
    bc, tc, _ = x_prompt.shape
    bl, tl, _ = x_sample.shape
    depth = w_in.shape[0]
    nc = bc * tc
    n = nc + bl * tl
    past = cache_k.shape[2]
    assert tc == ROW_TILE and tl % ROW_TILE == 0 and nc % tl == 0 and bl + 1 <= SUBLANES
    assert n % ROW_TILE == 0 and n % COMBINE_TILE == 0

    def seq_row_for(tile):
        def seq_row(i):
            return jnp.where(i < nc // tile, 0, 1 + (i - nc // tile) // (tl // tile))
        return seq_row

    seq_row = seq_row_for(ROW_TILE)
    seq_row_c = seq_row_for(COMBINE_TILE)

    x = jnp.concatenate([x_prompt.reshape(nc, D_MODEL), x_sample.reshape(bl * tl, D_MODEL)], axis=0)
    cond = jnp.zeros((SUBLANES, D_MODEL), f32).at[0].set(c_ctx).at[1:1 + bl].set(c)
    mods = _ada_call(cond, ada_w, ada_b)

    bmq = _block_avg(ATTN_W, HEAD_DIM)
    bmk = _block_avg(KV_W, HEAD_DIM)
    bmg = _block_avg(GLA_W, GLA_DK)
    rep = _kv_replicate()
    cos_t, sin_t = _rope_lane_tables(tl)
    n_slots = -(-(n * TOP_K + N_EXPERTS * (SLOT_BLOCK - 1)) // SLOT_BLOCK) * SLOT_BLOCK
    tile8 = lambda v: jnp.tile(v, N_HEADS)[None, :]

    ks, vs, lrus, glas = [], [], [], []
    for l in range(depth):
        mod = mods[l].reshape(SUBLANES, 1, 6 * D_MODEL)
        w_in_p = jnp.pad(w_in[l].astype(bf16), ((0, 0), (0, LANES - 2 * GLA_RANK)))
        pa, pl_, pg = _proj_call(x, mod, norm1_g[l][None, :], w_in_p, seq_row)

        qg, kg, og = tile8(q_norm_g[l]), jnp.tile(k_norm_g[l], N_KV_HEADS)[None, :], attn_out_g[l][None, :]
        attn_c, k_new, v_new = _attn_ctx_call(pa, bc, tc, qg, kg, og, bmq, bmk, rep)
        attn_l = _attn_lat_call(pa, nc, bl, tl, cache_k[:, l].reshape(bl, past, KV_W),
                                cache_v[:, l].reshape(bl, past, KV_W), cos_t, sin_t, qg, kg, og, bmq, bmk, rep)
        ks.append(k_new.reshape(bc, tc, N_KV_HEADS, HEAD_DIM))
        vs.append(v_new.reshape(bc, tc, N_KV_HEADS, HEAD_DIM))

        wg = jnp.concatenate([_block_diag(lru_wa[l, 0]), _block_diag(lru_wi[l, 0]),
                              _block_diag(lru_wa[l, 1]), _block_diag(lru_wi[l, 1])], axis=1).astype(bf16)
        bg = jnp.concatenate([lru_ba[l, 0], lru_bi[l, 0], lru_ba[l, 1], lru_bi[l, 1]])[None, :]
        lru_args = (conv_w[l], conv_b[l][None, :], wg, bg, lru_lambda[l], lru_out_g[l][None, :])
        lru_c, lru_fin = _lru_call(pl_, 0, bc, tc, jnp.zeros((bc, 2, LRU_W), f32), *lru_args)
        lru_l, _ = _lru_call(pl_, nc, bl, tl, state_lru[:, l], *lru_args)
        lrus.append(lru_fin)

        wa = jnp.zeros((LANES, 2 * GLA_W), f32)
        wa = wa.at[0:GLA_RANK, 0:GLA_W].set(gla_wa2[l, 0]).at[GLA_RANK:2 * GLA_RANK, GLA_W:].set(gla_wa2[l, 1])
        gla_args = (wa.astype(bf16), gla_ba[l].reshape(1, 2 * GLA_W), gla_out_g[l].reshape(1, GLA_W), bmg)
        gla_c, gla_fin = _gla_call(pg, 0, bc, tc, None, *gla_args)
        gla_l, _ = _gla_call(pg, nc, bl, tl, _gla_state_in(state_gla[:, l]), *gla_args)
        glas.append(_gla_state_out(gla_fin))

        rw_hi = router_w[l].astype(bf16)
        rw_lo = (router_w[l] - rw_hi.astype(f32)).astype(bf16)
        xbase, h2, scores = _out_call(x, (attn_c, lru_c, gla_c), (attn_l, lru_l, gla_l), mod, w_out[l].astype(bf16),
                                      norm2_g[l][None, :], rw_hi, rw_lo, sh_w1[l].astype(bf16),
                                      sh_w3[l].astype(bf16), sh_w2[l].astype(bf16), seq_row)

        idx, gates, rank, counts = _route_call(scores, router_b[l][None, :])
        dest = _dest_call(idx, rank, counts)[:, :TOP_K]
        cnt = counts[0].astype(i32)
        padded = (cnt + SLOT_BLOCK - 1) // SLOT_BLOCK * SLOT_BLOCK
        padded_end = jnp.cumsum(padded)
        nb = n_slots // SLOT_BLOCK
        blk_first = jnp.arange(nb, dtype=i32) * SLOT_BLOCK
        blk_e = jnp.minimum(jnp.sum((padded_end[None, :] <= blk_first[:, None]).astype(i32), axis=1), N_EXPERTS - 1)
        blk_valid = jnp.clip((padded_end - padded + cnt)[blk_e] - blk_first, 0, SLOT_BLOCK)
        n_used = padded_end[-1:] // SLOT_BLOCK
        dest_km = dest.T
        slots = _scatter_rows_call(h2, dest_km, n_slots)
        slots = _expert_call(blk_e, n_used, blk_valid, slots, exp_w1, exp_w3, exp_w2, l)
        ygath = _gather_rows_call(slots, dest_km.reshape(-1)).reshape(TOP_K, n, D_MODEL)
        x = _combine_call(gates, xbase, mod, ygath, seq_row_c)

    y_prompt = x[:nc].reshape(bc, tc, D_MODEL)
    y_sample = x[nc:].reshape(bl, tl, D_MODEL)
    return (y_prompt, y_sample, jnp.stack(ks, axis=1), jnp.stack(vs, axis=1),
            jnp.stack(lrus, axis=1), jnp.stack(glas, axis=1))
```

```python
import functools

import jax
import jax.numpy as jnp
import numpy as np
from jax import lax
from jax.experimental import pallas as pl
from jax.experimental.pallas import tpu as pltpu
from jax.experimental.pallas import tpu_sc as plsc

f32 = jnp.float32
bf16 = jnp.bfloat16
i32 = jnp.int32

D_MODEL = 1024
N_HEADS = 8
N_KV_HEADS = 2
HEAD_DIM = 64
ATTN_W = N_HEADS * HEAD_DIM
KV_W = N_KV_HEADS * HEAD_DIM
GRID_W = 64
ROPE_FREQ = HEAD_DIM // 4
ROPE_THETA = 10000.0
LRU_W = 256
LRU_BLOCKS = 4
LRU_C = 8.0
CONV_W = 4
CONV_LEFT = 2
GLA_H = 4
GLA_DK = 64
GLA_W = 256
GLA_RANK = 16
GLA_TAU = 16.0
N_EXPERTS = 256
TOP_K = 8
EXPERT_FF = 256
ROUTED_SCALE = 2.5
EPS = 1e-6

LANES = 128
SUBLANES = 8
ROW_TILE = 256
GLA_BLOCK = 256
SLOT_BLOCK = 256
COMBINE_TILE = 128
SC_CHUNK = 64
D_PACK = D_MODEL // 2
VMEM_LIMIT = 56 * 1024 * 1024


def _cparams(sem, vmem=VMEM_LIMIT):
    return pltpu.CompilerParams(dimension_semantics=sem, vmem_limit_bytes=vmem)


def _dot(a, b):
    return jnp.dot(a, b, preferred_element_type=f32)


def _dot_nt(a, b):
    return lax.dot_general(a, b, (((1,), (1,)), ((), ())), preferred_element_type=f32)


def _dot_tn(a, b):
    return lax.dot_general(a, b, (((0,), (0,)), ((), ())), preferred_element_type=f32)


def _split(x):
    hi = x.astype(bf16)
    lo = (x - hi.astype(f32)).astype(bf16)
    return hi, lo


def _dot_x2(x, w):
    hi, lo = _split(x)
    return _dot(hi, w) + _dot(lo, w)


def _dot_2x(m, x):
    hi, lo = _split(x)
    return _dot(m, hi) + _dot(m, lo)


def _dot3(a, b_hi, b_lo):
    a_hi, a_lo = _split(a)
    return _dot(a_hi, b_hi) + _dot(a_lo, b_hi) + _dot(a_hi, b_lo)


def _sigmoid(x):
    return 1.0 / (1.0 + jnp.exp(-x))


def _silu(x):
    return x * _sigmoid(x)


def _softplus(x):
    return jnp.maximum(x, 0.0) + jnp.log(1.0 + jnp.exp(-jnp.abs(x)))


def _gelu_tanh(x):
    return 0.5 * x * (1.0 + jnp.tanh(0.7978845608028654 * (x + 0.044715 * x * x * x)))


def _rms(x, g):
    return x * lax.rsqrt(jnp.mean(x * x, axis=-1, keepdims=True) + EPS) * g


def _pack_pairs(x):
    c = x.shape[1] // 2
    hi = lax.bitcast_convert_type(x[:, :c].astype(bf16).astype(f32), i32)
    lo = lax.bitcast_convert_type(x[:, c:].astype(bf16).astype(f32), i32)
    return hi | lax.shift_right_logical(lo, 16)


def _unpack_pairs(w):
    hi = lax.bitcast_convert_type(w & jnp.int32(-65536), f32)
    lo = lax.bitcast_convert_type(w << 16, f32)
    return hi, lo


def _head_rms(x, g, bm):
    ms = _dot_x2(x * x, bm)
    return x * lax.rsqrt(ms + EPS) * g


def _ada_kernel(c_ref, w_ref, b_ref, o_ref):
    s = _silu(c_ref[...])
    w = w_ref[0]
    w_hi, w_lo = _split(w)
    o_ref[0] = _dot3(s, w_hi, w_lo) + b_ref[0]


def _ada_call(cond, ada_w, ada_b):
    depth = ada_w.shape[0]
    nt = 1536
    return pl.pallas_call(
        _ada_kernel,
        grid=(depth, 6 * D_MODEL // nt),
        in_specs=[pl.BlockSpec((SUBLANES, D_MODEL), lambda l, j: (0, 0)),
                  pl.BlockSpec((1, D_MODEL, nt), lambda l, j: (l, 0, j)),
                  pl.BlockSpec((1, 1, nt), lambda l, j: (l, 0, j))],
        out_specs=pl.BlockSpec((1, SUBLANES, nt), lambda l, j: (l, 0, j)),
        out_shape=jax.ShapeDtypeStruct((depth, SUBLANES, 6 * D_MODEL), f32),
        compiler_params=_cparams(("parallel", "parallel")),
        name="ada_mod",
    )(cond, ada_w, ada_b.reshape(depth, 1, 6 * D_MODEL))


PROJ_A = ATTN_W + 2 * KV_W
PROJ_L = 2 * LRU_W
PROJ_G = 4 * GLA_W + LANES


def _proj_kernel(x_ref, mod_ref, g_ref, w_ref, oa_ref, ol_ref, og_ref):
    mod = mod_ref[0]
    sh = mod[:, 0:D_MODEL]
    sc = mod[:, D_MODEL:2 * D_MODEL]
    h = (_rms(x_ref[...], g_ref[...]) * (1.0 + sc) + sh).astype(bf16)
    p = _dot(h, w_ref[...])
    oa_ref[...] = p[:, 0:PROJ_A]
    ol_ref[...] = p[:, PROJ_A:PROJ_A + PROJ_L]
    og_ref[...] = p[:, PROJ_A + PROJ_L:PROJ_A + PROJ_L + PROJ_G]


def _proj_call(x, mod, g, w, seq_row):
    n = x.shape[0]
    cols = PROJ_A + PROJ_L + PROJ_G
    return pl.pallas_call(
        _proj_kernel,
        grid=(n // ROW_TILE,),
        in_specs=[pl.BlockSpec((ROW_TILE, D_MODEL), lambda i: (i, 0)),
                  pl.BlockSpec((1, 1, 6 * D_MODEL), lambda i: (seq_row(i), 0, 0)),
                  pl.BlockSpec((1, D_MODEL), lambda i: (0, 0)),
                  pl.BlockSpec((D_MODEL, cols), lambda i: (0, 0))],
        out_specs=[pl.BlockSpec((ROW_TILE, PROJ_A), lambda i: (i, 0)),
                   pl.BlockSpec((ROW_TILE, PROJ_L), lambda i: (i, 0)),
                   pl.BlockSpec((ROW_TILE, PROJ_G), lambda i: (i, 0))],
        out_shape=[jax.ShapeDtypeStruct((n, PROJ_A), f32),
                   jax.ShapeDtypeStruct((n, PROJ_L), f32),
                   jax.ShapeDtypeStruct((n, PROJ_G), f32)],
        compiler_params=_cparams(("parallel",)),
        name="in_proj",
    )(x, mod, g, w)


def _rope(x, cos_t, sin_t):
    w = x.shape[1]
    up = pltpu.roll(x, w - ROPE_FREQ, 1)
    dn = pltpu.roll(x, ROPE_FREQ, 1)
    lane = lax.broadcasted_iota(i32, x.shape, 1)
    partner = jnp.where((lane & (2 * ROPE_FREQ - 1)) < ROPE_FREQ, up, dn)
    return x * cos_t + partner * sin_t


def _attend(q, kk_ref, vv_ref, o_ref):
    tq = q.shape[0]
    gw = ATTN_W // N_KV_HEADS
    lane = lax.broadcasted_iota(i32, (tq, gw), 1)
    for g in range(N_KV_HEADS):
        qg = q[:, g * gw:(g + 1) * gw]
        kg = kk_ref[:, g * gw:(g + 1) * gw]
        vg = vv_ref[:, g * gw:(g + 1) * gw]
        acc = jnp.zeros((tq, gw), f32)
        for hh in range(N_HEADS // N_KV_HEADS):
            hm = (lane >> 6) == hh
            s = _dot_nt(jnp.where(hm, qg, 0.0).astype(bf16), kg)
            m = jnp.max(s, axis=-1, keepdims=True)
            p = jnp.exp(s - m)
            l = jnp.sum(p, axis=-1, keepdims=True)
            o = _dot(p.astype(bf16), vg) / l
            acc = jnp.where(hm, o, acc)
        o_ref[:, g * gw:(g + 1) * gw] = acc


def _attn_ctx_kernel(p_ref, qg_ref, kg_ref, og_ref, bmq_ref, bmk_ref, rep_ref,
                     o_ref, ko_ref, vo_ref, kk_s, vv_s, o_s):
    p = p_ref[...]
    q = _head_rms(p[:, 0:ATTN_W], qg_ref[...], bmq_ref[...])
    k = _head_rms(p[:, ATTN_W:ATTN_W + KV_W], kg_ref[...], bmk_ref[...])
    v = p[:, ATTN_W + KV_W:PROJ_A]
    ko_ref[...] = k
    vo_ref[...] = v
    kk_s[...] = _dot(k.astype(bf16), rep_ref[...]).astype(bf16)
    vv_s[...] = _dot(v.astype(bf16), rep_ref[...]).astype(bf16)
    _attend(q * HEAD_DIM ** -0.5, kk_s, vv_s, o_s)
    o_ref[...] = _rms(o_s[...], og_ref[...])


def _attn_ctx_call(pa, n_seq, t, qg, kg, og, bmq, bmk, rep):
    assert t == ROW_TILE
    const = lambda i: (0, 0)
    return pl.pallas_call(
        _attn_ctx_kernel,
        grid=(n_seq,),
        in_specs=[pl.BlockSpec((t, PROJ_A), lambda i: (i, 0)),
                  pl.BlockSpec((1, ATTN_W), const), pl.BlockSpec((1, KV_W), const), pl.BlockSpec((1, ATTN_W), const),
                  pl.BlockSpec((ATTN_W, ATTN_W), const), pl.BlockSpec((KV_W, KV_W), const),
                  pl.BlockSpec((KV_W, ATTN_W), const)],
        out_specs=[pl.BlockSpec((t, ATTN_W), lambda i: (i, 0)),
                   pl.BlockSpec((t, KV_W), lambda i: (i, 0)),
                   pl.BlockSpec((t, KV_W), lambda i: (i, 0))],
        out_shape=[jax.ShapeDtypeStruct((n_seq * t, ATTN_W), f32),
                   jax.ShapeDtypeStruct((n_seq * t, KV_W), f32),
                   jax.ShapeDtypeStruct((n_seq * t, KV_W), f32)],
        scratch_shapes=[pltpu.VMEM((t, ATTN_W), bf16), pltpu.VMEM((t, ATTN_W), bf16), pltpu.VMEM((t, ATTN_W), f32)],
        compiler_params=_cparams(("parallel",)),
        name="attn_ctx",
    )(pa, qg, kg, og, bmq, bmk, rep)


def _attn_lat_kernel(q_ref, kv_ref, ck_ref, cv_ref, cq_ref, sq_ref, ckk_ref, skk_ref,
                     qg_ref, kg_ref, og_ref, bmq_ref, bmk_ref, rep_ref,
                     o_ref, kk_s, vv_s, o_s, *, past):
    @pl.when(pl.program_id(1) == 0)
    def _():
        kv = kv_ref[...]
        k = _head_rms(kv[:, 0:KV_W], kg_ref[...], bmk_ref[...])
        k = _rope(k, ckk_ref[...], skk_ref[...])
        v = kv[:, KV_W:2 * KV_W]
        kk_s[0:past, :] = _dot(ck_ref[0].astype(bf16), rep_ref[...]).astype(bf16)
        vv_s[0:past, :] = _dot(cv_ref[0].astype(bf16), rep_ref[...]).astype(bf16)
        kk_s[past:, :] = _dot(k.astype(bf16), rep_ref[...]).astype(bf16)
        vv_s[past:, :] = _dot(v.astype(bf16), rep_ref[...]).astype(bf16)

    q = _head_rms(q_ref[...], qg_ref[...], bmq_ref[...])
    q = _rope(q, cq_ref[...], sq_ref[...])
    _attend(q * HEAD_DIM ** -0.5, kk_s, vv_s, o_s)
    o_ref[...] = _rms(o_s[...], og_ref[...])


def _attn_lat_call(pa, row0, n_seq, t, cache_k, cache_v, cq, sq, qg, kg, og, bmq, bmk, rep):
    past = cache_k.shape[1]
    nq = t // ROW_TILE
    assert row0 % t == 0 and t % ROW_TILE == 0
    qb0 = row0 // ROW_TILE
    sb0 = row0 // t
    const = lambda b, j: (0, 0)
    return pl.pallas_call(
        functools.partial(_attn_lat_kernel, past=past),
        grid=(n_seq, nq),
        in_specs=[pl.BlockSpec((ROW_TILE, ATTN_W), lambda b, j: (qb0 + b * nq + j, 0)),
                  pl.BlockSpec((t, 2 * KV_W), lambda b, j: (sb0 + b, ATTN_W // (2 * KV_W))),
                  pl.BlockSpec((1, past, KV_W), lambda b, j: (b, 0, 0)),
                  pl.BlockSpec((1, past, KV_W), lambda b, j: (b, 0, 0)),
                  pl.BlockSpec((ROW_TILE, ATTN_W), lambda b, j: (j, 0)),
                  pl.BlockSpec((ROW_TILE, ATTN_W), lambda b, j: (j, 0)),
                  pl.BlockSpec((t, KV_W), const), pl.BlockSpec((t, KV_W), const),
                  pl.BlockSpec((1, ATTN_W), const), pl.BlockSpec((1, KV_W), const), pl.BlockSpec((1, ATTN_W), const),
                  pl.BlockSpec((ATTN_W, ATTN_W), const), pl.BlockSpec((KV_W, KV_W), const),
                  pl.BlockSpec((KV_W, ATTN_W), const)],
        out_specs=pl.BlockSpec((ROW_TILE, ATTN_W), lambda b, j: (b * nq + j, 0)),
        out_shape=jax.ShapeDtypeStruct((n_seq * t, ATTN_W), f32),
        scratch_shapes=[pltpu.VMEM((past + t, ATTN_W), bf16), pltpu.VMEM((past + t, ATTN_W), bf16),
                        pltpu.VMEM((ROW_TILE, ATTN_W), f32)],
        compiler_params=_cparams(("parallel", "arbitrary")),
        name="attn_lat",
    )(pa, pa, cache_k, cache_v, cq, sq, cq, sq, qg, kg, og, bmq, bmk, rep)


def _lru_kernel(x_ref, h0_ref, cw_ref, cb_ref, wg_ref, bg_ref, lam_ref, g_ref,
                o_ref, fin_ref, xs, a_s, u_s, h_s, *, t):
    ng = t // SUBLANES
    lx = x_ref[:, 0:LRU_W]
    ly = x_ref[:, LRU_W:2 * LRU_W]
    xs[0:SUBLANES, :] = jnp.zeros((SUBLANES, LRU_W), f32)
    xs[SUBLANES + t:2 * SUBLANES + t, :] = jnp.zeros((SUBLANES, LRU_W), f32)
    xs[SUBLANES:SUBLANES + t, :] = lx
    xc = jnp.broadcast_to(cb_ref[...], (t, LRU_W))
    for j in range(CONV_W):
        off = SUBLANES + j - CONV_LEFT
        xc = xc + xs[off:off + t, :] * cw_ref[j:j + 1, :]
    gates = _sigmoid(_dot(xc.astype(bf16), wg_ref[...]) + bg_ref[...])
    sp = _softplus(-lam_ref[...])
    sub = lax.broadcasted_iota(i32, (t, LRU_W), 0) & (SUBLANES - 1)
    for d in range(2):
        r = gates[:, (2 * d) * LRU_W:(2 * d + 1) * LRU_W]
        gi = gates[:, (2 * d + 1) * LRU_W:(2 * d + 2) * LRU_W]
        log_a = -LRU_C * r * sp[d:d + 1, :]
        a = jnp.exp(log_a)
        th = jnp.tanh(log_a)
        u = jnp.sqrt(-2.0 * th / (1.0 - th)) * (gi * xc)
        for s in (1, 2, 4):
            if d == 0:
                a_sh = pltpu.roll(a, s, 0)
                u_sh = pltpu.roll(u, s, 0)
                ok = sub >= s
            else:
                a_sh = pltpu.roll(a, t - s, 0)
                u_sh = pltpu.roll(u, t - s, 0)
                ok = sub < SUBLANES - s
            u = jnp.where(ok, a * u_sh + u, u)
            a = jnp.where(ok, a * a_sh, a)
        a_s[...] = a
        u_s[...] = u
        h0 = jnp.broadcast_to(h0_ref[0, d:d + 1, :], (SUBLANES, LRU_W))
        edge = SUBLANES - 1 if d == 0 else 0

        def body(i, carry, d=d, edge=edge):
            g = i if d == 0 else ng - 1 - i
            rows = pl.ds(pl.multiple_of(g * SUBLANES, SUBLANES), SUBLANES)
            h = a_s[rows, :] * carry + u_s[rows, :]
            if d == 0:
                h_s[rows, :] = h
            else:
                h_s[rows, :] = h_s[rows, :] + h
            return jnp.broadcast_to(h[edge:edge + 1, :], (SUBLANES, LRU_W))

        last = lax.fori_loop(0, ng, body, h0)
        fin_ref[0, d:d + 1, :] = last[0:1, :]
    o_ref[...] = _rms(h_s[...] * _gelu_tanh(ly), g_ref[...])


def _lru_call(pl_, row0, n_seq, t, h0, cw, cb, wg, bg, lam, g):
    sb0 = row0 // t
    assert row0 % t == 0
    const = lambda b: (0, 0)
    in_specs = [pl.BlockSpec((t, PROJ_L), lambda b: (sb0 + b, 0)),
                pl.BlockSpec((1, 2, LRU_W), lambda b: (b, 0, 0)),
                pl.BlockSpec((CONV_W, LRU_W), const), pl.BlockSpec((1, LRU_W), const),
                pl.BlockSpec((LRU_W, 4 * LRU_W), const), pl.BlockSpec((1, 4 * LRU_W), const),
                pl.BlockSpec((2, LRU_W), const), pl.BlockSpec((1, LRU_W), const)]
    args = [pl_, h0, cw, cb, wg, bg, lam, g]
    return pl.pallas_call(
        functools.partial(_lru_kernel, t=t),
        grid=(n_seq,),
        in_specs=in_specs,
        out_specs=[pl.BlockSpec((t, LRU_W), lambda b: (b, 0)),
                   pl.BlockSpec((1, 2, LRU_W), lambda b: (b, 0, 0))],
        out_shape=[jax.ShapeDtypeStruct((n_seq * t, LRU_W), f32), jax.ShapeDtypeStruct((n_seq, 2, LRU_W), f32)],
        scratch_shapes=[pltpu.VMEM((t + 2 * SUBLANES, LRU_W), f32), pltpu.VMEM((t, LRU_W), f32),
                        pltpu.VMEM((t, LRU_W), f32), pltpu.VMEM((t, LRU_W), f32)],
        compiler_params=_cparams(("parallel",)),
        name="rglru_t%d" % t,
    )(*args)


def _bcast_rows(b, period, off):
    w = b.shape[1]
    return jnp.concatenate(
        [jnp.broadcast_to(b[i * period + off:i * period + off + 1, :], (period, w)) for i in range(b.shape[0] // period)],
        axis=0)


def _gla_block(q, k, v, la, reverse, st_ref):
    n = GLA_BLOCK
    row = lax.broadcasted_iota(i32, (n, n), 0)
    col = lax.broadcasted_iota(i32, (n, n), 1)
    same64 = (row >> 6) == (col >> 6)
    same32 = (row >> 5) == (col >> 5)
    same16 = (row >> 4) == (col >> 4)
    if not reverse:
        cum = (same64 & (col <= row)).astype(bf16)
        m1 = same64 & ((row & 63) >= 32) & ((col & 63) < 32)
        m2 = same32 & ((row & 31) >= 16) & ((col & 31) < 16)
        m3 = same16 & (col <= row)
        offs = (31, 15, 7, 63)
    else:
        cum = (same64 & (col >= row)).astype(bf16)
        m1 = same64 & ((row & 63) < 32) & ((col & 63) >= 32)
        m2 = same32 & ((row & 31) < 16) & ((col & 31) >= 16)
        m3 = same16 & (col >= row)
        offs = (32, 16, 8, 0)
    b = _dot_2x(cum, la)
    r1 = _bcast_rows(b, 64, offs[0])
    r2 = _bcast_rows(b, 32, offs[1])
    r3 = _bcast_rows(b, 16, offs[2])
    bl = _bcast_rows(b, 64, offs[3])
    q1 = q * jnp.exp(jnp.minimum(b - r1, 0.0))
    k1 = (k * jnp.exp(jnp.minimum(r1 - b, 0.0))).astype(bf16)
    q2 = q * jnp.exp(jnp.minimum(b - r2, 0.0))
    k2 = (k * jnp.exp(jnp.minimum(r2 - b, 0.0))).astype(bf16)
    q3 = q * jnp.exp(b - r3)
    k3 = (k * jnp.exp(r3 - b)).astype(bf16)
    qe = (q * jnp.exp(b)).astype(bf16)
    kl = (k * jnp.exp(bl - b)).astype(bf16)
    dec = jnp.exp(bl)
    vb = v.astype(bf16)
    zq = jnp.zeros((n, n), f32)
    intra = jnp.zeros((n, n), f32)
    for h in range(GLA_H):
        hm = (col >> 6) == h
        a1 = _dot_nt(jnp.where(hm, q1, zq).astype(bf16), k1)
        a2 = _dot_nt(jnp.where(hm, q2, zq).astype(bf16), k2)
        a3 = _dot_nt(jnp.where(hm, q3, zq).astype(bf16), k3)
        att = jnp.where(m1, a1, jnp.where(m2, a2, jnp.where(m3, a3, zq)))
        intra = intra + _dot(att.astype(bf16), jnp.where(hm, v, zq).astype(bf16))
    outs = [None] * 4
    for c in (range(4) if not reverse else range(3, -1, -1)):
        rs = slice(64 * c, 64 * c + 64)
        st = st_ref[...]
        inter = _dot_nt(qe[rs], st.astype(bf16))
        kv = _dot_tn(vb[rs], kl[rs])
        drow = dec[64 * c:64 * c + 1, :]
        st_ref[...] = st * drow + jnp.where(same64, kv, zq)
        outs[c] = intra[rs] + inter
    return jnp.concatenate(outs, axis=0)


def _gla_kernel(x_ref, s0_ref, wa_ref, ba_ref, g_ref, bm_ref, o_ref, fin_ref, st_s, o_s, *, t):
    nblk = t // GLA_BLOCK

    def block(j, reverse):
        rows = pl.ds(j * GLA_BLOCK if isinstance(j, int) else pl.multiple_of(j * GLA_BLOCK, GLA_BLOCK), GLA_BLOCK)
        q = x_ref[rows, 0:GLA_W] * GLA_DK ** -0.5
        k = x_ref[rows, GLA_W:2 * GLA_W]
        v = x_ref[rows, 2 * GLA_W:3 * GLA_W]
        ga = x_ref[rows, 4 * GLA_W:4 * GLA_W + LANES]
        d = 1 if reverse else 0
        z = _dot(ga.astype(bf16), wa_ref[:, d * GLA_W:(d + 1) * GLA_W]) + ba_ref[:, d * GLA_W:(d + 1) * GLA_W]
        la = -_softplus(-z) * (1.0 / GLA_TAU)
        o = _gla_block(q, k, v, la, reverse, st_s)
        if reverse:
            o_s[rows, :] = o_s[rows, :] + o
        else:
            o_s[rows, :] = o

    for d in range(2):
        if s0_ref is None:
            st_s[...] = jnp.zeros((GLA_W, GLA_W), f32)
        else:
            st_s[...] = s0_ref[0, d]
        if nblk == 1:
            block(0, d == 1)
        else:
            def body(i, carry, d=d):
                block(i if d == 0 else nblk - 1 - i, d == 1)
                return carry
            lax.fori_loop(0, nblk, body, 0)
        fin_ref[0, d] = st_s[...]
    gg = x_ref[:, 3 * GLA_W:4 * GLA_W]
    o_ref[...] = _head_rms(o_s[...], g_ref[...], bm_ref[...]) * _silu(gg)


def _gla_call(pg, row0, n_seq, t, s0, wa, ba, g, bm):
    sb0 = row0 // t
    assert row0 % t == 0 and t % GLA_BLOCK == 0
    const = lambda b: (0, 0)
    in_specs = [pl.BlockSpec((t, PROJ_G), lambda b: (sb0 + b, 0))]
    args = [pg]
    if s0 is not None:
        in_specs.append(pl.BlockSpec((1, 2, GLA_W, GLA_W), lambda b: (b, 0, 0, 0)))
        args.append(s0)
    in_specs += [pl.BlockSpec((LANES, 2 * GLA_W), const), pl.BlockSpec((1, 2 * GLA_W), const),
                 pl.BlockSpec((1, GLA_W), const), pl.BlockSpec((GLA_W, GLA_W), const)]
    args += [wa, ba, g, bm]

    def kern(*refs):
        refs = list(refs)
        x_ref = refs.pop(0)
        s0_ref = refs.pop(0) if s0 is not None else None
        wa_ref, ba_ref, g_ref, bm_ref = refs[:4]
        _gla_kernel(x_ref, s0_ref, wa_ref, ba_ref, g_ref, bm_ref, *refs[4:], t=t)

    return pl.pallas_call(
        kern,
        grid=(n_seq,),
        in_specs=in_specs,
        out_specs=[pl.BlockSpec((t, GLA_W), lambda b: (b, 0)),
                   pl.BlockSpec((1, 2, GLA_W, GLA_W), lambda b: (b, 0, 0, 0))],
        out_shape=[jax.ShapeDtypeStruct((n_seq * t, GLA_W), f32), jax.ShapeDtypeStruct((n_seq, 2, GLA_W, GLA_W), f32)],
        scratch_shapes=[pltpu.VMEM((GLA_W, GLA_W), f32), pltpu.VMEM((t, GLA_W), f32)],
        compiler_params=_cparams(("parallel",)),
        name="gla_t%d" % t,
    )(*args)


def _out_kernel(x_ref, ac_ref, lc_ref, gc_ref, al_ref, ll_ref, gl_ref, mod_ref, wo_ref, g2_ref, rwh_ref, rwl_ref,
                s1_ref, s3_ref, s2_ref, xb_ref, h2_ref, sc_ref, *, n_ctx_tiles):
    is_ctx = pl.program_id(0) < n_ctx_tiles
    pick = lambda c_ref, l_ref: jnp.where(is_ctx, c_ref[...], l_ref[...]).astype(bf16)
    mod = mod_ref[0]
    g1 = mod[:, 2 * D_MODEL:3 * D_MODEL]
    sh2 = mod[:, 3 * D_MODEL:4 * D_MODEL]
    sc2 = mod[:, 4 * D_MODEL:5 * D_MODEL]
    g2 = mod[:, 5 * D_MODEL:6 * D_MODEL]
    m = (_dot(pick(ac_ref, al_ref), wo_ref[0:ATTN_W, :])
         + _dot(pick(lc_ref, ll_ref), wo_ref[ATTN_W:ATTN_W + LRU_W, :])
         + _dot(pick(gc_ref, gl_ref), wo_ref[ATTN_W + LRU_W:ATTN_W + LRU_W + GLA_W, :]))
    x1 = x_ref[...] + g1 * m
    h2 = _rms(x1, g2_ref[...]) * (1.0 + sc2) + sh2
    h2_ref[...] = _pack_pairs(h2)
    sc_ref[...] = _sigmoid(_dot3(h2, rwh_ref[...], rwl_ref[...]))
    hb = h2.astype(bf16)
    act = (_silu(_dot(hb, s1_ref[...])) * _dot(hb, s3_ref[...])).astype(bf16)
    xb_ref[...] = x1 + g2 * _dot(act, s2_ref[...])


def _out_call(x, mix_ctx, mix_lat, mod, wo, g2, rwh, rwl, s1, s3, s2, seq_row):
    n = x.shape[0]
    nct = mix_ctx[0].shape[0] // ROW_TILE
    const = lambda i: (0, 0)
    rowb = lambda w: pl.BlockSpec((ROW_TILE, w), lambda i: (i, 0))
    ctxb = lambda w: pl.BlockSpec((ROW_TILE, w), lambda i: (jnp.minimum(i, nct - 1), 0))
    latb = lambda w: pl.BlockSpec((ROW_TILE, w), lambda i: (jnp.maximum(i - nct, 0), 0))
    return pl.pallas_call(
        functools.partial(_out_kernel, n_ctx_tiles=nct),
        grid=(n // ROW_TILE,),
        in_specs=[rowb(D_MODEL), ctxb(ATTN_W), ctxb(LRU_W), ctxb(GLA_W), latb(ATTN_W), latb(LRU_W), latb(GLA_W),
                  pl.BlockSpec((1, 1, 6 * D_MODEL), lambda i: (seq_row(i), 0, 0)),
                  pl.BlockSpec((D_MODEL, D_MODEL), const), pl.BlockSpec((1, D_MODEL), const),
                  pl.BlockSpec((D_MODEL, N_EXPERTS), const), pl.BlockSpec((D_MODEL, N_EXPERTS), const),
                  pl.BlockSpec((D_MODEL, EXPERT_FF), const), pl.BlockSpec((D_MODEL, EXPERT_FF), const),
                  pl.BlockSpec((EXPERT_FF, D_MODEL), const)],
        out_specs=[rowb(D_MODEL),
                   rowb(D_PACK),
                   rowb(N_EXPERTS)],
        out_shape=[jax.ShapeDtypeStruct((n, D_MODEL), f32),
                   jax.ShapeDtypeStruct((n, D_PACK), i32),
                   jax.ShapeDtypeStruct((n, N_EXPERTS), f32)],
        compiler_params=_cparams(("parallel",)),
        name="out_proj",
    )(x, *mix_ctx, *mix_lat, mod, wo, g2, rwh, rwl, s1, s3, s2)


def _route_kernel(sc_ref, rb_ref, idx_ref, gate_ref, rank_ref, cnt_ref, run_s):
    tm = sc_ref.shape[0]

    @pl.when(pl.program_id(0) == 0)
    def _():
        run_s[...] = jnp.zeros_like(run_s)

    scores = sc_ref[...]
    sel = scores + rb_ref[...]
    lane_f = lax.broadcasted_iota(i32, (tm, N_EXPERTS), 1).astype(f32)
    lane_o = lax.broadcasted_iota(i32, (tm, LANES), 1)
    neg = jnp.full((tm, N_EXPERTS), -jnp.inf, f32)
    hots = []
    idx_o = jnp.zeros((tm, LANES), i32)
    gate_o = jnp.zeros((tm, LANES), f32)
    gsum = jnp.zeros((tm, 1), f32)
    chosen = jnp.zeros((tm, N_EXPERTS), f32)
    for k in range(TOP_K):
        m = jnp.max(sel, axis=-1, keepdims=True)
        idx_f = jnp.min(jnp.where(sel == m, lane_f, float(N_EXPERTS)), axis=-1, keepdims=True)
        hot = lane_f == idx_f
        idx = idx_f.astype(i32)
        gk = jnp.sum(jnp.where(hot, scores, 0.0), axis=-1, keepdims=True)
        sel = jnp.where(hot, neg, sel)
        hots.append(hot)
        chosen = jnp.where(hot, 1.0, chosen)
        gsum = gsum + gk
        idx_o = jnp.where(lane_o == k, idx, idx_o)
        gate_o = jnp.where(lane_o == k, gk, gate_o)
    gate_ref[...] = gate_o / gsum * ROUTED_SCALE
    idx_ref[...] = idx_o
    r = lax.broadcasted_iota(i32, (tm, tm), 0)
    c = lax.broadcasted_iota(i32, (tm, tm), 1)
    pos = _dot((c < r).astype(bf16), chosen.astype(bf16)) + run_s[...]
    rank_o = jnp.zeros((tm, LANES), i32)
    for k in range(TOP_K):
        rk = jnp.sum(jnp.where(hots[k], pos, 0.0), axis=-1, keepdims=True)
        rank_o = jnp.where(lane_o == k, rk.astype(i32), rank_o)
    rank_ref[...] = rank_o
    run_s[...] = run_s[...] + jnp.sum(chosen, axis=0, keepdims=True)
    cnt_ref[...] = run_s[...]


def _route_call(scores, rb):
    n = scores.shape[0]
    rowb = lambda w: pl.BlockSpec((ROW_TILE, w), lambda i: (i, 0))
    return pl.pallas_call(
        _route_kernel,
        grid=(n // ROW_TILE,),
        in_specs=[rowb(N_EXPERTS), pl.BlockSpec((1, N_EXPERTS), lambda i: (0, 0))],
        out_specs=[rowb(LANES), rowb(LANES), rowb(LANES), pl.BlockSpec((1, N_EXPERTS), lambda i: (0, 0))],
        out_shape=[jax.ShapeDtypeStruct((n, LANES), i32), jax.ShapeDtypeStruct((n, LANES), f32),
                   jax.ShapeDtypeStruct((n, LANES), i32), jax.ShapeDtypeStruct((1, N_EXPERTS), f32)],
        scratch_shapes=[pltpu.VMEM((1, N_EXPERTS), f32)],
        compiler_params=_cparams(("arbitrary",)),
        name="route",
    )(scores, rb)


def _dest_kernel(idx_ref, rank_ref, cnt_ref, dest_ref):
    tm = idx_ref.shape[0]
    cnt = jnp.broadcast_to(cnt_ref[...], (SUBLANES, N_EXPERTS)).astype(i32)
    padded = ((cnt + (SLOT_BLOCK - 1)) // SLOT_BLOCK) * SLOT_BLOCK
    lane8 = lax.broadcasted_iota(i32, (SUBLANES, N_EXPERTS), 1)
    inc = padded
    s = 1
    while s < N_EXPERTS:
        inc = inc + jnp.where(lane8 >= s, pltpu.roll(inc, s, 1), 0)
        s *= 2
    start = (inc - padded)[0:1, :].astype(f32)
    lane = lax.broadcasted_iota(i32, (tm, N_EXPERTS), 1)
    lane_o = lax.broadcasted_iota(i32, (tm, LANES), 1)
    idx = idx_ref[...]
    rank = rank_ref[...]
    dest = jnp.zeros((tm, LANES), i32)
    for k in range(TOP_K):
        hot = lane == idx[:, k:k + 1]
        sk = jnp.sum(jnp.where(hot, start, 0.0), axis=-1, keepdims=True).astype(i32)
        dest = jnp.where(lane_o == k, sk + rank[:, k:k + 1], dest)
    dest_ref[...] = dest


def _dest_call(idx, rank, counts):
    n = idx.shape[0]
    rowb = lambda w: pl.BlockSpec((ROW_TILE, w), lambda i: (i, 0))
    return pl.pallas_call(
        _dest_kernel,
        grid=(n // ROW_TILE,),
        in_specs=[rowb(LANES), rowb(LANES), pl.BlockSpec((1, N_EXPERTS), lambda i: (0, 0))],
        out_specs=rowb(LANES),
        out_shape=jax.ShapeDtypeStruct((n, LANES), i32),
        compiler_params=_cparams(("parallel",)),
        name="dest",
    )(idx, rank, counts)


def _sc_workers():
    info = plsc.get_sparse_core_info()
    mesh = plsc.VectorSubcoreMesh(core_axis_name="c", subcore_axis_name="s")
    worker_id = lambda: lax.axis_index("s") * info.num_cores + lax.axis_index("c")
    return mesh, info.num_cores * info.num_subcores, worker_id


def _scatter_rows_call(rows, dest_km, n_slots):
    n, d = rows.shape
    mesh, n_workers, worker_id = _sc_workers()
    per_worker = n // n_workers
    assert n % (n_workers * SC_CHUNK) == 0

    def body(rows_hbm, idx_hbm, out_hbm, *scratch):
        idx_vs, rows_v, sem = scratch[:TOP_K], scratch[TOP_K], scratch[TOP_K + 1]
        base = worker_id() * per_worker

        @pl.loop(0, per_worker // SC_CHUNK)
        def _(i):
            t0 = pl.multiple_of(base + i * SC_CHUNK, SC_CHUNK)
            pltpu.sync_copy(rows_hbm.at[pl.ds(t0, SC_CHUNK)], rows_v)
            for k in range(TOP_K):
                pltpu.sync_copy(idx_hbm.at[pl.ds(k * n + t0, SC_CHUNK)], idx_vs[k])
            copies = [pltpu.async_copy(rows_v, out_hbm.at[idx_vs[k]], sem) for k in range(TOP_K)]
            for cp in copies:
                cp.wait()

    return pl.kernel(
        body,
        out_type=jax.ShapeDtypeStruct((n_slots, d), rows.dtype),
        mesh=mesh,
        scratch_types=[pltpu.VMEM((SC_CHUNK,), i32)] * TOP_K + [pltpu.VMEM((SC_CHUNK, d), rows.dtype),
                                                                pltpu.SemaphoreType.DMA],
        name="scatter_rows",
    )(rows, dest_km.reshape(-1))


def _expert_kernel(be_ref, nu_ref, bv_ref, x_ref, w1_ref, w3_ref, w2_ref, y_ref, w1_s, w3_s, w2_s):
    i = pl.program_id(0)

    @pl.when(i < nu_ref[0])
    def _():
        prev = be_ref[jnp.maximum(i - 1, 0)]

        @pl.when((i == 0) | (be_ref[i] != prev))
        def _():
            w1_s[...] = w1_ref[0, 0].astype(bf16)
            w3_s[...] = w3_ref[0, 0].astype(bf16)
            w2_s[...] = w2_ref[0, 0].astype(bf16)

        row = lax.broadcasted_iota(i32, x_ref.shape, 0)
        x_hi, x_lo = _unpack_pairs(jnp.where(row < bv_ref[i], x_ref[...], 0))
        x = jnp.concatenate([x_hi, x_lo], axis=1).astype(bf16)
        act = (_silu(_dot(x, w1_s[...])) * _dot(x, w3_s[...])).astype(bf16)
        y_ref[...] = _pack_pairs(_dot(act, w2_s[...]))


def _expert_call(blk_e, n_used, blk_valid, slots, w1, w3, w2, layer):
    nb = slots.shape[0] // SLOT_BLOCK
    blk = lambda i, be, nu, bv: (jnp.minimum(i, nu[0] - 1), 0)
    wsel = lambda i, be, nu, bv: (layer, be[jnp.minimum(i, nu[0] - 1)], 0, 0)
    grid_spec = pltpu.PrefetchScalarGridSpec(
        num_scalar_prefetch=3,
        grid=(nb,),
        in_specs=[pl.BlockSpec((SLOT_BLOCK, D_PACK), blk),
                  pl.BlockSpec((1, 1, D_MODEL, EXPERT_FF), wsel),
                  pl.BlockSpec((1, 1, D_MODEL, EXPERT_FF), wsel),
                  pl.BlockSpec((1, 1, EXPERT_FF, D_MODEL), wsel)],
        out_specs=pl.BlockSpec((SLOT_BLOCK, D_PACK), blk),
        scratch_shapes=[pltpu.VMEM((D_MODEL, EXPERT_FF), bf16), pltpu.VMEM((D_MODEL, EXPERT_FF), bf16),
                        pltpu.VMEM((EXPERT_FF, D_MODEL), bf16)],
    )
    return pl.pallas_call(
        _expert_kernel,
        grid_spec=grid_spec,
        out_shape=jax.ShapeDtypeStruct(slots.shape, slots.dtype),
        input_output_aliases={3: 0},
        compiler_params=_cparams(("arbitrary",)),
        name="experts",
    )(blk_e, n_used, blk_valid, slots, w1, w3, w2)


def _gather_rows_call(table, idx):
    n_idx = idx.shape[0]
    d = table.shape[1]
    mesh, n_workers, worker_id = _sc_workers()
    per_worker = n_idx // n_workers
    assert n_idx % (n_workers * SC_CHUNK) == 0

    def body(table_hbm, idx_hbm, out_hbm, idx_v, rows_v, sem):
        base = worker_id() * per_worker

        @pl.loop(0, per_worker // SC_CHUNK)
        def _(i):
            off = pl.multiple_of(base + i * SC_CHUNK, SC_CHUNK)
            pltpu.sync_copy(idx_hbm.at[pl.ds(off, SC_CHUNK)], idx_v)
            pltpu.async_copy(table_hbm.at[idx_v], rows_v, sem).wait()
            pltpu.sync_copy(rows_v, out_hbm.at[pl.ds(off, SC_CHUNK)])

    return pl.kernel(
        body,
        out_type=jax.ShapeDtypeStruct((n_idx, d), table.dtype),
        mesh=mesh,
        scratch_types=[pltpu.VMEM((SC_CHUNK,), i32), pltpu.VMEM((SC_CHUNK, d), table.dtype),
                       pltpu.SemaphoreType.DMA],
        name="gather_rows",
    )(table, idx)


def _combine_kernel(gate_ref, xb_ref, mod_ref, y_ref, o_ref):
    g2 = mod_ref[0][:, 5 * D_MODEL:6 * D_MODEL]
    gates = gate_ref[...]
    acc_hi = jnp.zeros((gates.shape[0], D_PACK), f32)
    acc_lo = jnp.zeros((gates.shape[0], D_PACK), f32)
    for k in range(TOP_K):
        y_hi, y_lo = _unpack_pairs(y_ref[k])
        acc_hi = acc_hi + gates[:, k:k + 1] * y_hi
        acc_lo = acc_lo + gates[:, k:k + 1] * y_lo
    o_ref[:, 0:D_PACK] = xb_ref[:, 0:D_PACK] + g2[:, 0:D_PACK] * acc_hi
    o_ref[:, D_PACK:D_MODEL] = xb_ref[:, D_PACK:D_MODEL] + g2[:, D_PACK:D_MODEL] * acc_lo


def _combine_call(gates, xbase, mod, ygath, seq_row_c):
    n = xbase.shape[0]
    tm = COMBINE_TILE
    return pl.pallas_call(
        _combine_kernel,
        grid=(n // tm,),
        in_specs=[pl.BlockSpec((tm, LANES), lambda i: (i, 0)),
                  pl.BlockSpec((tm, D_MODEL), lambda i: (i, 0)),
                  pl.BlockSpec((1, 1, 6 * D_MODEL), lambda i: (seq_row_c(i), 0, 0)),
                  pl.BlockSpec((TOP_K, tm, D_PACK), lambda i: (0, i, 0))],
        out_specs=pl.BlockSpec((tm, D_MODEL), lambda i: (i, 0)),
        out_shape=jax.ShapeDtypeStruct((n, D_MODEL), f32),
        compiler_params=_cparams(("parallel",)),
        name="combine",
    )(gates, xbase, mod, ygath)


def _block_avg(width, group):
    r = np.arange(width)
    return jnp.asarray((r[:, None] // group == r[None, :] // group).astype(np.float32) / group, dtype=bf16)


def _kv_replicate():
    c = np.arange(ATTN_W)
    src = (c // (ATTN_W // N_KV_HEADS)) * HEAD_DIM + c % HEAD_DIM
    return jnp.asarray((np.arange(KV_W)[:, None] == src[None, :]).astype(np.float32), dtype=bf16)


def _rope_lane_tables(n_tok):
    rows = n_tok // GRID_W
    r = jnp.repeat(jnp.arange(rows, dtype=f32), GRID_W)
    col = jnp.tile(jnp.arange(GRID_W, dtype=f32), rows)
    inv = ROPE_THETA ** (-jnp.arange(ROPE_FREQ, dtype=f32) / ROPE_FREQ)
    ar = r[:, None] * inv
    ac = col[:, None] * inv
    cos_h = jnp.concatenate([jnp.cos(ar), jnp.cos(ar), jnp.cos(ac), jnp.cos(ac)], axis=-1)
    sin_h = jnp.concatenate([-jnp.sin(ar), jnp.sin(ar), -jnp.sin(ac), jnp.sin(ac)], axis=-1)
    return jnp.tile(cos_h, (1, N_HEADS)), jnp.tile(sin_h, (1, N_HEADS))


def _block_diag(w):
    nb, bw, _ = w.shape
    eye = jnp.eye(nb, dtype=w.dtype)
    return (w[:, :, None, :] * eye[:, None, :, None]).reshape(nb * bw, nb * bw)


def _gla_state_in(s):
    bsz = s.shape[0]
    eye = jnp.eye(GLA_H, dtype=s.dtype)
    st = jnp.swapaxes(s, -1, -2)
    big = st[:, :, :, :, None, :] * eye[None, None, :, None, :, None]
    return big.reshape(bsz, 2, GLA_W, GLA_W)


def _gla_state_out(st):
    bsz = st.shape[0]
    s6 = st.reshape(bsz, 2, GLA_H, GLA_DK, GLA_H, GLA_DK)
    diag = jnp.stack([s6[:, :, h, :, h, :] for h in range(GLA_H)], axis=2)
    return jnp.swapaxes(diag, -1, -2)


def kernel(x_prompt, x_sample, cache_k, cache_v, state_lru, state_gla, c, c_ctx, ada_w, ada_b, norm1_g, norm2_g, w_in, q_norm_g, k_norm_g, attn_out_g, conv_w, conv_b, lru_wa, lru_ba, lru_wi, lru_bi, lru_lambda, lru_out_g, gla_wa2, gla_ba, gla_out_g, w_out, router_w, router_b, exp_w1, exp_w3, exp_w2, sh_w1, sh_w3, sh_w2):
    bc, tc, _ = x_prompt.shape
    bl, tl, _ = x_sample.shape
    depth = w_in.shape[0]
    nc = bc * tc
    n = nc + bl * tl
    past = cache_k.shape[2]
    assert tc == ROW_TILE and tl % ROW_TILE == 0 and nc % tl == 0 and bl + 1 <= SUBLANES
    assert n % ROW_TILE == 0 and n % COMBINE_TILE == 0

    def seq_row_for(tile):
        def seq_row(i):
            return jnp.where(i < nc // tile, 0, 1 + (i - nc // tile) // (tl // tile))
        return seq_row

    seq_row = seq_row_for(ROW_TILE)
    seq_row_c = seq_row_for(COMBINE_TILE)

    x = jnp.concatenate([x_prompt.reshape(nc, D_MODEL), x_sample.reshape(bl * tl, D_MODEL)], axis=0)
    cond = jnp.zeros((SUBLANES, D_MODEL), f32).at[0].set(c_ctx).at[1:1 + bl].set(c)
    mods = _ada_call(cond, ada_w, ada_b)

    bmq = _block_avg(ATTN_W, HEAD_DIM)
    bmk = _block_avg(KV_W, HEAD_DIM)
    bmg = _block_avg(GLA_W, GLA_DK)
    rep = _kv_replicate()
    cos_t, sin_t = _rope_lane_tables(tl)
    n_slots = -(-(n * TOP_K + N_EXPERTS * (SLOT_BLOCK - 1)) // SLOT_BLOCK) * SLOT_BLOCK
    tile8 = lambda v: jnp.tile(v, N_HEADS)[None, :]

    ks, vs, lrus, glas = [], [], [], []
    for l in range(depth):
        mod = mods[l].reshape(SUBLANES, 1, 6 * D_MODEL)
        w_in_p = jnp.pad(w_in[l].astype(bf16), ((0, 0), (0, LANES - 2 * GLA_RANK)))
        pa, pl_, pg = _proj_call(x, mod, norm1_g[l][None, :], w_in_p, seq_row)

        qg, kg, og = tile8(q_norm_g[l]), jnp.tile(k_norm_g[l], N_KV_HEADS)[None, :], attn_out_g[l][None, :]
        attn_c, k_new, v_new = _attn_ctx_call(pa, bc, tc, qg, kg, og, bmq, bmk, rep)
        attn_l = _attn_lat_call(pa, nc, bl, tl, cache_k[:, l].reshape(bl, past, KV_W),
                                cache_v[:, l].reshape(bl, past, KV_W), cos_t, sin_t, qg, kg, og, bmq, bmk, rep)
        ks.append(k_new.reshape(bc, tc, N_KV_HEADS, HEAD_DIM))
        vs.append(v_new.reshape(bc, tc, N_KV_HEADS, HEAD_DIM))

        wg = jnp.concatenate([_block_diag(lru_wa[l, 0]), _block_diag(lru_wi[l, 0]),
                              _block_diag(lru_wa[l, 1]), _block_diag(lru_wi[l, 1])], axis=1).astype(bf16)
        bg = jnp.concatenate([lru_ba[l, 0], lru_bi[l, 0], lru_ba[l, 1], lru_bi[l, 1]])[None, :]
        lru_args = (conv_w[l], conv_b[l][None, :], wg, bg, lru_lambda[l], lru_out_g[l][None, :])
        lru_c, lru_fin = _lru_call(pl_, 0, bc, tc, jnp.zeros((bc, 2, LRU_W), f32), *lru_args)
        lru_l, _ = _lru_call(pl_, nc, bl, tl, state_lru[:, l], *lru_args)
        lrus.append(lru_fin)

        wa = jnp.zeros((LANES, 2 * GLA_W), f32)
        wa = wa.at[0:GLA_RANK, 0:GLA_W].set(gla_wa2[l, 0]).at[GLA_RANK:2 * GLA_RANK, GLA_W:].set(gla_wa2[l, 1])
        gla_args = (wa.astype(bf16), gla_ba[l].reshape(1, 2 * GLA_W), gla_out_g[l].reshape(1, GLA_W), bmg)
        gla_c, gla_fin = _gla_call(pg, 0, bc, tc, None, *gla_args)
        gla_l, _ = _gla_call(pg, nc, bl, tl, _gla_state_in(state_gla[:, l]), *gla_args)
        glas.append(_gla_state_out(gla_fin))

        rw_hi = router_w[l].astype(bf16)
        rw_lo = (router_w[l] - rw_hi.astype(f32)).astype(bf16)
        xbase, h2, scores = _out_call(x, (attn_c, lru_c, gla_c), (attn_l, lru_l, gla_l), mod, w_out[l].astype(bf16),
                                      norm2_g[l][None, :], rw_hi, rw_lo, sh_w1[l].astype(bf16),
                                      sh_w3[l].astype(bf16), sh_w2[l].astype(bf16), seq_row)

        idx, gates, rank, counts = _route_call(scores, router_b[l][None, :])
        dest = _dest_call(idx, rank, counts)[:, :TOP_K]
        cnt = counts[0].astype(i32)
        padded = (cnt + SLOT_BLOCK - 1) // SLOT_BLOCK * SLOT_BLOCK
        padded_end = jnp.cumsum(padded)
        nb = n_slots // SLOT_BLOCK
        blk_first = jnp.arange(nb, dtype=i32) * SLOT_BLOCK
        blk_e = jnp.minimum(jnp.sum((padded_end[None, :] <= blk_first[:, None]).astype(i32), axis=1), N_EXPERTS - 1)
        blk_valid = jnp.clip((padded_end - padded + cnt)[blk_e] - blk_first, 0, SLOT_BLOCK)
        n_used = padded_end[-1:] // SLOT_BLOCK
        dest_km = dest.T
        slots = _scatter_rows_call(h2, dest_km, n_slots)
        slots = _expert_call(blk_e, n_used, blk_valid, slots, exp_w1, exp_w3, exp_w2, l)
        ygath = _gather_rows_call(slots, dest_km.reshape(-1)).reshape(TOP_K, n, D_PACK)
        x = _combine_call(gates, xbase, mod, ygath, seq_row_c)

    y_prompt = x[:nc].reshape(bc, tc, D_MODEL)
    y_sample = x[nc:].reshape(bl, tl, D_MODEL)
    return (y_prompt, y_sample, jnp.stack(ks, axis=1), jnp.stack(vs, axis=1),
            jnp.stack(lrus, axis=1), jnp.stack(glas, axis=1))
```

```python
import functools

import jax
import jax.numpy as jnp
import numpy as np
from jax import lax
from jax.experimental import pallas as pl
from jax.experimental.pallas import tpu as pltpu
from jax.experimental.pallas import tpu_sc as plsc

f32 = jnp.float32
bf16 = jnp.bfloat16
i32 = jnp.int32

D_MODEL = 1024
N_HEADS = 8
N_KV_HEADS = 2
HEAD_DIM = 64
ATTN_W = N_HEADS * HEAD_DIM
KV_W = N_KV_HEADS * HEAD_DIM
GRID_W = 64
ROPE_FREQ = HEAD_DIM // 4
ROPE_THETA = 10000.0
LRU_W = 256
LRU_BLOCKS = 4
LRU_C = 8.0
CONV_W = 4
CONV_LEFT = 2
GLA_H = 4
GLA_DK = 64
GLA_W = 256
GLA_RANK = 16
GLA_TAU = 16.0
N_EXPERTS = 256
TOP_K = 8
EXPERT_FF = 256
ROUTED_SCALE = 2.5
EPS = 1e-6

LANES = 128
SUBLANES = 8
ROW_TILE = 256
GLA_BLOCK = 256
SLOT_BLOCK = 256
W_RING = 3
COMBINE_TILE = 128
SC_CHUNK = 64
D_PACK = D_MODEL // 2
VMEM_LIMIT = 56 * 1024 * 1024


def _cparams(sem, vmem=VMEM_LIMIT):
    return pltpu.CompilerParams(dimension_semantics=sem, vmem_limit_bytes=vmem)


def _dot(a, b):
    return jnp.dot(a, b, preferred_element_type=f32)


def _dot_nt(a, b):
    return lax.dot_general(a, b, (((1,), (1,)), ((), ())), preferred_element_type=f32)


def _dot_tn(a, b):
    return lax.dot_general(a, b, (((0,), (0,)), ((), ())), preferred_element_type=f32)


def _split(x):
    hi = x.astype(bf16)
    lo = (x - hi.astype(f32)).astype(bf16)
    return hi, lo


def _dot_x2(x, w):
    hi, lo = _split(x)
    return _dot(hi, w) + _dot(lo, w)


def _dot_2x(m, x):
    hi, lo = _split(x)
    return _dot(m, hi) + _dot(m, lo)


def _dot3(a, b_hi, b_lo):
    a_hi, a_lo = _split(a)
    return _dot(a_hi, b_hi) + _dot(a_lo, b_hi) + _dot(a_hi, b_lo)


def _sigmoid(x):
    return 1.0 / (1.0 + jnp.exp(-x))


def _silu(x):
    return x * _sigmoid(x)


def _softplus(x):
    return jnp.maximum(x, 0.0) + jnp.log(1.0 + jnp.exp(-jnp.abs(x)))


def _gelu_tanh(x):
    return 0.5 * x * (1.0 + jnp.tanh(0.7978845608028654 * (x + 0.044715 * x * x * x)))


def _rms(x, g):
    return x * lax.rsqrt(jnp.mean(x * x, axis=-1, keepdims=True) + EPS) * g


def _pack_pairs(x):
    c = x.shape[1] // 2
    hi = lax.bitcast_convert_type(x[:, :c].astype(bf16).astype(f32), i32)
    lo = lax.bitcast_convert_type(x[:, c:].astype(bf16).astype(f32), i32)
    return hi | lax.shift_right_logical(lo, 16)


def _unpack_pairs(w):
    hi = lax.bitcast_convert_type(w & jnp.int32(-65536), f32)
    lo = lax.bitcast_convert_type(w << 16, f32)
    return hi, lo


def _head_rms(x, g, bm):
    ms = _dot_x2(x * x, bm)
    return x * lax.rsqrt(ms + EPS) * g


def _ada_kernel(c_ref, w_ref, b_ref, o_ref):
    s = _silu(c_ref[...])
    w = w_ref[0]
    w_hi, w_lo = _split(w)
    o_ref[0] = _dot3(s, w_hi, w_lo) + b_ref[0]


def _ada_call(cond, ada_w, ada_b):
    depth = ada_w.shape[0]
    nt = 1536
    return pl.pallas_call(
        _ada_kernel,
        grid=(depth, 6 * D_MODEL // nt),
        in_specs=[pl.BlockSpec((SUBLANES, D_MODEL), lambda l, j: (0, 0)),
                  pl.BlockSpec((1, D_MODEL, nt), lambda l, j: (l, 0, j)),
                  pl.BlockSpec((1, 1, nt), lambda l, j: (l, 0, j))],
        out_specs=pl.BlockSpec((1, SUBLANES, nt), lambda l, j: (l, 0, j)),
        out_shape=jax.ShapeDtypeStruct((depth, SUBLANES, 6 * D_MODEL), f32),
        compiler_params=_cparams(("parallel", "parallel")),
        name="ada_mod",
    )(cond, ada_w, ada_b.reshape(depth, 1, 6 * D_MODEL))


PROJ_A = ATTN_W + 2 * KV_W
PROJ_L = 2 * LRU_W
PROJ_G = 4 * GLA_W + LANES


def _proj_kernel(x_ref, mod_ref, g_ref, w_ref, oa_ref, ol_ref, og_ref):
    mod = mod_ref[0]
    sh = mod[:, 0:D_MODEL]
    sc = mod[:, D_MODEL:2 * D_MODEL]
    h = (_rms(x_ref[...], g_ref[...]) * (1.0 + sc) + sh).astype(bf16)
    p = _dot(h, w_ref[...])
    oa_ref[...] = p[:, 0:PROJ_A]
    ol_ref[...] = p[:, PROJ_A:PROJ_A + PROJ_L]
    og_ref[...] = p[:, PROJ_A + PROJ_L:PROJ_A + PROJ_L + PROJ_G]


def _proj_call(x, mod, g, w, seq_row):
    n = x.shape[0]
    cols = PROJ_A + PROJ_L + PROJ_G
    return pl.pallas_call(
        _proj_kernel,
        grid=(n // ROW_TILE,),
        in_specs=[pl.BlockSpec((ROW_TILE, D_MODEL), lambda i: (i, 0)),
                  pl.BlockSpec((1, 1, 6 * D_MODEL), lambda i: (seq_row(i), 0, 0)),
                  pl.BlockSpec((1, D_MODEL), lambda i: (0, 0)),
                  pl.BlockSpec((D_MODEL, cols), lambda i: (0, 0))],
        out_specs=[pl.BlockSpec((ROW_TILE, PROJ_A), lambda i: (i, 0)),
                   pl.BlockSpec((ROW_TILE, PROJ_L), lambda i: (i, 0)),
                   pl.BlockSpec((ROW_TILE, PROJ_G), lambda i: (i, 0))],
        out_shape=[jax.ShapeDtypeStruct((n, PROJ_A), f32),
                   jax.ShapeDtypeStruct((n, PROJ_L), f32),
                   jax.ShapeDtypeStruct((n, PROJ_G), f32)],
        compiler_params=_cparams(("parallel",)),
        name="in_proj",
    )(x, mod, g, w)


def _rope(x, cos_t, sin_t):
    w = x.shape[1]
    up = pltpu.roll(x, w - ROPE_FREQ, 1)
    dn = pltpu.roll(x, ROPE_FREQ, 1)
    lane = lax.broadcasted_iota(i32, x.shape, 1)
    partner = jnp.where((lane & (2 * ROPE_FREQ - 1)) < ROPE_FREQ, up, dn)
    return x * cos_t + partner * sin_t


def _attend(q, kk_ref, vv_ref, o_ref):
    tq = q.shape[0]
    gw = ATTN_W // N_KV_HEADS
    lane = lax.broadcasted_iota(i32, (tq, gw), 1)
    for g in range(N_KV_HEADS):
        qg = q[:, g * gw:(g + 1) * gw]
        kg = kk_ref[:, g * gw:(g + 1) * gw]
        vg = vv_ref[:, g * gw:(g + 1) * gw]
        acc = jnp.zeros((tq, gw), f32)
        for hh in range(N_HEADS // N_KV_HEADS):
            hm = (lane >> 6) == hh
            s = _dot_nt(jnp.where(hm, qg, 0.0).astype(bf16), kg)
            m = jnp.max(s, axis=-1, keepdims=True)
            p = jnp.exp(s - m)
            l = jnp.sum(p, axis=-1, keepdims=True)
            o = _dot(p.astype(bf16), vg) / l
            acc = jnp.where(hm, o, acc)
        o_ref[:, g * gw:(g + 1) * gw] = acc


def _attn_ctx_kernel(p_ref, qg_ref, kg_ref, og_ref, bmq_ref, bmk_ref, rep_ref,
                     o_ref, ko_ref, vo_ref, kk_s, vv_s, o_s):
    p = p_ref[...]
    q = _head_rms(p[:, 0:ATTN_W], qg_ref[...], bmq_ref[...])
    k = _head_rms(p[:, ATTN_W:ATTN_W + KV_W], kg_ref[...], bmk_ref[...])
    v = p[:, ATTN_W + KV_W:PROJ_A]
    ko_ref[...] = k
    vo_ref[...] = v
    kk_s[...] = _dot(k.astype(bf16), rep_ref[...]).astype(bf16)
    vv_s[...] = _dot(v.astype(bf16), rep_ref[...]).astype(bf16)
    _attend(q * HEAD_DIM ** -0.5, kk_s, vv_s, o_s)
    o_ref[...] = _rms(o_s[...], og_ref[...])


def _attn_ctx_call(pa, n_seq, t, qg, kg, og, bmq, bmk, rep):
    assert t == ROW_TILE
    const = lambda i: (0, 0)
    return pl.pallas_call(
        _attn_ctx_kernel,
        grid=(n_seq,),
        in_specs=[pl.BlockSpec((t, PROJ_A), lambda i: (i, 0)),
                  pl.BlockSpec((1, ATTN_W), const), pl.BlockSpec((1, KV_W), const), pl.BlockSpec((1, ATTN_W), const),
                  pl.BlockSpec((ATTN_W, ATTN_W), const), pl.BlockSpec((KV_W, KV_W), const),
                  pl.BlockSpec((KV_W, ATTN_W), const)],
        out_specs=[pl.BlockSpec((t, ATTN_W), lambda i: (i, 0)),
                   pl.BlockSpec((t, KV_W), lambda i: (i, 0)),
                   pl.BlockSpec((t, KV_W), lambda i: (i, 0))],
        out_shape=[jax.ShapeDtypeStruct((n_seq * t, ATTN_W), f32),
                   jax.ShapeDtypeStruct((n_seq * t, KV_W), f32),
                   jax.ShapeDtypeStruct((n_seq * t, KV_W), f32)],
        scratch_shapes=[pltpu.VMEM((t, ATTN_W), bf16), pltpu.VMEM((t, ATTN_W), bf16), pltpu.VMEM((t, ATTN_W), f32)],
        compiler_params=_cparams(("parallel",)),
        name="attn_ctx",
    )(pa, qg, kg, og, bmq, bmk, rep)


def _attn_lat_kernel(q_ref, kv_ref, ck_ref, cv_ref, cq_ref, sq_ref, ckk_ref, skk_ref,
                     qg_ref, kg_ref, og_ref, bmq_ref, bmk_ref, rep_ref,
                     o_ref, kk_s, vv_s, o_s, *, past):
    @pl.when(pl.program_id(1) == 0)
    def _():
        kv = kv_ref[...]
        k = _head_rms(kv[:, 0:KV_W], kg_ref[...], bmk_ref[...])
        k = _rope(k, ckk_ref[...], skk_ref[...])
        v = kv[:, KV_W:2 * KV_W]
        kk_s[0:past, :] = _dot(ck_ref[0].astype(bf16), rep_ref[...]).astype(bf16)
        vv_s[0:past, :] = _dot(cv_ref[0].astype(bf16), rep_ref[...]).astype(bf16)
        kk_s[past:, :] = _dot(k.astype(bf16), rep_ref[...]).astype(bf16)
        vv_s[past:, :] = _dot(v.astype(bf16), rep_ref[...]).astype(bf16)

    q = _head_rms(q_ref[...], qg_ref[...], bmq_ref[...])
    q = _rope(q, cq_ref[...], sq_ref[...])
    _attend(q * HEAD_DIM ** -0.5, kk_s, vv_s, o_s)
    o_ref[...] = _rms(o_s[...], og_ref[...])


def _attn_lat_call(pa, row0, n_seq, t, cache_k, cache_v, cq, sq, qg, kg, og, bmq, bmk, rep):
    past = cache_k.shape[1]
    nq = t // ROW_TILE
    assert row0 % t == 0 and t % ROW_TILE == 0
    qb0 = row0 // ROW_TILE
    sb0 = row0 // t
    const = lambda b, j: (0, 0)
    return pl.pallas_call(
        functools.partial(_attn_lat_kernel, past=past),
        grid=(n_seq, nq),
        in_specs=[pl.BlockSpec((ROW_TILE, ATTN_W), lambda b, j: (qb0 + b * nq + j, 0)),
                  pl.BlockSpec((t, 2 * KV_W), lambda b, j: (sb0 + b, ATTN_W // (2 * KV_W))),
                  pl.BlockSpec((1, past, KV_W), lambda b, j: (b, 0, 0)),
                  pl.BlockSpec((1, past, KV_W), lambda b, j: (b, 0, 0)),
                  pl.BlockSpec((ROW_TILE, ATTN_W), lambda b, j: (j, 0)),
                  pl.BlockSpec((ROW_TILE, ATTN_W), lambda b, j: (j, 0)),
                  pl.BlockSpec((t, KV_W), const), pl.BlockSpec((t, KV_W), const),
                  pl.BlockSpec((1, ATTN_W), const), pl.BlockSpec((1, KV_W), const), pl.BlockSpec((1, ATTN_W), const),
                  pl.BlockSpec((ATTN_W, ATTN_W), const), pl.BlockSpec((KV_W, KV_W), const),
                  pl.BlockSpec((KV_W, ATTN_W), const)],
        out_specs=pl.BlockSpec((ROW_TILE, ATTN_W), lambda b, j: (b * nq + j, 0)),
        out_shape=jax.ShapeDtypeStruct((n_seq * t, ATTN_W), f32),
        scratch_shapes=[pltpu.VMEM((past + t, ATTN_W), bf16), pltpu.VMEM((past + t, ATTN_W), bf16),
                        pltpu.VMEM((ROW_TILE, ATTN_W), f32)],
        compiler_params=_cparams(("parallel", "arbitrary")),
        name="attn_lat",
    )(pa, pa, cache_k, cache_v, cq, sq, cq, sq, qg, kg, og, bmq, bmk, rep)


def _lru_kernel(x_ref, h0_ref, cw_ref, cb_ref, wg_ref, bg_ref, lam_ref, g_ref,
                o_ref, fin_ref, xs, a_s, u_s, h_s, *, t):
    ng = t // SUBLANES
    lx = x_ref[:, 0:LRU_W]
    ly = x_ref[:, LRU_W:2 * LRU_W]
    xs[0:SUBLANES, :] = jnp.zeros((SUBLANES, LRU_W), f32)
    xs[SUBLANES + t:2 * SUBLANES + t, :] = jnp.zeros((SUBLANES, LRU_W), f32)
    xs[SUBLANES:SUBLANES + t, :] = lx
    xc = jnp.broadcast_to(cb_ref[...], (t, LRU_W))
    for j in range(CONV_W):
        off = SUBLANES + j - CONV_LEFT
        xc = xc + xs[off:off + t, :] * cw_ref[j:j + 1, :]
    gates = _sigmoid(_dot(xc.astype(bf16), wg_ref[...]) + bg_ref[...])
    sp = _softplus(-lam_ref[...])
    sub = lax.broadcasted_iota(i32, (t, LRU_W), 0) & (SUBLANES - 1)
    for d in range(2):
        r = gates[:, (2 * d) * LRU_W:(2 * d + 1) * LRU_W]
        gi = gates[:, (2 * d + 1) * LRU_W:(2 * d + 2) * LRU_W]
        log_a = -LRU_C * r * sp[d:d + 1, :]
        a = jnp.exp(log_a)
        th = jnp.tanh(log_a)
        u = jnp.sqrt(-2.0 * th / (1.0 - th)) * (gi * xc)
        for s in (1, 2, 4):
            if d == 0:
                a_sh = pltpu.roll(a, s, 0)
                u_sh = pltpu.roll(u, s, 0)
                ok = sub >= s
            else:
                a_sh = pltpu.roll(a, t - s, 0)
                u_sh = pltpu.roll(u, t - s, 0)
                ok = sub < SUBLANES - s
            u = jnp.where(ok, a * u_sh + u, u)
            a = jnp.where(ok, a * a_sh, a)
        a_s[...] = a
        u_s[...] = u
        h0 = jnp.broadcast_to(h0_ref[0, d:d + 1, :], (SUBLANES, LRU_W))
        edge = SUBLANES - 1 if d == 0 else 0

        def body(i, carry, d=d, edge=edge):
            g = i if d == 0 else ng - 1 - i
            rows = pl.ds(pl.multiple_of(g * SUBLANES, SUBLANES), SUBLANES)
            h = a_s[rows, :] * carry + u_s[rows, :]
            if d == 0:
                h_s[rows, :] = h
            else:
                h_s[rows, :] = h_s[rows, :] + h
            return jnp.broadcast_to(h[edge:edge + 1, :], (SUBLANES, LRU_W))

        last = lax.fori_loop(0, ng, body, h0)
        fin_ref[0, d:d + 1, :] = last[0:1, :]
    o_ref[...] = _rms(h_s[...] * _gelu_tanh(ly), g_ref[...])


def _lru_call(pl_, row0, n_seq, t, h0, cw, cb, wg, bg, lam, g):
    sb0 = row0 // t
    assert row0 % t == 0
    const = lambda b: (0, 0)
    in_specs = [pl.BlockSpec((t, PROJ_L), lambda b: (sb0 + b, 0)),
                pl.BlockSpec((1, 2, LRU_W), lambda b: (b, 0, 0)),
                pl.BlockSpec((CONV_W, LRU_W), const), pl.BlockSpec((1, LRU_W), const),
                pl.BlockSpec((LRU_W, 4 * LRU_W), const), pl.BlockSpec((1, 4 * LRU_W), const),
                pl.BlockSpec((2, LRU_W), const), pl.BlockSpec((1, LRU_W), const)]
    args = [pl_, h0, cw, cb, wg, bg, lam, g]
    return pl.pallas_call(
        functools.partial(_lru_kernel, t=t),
        grid=(n_seq,),
        in_specs=in_specs,
        out_specs=[pl.BlockSpec((t, LRU_W), lambda b: (b, 0)),
                   pl.BlockSpec((1, 2, LRU_W), lambda b: (b, 0, 0))],
        out_shape=[jax.ShapeDtypeStruct((n_seq * t, LRU_W), f32), jax.ShapeDtypeStruct((n_seq, 2, LRU_W), f32)],
        scratch_shapes=[pltpu.VMEM((t + 2 * SUBLANES, LRU_W), f32), pltpu.VMEM((t, LRU_W), f32),
                        pltpu.VMEM((t, LRU_W), f32), pltpu.VMEM((t, LRU_W), f32)],
        compiler_params=_cparams(("parallel",)),
        name="rglru_t%d" % t,
    )(*args)


def _bcast_rows(b, period, off):
    w = b.shape[1]
    return jnp.concatenate(
        [jnp.broadcast_to(b[i * period + off:i * period + off + 1, :], (period, w)) for i in range(b.shape[0] // period)],
        axis=0)


def _gla_block(q, k, v, la, reverse, st_ref):
    n = GLA_BLOCK
    row = lax.broadcasted_iota(i32, (n, n), 0)
    col = lax.broadcasted_iota(i32, (n, n), 1)
    same64 = (row >> 6) == (col >> 6)
    same32 = (row >> 5) == (col >> 5)
    same16 = (row >> 4) == (col >> 4)
    if not reverse:
        cum = (same64 & (col <= row)).astype(bf16)
        m1 = same64 & ((row & 63) >= 32) & ((col & 63) < 32)
        m2 = same32 & ((row & 31) >= 16) & ((col & 31) < 16)
        m3 = same16 & (col <= row)
        offs = (31, 15, 7, 63)
    else:
        cum = (same64 & (col >= row)).astype(bf16)
        m1 = same64 & ((row & 63) < 32) & ((col & 63) >= 32)
        m2 = same32 & ((row & 31) < 16) & ((col & 31) >= 16)
        m3 = same16 & (col >= row)
        offs = (32, 16, 8, 0)
    b = _dot_2x(cum, la)
    r1 = _bcast_rows(b, 64, offs[0])
    r2 = _bcast_rows(b, 32, offs[1])
    r3 = _bcast_rows(b, 16, offs[2])
    bl = _bcast_rows(b, 64, offs[3])
    q1 = q * jnp.exp(jnp.minimum(b - r1, 0.0))
    k1 = (k * jnp.exp(jnp.minimum(r1 - b, 0.0))).astype(bf16)
    q2 = q * jnp.exp(jnp.minimum(b - r2, 0.0))
    k2 = (k * jnp.exp(jnp.minimum(r2 - b, 0.0))).astype(bf16)
    q3 = q * jnp.exp(b - r3)
    k3 = (k * jnp.exp(r3 - b)).astype(bf16)
    qe = (q * jnp.exp(b)).astype(bf16)
    kl = (k * jnp.exp(bl - b)).astype(bf16)
    dec = jnp.exp(bl)
    vb = v.astype(bf16)
    zq = jnp.zeros((n, n), f32)
    intra = jnp.zeros((n, n), f32)
    for h in range(GLA_H):
        hm = (col >> 6) == h
        a1 = _dot_nt(jnp.where(hm, q1, zq).astype(bf16), k1)
        a2 = _dot_nt(jnp.where(hm, q2, zq).astype(bf16), k2)
        a3 = _dot_nt(jnp.where(hm, q3, zq).astype(bf16), k3)
        att = jnp.where(m1, a1, jnp.where(m2, a2, jnp.where(m3, a3, zq)))
        intra = intra + _dot(att.astype(bf16), jnp.where(hm, v, zq).astype(bf16))
    outs = [None] * 4
    for c in (range(4) if not reverse else range(3, -1, -1)):
        rs = slice(64 * c, 64 * c + 64)
        st = st_ref[...]
        inter = _dot_nt(qe[rs], st.astype(bf16))
        kv = _dot_tn(vb[rs], kl[rs])
        drow = dec[64 * c:64 * c + 1, :]
        st_ref[...] = st * drow + jnp.where(same64, kv, zq)
        outs[c] = intra[rs] + inter
    return jnp.concatenate(outs, axis=0)


def _gla_kernel(x_ref, s0_ref, wa_ref, ba_ref, g_ref, bm_ref, o_ref, fin_ref, st_s, o_s, *, t):
    nblk = t // GLA_BLOCK

    def block(j, reverse):
        rows = pl.ds(j * GLA_BLOCK if isinstance(j, int) else pl.multiple_of(j * GLA_BLOCK, GLA_BLOCK), GLA_BLOCK)
        q = x_ref[rows, 0:GLA_W] * GLA_DK ** -0.5
        k = x_ref[rows, GLA_W:2 * GLA_W]
        v = x_ref[rows, 2 * GLA_W:3 * GLA_W]
        ga = x_ref[rows, 4 * GLA_W:4 * GLA_W + LANES]
        d = 1 if reverse else 0
        z = _dot(ga.astype(bf16), wa_ref[:, d * GLA_W:(d + 1) * GLA_W]) + ba_ref[:, d * GLA_W:(d + 1) * GLA_W]
        la = -_softplus(-z) * (1.0 / GLA_TAU)
        o = _gla_block(q, k, v, la, reverse, st_s)
        if reverse:
            o_s[rows, :] = o_s[rows, :] + o
        else:
            o_s[rows, :] = o

    for d in range(2):
        if s0_ref is None:
            st_s[...] = jnp.zeros((GLA_W, GLA_W), f32)
        else:
            st_s[...] = s0_ref[0, d]
        if nblk == 1:
            block(0, d == 1)
        else:
            def body(i, carry, d=d):
                block(i if d == 0 else nblk - 1 - i, d == 1)
                return carry
            lax.fori_loop(0, nblk, body, 0)
        fin_ref[0, d] = st_s[...]
    gg = x_ref[:, 3 * GLA_W:4 * GLA_W]
    o_ref[...] = _head_rms(o_s[...], g_ref[...], bm_ref[...]) * _silu(gg)


def _gla_call(pg, row0, n_seq, t, s0, wa, ba, g, bm):
    sb0 = row0 // t
    assert row0 % t == 0 and t % GLA_BLOCK == 0
    const = lambda b: (0, 0)
    in_specs = [pl.BlockSpec((t, PROJ_G), lambda b: (sb0 + b, 0))]
    args = [pg]
    if s0 is not None:
        in_specs.append(pl.BlockSpec((1, 2, GLA_W, GLA_W), lambda b: (b, 0, 0, 0)))
        args.append(s0)
    in_specs += [pl.BlockSpec((LANES, 2 * GLA_W), const), pl.BlockSpec((1, 2 * GLA_W), const),
                 pl.BlockSpec((1, GLA_W), const), pl.BlockSpec((GLA_W, GLA_W), const)]
    args += [wa, ba, g, bm]

    def kern(*refs):
        refs = list(refs)
        x_ref = refs.pop(0)
        s0_ref = refs.pop(0) if s0 is not None else None
        wa_ref, ba_ref, g_ref, bm_ref = refs[:4]
        _gla_kernel(x_ref, s0_ref, wa_ref, ba_ref, g_ref, bm_ref, *refs[4:], t=t)

    return pl.pallas_call(
        kern,
        grid=(n_seq,),
        in_specs=in_specs,
        out_specs=[pl.BlockSpec((t, GLA_W), lambda b: (b, 0)),
                   pl.BlockSpec((1, 2, GLA_W, GLA_W), lambda b: (b, 0, 0, 0))],
        out_shape=[jax.ShapeDtypeStruct((n_seq * t, GLA_W), f32), jax.ShapeDtypeStruct((n_seq, 2, GLA_W, GLA_W), f32)],
        scratch_shapes=[pltpu.VMEM((GLA_W, GLA_W), f32), pltpu.VMEM((t, GLA_W), f32)],
        compiler_params=_cparams(("parallel",)),
        name="gla_t%d" % t,
    )(*args)


def _out_kernel(x_ref, ac_ref, lc_ref, gc_ref, al_ref, ll_ref, gl_ref, mod_ref, wo_ref, g2_ref, rwh_ref, rwl_ref,
                s1_ref, s3_ref, s2_ref, xb_ref, h2_ref, sc_ref, *, n_ctx_tiles):
    is_ctx = pl.program_id(0) < n_ctx_tiles
    pick = lambda c_ref, l_ref: jnp.where(is_ctx, c_ref[...], l_ref[...]).astype(bf16)
    mod = mod_ref[0]
    g1 = mod[:, 2 * D_MODEL:3 * D_MODEL]
    sh2 = mod[:, 3 * D_MODEL:4 * D_MODEL]
    sc2 = mod[:, 4 * D_MODEL:5 * D_MODEL]
    g2 = mod[:, 5 * D_MODEL:6 * D_MODEL]
    m = (_dot(pick(ac_ref, al_ref), wo_ref[0:ATTN_W, :])
         + _dot(pick(lc_ref, ll_ref), wo_ref[ATTN_W:ATTN_W + LRU_W, :])
         + _dot(pick(gc_ref, gl_ref), wo_ref[ATTN_W + LRU_W:ATTN_W + LRU_W + GLA_W, :]))
    x1 = x_ref[...] + g1 * m
    h2 = _rms(x1, g2_ref[...]) * (1.0 + sc2) + sh2
    h2_ref[...] = _pack_pairs(h2)
    sc_ref[...] = _sigmoid(_dot3(h2, rwh_ref[...], rwl_ref[...]))
    hb = h2.astype(bf16)
    act = (_silu(_dot(hb, s1_ref[...])) * _dot(hb, s3_ref[...])).astype(bf16)
    xb_ref[...] = x1 + g2 * _dot(act, s2_ref[...])


def _out_call(x, mix_ctx, mix_lat, mod, wo, g2, rwh, rwl, s1, s3, s2, seq_row):
    n = x.shape[0]
    nct = mix_ctx[0].shape[0] // ROW_TILE
    const = lambda i: (0, 0)
    rowb = lambda w: pl.BlockSpec((ROW_TILE, w), lambda i: (i, 0))
    ctxb = lambda w: pl.BlockSpec((ROW_TILE, w), lambda i: (jnp.minimum(i, nct - 1), 0))
    latb = lambda w: pl.BlockSpec((ROW_TILE, w), lambda i: (jnp.maximum(i - nct, 0), 0))
    return pl.pallas_call(
        functools.partial(_out_kernel, n_ctx_tiles=nct),
        grid=(n // ROW_TILE,),
        in_specs=[rowb(D_MODEL), ctxb(ATTN_W), ctxb(LRU_W), ctxb(GLA_W), latb(ATTN_W), latb(LRU_W), latb(GLA_W),
                  pl.BlockSpec((1, 1, 6 * D_MODEL), lambda i: (seq_row(i), 0, 0)),
                  pl.BlockSpec((D_MODEL, D_MODEL), const), pl.BlockSpec((1, D_MODEL), const),
                  pl.BlockSpec((D_MODEL, N_EXPERTS), const), pl.BlockSpec((D_MODEL, N_EXPERTS), const),
                  pl.BlockSpec((D_MODEL, EXPERT_FF), const), pl.BlockSpec((D_MODEL, EXPERT_FF), const),
                  pl.BlockSpec((EXPERT_FF, D_MODEL), const)],
        out_specs=[rowb(D_MODEL),
                   rowb(D_PACK),
                   rowb(N_EXPERTS)],
        out_shape=[jax.ShapeDtypeStruct((n, D_MODEL), f32),
                   jax.ShapeDtypeStruct((n, D_PACK), i32),
                   jax.ShapeDtypeStruct((n, N_EXPERTS), f32)],
        compiler_params=_cparams(("parallel",)),
        name="out_proj",
    )(x, *mix_ctx, *mix_lat, mod, wo, g2, rwh, rwl, s1, s3, s2)


def _route_kernel(sc_ref, rb_ref, idx_ref, gate_ref, rank_ref, cnt_ref, run_s):
    tm = sc_ref.shape[0]

    @pl.when(pl.program_id(0) == 0)
    def _():
        run_s[...] = jnp.zeros_like(run_s)

    scores = sc_ref[...]
    sel = scores + rb_ref[...]
    lane_f = lax.broadcasted_iota(i32, (tm, N_EXPERTS), 1).astype(f32)
    lane_o = lax.broadcasted_iota(i32, (tm, LANES), 1)
    neg = jnp.full((tm, N_EXPERTS), -jnp.inf, f32)
    hots = []
    idx_o = jnp.zeros((tm, LANES), i32)
    gate_o = jnp.zeros((tm, LANES), f32)
    gsum = jnp.zeros((tm, 1), f32)
    chosen = jnp.zeros((tm, N_EXPERTS), f32)
    for k in range(TOP_K):
        m = jnp.max(sel, axis=-1, keepdims=True)
        idx_f = jnp.min(jnp.where(sel == m, lane_f, float(N_EXPERTS)), axis=-1, keepdims=True)
        hot = lane_f == idx_f
        idx = idx_f.astype(i32)
        gk = jnp.sum(jnp.where(hot, scores, 0.0), axis=-1, keepdims=True)
        sel = jnp.where(hot, neg, sel)
        hots.append(hot)
        chosen = jnp.where(hot, 1.0, chosen)
        gsum = gsum + gk
        idx_o = jnp.where(lane_o == k, idx, idx_o)
        gate_o = jnp.where(lane_o == k, gk, gate_o)
    gate_ref[...] = gate_o / gsum * ROUTED_SCALE
    idx_ref[...] = idx_o
    r = lax.broadcasted_iota(i32, (tm, tm), 0)
    c = lax.broadcasted_iota(i32, (tm, tm), 1)
    pos = _dot((c < r).astype(bf16), chosen.astype(bf16)) + run_s[...]
    rank_o = jnp.zeros((tm, LANES), i32)
    for k in range(TOP_K):
        rk = jnp.sum(jnp.where(hots[k], pos, 0.0), axis=-1, keepdims=True)
        rank_o = jnp.where(lane_o == k, rk.astype(i32), rank_o)
    rank_ref[...] = rank_o
    run_s[...] = run_s[...] + jnp.sum(chosen, axis=0, keepdims=True)
    cnt_ref[...] = run_s[...]


def _route_call(scores, rb):
    n = scores.shape[0]
    rowb = lambda w: pl.BlockSpec((ROW_TILE, w), lambda i: (i, 0))
    return pl.pallas_call(
        _route_kernel,
        grid=(n // ROW_TILE,),
        in_specs=[rowb(N_EXPERTS), pl.BlockSpec((1, N_EXPERTS), lambda i: (0, 0))],
        out_specs=[rowb(LANES), rowb(LANES), rowb(LANES), pl.BlockSpec((1, N_EXPERTS), lambda i: (0, 0))],
        out_shape=[jax.ShapeDtypeStruct((n, LANES), i32), jax.ShapeDtypeStruct((n, LANES), f32),
                   jax.ShapeDtypeStruct((n, LANES), i32), jax.ShapeDtypeStruct((1, N_EXPERTS), f32)],
        scratch_shapes=[pltpu.VMEM((1, N_EXPERTS), f32)],
        compiler_params=_cparams(("arbitrary",)),
        name="route",
    )(scores, rb)


def _dest_kernel(idx_ref, rank_ref, cnt_ref, dest_ref):
    tm = idx_ref.shape[0]
    cnt = jnp.broadcast_to(cnt_ref[...], (SUBLANES, N_EXPERTS)).astype(i32)
    padded = ((cnt + (SLOT_BLOCK - 1)) // SLOT_BLOCK) * SLOT_BLOCK
    lane8 = lax.broadcasted_iota(i32, (SUBLANES, N_EXPERTS), 1)
    inc = padded
    s = 1
    while s < N_EXPERTS:
        inc = inc + jnp.where(lane8 >= s, pltpu.roll(inc, s, 1), 0)
        s *= 2
    start = (inc - padded)[0:1, :].astype(f32)
    lane = lax.broadcasted_iota(i32, (tm, N_EXPERTS), 1)
    lane_o = lax.broadcasted_iota(i32, (tm, LANES), 1)
    idx = idx_ref[...]
    rank = rank_ref[...]
    dest = jnp.zeros((tm, LANES), i32)
    for k in range(TOP_K):
        hot = lane == idx[:, k:k + 1]
        sk = jnp.sum(jnp.where(hot, start, 0.0), axis=-1, keepdims=True).astype(i32)
        dest = jnp.where(lane_o == k, sk + rank[:, k:k + 1], dest)
    dest_ref[...] = dest


def _dest_call(idx, rank, counts):
    n = idx.shape[0]
    rowb = lambda w: pl.BlockSpec((ROW_TILE, w), lambda i: (i, 0))
    return pl.pallas_call(
        _dest_kernel,
        grid=(n // ROW_TILE,),
        in_specs=[rowb(LANES), rowb(LANES), pl.BlockSpec((1, N_EXPERTS), lambda i: (0, 0))],
        out_specs=rowb(LANES),
        out_shape=jax.ShapeDtypeStruct((n, LANES), i32),
        compiler_params=_cparams(("parallel",)),
        name="dest",
    )(idx, rank, counts)


def _sc_workers():
    info = plsc.get_sparse_core_info()
    mesh = plsc.VectorSubcoreMesh(core_axis_name="c", subcore_axis_name="s")
    worker_id = lambda: lax.axis_index("s") * info.num_cores + lax.axis_index("c")
    return mesh, info.num_cores * info.num_subcores, worker_id


def _scatter_rows_call(rows, dest_km, n_slots):
    n, d = rows.shape
    mesh, n_workers, worker_id = _sc_workers()
    per_worker = n // n_workers
    assert n % (n_workers * SC_CHUNK) == 0

    def body(rows_hbm, idx_hbm, out_hbm, *scratch):
        idx_vs, rows_v, sem = scratch[:TOP_K], scratch[TOP_K], scratch[TOP_K + 1]
        base = worker_id() * per_worker

        @pl.loop(0, per_worker // SC_CHUNK)
        def _(i):
            t0 = pl.multiple_of(base + i * SC_CHUNK, SC_CHUNK)
            pltpu.sync_copy(rows_hbm.at[pl.ds(t0, SC_CHUNK)], rows_v)
            for k in range(TOP_K):
                pltpu.sync_copy(idx_hbm.at[pl.ds(k * n + t0, SC_CHUNK)], idx_vs[k])
            copies = [pltpu.async_copy(rows_v, out_hbm.at[idx_vs[k]], sem) for k in range(TOP_K)]
            for cp in copies:
                cp.wait()

    return pl.kernel(
        body,
        out_type=jax.ShapeDtypeStruct((n_slots, d), rows.dtype),
        mesh=mesh,
        scratch_types=[pltpu.VMEM((SC_CHUNK,), i32)] * TOP_K + [pltpu.VMEM((SC_CHUNK, d), rows.dtype),
                                                                pltpu.SemaphoreType.DMA],
        name="scatter_rows",
    )(rows, dest_km.reshape(-1))


def _expert_kernel(nu_ref, bv_ref, ord_ref, eseq_ref, nex_ref, x_ref, w1_hbm, w3_hbm, w2_hbm, y_ref,
                   w1_f, w3_f, w2_f, w1_s, w3_s, w2_s, sems, *, layer):
    i = pl.program_id(0)

    def weight_copies(j):
        e = eseq_ref[j]
        slot = j % W_RING
        return [pltpu.make_async_copy(w1_hbm.at[layer, e], w1_f.at[slot], sems.at[slot, 0]),
                pltpu.make_async_copy(w3_hbm.at[layer, e], w3_f.at[slot], sems.at[slot, 1]),
                pltpu.make_async_copy(w2_hbm.at[layer, e], w2_f.at[slot], sems.at[slot, 2])]

    def start_if_exists(j):
        @pl.when(j < nex_ref[0])
        def _():
            for cp in weight_copies(j):
                cp.start()

    @pl.when(i < nu_ref[0])
    def _():
        j = ord_ref[i]

        @pl.when(i == 0)
        def _():
            for ahead in range(W_RING - 1):
                start_if_exists(ahead)

        @pl.when((i == 0) | (j != ord_ref[jnp.maximum(i - 1, 0)]))
        def _():
            start_if_exists(j + W_RING - 1)
            for cp in weight_copies(j):
                cp.wait()
            slot = j % W_RING
            w1_s[...] = w1_f[slot].astype(bf16)
            w3_s[...] = w3_f[slot].astype(bf16)
            w2_s[...] = w2_f[slot].astype(bf16)

        row = lax.broadcasted_iota(i32, x_ref.shape, 0)
        x_hi, x_lo = _unpack_pairs(jnp.where(row < bv_ref[i], x_ref[...], 0))
        x = jnp.concatenate([x_hi, x_lo], axis=1).astype(bf16)
        act = (_silu(_dot(x, w1_s[...])) * _dot(x, w3_s[...])).astype(bf16)
        y_ref[...] = _pack_pairs(_dot(act, w2_s[...]))


def _expert_call(n_used, blk_valid, blk_ord, expert_seq, n_seq_experts, slots, w1, w3, w2, layer):
    nb = slots.shape[0] // SLOT_BLOCK
    blk = lambda i, nu, *_: (jnp.minimum(i, nu[0] - 1), 0)
    hbm = pl.BlockSpec(memory_space=pl.ANY)
    grid_spec = pltpu.PrefetchScalarGridSpec(
        num_scalar_prefetch=5,
        grid=(nb,),
        in_specs=[pl.BlockSpec((SLOT_BLOCK, D_PACK), blk), hbm, hbm, hbm],
        out_specs=pl.BlockSpec((SLOT_BLOCK, D_PACK), blk),
        scratch_shapes=[pltpu.VMEM((W_RING, D_MODEL, EXPERT_FF), f32), pltpu.VMEM((W_RING, D_MODEL, EXPERT_FF), f32),
                        pltpu.VMEM((W_RING, EXPERT_FF, D_MODEL), f32),
                        pltpu.VMEM((D_MODEL, EXPERT_FF), bf16), pltpu.VMEM((D_MODEL, EXPERT_FF), bf16),
                        pltpu.VMEM((EXPERT_FF, D_MODEL), bf16),
                        pltpu.SemaphoreType.DMA((W_RING, 3))],
    )
    return pl.pallas_call(
        functools.partial(_expert_kernel, layer=layer),
        grid_spec=grid_spec,
        out_shape=jax.ShapeDtypeStruct(slots.shape, slots.dtype),
        input_output_aliases={5: 0},
        compiler_params=_cparams(("arbitrary",)),
        name="experts",
    )(n_used, blk_valid, blk_ord, expert_seq, n_seq_experts, slots, w1, w3, w2)


def _gather_rows_call(table, idx):
    n_idx = idx.shape[0]
    d = table.shape[1]
    mesh, n_workers, worker_id = _sc_workers()
    per_worker = n_idx // n_workers
    assert n_idx % (n_workers * SC_CHUNK) == 0

    n_chunks = per_worker // SC_CHUNK
    assert n_chunks % 2 == 0

    def body(table_hbm, idx_hbm, out_hbm, idx_a, idx_b, rows_a, rows_b, sem_a, sem_b):
        base = worker_id() * per_worker
        bufs = ((idx_a, rows_a, sem_a), (idx_b, rows_b, sem_b))

        def start_gather(c, buf):
            idx_v, rows_v, sem = buf
            off = pl.multiple_of(base + c * SC_CHUNK, SC_CHUNK)
            pltpu.sync_copy(idx_hbm.at[pl.ds(off, SC_CHUNK)], idx_v)
            pltpu.async_copy(table_hbm.at[idx_v], rows_v, sem)

        start_gather(0, bufs[0])

        @pl.loop(0, n_chunks, step=2)
        def _(c0):
            for b in range(2):
                c = c0 + b
                idx_v, rows_v, sem = bufs[b]

                @pl.when(c + 1 < n_chunks)
                def _():
                    start_gather(c + 1, bufs[1 - b])

                pltpu.make_async_copy(table_hbm.at[idx_v], rows_v, sem).wait()
                off = pl.multiple_of(base + c * SC_CHUNK, SC_CHUNK)
                pltpu.sync_copy(rows_v, out_hbm.at[pl.ds(off, SC_CHUNK)])

    return pl.kernel(
        body,
        out_type=jax.ShapeDtypeStruct((n_idx, d), table.dtype),
        mesh=mesh,
        scratch_types=[pltpu.VMEM((SC_CHUNK,), i32), pltpu.VMEM((SC_CHUNK,), i32),
                       pltpu.VMEM((SC_CHUNK, d), table.dtype), pltpu.VMEM((SC_CHUNK, d), table.dtype),
                       pltpu.SemaphoreType.DMA, pltpu.SemaphoreType.DMA],
        name="gather_rows",
    )(table, idx)


def _combine_kernel(gate_ref, xb_ref, mod_ref, y_ref, o_ref):
    g2 = mod_ref[0][:, 5 * D_MODEL:6 * D_MODEL]
    gates = gate_ref[...]
    acc_hi = jnp.zeros((gates.shape[0], D_PACK), f32)
    acc_lo = jnp.zeros((gates.shape[0], D_PACK), f32)
    for k in range(TOP_K):
        y_hi, y_lo = _unpack_pairs(y_ref[k])
        acc_hi = acc_hi + gates[:, k:k + 1] * y_hi
        acc_lo = acc_lo + gates[:, k:k + 1] * y_lo
    o_ref[:, 0:D_PACK] = xb_ref[:, 0:D_PACK] + g2[:, 0:D_PACK] * acc_hi
    o_ref[:, D_PACK:D_MODEL] = xb_ref[:, D_PACK:D_MODEL] + g2[:, D_PACK:D_MODEL] * acc_lo


def _combine_call(gates, xbase, mod, ygath, seq_row_c):
    n = xbase.shape[0]
    tm = COMBINE_TILE
    return pl.pallas_call(
        _combine_kernel,
        grid=(n // tm,),
        in_specs=[pl.BlockSpec((tm, LANES), lambda i: (i, 0)),
                  pl.BlockSpec((tm, D_MODEL), lambda i: (i, 0)),
                  pl.BlockSpec((1, 1, 6 * D_MODEL), lambda i: (seq_row_c(i), 0, 0)),
                  pl.BlockSpec((TOP_K, tm, D_PACK), lambda i: (0, i, 0))],
        out_specs=pl.BlockSpec((tm, D_MODEL), lambda i: (i, 0)),
        out_shape=jax.ShapeDtypeStruct((n, D_MODEL), f32),
        compiler_params=_cparams(("parallel",)),
        name="combine",
    )(gates, xbase, mod, ygath)


def _block_avg(width, group):
    r = np.arange(width)
    return jnp.asarray((r[:, None] // group == r[None, :] // group).astype(np.float32) / group, dtype=bf16)


def _kv_replicate():
    c = np.arange(ATTN_W)
    src = (c // (ATTN_W // N_KV_HEADS)) * HEAD_DIM + c % HEAD_DIM
    return jnp.asarray((np.arange(KV_W)[:, None] == src[None, :]).astype(np.float32), dtype=bf16)


def _rope_lane_tables(n_tok):
    rows = n_tok // GRID_W
    r = jnp.repeat(jnp.arange(rows, dtype=f32), GRID_W)
    col = jnp.tile(jnp.arange(GRID_W, dtype=f32), rows)
    inv = ROPE_THETA ** (-jnp.arange(ROPE_FREQ, dtype=f32) / ROPE_FREQ)
    ar = r[:, None] * inv
    ac = col[:, None] * inv
    cos_h = jnp.concatenate([jnp.cos(ar), jnp.cos(ar), jnp.cos(ac), jnp.cos(ac)], axis=-1)
    sin_h = jnp.concatenate([-jnp.sin(ar), jnp.sin(ar), -jnp.sin(ac), jnp.sin(ac)], axis=-1)
    return jnp.tile(cos_h, (1, N_HEADS)), jnp.tile(sin_h, (1, N_HEADS))


def _block_diag(w):
    nb, bw, _ = w.shape
    eye = jnp.eye(nb, dtype=w.dtype)
    return (w[:, :, None, :] * eye[:, None, :, None]).reshape(nb * bw, nb * bw)


def _gla_state_in(s):
    bsz = s.shape[0]
    eye = jnp.eye(GLA_H, dtype=s.dtype)
    st = jnp.swapaxes(s, -1, -2)
    big = st[:, :, :, :, None, :] * eye[None, None, :, None, :, None]
    return big.reshape(bsz, 2, GLA_W, GLA_W)


def _gla_state_out(st):
    bsz = st.shape[0]
    s6 = st.reshape(bsz, 2, GLA_H, GLA_DK, GLA_H, GLA_DK)
    diag = jnp.stack([s6[:, :, h, :, h, :] for h in range(GLA_H)], axis=2)
    return jnp.swapaxes(diag, -1, -2)


def kernel(x_prompt, x_sample, cache_k, cache_v, state_lru, state_gla, c, c_ctx, ada_w, ada_b, norm1_g, norm2_g, w_in, q_norm_g, k_norm_g, attn_out_g, conv_w, conv_b, lru_wa, lru_ba, lru_wi, lru_bi, lru_lambda, lru_out_g, gla_wa2, gla_ba, gla_out_g, w_out, router_w, router_b, exp_w1, exp_w3, exp_w2, sh_w1, sh_w3, sh_w2):
    bc, tc, _ = x_prompt.shape
    bl, tl, _ = x_sample.shape
    depth = w_in.shape[0]
    nc = bc * tc
    n = nc + bl * tl
    past = cache_k.shape[2]
    assert tc == ROW_TILE and tl % ROW_TILE == 0 and nc % tl == 0 and bl + 1 <= SUBLANES
    assert n % ROW_TILE == 0 and n % COMBINE_TILE == 0

    def seq_row_for(tile):
        def seq_row(i):
            return jnp.where(i < nc // tile, 0, 1 + (i - nc // tile) // (tl // tile))
        return seq_row

    seq_row = seq_row_for(ROW_TILE)
    seq_row_c = seq_row_for(COMBINE_TILE)

    x = jnp.concatenate([x_prompt.reshape(nc, D_MODEL), x_sample.reshape(bl * tl, D_MODEL)], axis=0)
    cond = jnp.zeros((SUBLANES, D_MODEL), f32).at[0].set(c_ctx).at[1:1 + bl].set(c)
    mods = _ada_call(cond, ada_w, ada_b)

    bmq = _block_avg(ATTN_W, HEAD_DIM)
    bmk = _block_avg(KV_W, HEAD_DIM)
    bmg = _block_avg(GLA_W, GLA_DK)
    rep = _kv_replicate()
    cos_t, sin_t = _rope_lane_tables(tl)
    n_slots = -(-(n * TOP_K + N_EXPERTS * (SLOT_BLOCK - 1)) // SLOT_BLOCK) * SLOT_BLOCK
    tile8 = lambda v: jnp.tile(v, N_HEADS)[None, :]

    ks, vs, lrus, glas = [], [], [], []
    for l in range(depth):
        mod = mods[l].reshape(SUBLANES, 1, 6 * D_MODEL)
        w_in_p = jnp.pad(w_in[l].astype(bf16), ((0, 0), (0, LANES - 2 * GLA_RANK)))
        pa, pl_, pg = _proj_call(x, mod, norm1_g[l][None, :], w_in_p, seq_row)

        qg, kg, og = tile8(q_norm_g[l]), jnp.tile(k_norm_g[l], N_KV_HEADS)[None, :], attn_out_g[l][None, :]
        attn_c, k_new, v_new = _attn_ctx_call(pa, bc, tc, qg, kg, og, bmq, bmk, rep)
        attn_l = _attn_lat_call(pa, nc, bl, tl, cache_k[:, l].reshape(bl, past, KV_W),
                                cache_v[:, l].reshape(bl, past, KV_W), cos_t, sin_t, qg, kg, og, bmq, bmk, rep)
        ks.append(k_new.reshape(bc, tc, N_KV_HEADS, HEAD_DIM))
        vs.append(v_new.reshape(bc, tc, N_KV_HEADS, HEAD_DIM))

        wg = jnp.concatenate([_block_diag(lru_wa[l, 0]), _block_diag(lru_wi[l, 0]),
                              _block_diag(lru_wa[l, 1]), _block_diag(lru_wi[l, 1])], axis=1).astype(bf16)
        bg = jnp.concatenate([lru_ba[l, 0], lru_bi[l, 0], lru_ba[l, 1], lru_bi[l, 1]])[None, :]
        lru_args = (conv_w[l], conv_b[l][None, :], wg, bg, lru_lambda[l], lru_out_g[l][None, :])
        lru_c, lru_fin = _lru_call(pl_, 0, bc, tc, jnp.zeros((bc, 2, LRU_W), f32), *lru_args)
        lru_l, _ = _lru_call(pl_, nc, bl, tl, state_lru[:, l], *lru_args)
        lrus.append(lru_fin)

        wa = jnp.zeros((LANES, 2 * GLA_W), f32)
        wa = wa.at[0:GLA_RANK, 0:GLA_W].set(gla_wa2[l, 0]).at[GLA_RANK:2 * GLA_RANK, GLA_W:].set(gla_wa2[l, 1])
        gla_args = (wa.astype(bf16), gla_ba[l].reshape(1, 2 * GLA_W), gla_out_g[l].reshape(1, GLA_W), bmg)
        gla_c, gla_fin = _gla_call(pg, 0, bc, tc, None, *gla_args)
        gla_l, _ = _gla_call(pg, nc, bl, tl, _gla_state_in(state_gla[:, l]), *gla_args)
        glas.append(_gla_state_out(gla_fin))

        rw_hi = router_w[l].astype(bf16)
        rw_lo = (router_w[l] - rw_hi.astype(f32)).astype(bf16)
        xbase, h2, scores = _out_call(x, (attn_c, lru_c, gla_c), (attn_l, lru_l, gla_l), mod, w_out[l].astype(bf16),
                                      norm2_g[l][None, :], rw_hi, rw_lo, sh_w1[l].astype(bf16),
                                      sh_w3[l].astype(bf16), sh_w2[l].astype(bf16), seq_row)

        idx, gates, rank, counts = _route_call(scores, router_b[l][None, :])
        dest = _dest_call(idx, rank, counts)[:, :TOP_K]
        cnt = counts[0].astype(i32)
        padded = (cnt + SLOT_BLOCK - 1) // SLOT_BLOCK * SLOT_BLOCK
        padded_end = jnp.cumsum(padded)
        nb = n_slots // SLOT_BLOCK
        blk_first = jnp.arange(nb, dtype=i32) * SLOT_BLOCK
        blk_e = jnp.minimum(jnp.sum((padded_end[None, :] <= blk_first[:, None]).astype(i32), axis=1), N_EXPERTS - 1)
        blk_valid = jnp.clip((padded_end - padded + cnt)[blk_e] - blk_first, 0, SLOT_BLOCK)
        n_used = padded_end[-1:] // SLOT_BLOCK
        owns = jnp.cumsum((cnt > 0).astype(i32))
        blk_ord = (owns - 1)[blk_e]
        expert_seq = jnp.minimum(jnp.sum((owns[None, :] <= jnp.arange(N_EXPERTS, dtype=i32)[:, None]).astype(i32),
                                         axis=1), N_EXPERTS - 1)
        dest_km = dest.T
        slots = _scatter_rows_call(h2, dest_km, n_slots)
        slots = _expert_call(n_used, blk_valid, blk_ord, expert_seq, owns[-1:], slots, exp_w1, exp_w3, exp_w2, l)
        ygath = _gather_rows_call(slots, dest_km.reshape(-1)).reshape(TOP_K, n, D_PACK)
        x = _combine_call(gates, xbase, mod, ygath, seq_row_c)

    y_prompt = x[:nc].reshape(bc, tc, D_MODEL)
    y_sample = x[nc:].reshape(bl, tl, D_MODEL)
    return (y_prompt, y_sample, jnp.stack(ks, axis=1), jnp.stack(vs, axis=1),
            jnp.stack(lrus, axis=1), jnp.stack(glas, axis=1))
```

```python
import functools

import jax
import jax.numpy as jnp
import numpy as np
from jax import lax
from jax.experimental import pallas as pl
from jax.experimental.pallas import tpu as pltpu
from jax.experimental.pallas import tpu_sc as plsc

f32 = jnp.float32
bf16 = jnp.bfloat16
i32 = jnp.int32

D_MODEL = 1024
N_HEADS = 8
N_KV_HEADS = 2
HEAD_DIM = 64
ATTN_W = N_HEADS * HEAD_DIM
KV_W = N_KV_HEADS * HEAD_DIM
GRID_W = 64
ROPE_FREQ = HEAD_DIM // 4
ROPE_THETA = 10000.0
LRU_W = 256
LRU_BLOCKS = 4
LRU_C = 8.0
CONV_W = 4
CONV_LEFT = 2
GLA_H = 4
GLA_DK = 64
GLA_W = 256
GLA_RANK = 16
GLA_TAU = 16.0
N_EXPERTS = 256
TOP_K = 8
EXPERT_FF = 256
ROUTED_SCALE = 2.5
EPS = 1e-6

LANES = 128
SUBLANES = 8
ROW_TILE = 256
GLA_BLOCK = 256
SLOT_BLOCK = 256
W_RING = 3
COMBINE_TILE = 128
SC_CHUNK = 64
D_PACK = D_MODEL // 2
VMEM_LIMIT = 56 * 1024 * 1024


def _cparams(sem, vmem=VMEM_LIMIT):
    return pltpu.CompilerParams(dimension_semantics=sem, vmem_limit_bytes=vmem)


def _dot(a, b):
    return jnp.dot(a, b, preferred_element_type=f32)


def _dot_nt(a, b):
    return lax.dot_general(a, b, (((1,), (1,)), ((), ())), preferred_element_type=f32)


def _dot_tn(a, b):
    return lax.dot_general(a, b, (((0,), (0,)), ((), ())), preferred_element_type=f32)


def _split(x):
    hi = x.astype(bf16)
    lo = (x - hi.astype(f32)).astype(bf16)
    return hi, lo


def _dot_x2(x, w):
    hi, lo = _split(x)
    return _dot(hi, w) + _dot(lo, w)


def _dot_2x(m, x):
    hi, lo = _split(x)
    return _dot(m, hi) + _dot(m, lo)


def _dot3(a, b_hi, b_lo):
    a_hi, a_lo = _split(a)
    return _dot(a_hi, b_hi) + _dot(a_lo, b_hi) + _dot(a_hi, b_lo)


def _sigmoid(x):
    return 1.0 / (1.0 + jnp.exp(-x))


def _silu(x):
    return x * _sigmoid(x)


def _softplus(x):
    return jnp.maximum(x, 0.0) + jnp.log(1.0 + jnp.exp(-jnp.abs(x)))


def _gelu_tanh(x):
    return 0.5 * x * (1.0 + jnp.tanh(0.7978845608028654 * (x + 0.044715 * x * x * x)))


def _rms(x, g):
    return x * lax.rsqrt(jnp.mean(x * x, axis=-1, keepdims=True) + EPS) * g


def _pack_pairs(x):
    c = x.shape[1] // 2
    hi = lax.bitcast_convert_type(x[:, :c].astype(bf16).astype(f32), i32)
    lo = lax.bitcast_convert_type(x[:, c:].astype(bf16).astype(f32), i32)
    return hi | lax.shift_right_logical(lo, 16)


def _unpack_pairs(w):
    hi = lax.bitcast_convert_type(w & jnp.int32(-65536), f32)
    lo = lax.bitcast_convert_type(w << 16, f32)
    return hi, lo


def _head_rms(x, g, bm):
    ms = _dot_x2(x * x, bm)
    return x * lax.rsqrt(ms + EPS) * g


def _ada_kernel(c_ref, w_ref, b_ref, o_ref):
    s = _silu(c_ref[...])
    w = w_ref[0]
    w_hi, w_lo = _split(w)
    o_ref[0] = _dot3(s, w_hi, w_lo) + b_ref[0]


def _ada_call(cond, ada_w, ada_b):
    depth = ada_w.shape[0]
    nt = 1536
    return pl.pallas_call(
        _ada_kernel,
        grid=(depth, 6 * D_MODEL // nt),
        in_specs=[pl.BlockSpec((SUBLANES, D_MODEL), lambda l, j: (0, 0)),
                  pl.BlockSpec((1, D_MODEL, nt), lambda l, j: (l, 0, j)),
                  pl.BlockSpec((1, 1, nt), lambda l, j: (l, 0, j))],
        out_specs=pl.BlockSpec((1, SUBLANES, nt), lambda l, j: (l, 0, j)),
        out_shape=jax.ShapeDtypeStruct((depth, SUBLANES, 6 * D_MODEL), f32),
        compiler_params=_cparams(("parallel", "parallel")),
        name="ada_mod",
    )(cond, ada_w, ada_b.reshape(depth, 1, 6 * D_MODEL))


PROJ_A = ATTN_W + 2 * KV_W
PROJ_L = 2 * LRU_W
PROJ_G = 4 * GLA_W + LANES


def _proj_kernel(x_ref, mod_ref, g_ref, w_ref, oa_ref, ol_ref, og_ref):
    mod = mod_ref[0]
    sh = mod[:, 0:D_MODEL]
    sc = mod[:, D_MODEL:2 * D_MODEL]
    h = (_rms(x_ref[...], g_ref[...]) * (1.0 + sc) + sh).astype(bf16)
    p = _dot(h, w_ref[...])
    oa_ref[...] = p[:, 0:PROJ_A]
    ol_ref[...] = p[:, PROJ_A:PROJ_A + PROJ_L]
    og_ref[...] = p[:, PROJ_A + PROJ_L:PROJ_A + PROJ_L + PROJ_G]


def _proj_call(x, mod, g, w, seq_row):
    n = x.shape[0]
    cols = PROJ_A + PROJ_L + PROJ_G
    return pl.pallas_call(
        _proj_kernel,
        grid=(n // ROW_TILE,),
        in_specs=[pl.BlockSpec((ROW_TILE, D_MODEL), lambda i: (i, 0)),
                  pl.BlockSpec((1, 1, 6 * D_MODEL), lambda i: (seq_row(i), 0, 0)),
                  pl.BlockSpec((1, D_MODEL), lambda i: (0, 0)),
                  pl.BlockSpec((D_MODEL, cols), lambda i: (0, 0))],
        out_specs=[pl.BlockSpec((ROW_TILE, PROJ_A), lambda i: (i, 0)),
                   pl.BlockSpec((ROW_TILE, PROJ_L), lambda i: (i, 0)),
                   pl.BlockSpec((ROW_TILE, PROJ_G), lambda i: (i, 0))],
        out_shape=[jax.ShapeDtypeStruct((n, PROJ_A), f32),
                   jax.ShapeDtypeStruct((n, PROJ_L), f32),
                   jax.ShapeDtypeStruct((n, PROJ_G), f32)],
        compiler_params=_cparams(("parallel",)),
        name="in_proj",
    )(x, mod, g, w)


def _rope(x, cos_t, sin_t):
    w = x.shape[1]
    up = pltpu.roll(x, w - ROPE_FREQ, 1)
    dn = pltpu.roll(x, ROPE_FREQ, 1)
    lane = lax.broadcasted_iota(i32, x.shape, 1)
    partner = jnp.where((lane & (2 * ROPE_FREQ - 1)) < ROPE_FREQ, up, dn)
    return x * cos_t + partner * sin_t


def _attend(q, kk_ref, vv_ref, o_ref):
    tq = q.shape[0]
    gw = ATTN_W // N_KV_HEADS
    lane = lax.broadcasted_iota(i32, (tq, gw), 1)
    for g in range(N_KV_HEADS):
        qg = q[:, g * gw:(g + 1) * gw]
        kg = kk_ref[:, g * gw:(g + 1) * gw]
        vg = vv_ref[:, g * gw:(g + 1) * gw]
        acc = jnp.zeros((tq, gw), f32)
        for hh in range(N_HEADS // N_KV_HEADS):
            hm = (lane >> 6) == hh
            s = _dot_nt(jnp.where(hm, qg, 0.0).astype(bf16), kg)
            m = jnp.max(s, axis=-1, keepdims=True)
            p = jnp.exp(s - m)
            l = jnp.sum(p, axis=-1, keepdims=True)
            o = _dot(p.astype(bf16), vg) / l
            acc = jnp.where(hm, o, acc)
        o_ref[:, g * gw:(g + 1) * gw] = acc


def _attn_ctx_kernel(p_ref, qg_ref, kg_ref, og_ref, bmq_ref, bmk_ref, rep_ref,
                     o_ref, ko_ref, vo_ref, kk_s, vv_s, o_s):
    p = p_ref[...]
    q = _head_rms(p[:, 0:ATTN_W], qg_ref[...], bmq_ref[...])
    k = _head_rms(p[:, ATTN_W:ATTN_W + KV_W], kg_ref[...], bmk_ref[...])
    v = p[:, ATTN_W + KV_W:PROJ_A]
    ko_ref[...] = k
    vo_ref[...] = v
    kk_s[...] = _dot(k.astype(bf16), rep_ref[...]).astype(bf16)
    vv_s[...] = _dot(v.astype(bf16), rep_ref[...]).astype(bf16)
    _attend(q * HEAD_DIM ** -0.5, kk_s, vv_s, o_s)
    o_ref[...] = _rms(o_s[...], og_ref[...])


def _attn_ctx_call(pa, n_seq, t, qg, kg, og, bmq, bmk, rep):
    assert t == ROW_TILE
    const = lambda i: (0, 0)
    return pl.pallas_call(
        _attn_ctx_kernel,
        grid=(n_seq,),
        in_specs=[pl.BlockSpec((t, PROJ_A), lambda i: (i, 0)),
                  pl.BlockSpec((1, ATTN_W), const), pl.BlockSpec((1, KV_W), const), pl.BlockSpec((1, ATTN_W), const),
                  pl.BlockSpec((ATTN_W, ATTN_W), const), pl.BlockSpec((KV_W, KV_W), const),
                  pl.BlockSpec((KV_W, ATTN_W), const)],
        out_specs=[pl.BlockSpec((t, ATTN_W), lambda i: (i, 0)),
                   pl.BlockSpec((t, KV_W), lambda i: (i, 0)),
                   pl.BlockSpec((t, KV_W), lambda i: (i, 0))],
        out_shape=[jax.ShapeDtypeStruct((n_seq * t, ATTN_W), f32),
                   jax.ShapeDtypeStruct((n_seq * t, KV_W), f32),
                   jax.ShapeDtypeStruct((n_seq * t, KV_W), f32)],
        scratch_shapes=[pltpu.VMEM((t, ATTN_W), bf16), pltpu.VMEM((t, ATTN_W), bf16), pltpu.VMEM((t, ATTN_W), f32)],
        compiler_params=_cparams(("parallel",)),
        name="attn_ctx",
    )(pa, qg, kg, og, bmq, bmk, rep)


def _attn_lat_kernel(q_ref, kv_ref, ck_ref, cv_ref, cq_ref, sq_ref, ckk_ref, skk_ref,
                     qg_ref, kg_ref, og_ref, bmq_ref, bmk_ref, rep_ref,
                     o_ref, kk_s, vv_s, o_s, *, past):
    @pl.when(pl.program_id(1) == 0)
    def _():
        kv = kv_ref[...]
        k = _head_rms(kv[:, 0:KV_W], kg_ref[...], bmk_ref[...])
        k = _rope(k, ckk_ref[...], skk_ref[...])
        v = kv[:, KV_W:2 * KV_W]
        kk_s[0:past, :] = _dot(ck_ref[0].astype(bf16), rep_ref[...]).astype(bf16)
        vv_s[0:past, :] = _dot(cv_ref[0].astype(bf16), rep_ref[...]).astype(bf16)
        kk_s[past:, :] = _dot(k.astype(bf16), rep_ref[...]).astype(bf16)
        vv_s[past:, :] = _dot(v.astype(bf16), rep_ref[...]).astype(bf16)

    q = _head_rms(q_ref[...], qg_ref[...], bmq_ref[...])
    q = _rope(q, cq_ref[...], sq_ref[...])
    _attend(q * HEAD_DIM ** -0.5, kk_s, vv_s, o_s)
    o_ref[...] = _rms(o_s[...], og_ref[...])


def _attn_lat_call(pa, row0, n_seq, t, cache_k, cache_v, cq, sq, qg, kg, og, bmq, bmk, rep):
    past = cache_k.shape[1]
    nq = t // ROW_TILE
    assert row0 % t == 0 and t % ROW_TILE == 0
    qb0 = row0 // ROW_TILE
    sb0 = row0 // t
    const = lambda b, j: (0, 0)
    return pl.pallas_call(
        functools.partial(_attn_lat_kernel, past=past),
        grid=(n_seq, nq),
        in_specs=[pl.BlockSpec((ROW_TILE, ATTN_W), lambda b, j: (qb0 + b * nq + j, 0)),
                  pl.BlockSpec((t, 2 * KV_W), lambda b, j: (sb0 + b, ATTN_W // (2 * KV_W))),
                  pl.BlockSpec((1, past, KV_W), lambda b, j: (b, 0, 0)),
                  pl.BlockSpec((1, past, KV_W), lambda b, j: (b, 0, 0)),
                  pl.BlockSpec((ROW_TILE, ATTN_W), lambda b, j: (j, 0)),
                  pl.BlockSpec((ROW_TILE, ATTN_W), lambda b, j: (j, 0)),
                  pl.BlockSpec((t, KV_W), const), pl.BlockSpec((t, KV_W), const),
                  pl.BlockSpec((1, ATTN_W), const), pl.BlockSpec((1, KV_W), const), pl.BlockSpec((1, ATTN_W), const),
                  pl.BlockSpec((ATTN_W, ATTN_W), const), pl.BlockSpec((KV_W, KV_W), const),
                  pl.BlockSpec((KV_W, ATTN_W), const)],
        out_specs=pl.BlockSpec((ROW_TILE, ATTN_W), lambda b, j: (b * nq + j, 0)),
        out_shape=jax.ShapeDtypeStruct((n_seq * t, ATTN_W), f32),
        scratch_shapes=[pltpu.VMEM((past + t, ATTN_W), bf16), pltpu.VMEM((past + t, ATTN_W), bf16),
                        pltpu.VMEM((ROW_TILE, ATTN_W), f32)],
        compiler_params=_cparams(("parallel", "arbitrary")),
        name="attn_lat",
    )(pa, pa, cache_k, cache_v, cq, sq, cq, sq, qg, kg, og, bmq, bmk, rep)


def _lru_kernel(x_ref, h0_ref, cw_ref, cb_ref, wg_ref, bg_ref, lam_ref, g_ref,
                o_ref, fin_ref, xs, a_s, u_s, h_s, *, t):
    ng = t // SUBLANES
    lx = x_ref[:, 0:LRU_W]
    ly = x_ref[:, LRU_W:2 * LRU_W]
    xs[0:SUBLANES, :] = jnp.zeros((SUBLANES, LRU_W), f32)
    xs[SUBLANES + t:2 * SUBLANES + t, :] = jnp.zeros((SUBLANES, LRU_W), f32)
    xs[SUBLANES:SUBLANES + t, :] = lx
    xc = jnp.broadcast_to(cb_ref[...], (t, LRU_W))
    for j in range(CONV_W):
        off = SUBLANES + j - CONV_LEFT
        xc = xc + xs[off:off + t, :] * cw_ref[j:j + 1, :]
    gates = _sigmoid(_dot(xc.astype(bf16), wg_ref[...]) + bg_ref[...])
    sp = _softplus(-lam_ref[...])
    sub = lax.broadcasted_iota(i32, (t, LRU_W), 0) & (SUBLANES - 1)
    for d in range(2):
        r = gates[:, (2 * d) * LRU_W:(2 * d + 1) * LRU_W]
        gi = gates[:, (2 * d + 1) * LRU_W:(2 * d + 2) * LRU_W]
        log_a = -LRU_C * r * sp[d:d + 1, :]
        a = jnp.exp(log_a)
        th = jnp.tanh(log_a)
        u = jnp.sqrt(-2.0 * th / (1.0 - th)) * (gi * xc)
        for s in (1, 2, 4):
            if d == 0:
                a_sh = pltpu.roll(a, s, 0)
                u_sh = pltpu.roll(u, s, 0)
                ok = sub >= s
            else:
                a_sh = pltpu.roll(a, t - s, 0)
                u_sh = pltpu.roll(u, t - s, 0)
                ok = sub < SUBLANES - s
            u = jnp.where(ok, a * u_sh + u, u)
            a = jnp.where(ok, a * a_sh, a)
        a_s[...] = a
        u_s[...] = u
        h0 = jnp.broadcast_to(h0_ref[0, d:d + 1, :], (SUBLANES, LRU_W))
        edge = SUBLANES - 1 if d == 0 else 0

        def body(i, carry, d=d, edge=edge):
            g = i if d == 0 else ng - 1 - i
            rows = pl.ds(pl.multiple_of(g * SUBLANES, SUBLANES), SUBLANES)
            h = a_s[rows, :] * carry + u_s[rows, :]
            if d == 0:
                h_s[rows, :] = h
            else:
                h_s[rows, :] = h_s[rows, :] + h
            return jnp.broadcast_to(h[edge:edge + 1, :], (SUBLANES, LRU_W))

        last = lax.fori_loop(0, ng, body, h0)
        fin_ref[0, d:d + 1, :] = last[0:1, :]
    o_ref[...] = _rms(h_s[...] * _gelu_tanh(ly), g_ref[...])


def _lru_call(pl_, row0, n_seq, t, h0, cw, cb, wg, bg, lam, g):
    sb0 = row0 // t
    assert row0 % t == 0
    const = lambda b: (0, 0)
    in_specs = [pl.BlockSpec((t, PROJ_L), lambda b: (sb0 + b, 0)),
                pl.BlockSpec((1, 2, LRU_W), lambda b: (b, 0, 0)),
                pl.BlockSpec((CONV_W, LRU_W), const), pl.BlockSpec((1, LRU_W), const),
                pl.BlockSpec((LRU_W, 4 * LRU_W), const), pl.BlockSpec((1, 4 * LRU_W), const),
                pl.BlockSpec((2, LRU_W), const), pl.BlockSpec((1, LRU_W), const)]
    args = [pl_, h0, cw, cb, wg, bg, lam, g]
    return pl.pallas_call(
        functools.partial(_lru_kernel, t=t),
        grid=(n_seq,),
        in_specs=in_specs,
        out_specs=[pl.BlockSpec((t, LRU_W), lambda b: (b, 0)),
                   pl.BlockSpec((1, 2, LRU_W), lambda b: (b, 0, 0))],
        out_shape=[jax.ShapeDtypeStruct((n_seq * t, LRU_W), f32), jax.ShapeDtypeStruct((n_seq, 2, LRU_W), f32)],
        scratch_shapes=[pltpu.VMEM((t + 2 * SUBLANES, LRU_W), f32), pltpu.VMEM((t, LRU_W), f32),
                        pltpu.VMEM((t, LRU_W), f32), pltpu.VMEM((t, LRU_W), f32)],
        compiler_params=_cparams(("parallel",)),
        name="rglru_t%d" % t,
    )(*args)


def _bcast_rows(b, period, off):
    w = b.shape[1]
    return jnp.concatenate(
        [jnp.broadcast_to(b[i * period + off:i * period + off + 1, :], (period, w)) for i in range(b.shape[0] // period)],
        axis=0)


def _gla_block(q, k, v, la, reverse, st_ref):
    n = GLA_BLOCK
    row = lax.broadcasted_iota(i32, (n, n), 0)
    col = lax.broadcasted_iota(i32, (n, n), 1)
    same64 = (row >> 6) == (col >> 6)
    same32 = (row >> 5) == (col >> 5)
    same16 = (row >> 4) == (col >> 4)
    if not reverse:
        cum = (same64 & (col <= row)).astype(bf16)
        m1 = same64 & ((row & 63) >= 32) & ((col & 63) < 32)
        m2 = same32 & ((row & 31) >= 16) & ((col & 31) < 16)
        m3 = same16 & (col <= row)
        offs = (31, 15, 7, 63)
    else:
        cum = (same64 & (col >= row)).astype(bf16)
        m1 = same64 & ((row & 63) < 32) & ((col & 63) >= 32)
        m2 = same32 & ((row & 31) < 16) & ((col & 31) >= 16)
        m3 = same16 & (col >= row)
        offs = (32, 16, 8, 0)
    b = _dot_2x(cum, la)
    r1 = _bcast_rows(b, 64, offs[0])
    r2 = _bcast_rows(b, 32, offs[1])
    r3 = _bcast_rows(b, 16, offs[2])
    bl = _bcast_rows(b, 64, offs[3])
    q1 = q * jnp.exp(jnp.minimum(b - r1, 0.0))
    k1 = (k * jnp.exp(jnp.minimum(r1 - b, 0.0))).astype(bf16)
    q2 = q * jnp.exp(jnp.minimum(b - r2, 0.0))
    k2 = (k * jnp.exp(jnp.minimum(r2 - b, 0.0))).astype(bf16)
    q3 = q * jnp.exp(b - r3)
    k3 = (k * jnp.exp(r3 - b)).astype(bf16)
    qe = (q * jnp.exp(b)).astype(bf16)
    kl = (k * jnp.exp(bl - b)).astype(bf16)
    dec = jnp.exp(bl)
    vb = v.astype(bf16)
    zq = jnp.zeros((n, n), f32)
    intra = jnp.zeros((n, n), f32)
    for h in range(GLA_H):
        hm = (col >> 6) == h
        a1 = _dot_nt(jnp.where(hm, q1, zq).astype(bf16), k1)
        a2 = _dot_nt(jnp.where(hm, q2, zq).astype(bf16), k2)
        a3 = _dot_nt(jnp.where(hm, q3, zq).astype(bf16), k3)
        att = jnp.where(m1, a1, jnp.where(m2, a2, jnp.where(m3, a3, zq)))
        intra = intra + _dot(att.astype(bf16), jnp.where(hm, v, zq).astype(bf16))
    outs = [None] * 4
    for c in (range(4) if not reverse else range(3, -1, -1)):
        rs = slice(64 * c, 64 * c + 64)
        st = st_ref[...]
        inter = _dot_nt(qe[rs], st.astype(bf16))
        kv = _dot_tn(vb[rs], kl[rs])
        drow = dec[64 * c:64 * c + 1, :]
        st_ref[...] = st * drow + jnp.where(same64, kv, zq)
        outs[c] = intra[rs] + inter
    return jnp.concatenate(outs, axis=0)


def _gla_kernel(x_ref, s0_ref, wa_ref, ba_ref, g_ref, bm_ref, o_ref, fin_ref, st_s, o_s, *, t):
    nblk = t // GLA_BLOCK

    def block(j, reverse):
        rows = pl.ds(j * GLA_BLOCK if isinstance(j, int) else pl.multiple_of(j * GLA_BLOCK, GLA_BLOCK), GLA_BLOCK)
        q = x_ref[rows, 0:GLA_W] * GLA_DK ** -0.5
        k = x_ref[rows, GLA_W:2 * GLA_W]
        v = x_ref[rows, 2 * GLA_W:3 * GLA_W]
        ga = x_ref[rows, 4 * GLA_W:4 * GLA_W + LANES]
        d = 1 if reverse else 0
        z = _dot(ga.astype(bf16), wa_ref[:, d * GLA_W:(d + 1) * GLA_W]) + ba_ref[:, d * GLA_W:(d + 1) * GLA_W]
        la = -_softplus(-z) * (1.0 / GLA_TAU)
        o = _gla_block(q, k, v, la, reverse, st_s)
        if reverse:
            o_s[rows, :] = o_s[rows, :] + o
        else:
            o_s[rows, :] = o

    for d in range(2):
        if s0_ref is None:
            st_s[...] = jnp.zeros((GLA_W, GLA_W), f32)
        else:
            st_s[...] = s0_ref[0, d]
        if nblk == 1:
            block(0, d == 1)
        else:
            def body(i, carry, d=d):
                block(i if d == 0 else nblk - 1 - i, d == 1)
                return carry
            lax.fori_loop(0, nblk, body, 0)
        st_t = st_s[...].T
        for h in range(GLA_H):
            fin_ref[0, d, h] = st_t[h * GLA_DK:(h + 1) * GLA_DK, h * GLA_DK:(h + 1) * GLA_DK]
    gg = x_ref[:, 3 * GLA_W:4 * GLA_W]
    o_ref[...] = _head_rms(o_s[...], g_ref[...], bm_ref[...]) * _silu(gg)


def _gla_call(pg, row0, n_seq, t, s0, wa, ba, g, bm):
    sb0 = row0 // t
    assert row0 % t == 0 and t % GLA_BLOCK == 0
    const = lambda b: (0, 0)
    in_specs = [pl.BlockSpec((t, PROJ_G), lambda b: (sb0 + b, 0))]
    args = [pg]
    if s0 is not None:
        in_specs.append(pl.BlockSpec((1, 2, GLA_W, GLA_W), lambda b: (b, 0, 0, 0)))
        args.append(s0)
    in_specs += [pl.BlockSpec((LANES, 2 * GLA_W), const), pl.BlockSpec((1, 2 * GLA_W), const),
                 pl.BlockSpec((1, GLA_W), const), pl.BlockSpec((GLA_W, GLA_W), const)]
    args += [wa, ba, g, bm]

    def kern(*refs):
        refs = list(refs)
        x_ref = refs.pop(0)
        s0_ref = refs.pop(0) if s0 is not None else None
        wa_ref, ba_ref, g_ref, bm_ref = refs[:4]
        _gla_kernel(x_ref, s0_ref, wa_ref, ba_ref, g_ref, bm_ref, *refs[4:], t=t)

    return pl.pallas_call(
        kern,
        grid=(n_seq,),
        in_specs=in_specs,
        out_specs=[pl.BlockSpec((t, GLA_W), lambda b: (b, 0)),
                   pl.BlockSpec((1, 2, GLA_H, GLA_DK, GLA_DK), lambda b: (b, 0, 0, 0, 0))],
        out_shape=[jax.ShapeDtypeStruct((n_seq * t, GLA_W), f32),
                   jax.ShapeDtypeStruct((n_seq, 2, GLA_H, GLA_DK, GLA_DK), f32)],
        scratch_shapes=[pltpu.VMEM((GLA_W, GLA_W), f32), pltpu.VMEM((t, GLA_W), f32)],
        compiler_params=_cparams(("parallel",)),
        name="gla_t%d" % t,
    )(*args)


def _out_kernel(x_ref, ac_ref, lc_ref, gc_ref, al_ref, ll_ref, gl_ref, mod_ref, wo_ref, g2_ref, rwh_ref, rwl_ref,
                s1_ref, s3_ref, s2_ref, xb_ref, h2_ref, sc_ref, *, n_ctx_tiles):
    is_ctx = pl.program_id(0) < n_ctx_tiles
    pick = lambda c_ref, l_ref: jnp.where(is_ctx, c_ref[...], l_ref[...]).astype(bf16)
    mod = mod_ref[0]
    g1 = mod[:, 2 * D_MODEL:3 * D_MODEL]
    sh2 = mod[:, 3 * D_MODEL:4 * D_MODEL]
    sc2 = mod[:, 4 * D_MODEL:5 * D_MODEL]
    g2 = mod[:, 5 * D_MODEL:6 * D_MODEL]
    m = (_dot(pick(ac_ref, al_ref), wo_ref[0:ATTN_W, :])
         + _dot(pick(lc_ref, ll_ref), wo_ref[ATTN_W:ATTN_W + LRU_W, :])
         + _dot(pick(gc_ref, gl_ref), wo_ref[ATTN_W + LRU_W:ATTN_W + LRU_W + GLA_W, :]))
    x1 = x_ref[...] + g1 * m
    h2 = _rms(x1, g2_ref[...]) * (1.0 + sc2) + sh2
    h2_ref[...] = _pack_pairs(h2)
    h_hi, h_lo = _split(h2)
    logits_t = _dot_nt(rwh_ref[...], h_hi) + _dot_nt(rwh_ref[...], h_lo) + _dot_nt(rwl_ref[...], h_hi)
    sc_ref[...] = _sigmoid(logits_t)
    hb = h2.astype(bf16)
    act = (_silu(_dot(hb, s1_ref[...])) * _dot(hb, s3_ref[...])).astype(bf16)
    xb_ref[...] = x1 + g2 * _dot(act, s2_ref[...])


def _out_call(x, mix_ctx, mix_lat, mod, wo, g2, rwh, rwl, s1, s3, s2, seq_row):
    n = x.shape[0]
    nct = mix_ctx[0].shape[0] // ROW_TILE
    const = lambda i: (0, 0)
    rowb = lambda w: pl.BlockSpec((ROW_TILE, w), lambda i: (i, 0))
    ctxb = lambda w: pl.BlockSpec((ROW_TILE, w), lambda i: (jnp.minimum(i, nct - 1), 0))
    latb = lambda w: pl.BlockSpec((ROW_TILE, w), lambda i: (jnp.maximum(i - nct, 0), 0))
    return pl.pallas_call(
        functools.partial(_out_kernel, n_ctx_tiles=nct),
        grid=(n // ROW_TILE,),
        in_specs=[rowb(D_MODEL), ctxb(ATTN_W), ctxb(LRU_W), ctxb(GLA_W), latb(ATTN_W), latb(LRU_W), latb(GLA_W),
                  pl.BlockSpec((1, 1, 6 * D_MODEL), lambda i: (seq_row(i), 0, 0)),
                  pl.BlockSpec((D_MODEL, D_MODEL), const), pl.BlockSpec((1, D_MODEL), const),
                  pl.BlockSpec((N_EXPERTS, D_MODEL), const), pl.BlockSpec((N_EXPERTS, D_MODEL), const),
                  pl.BlockSpec((D_MODEL, EXPERT_FF), const), pl.BlockSpec((D_MODEL, EXPERT_FF), const),
                  pl.BlockSpec((EXPERT_FF, D_MODEL), const)],
        out_specs=[rowb(D_MODEL),
                   rowb(D_PACK),
                   pl.BlockSpec((N_EXPERTS, ROW_TILE), lambda i: (0, i))],
        out_shape=[jax.ShapeDtypeStruct((n, D_MODEL), f32),
                   jax.ShapeDtypeStruct((n, D_PACK), i32),
                   jax.ShapeDtypeStruct((N_EXPERTS, n), f32)],
        compiler_params=_cparams(("parallel",)),
        name="out_proj",
    )(x, *mix_ctx, *mix_lat, mod, wo, g2, rwh, rwl, s1, s3, s2)


def _set_row(acc, k, row):
    sub = lax.broadcasted_iota(i32, acc.shape, 0)
    return jnp.where(sub == k, jnp.broadcast_to(row, acc.shape), acc)


def _route_kernel(sc_ref, rb_ref, idx_ref, gate_ref, rank_ref, cnt_ref, run_s):
    tm = sc_ref.shape[1]

    @pl.when(pl.program_id(0) == 0)
    def _():
        run_s[...] = jnp.zeros_like(run_s)

    scores = sc_ref[...]
    sel = scores + rb_ref[...]
    erow = lax.broadcasted_iota(i32, (N_EXPERTS, tm), 0).astype(f32)
    neg = jnp.full((N_EXPERTS, tm), -jnp.inf, f32)
    hots = []
    idx_o = jnp.zeros((TOP_K, tm), f32)
    gate_o = jnp.zeros((TOP_K, tm), f32)
    gsum = jnp.zeros((1, tm), f32)
    chosen = jnp.zeros((N_EXPERTS, tm), f32)
    for k in range(TOP_K):
        m = jnp.max(sel, axis=0, keepdims=True)
        idx_f = jnp.min(jnp.where(sel == m, erow, float(N_EXPERTS)), axis=0, keepdims=True)
        hot = erow == idx_f
        gk = jnp.sum(jnp.where(hot, scores, 0.0), axis=0, keepdims=True)
        sel = jnp.where(hot, neg, sel)
        hots.append(hot)
        chosen = jnp.where(hot, 1.0, chosen)
        gsum = gsum + gk
        idx_o = _set_row(idx_o, k, idx_f)
        gate_o = _set_row(gate_o, k, gk)
    gate_ref[...] = gate_o / gsum * ROUTED_SCALE
    idx_ref[...] = idx_o.astype(i32)
    r = lax.broadcasted_iota(i32, (tm, tm), 0)
    c = lax.broadcasted_iota(i32, (tm, tm), 1)
    pos = _dot(chosen.astype(bf16), (r < c).astype(bf16)) + run_s[...]
    rank_o = jnp.zeros((TOP_K, tm), f32)
    for k in range(TOP_K):
        rank_o = _set_row(rank_o, k, jnp.sum(jnp.where(hots[k], pos, 0.0), axis=0, keepdims=True))
    rank_ref[...] = rank_o.astype(i32)
    run_s[...] = run_s[...] + jnp.sum(chosen, axis=1, keepdims=True)
    cnt_ref[...] = run_s[...]


def _route_call(scores_t, rb):
    n = scores_t.shape[1]
    colb = lambda r: pl.BlockSpec((r, ROW_TILE), lambda i: (0, i))
    cnt_spec = pl.BlockSpec((N_EXPERTS, 1), lambda i: (0, 0))
    return pl.pallas_call(
        _route_kernel,
        grid=(n // ROW_TILE,),
        in_specs=[colb(N_EXPERTS), cnt_spec],
        out_specs=[colb(TOP_K), colb(TOP_K), colb(TOP_K), cnt_spec],
        out_shape=[jax.ShapeDtypeStruct((TOP_K, n), i32), jax.ShapeDtypeStruct((TOP_K, n), f32),
                   jax.ShapeDtypeStruct((TOP_K, n), i32), jax.ShapeDtypeStruct((N_EXPERTS, 1), f32)],
        scratch_shapes=[pltpu.VMEM((N_EXPERTS, 1), f32)],
        compiler_params=_cparams(("arbitrary",)),
        name="route",
    )(scores_t, rb)


def _dest_kernel(idx_ref, rank_ref, start_ref, dest_ref):
    tm = idx_ref.shape[1]
    erow = lax.broadcasted_iota(i32, (N_EXPERTS, tm), 0)
    start = jnp.broadcast_to(start_ref[...], (N_EXPERTS, tm))
    idx = idx_ref[...]
    dest = jnp.zeros((TOP_K, tm), f32)
    for k in range(TOP_K):
        hot = erow == idx[k:k + 1, :]
        dest = _set_row(dest, k, jnp.sum(jnp.where(hot, start, 0.0), axis=0, keepdims=True))
    dest_ref[...] = dest.astype(i32) + rank_ref[...]


def _dest_call(idx, rank, start):
    n = idx.shape[1]
    colb = lambda r: pl.BlockSpec((r, ROW_TILE), lambda i: (0, i))
    return pl.pallas_call(
        _dest_kernel,
        grid=(n // ROW_TILE,),
        in_specs=[colb(TOP_K), colb(TOP_K), pl.BlockSpec((N_EXPERTS, 1), lambda i: (0, 0))],
        out_specs=colb(TOP_K),
        out_shape=jax.ShapeDtypeStruct((TOP_K, n), i32),
        compiler_params=_cparams(("parallel",)),
        name="dest",
    )(idx, rank, start)


def _sc_workers():
    info = plsc.get_sparse_core_info()
    mesh = plsc.VectorSubcoreMesh(core_axis_name="c", subcore_axis_name="s")
    worker_id = lambda: lax.axis_index("s") * info.num_cores + lax.axis_index("c")
    return mesh, info.num_cores * info.num_subcores, worker_id


def _scatter_rows_call(rows, dest_km, n_slots):
    n, d = rows.shape
    mesh, n_workers, worker_id = _sc_workers()
    per_worker = n // n_workers
    assert n % (n_workers * SC_CHUNK) == 0

    def body(rows_hbm, idx_hbm, out_hbm, *scratch):
        idx_vs, rows_v, sem = scratch[:TOP_K], scratch[TOP_K], scratch[TOP_K + 1]
        base = worker_id() * per_worker

        @pl.loop(0, per_worker // SC_CHUNK)
        def _(i):
            t0 = pl.multiple_of(base + i * SC_CHUNK, SC_CHUNK)
            pltpu.sync_copy(rows_hbm.at[pl.ds(t0, SC_CHUNK)], rows_v)
            for k in range(TOP_K):
                pltpu.sync_copy(idx_hbm.at[pl.ds(k * n + t0, SC_CHUNK)], idx_vs[k])
            copies = [pltpu.async_copy(rows_v, out_hbm.at[idx_vs[k]], sem) for k in range(TOP_K)]
            for cp in copies:
                cp.wait()

    return pl.kernel(
        body,
        out_type=jax.ShapeDtypeStruct((n_slots, d), rows.dtype),
        mesh=mesh,
        scratch_types=[pltpu.VMEM((SC_CHUNK,), i32)] * TOP_K + [pltpu.VMEM((SC_CHUNK, d), rows.dtype),
                                                                pltpu.SemaphoreType.DMA],
        name="scatter_rows",
    )(rows, dest_km.reshape(-1))


def _expert_kernel(nu_ref, bv_ref, ord_ref, eseq_ref, nex_ref, x_ref, w1_hbm, w3_hbm, w2_hbm, y_ref,
                   w1_f, w3_f, w2_f, w1_s, w3_s, w2_s, sems, *, layer):
    i = pl.program_id(0)

    def weight_copies(j):
        e = eseq_ref[j]
        slot = j % W_RING
        return [pltpu.make_async_copy(w1_hbm.at[layer, e], w1_f.at[slot], sems.at[slot, 0]),
                pltpu.make_async_copy(w3_hbm.at[layer, e], w3_f.at[slot], sems.at[slot, 1]),
                pltpu.make_async_copy(w2_hbm.at[layer, e], w2_f.at[slot], sems.at[slot, 2])]

    def start_if_exists(j):
        @pl.when(j < nex_ref[0])
        def _():
            for cp in weight_copies(j):
                cp.start()

    @pl.when(i < nu_ref[0])
    def _():
        j = ord_ref[i]

        @pl.when(i == 0)
        def _():
            for ahead in range(W_RING - 1):
                start_if_exists(ahead)

        @pl.when((i == 0) | (j != ord_ref[jnp.maximum(i - 1, 0)]))
        def _():
            start_if_exists(j + W_RING - 1)
            for cp in weight_copies(j):
                cp.wait()
            slot = j % W_RING
            w1_s[...] = w1_f[slot].astype(bf16)
            w3_s[...] = w3_f[slot].astype(bf16)
            w2_s[...] = w2_f[slot].astype(bf16)

        row = lax.broadcasted_iota(i32, x_ref.shape, 0)
        x_hi, x_lo = _unpack_pairs(jnp.where(row < bv_ref[i], x_ref[...], 0))
        x = jnp.concatenate([x_hi, x_lo], axis=1).astype(bf16)
        act = (_silu(_dot(x, w1_s[...])) * _dot(x, w3_s[...])).astype(bf16)
        y_ref[...] = _pack_pairs(_dot(act, w2_s[...]))


def _expert_call(n_used, blk_valid, blk_ord, expert_seq, n_seq_experts, slots, w1, w3, w2, layer):
    nb = slots.shape[0] // SLOT_BLOCK
    blk = lambda i, nu, *_: (jnp.minimum(i, nu[0] - 1), 0)
    hbm = pl.BlockSpec(memory_space=pl.ANY)
    grid_spec = pltpu.PrefetchScalarGridSpec(
        num_scalar_prefetch=5,
        grid=(nb,),
        in_specs=[pl.BlockSpec((SLOT_BLOCK, D_PACK), blk), hbm, hbm, hbm],
        out_specs=pl.BlockSpec((SLOT_BLOCK, D_PACK), blk),
        scratch_shapes=[pltpu.VMEM((W_RING, D_MODEL, EXPERT_FF), f32), pltpu.VMEM((W_RING, D_MODEL, EXPERT_FF), f32),
                        pltpu.VMEM((W_RING, EXPERT_FF, D_MODEL), f32),
                        pltpu.VMEM((D_MODEL, EXPERT_FF), bf16), pltpu.VMEM((D_MODEL, EXPERT_FF), bf16),
                        pltpu.VMEM((EXPERT_FF, D_MODEL), bf16),
                        pltpu.SemaphoreType.DMA((W_RING, 3))],
    )
    return pl.pallas_call(
        functools.partial(_expert_kernel, layer=layer),
        grid_spec=grid_spec,
        out_shape=jax.ShapeDtypeStruct(slots.shape, slots.dtype),
        input_output_aliases={5: 0},
        compiler_params=_cparams(("arbitrary",)),
        name="experts",
    )(n_used, blk_valid, blk_ord, expert_seq, n_seq_experts, slots, w1, w3, w2)


def _gather_rows_call(table, idx):
    n_idx = idx.shape[0]
    d = table.shape[1]
    mesh, n_workers, worker_id = _sc_workers()
    per_worker = n_idx // n_workers
    assert n_idx % (n_workers * SC_CHUNK) == 0

    n_chunks = per_worker // SC_CHUNK
    assert n_chunks % 2 == 0

    def body(table_hbm, idx_hbm, out_hbm, idx_a, idx_b, rows_a, rows_b, sem_a, sem_b):
        base = worker_id() * per_worker
        bufs = ((idx_a, rows_a, sem_a), (idx_b, rows_b, sem_b))

        def start_gather(c, buf):
            idx_v, rows_v, sem = buf
            off = pl.multiple_of(base + c * SC_CHUNK, SC_CHUNK)
            pltpu.sync_copy(idx_hbm.at[pl.ds(off, SC_CHUNK)], idx_v)
            pltpu.async_copy(table_hbm.at[idx_v], rows_v, sem)

        start_gather(0, bufs[0])

        @pl.loop(0, n_chunks, step=2)
        def _(c0):
            for b in range(2):
                c = c0 + b
                idx_v, rows_v, sem = bufs[b]

                @pl.when(c + 1 < n_chunks)
                def _():
                    start_gather(c + 1, bufs[1 - b])

                pltpu.make_async_copy(table_hbm.at[idx_v], rows_v, sem).wait()
                off = pl.multiple_of(base + c * SC_CHUNK, SC_CHUNK)
                pltpu.sync_copy(rows_v, out_hbm.at[pl.ds(off, SC_CHUNK)])

    return pl.kernel(
        body,
        out_type=jax.ShapeDtypeStruct((n_idx, d), table.dtype),
        mesh=mesh,
        scratch_types=[pltpu.VMEM((SC_CHUNK,), i32), pltpu.VMEM((SC_CHUNK,), i32),
                       pltpu.VMEM((SC_CHUNK, d), table.dtype), pltpu.VMEM((SC_CHUNK, d), table.dtype),
                       pltpu.SemaphoreType.DMA, pltpu.SemaphoreType.DMA],
        name="gather_rows",
    )(table, idx)


def _combine_kernel(gate_ref, xb_ref, mod_ref, y_ref, o_ref):
    g2 = mod_ref[0][:, 5 * D_MODEL:6 * D_MODEL]
    gates = gate_ref[...]
    acc_hi = jnp.zeros((gates.shape[0], D_PACK), f32)
    acc_lo = jnp.zeros((gates.shape[0], D_PACK), f32)
    for k in range(TOP_K):
        y_hi, y_lo = _unpack_pairs(y_ref[k])
        acc_hi = acc_hi + gates[:, k:k + 1] * y_hi
        acc_lo = acc_lo + gates[:, k:k + 1] * y_lo
    o_ref[:, 0:D_PACK] = xb_ref[:, 0:D_PACK] + g2[:, 0:D_PACK] * acc_hi
    o_ref[:, D_PACK:D_MODEL] = xb_ref[:, D_PACK:D_MODEL] + g2[:, D_PACK:D_MODEL] * acc_lo


def _combine_call(gates, xbase, mod, ygath, seq_row_c, row0, n_rows):
    tm = COMBINE_TILE
    assert row0 % tm == 0 and n_rows % tm == 0
    b0 = row0 // tm
    return pl.pallas_call(
        _combine_kernel,
        grid=(n_rows // tm,),
        in_specs=[pl.BlockSpec((tm, TOP_K), lambda i: (b0 + i, 0)),
                  pl.BlockSpec((tm, D_MODEL), lambda i: (b0 + i, 0)),
                  pl.BlockSpec((1, 1, 6 * D_MODEL), lambda i: (seq_row_c(b0 + i), 0, 0)),
                  pl.BlockSpec((TOP_K, tm, D_PACK), lambda i: (0, b0 + i, 0))],
        out_specs=pl.BlockSpec((tm, D_MODEL), lambda i: (i, 0)),
        out_shape=jax.ShapeDtypeStruct((n_rows, D_MODEL), f32),
        compiler_params=_cparams(("parallel",)),
        name="combine",
    )(gates, xbase, mod, ygath)


def _block_avg(width, group):
    r = np.arange(width)
    return jnp.asarray((r[:, None] // group == r[None, :] // group).astype(np.float32) / group, dtype=bf16)


def _kv_replicate():
    c = np.arange(ATTN_W)
    src = (c // (ATTN_W // N_KV_HEADS)) * HEAD_DIM + c % HEAD_DIM
    return jnp.asarray((np.arange(KV_W)[:, None] == src[None, :]).astype(np.float32), dtype=bf16)


def _rope_lane_tables(n_tok):
    rows = n_tok // GRID_W
    r = jnp.repeat(jnp.arange(rows, dtype=f32), GRID_W)
    col = jnp.tile(jnp.arange(GRID_W, dtype=f32), rows)
    inv = ROPE_THETA ** (-jnp.arange(ROPE_FREQ, dtype=f32) / ROPE_FREQ)
    ar = r[:, None] * inv
    ac = col[:, None] * inv
    cos_h = jnp.concatenate([jnp.cos(ar), jnp.cos(ar), jnp.cos(ac), jnp.cos(ac)], axis=-1)
    sin_h = jnp.concatenate([-jnp.sin(ar), jnp.sin(ar), -jnp.sin(ac), jnp.sin(ac)], axis=-1)
    return jnp.tile(cos_h, (1, N_HEADS)), jnp.tile(sin_h, (1, N_HEADS))


def _block_diag(w):
    nb, bw, _ = w.shape
    eye = jnp.eye(nb, dtype=w.dtype)
    return (w[:, :, None, :] * eye[:, None, :, None]).reshape(nb * bw, nb * bw)


def _gla_state_in(s):
    bsz = s.shape[0]
    eye = jnp.eye(GLA_H, dtype=s.dtype)
    st = jnp.swapaxes(s, -1, -2)
    big = st[:, :, :, :, None, :] * eye[None, None, :, None, :, None]
    return big.reshape(bsz, 2, GLA_W, GLA_W)


def kernel(x_prompt, x_sample, cache_k, cache_v, state_lru, state_gla, c, c_ctx, ada_w, ada_b, norm1_g, norm2_g, w_in, q_norm_g, k_norm_g, attn_out_g, conv_w, conv_b, lru_wa, lru_ba, lru_wi, lru_bi, lru_lambda, lru_out_g, gla_wa2, gla_ba, gla_out_g, w_out, router_w, router_b, exp_w1, exp_w3, exp_w2, sh_w1, sh_w3, sh_w2):
    bc, tc, _ = x_prompt.shape
    bl, tl, _ = x_sample.shape
    depth = w_in.shape[0]
    nc = bc * tc
    n = nc + bl * tl
    past = cache_k.shape[2]
    assert tc == ROW_TILE and tl % ROW_TILE == 0 and nc % tl == 0 and bl + 1 <= SUBLANES
    assert n % ROW_TILE == 0 and n % COMBINE_TILE == 0

    def seq_row_for(tile):
        def seq_row(i):
            return jnp.where(i < nc // tile, 0, 1 + (i - nc // tile) // (tl // tile))
        return seq_row

    seq_row = seq_row_for(ROW_TILE)
    seq_row_c = seq_row_for(COMBINE_TILE)

    x = jnp.concatenate([x_prompt.reshape(nc, D_MODEL), x_sample.reshape(bl * tl, D_MODEL)], axis=0)
    cond = jnp.zeros((SUBLANES, D_MODEL), f32).at[0].set(c_ctx).at[1:1 + bl].set(c)
    mods = _ada_call(cond, ada_w, ada_b)

    bmq = _block_avg(ATTN_W, HEAD_DIM)
    bmk = _block_avg(KV_W, HEAD_DIM)
    bmg = _block_avg(GLA_W, GLA_DK)
    rep = _kv_replicate()
    cos_t, sin_t = _rope_lane_tables(tl)
    n_slots = -(-(n * TOP_K + N_EXPERTS * (SLOT_BLOCK - 1)) // SLOT_BLOCK) * SLOT_BLOCK
    tile8 = lambda v: jnp.tile(v, N_HEADS)[None, :]

    ks, vs, lrus, glas = [], [], [], []
    for l in range(depth):
        mod = mods[l].reshape(SUBLANES, 1, 6 * D_MODEL)
        w_in_p = jnp.pad(w_in[l].astype(bf16), ((0, 0), (0, LANES - 2 * GLA_RANK)))
        pa, pl_, pg = _proj_call(x, mod, norm1_g[l][None, :], w_in_p, seq_row)

        qg, kg, og = tile8(q_norm_g[l]), jnp.tile(k_norm_g[l], N_KV_HEADS)[None, :], attn_out_g[l][None, :]
        attn_c, k_new, v_new = _attn_ctx_call(pa, bc, tc, qg, kg, og, bmq, bmk, rep)
        attn_l = _attn_lat_call(pa, nc, bl, tl, cache_k[:, l].reshape(bl, past, KV_W),
                                cache_v[:, l].reshape(bl, past, KV_W), cos_t, sin_t, qg, kg, og, bmq, bmk, rep)
        ks.append(k_new.reshape(bc, tc, N_KV_HEADS, HEAD_DIM))
        vs.append(v_new.reshape(bc, tc, N_KV_HEADS, HEAD_DIM))

        wg = jnp.concatenate([_block_diag(lru_wa[l, 0]), _block_diag(lru_wi[l, 0]),
                              _block_diag(lru_wa[l, 1]), _block_diag(lru_wi[l, 1])], axis=1).astype(bf16)
        bg = jnp.concatenate([lru_ba[l, 0], lru_bi[l, 0], lru_ba[l, 1], lru_bi[l, 1]])[None, :]
        lru_args = (conv_w[l], conv_b[l][None, :], wg, bg, lru_lambda[l], lru_out_g[l][None, :])
        lru_c, lru_fin = _lru_call(pl_, 0, bc, tc, jnp.zeros((bc, 2, LRU_W), f32), *lru_args)
        lru_l, _ = _lru_call(pl_, nc, bl, tl, state_lru[:, l], *lru_args)
        lrus.append(lru_fin)

        wa = jnp.zeros((LANES, 2 * GLA_W), f32)
        wa = wa.at[0:GLA_RANK, 0:GLA_W].set(gla_wa2[l, 0]).at[GLA_RANK:2 * GLA_RANK, GLA_W:].set(gla_wa2[l, 1])
        gla_args = (wa.astype(bf16), gla_ba[l].reshape(1, 2 * GLA_W), gla_out_g[l].reshape(1, GLA_W), bmg)
        gla_c, gla_fin = _gla_call(pg, 0, bc, tc, None, *gla_args)
        gla_l, _ = _gla_call(pg, nc, bl, tl, _gla_state_in(state_gla[:, l]), *gla_args)
        glas.append(gla_fin)

        rw_t = router_w[l].T
        rw_hi = rw_t.astype(bf16)
        rw_lo = (rw_t - rw_hi.astype(f32)).astype(bf16)
        xbase, h2, scores_t = _out_call(x, (attn_c, lru_c, gla_c), (attn_l, lru_l, gla_l), mod, w_out[l].astype(bf16),
                                        norm2_g[l][None, :], rw_hi, rw_lo, sh_w1[l].astype(bf16),
                                        sh_w3[l].astype(bf16), sh_w2[l].astype(bf16), seq_row)

        idx, gates, rank, counts = _route_call(scores_t, router_b[l][:, None])
        cnt = counts[:, 0].astype(i32)
        padded = (cnt + SLOT_BLOCK - 1) // SLOT_BLOCK * SLOT_BLOCK
        padded_end = jnp.cumsum(padded)
        dest_km = _dest_call(idx, rank, (padded_end - padded).astype(f32)[:, None])
        nb = n_slots // SLOT_BLOCK
        blk_first = jnp.arange(nb, dtype=i32) * SLOT_BLOCK
        blk_e = jnp.minimum(jnp.sum((padded_end[None, :] <= blk_first[:, None]).astype(i32), axis=1), N_EXPERTS - 1)
        blk_valid = jnp.clip((padded_end - padded + cnt)[blk_e] - blk_first, 0, SLOT_BLOCK)
        n_used = padded_end[-1:] // SLOT_BLOCK
        owns = jnp.cumsum((cnt > 0).astype(i32))
        blk_ord = (owns - 1)[blk_e]
        expert_seq = jnp.minimum(jnp.sum((owns[None, :] <= jnp.arange(N_EXPERTS, dtype=i32)[:, None]).astype(i32),
                                         axis=1), N_EXPERTS - 1)
        slots = _scatter_rows_call(h2, dest_km, n_slots)
        slots = _expert_call(n_used, blk_valid, blk_ord, expert_seq, owns[-1:], slots, exp_w1, exp_w3, exp_w2, l)
        ygath = _gather_rows_call(slots, dest_km.reshape(-1)).reshape(TOP_K, n, D_PACK)
        gates_t = gates.T
        if l + 1 < depth:
            x = _combine_call(gates_t, xbase, mod, ygath, seq_row_c, 0, n)
        else:
            y_prompt = _combine_call(gates_t, xbase, mod, ygath, seq_row_c, 0, nc).reshape(bc, tc, D_MODEL)
            y_sample = _combine_call(gates_t, xbase, mod, ygath, seq_row_c, nc, n - nc).reshape(bl, tl, D_MODEL)

    return (y_prompt, y_sample, jnp.stack(ks, axis=1), jnp.stack(vs, axis=1),
            jnp.stack(lrus, axis=1), jnp.stack(glas, axis=1))
```

```python
import functools

import jax
import jax.numpy as jnp
import numpy as np
from jax import lax
from jax.experimental import pallas as pl
from jax.experimental.pallas import tpu as pltpu
from jax.experimental.pallas import tpu_sc as plsc

f32 = jnp.float32
bf16 = jnp.bfloat16
i32 = jnp.int32

D_MODEL = 1024
N_HEADS = 8
N_KV_HEADS = 2
HEAD_DIM = 64
ATTN_W = N_HEADS * HEAD_DIM
KV_W = N_KV_HEADS * HEAD_DIM
GRID_W = 64
ROPE_FREQ = HEAD_DIM // 4
ROPE_THETA = 10000.0
LRU_W = 256
LRU_BLOCKS = 4
LRU_C = 8.0
CONV_W = 4
CONV_LEFT = 2
GLA_H = 4
GLA_DK = 64
GLA_W = 256
GLA_RANK = 16
GLA_TAU = 16.0
N_EXPERTS = 256
TOP_K = 8
EXPERT_FF = 256
ROUTED_SCALE = 2.5
EPS = 1e-6

LANES = 128
SUBLANES = 8
ROW_TILE = 256
PROJ_TILE = 512
GLA_BLOCK = 256
SLOT_BLOCK = 256
STEP_BLOCKS = 4
W_RING = 3
COMBINE_TILE = 256
SC_CHUNK = 64
D_PACK = D_MODEL // 2
VMEM_LIMIT = 56 * 1024 * 1024


def _cparams(sem, vmem=VMEM_LIMIT):
    return pltpu.CompilerParams(dimension_semantics=sem, vmem_limit_bytes=vmem)


def _dot(a, b):
    return jnp.dot(a, b, preferred_element_type=f32)


def _dot_nt(a, b):
    return lax.dot_general(a, b, (((1,), (1,)), ((), ())), preferred_element_type=f32)


def _dot_tn(a, b):
    return lax.dot_general(a, b, (((0,), (0,)), ((), ())), preferred_element_type=f32)


def _split(x):
    hi = x.astype(bf16)
    lo = (x - hi.astype(f32)).astype(bf16)
    return hi, lo


def _dot_x2(x, w):
    hi, lo = _split(x)
    return _dot(hi, w) + _dot(lo, w)


def _dot_2x(m, x):
    hi, lo = _split(x)
    return _dot(m, hi) + _dot(m, lo)


def _dot3(a, b_hi, b_lo):
    a_hi, a_lo = _split(a)
    return _dot(a_hi, b_hi) + _dot(a_lo, b_hi) + _dot(a_hi, b_lo)


def _sigmoid(x):
    return 1.0 / (1.0 + jnp.exp(-x))


def _silu(x):
    return x * _sigmoid(x)


def _softplus(x):
    return jnp.maximum(x, 0.0) + jnp.log(1.0 + jnp.exp(-jnp.abs(x)))


def _gelu_tanh(x):
    return 0.5 * x * (1.0 + jnp.tanh(0.7978845608028654 * (x + 0.044715 * x * x * x)))


def _rms(x, g):
    return x * lax.rsqrt(jnp.mean(x * x, axis=-1, keepdims=True) + EPS) * g


def _pack_pairs(x):
    c = x.shape[1] // 2
    hi = lax.bitcast_convert_type(x[:, :c].astype(bf16).astype(f32), i32)
    lo = lax.bitcast_convert_type(x[:, c:].astype(bf16).astype(f32), i32)
    return hi | lax.shift_right_logical(lo, 16)


def _unpack_pairs(w):
    hi = lax.bitcast_convert_type(w & jnp.int32(-65536), f32)
    lo = lax.bitcast_convert_type(w << 16, f32)
    return hi, lo


def _head_rms(x, g, bm):
    ms = _dot_x2(x * x, bm)
    return x * lax.rsqrt(ms + EPS) * g


def _ada_kernel(c_ref, w_ref, b_ref, o_ref):
    s = _silu(c_ref[...])
    w = w_ref[0]
    w_hi, w_lo = _split(w)
    o_ref[0] = _dot3(s, w_hi, w_lo) + b_ref[0]


def _ada_call(cond, ada_w, ada_b):
    depth = ada_w.shape[0]
    nt = 1536
    return pl.pallas_call(
        _ada_kernel,
        grid=(depth, 6 * D_MODEL // nt),
        in_specs=[pl.BlockSpec((SUBLANES, D_MODEL), lambda l, j: (0, 0)),
                  pl.BlockSpec((1, D_MODEL, nt), lambda l, j: (l, 0, j)),
                  pl.BlockSpec((1, 1, nt), lambda l, j: (l, 0, j))],
        out_specs=pl.BlockSpec((1, SUBLANES, nt), lambda l, j: (l, 0, j)),
        out_shape=jax.ShapeDtypeStruct((depth, SUBLANES, 6 * D_MODEL), f32),
        compiler_params=_cparams(("parallel", "parallel")),
        name="ada_mod",
    )(cond, ada_w, ada_b.reshape(depth, 1, 6 * D_MODEL))


PROJ_A = ATTN_W + 2 * KV_W
PROJ_L = 2 * LRU_W
PROJ_G = 4 * GLA_W + LANES


def _proj_kernel(x_ref, mod_ref, g_ref, w_ref, oa_ref, ol_ref, og_ref):
    mod = mod_ref[0]
    sh = mod[:, 0:D_MODEL]
    sc = mod[:, D_MODEL:2 * D_MODEL]
    h = (_rms(x_ref[...], g_ref[...]) * (1.0 + sc) + sh).astype(bf16)
    p = _dot(h, w_ref[...])
    oa_ref[...] = p[:, 0:PROJ_A].astype(bf16)
    ol_ref[...] = p[:, PROJ_A:PROJ_A + PROJ_L].astype(bf16)
    og_ref[...] = p[:, PROJ_A + PROJ_L:PROJ_A + PROJ_L + PROJ_G].astype(bf16)


def _proj_call(x, mod, g, w, seq_row):
    n = x.shape[0]
    cols = PROJ_A + PROJ_L + PROJ_G
    return pl.pallas_call(
        _proj_kernel,
        grid=(n // PROJ_TILE,),
        in_specs=[pl.BlockSpec((PROJ_TILE, D_MODEL), lambda i: (i, 0)),
                  pl.BlockSpec((1, 1, 6 * D_MODEL), lambda i: (seq_row(i), 0, 0)),
                  pl.BlockSpec((1, D_MODEL), lambda i: (0, 0)),
                  pl.BlockSpec((D_MODEL, cols), lambda i: (0, 0))],
        out_specs=[pl.BlockSpec((PROJ_TILE, PROJ_A), lambda i: (i, 0)),
                   pl.BlockSpec((PROJ_TILE, PROJ_L), lambda i: (i, 0)),
                   pl.BlockSpec((PROJ_TILE, PROJ_G), lambda i: (i, 0))],
        out_shape=[jax.ShapeDtypeStruct((n, PROJ_A), bf16),
                   jax.ShapeDtypeStruct((n, PROJ_L), bf16),
                   jax.ShapeDtypeStruct((n, PROJ_G), bf16)],
        compiler_params=_cparams(("parallel",)),
        name="in_proj",
    )(x, mod, g, w)


def _rope(x, cos_t, sin_t):
    w = x.shape[1]
    up = pltpu.roll(x, w - ROPE_FREQ, 1)
    dn = pltpu.roll(x, ROPE_FREQ, 1)
    lane = lax.broadcasted_iota(i32, x.shape, 1)
    partner = jnp.where((lane & (2 * ROPE_FREQ - 1)) < ROPE_FREQ, up, dn)
    return x * cos_t + partner * sin_t


def _attend(q, kk_ref, vv_ref, o_ref):
    tq = q.shape[0]
    gw = ATTN_W // N_KV_HEADS
    lane = lax.broadcasted_iota(i32, (tq, gw), 1)
    for g in range(N_KV_HEADS):
        qg = q[:, g * gw:(g + 1) * gw]
        kg = kk_ref[:, g * gw:(g + 1) * gw]
        vg = vv_ref[:, g * gw:(g + 1) * gw]
        acc = jnp.zeros((tq, gw), f32)
        for hh in range(N_HEADS // N_KV_HEADS):
            hm = (lane >> 6) == hh
            s = _dot_nt(jnp.where(hm, qg, 0.0).astype(bf16), kg)
            m = jnp.max(s, axis=-1, keepdims=True)
            p = jnp.exp(s - m)
            l = jnp.sum(p, axis=-1, keepdims=True)
            o = _dot(p.astype(bf16), vg) / l
            acc = jnp.where(hm, o, acc)
        o_ref[:, g * gw:(g + 1) * gw] = acc


def _attn_ctx_kernel(p_ref, qg_ref, kg_ref, og_ref, bmq_ref, bmk_ref, rep_ref,
                     o_ref, ko_ref, vo_ref, kk_s, vv_s, o_s):
    p = p_ref[...].astype(f32)
    q = _head_rms(p[:, 0:ATTN_W], qg_ref[...], bmq_ref[...])
    k = _head_rms(p[:, ATTN_W:ATTN_W + KV_W], kg_ref[...], bmk_ref[...])
    v = p[:, ATTN_W + KV_W:PROJ_A]
    ko_ref[...] = k
    vo_ref[...] = v
    kk_s[...] = _dot(k.astype(bf16), rep_ref[...]).astype(bf16)
    vv_s[...] = _dot(v.astype(bf16), rep_ref[...]).astype(bf16)
    _attend(q * HEAD_DIM ** -0.5, kk_s, vv_s, o_s)
    o_ref[...] = _rms(o_s[...], og_ref[...]).astype(bf16)


def _attn_ctx_call(pa, n_seq, t, qg, kg, og, bmq, bmk, rep):
    assert t == ROW_TILE
    const = lambda i: (0, 0)
    return pl.pallas_call(
        _attn_ctx_kernel,
        grid=(n_seq,),
        in_specs=[pl.BlockSpec((t, PROJ_A), lambda i: (i, 0)),
                  pl.BlockSpec((1, ATTN_W), const), pl.BlockSpec((1, KV_W), const), pl.BlockSpec((1, ATTN_W), const),
                  pl.BlockSpec((ATTN_W, ATTN_W), const), pl.BlockSpec((KV_W, KV_W), const),
                  pl.BlockSpec((KV_W, ATTN_W), const)],
        out_specs=[pl.BlockSpec((t, ATTN_W), lambda i: (i, 0)),
                   pl.BlockSpec((t, KV_W), lambda i: (i, 0)),
                   pl.BlockSpec((t, KV_W), lambda i: (i, 0))],
        out_shape=[jax.ShapeDtypeStruct((n_seq * t, ATTN_W), bf16),
                   jax.ShapeDtypeStruct((n_seq * t, KV_W), f32),
                   jax.ShapeDtypeStruct((n_seq * t, KV_W), f32)],
        scratch_shapes=[pltpu.VMEM((t, ATTN_W), bf16), pltpu.VMEM((t, ATTN_W), bf16), pltpu.VMEM((t, ATTN_W), f32)],
        compiler_params=_cparams(("parallel",)),
        name="attn_ctx",
    )(pa, qg, kg, og, bmq, bmk, rep)


def _attn_lat_kernel(q_ref, kv_ref, ck_ref, cv_ref, cq_ref, sq_ref, ckk_ref, skk_ref,
                     qg_ref, kg_ref, og_ref, bmq_ref, bmk_ref, rep_ref,
                     o_ref, kk_s, vv_s, o_s, *, past):
    @pl.when(pl.program_id(1) == 0)
    def _():
        kv = kv_ref[...].astype(f32)
        k = _head_rms(kv[:, 0:KV_W], kg_ref[...], bmk_ref[...])
        k = _rope(k, ckk_ref[...], skk_ref[...])
        v = kv[:, KV_W:2 * KV_W]
        kk_s[0:past, :] = _dot(ck_ref[0].astype(bf16), rep_ref[...]).astype(bf16)
        vv_s[0:past, :] = _dot(cv_ref[0].astype(bf16), rep_ref[...]).astype(bf16)
        kk_s[past:, :] = _dot(k.astype(bf16), rep_ref[...]).astype(bf16)
        vv_s[past:, :] = _dot(v.astype(bf16), rep_ref[...]).astype(bf16)

    q = _head_rms(q_ref[...].astype(f32), qg_ref[...], bmq_ref[...])
    q = _rope(q, cq_ref[...], sq_ref[...])
    _attend(q * HEAD_DIM ** -0.5, kk_s, vv_s, o_s)
    o_ref[...] = _rms(o_s[...], og_ref[...]).astype(bf16)


def _attn_lat_call(pa, row0, n_seq, t, cache_k, cache_v, cq, sq, qg, kg, og, bmq, bmk, rep):
    past = cache_k.shape[1]
    nq = t // ROW_TILE
    assert row0 % t == 0 and t % ROW_TILE == 0
    qb0 = row0 // ROW_TILE
    sb0 = row0 // t
    const = lambda b, j: (0, 0)
    return pl.pallas_call(
        functools.partial(_attn_lat_kernel, past=past),
        grid=(n_seq, nq),
        in_specs=[pl.BlockSpec((ROW_TILE, ATTN_W), lambda b, j: (qb0 + b * nq + j, 0)),
                  pl.BlockSpec((t, 2 * KV_W), lambda b, j: (sb0 + b, ATTN_W // (2 * KV_W))),
                  pl.BlockSpec((1, past, KV_W), lambda b, j: (b, 0, 0)),
                  pl.BlockSpec((1, past, KV_W), lambda b, j: (b, 0, 0)),
                  pl.BlockSpec((ROW_TILE, ATTN_W), lambda b, j: (j, 0)),
                  pl.BlockSpec((ROW_TILE, ATTN_W), lambda b, j: (j, 0)),
                  pl.BlockSpec((t, KV_W), const), pl.BlockSpec((t, KV_W), const),
                  pl.BlockSpec((1, ATTN_W), const), pl.BlockSpec((1, KV_W), const), pl.BlockSpec((1, ATTN_W), const),
                  pl.BlockSpec((ATTN_W, ATTN_W), const), pl.BlockSpec((KV_W, KV_W), const),
                  pl.BlockSpec((KV_W, ATTN_W), const)],
        out_specs=pl.BlockSpec((ROW_TILE, ATTN_W), lambda b, j: (b * nq + j, 0)),
        out_shape=jax.ShapeDtypeStruct((n_seq * t, ATTN_W), bf16),
        scratch_shapes=[pltpu.VMEM((past + t, ATTN_W), bf16), pltpu.VMEM((past + t, ATTN_W), bf16),
                        pltpu.VMEM((ROW_TILE, ATTN_W), f32)],
        compiler_params=_cparams(("parallel", "arbitrary")),
        name="attn_lat",
    )(pa, pa, cache_k, cache_v, cq, sq, cq, sq, qg, kg, og, bmq, bmk, rep)


def _lru_kernel(x_ref, h0_ref, cw_ref, cb_ref, wg_ref, bg_ref, lam_ref, g_ref,
                o_ref, fin_ref, xs, a_s, u_s, h_s, *, t):
    ng = t // SUBLANES
    lx = x_ref[:, 0:LRU_W].astype(f32)
    ly = x_ref[:, LRU_W:2 * LRU_W].astype(f32)
    xs[0:SUBLANES, :] = jnp.zeros((SUBLANES, LRU_W), f32)
    xs[SUBLANES + t:2 * SUBLANES + t, :] = jnp.zeros((SUBLANES, LRU_W), f32)
    xs[SUBLANES:SUBLANES + t, :] = lx
    xc = jnp.broadcast_to(cb_ref[...], (t, LRU_W))
    for j in range(CONV_W):
        off = SUBLANES + j - CONV_LEFT
        xc = xc + xs[off:off + t, :] * cw_ref[j:j + 1, :]
    gates = _sigmoid(_dot(xc.astype(bf16), wg_ref[...]) + bg_ref[...])
    sp = _softplus(-lam_ref[...])
    sub = lax.broadcasted_iota(i32, (t, LRU_W), 0) & (SUBLANES - 1)
    for d in range(2):
        r = gates[:, (2 * d) * LRU_W:(2 * d + 1) * LRU_W]
        gi = gates[:, (2 * d + 1) * LRU_W:(2 * d + 2) * LRU_W]
        log_a = -LRU_C * r * sp[d:d + 1, :]
        a = jnp.exp(log_a)
        th = jnp.tanh(log_a)
        u = jnp.sqrt(-2.0 * th / (1.0 - th)) * (gi * xc)
        for s in (1, 2, 4):
            if d == 0:
                a_sh = pltpu.roll(a, s, 0)
                u_sh = pltpu.roll(u, s, 0)
                ok = sub >= s
            else:
                a_sh = pltpu.roll(a, t - s, 0)
                u_sh = pltpu.roll(u, t - s, 0)
                ok = sub < SUBLANES - s
            u = jnp.where(ok, a * u_sh + u, u)
            a = jnp.where(ok, a * a_sh, a)
        a_s[...] = a
        u_s[...] = u
        h0 = jnp.broadcast_to(h0_ref[0, d:d + 1, :], (SUBLANES, LRU_W))
        edge = SUBLANES - 1 if d == 0 else 0

        def body(i, carry, d=d, edge=edge):
            g = i if d == 0 else ng - 1 - i
            rows = pl.ds(pl.multiple_of(g * SUBLANES, SUBLANES), SUBLANES)
            h = a_s[rows, :] * carry + u_s[rows, :]
            if d == 0:
                h_s[rows, :] = h
            else:
                h_s[rows, :] = h_s[rows, :] + h
            return jnp.broadcast_to(h[edge:edge + 1, :], (SUBLANES, LRU_W))

        last = lax.fori_loop(0, ng, body, h0)
        fin_ref[0, d:d + 1, :] = last[0:1, :]
    o_ref[...] = _rms(h_s[...] * _gelu_tanh(ly), g_ref[...]).astype(bf16)


def _lru_call(pl_, row0, n_seq, t, h0, cw, cb, wg, bg, lam, g):
    sb0 = row0 // t
    assert row0 % t == 0
    const = lambda b: (0, 0)
    in_specs = [pl.BlockSpec((t, PROJ_L), lambda b: (sb0 + b, 0)),
                pl.BlockSpec((1, 2, LRU_W), lambda b: (b, 0, 0)),
                pl.BlockSpec((CONV_W, LRU_W), const), pl.BlockSpec((1, LRU_W), const),
                pl.BlockSpec((LRU_W, 4 * LRU_W), const), pl.BlockSpec((1, 4 * LRU_W), const),
                pl.BlockSpec((2, LRU_W), const), pl.BlockSpec((1, LRU_W), const)]
    args = [pl_, h0, cw, cb, wg, bg, lam, g]
    return pl.pallas_call(
        functools.partial(_lru_kernel, t=t),
        grid=(n_seq,),
        in_specs=in_specs,
        out_specs=[pl.BlockSpec((t, LRU_W), lambda b: (b, 0)),
                   pl.BlockSpec((1, 2, LRU_W), lambda b: (b, 0, 0))],
        out_shape=[jax.ShapeDtypeStruct((n_seq * t, LRU_W), bf16), jax.ShapeDtypeStruct((n_seq, 2, LRU_W), f32)],
        scratch_shapes=[pltpu.VMEM((t + 2 * SUBLANES, LRU_W), f32), pltpu.VMEM((t, LRU_W), f32),
                        pltpu.VMEM((t, LRU_W), f32), pltpu.VMEM((t, LRU_W), f32)],
        compiler_params=_cparams(("parallel",)),
        name="rglru_t%d" % t,
    )(*args)


def _bcast_rows(b, period, off):
    w = b.shape[1]
    return jnp.concatenate(
        [jnp.broadcast_to(b[i * period + off:i * period + off + 1, :], (period, w)) for i in range(b.shape[0] // period)],
        axis=0)


def _gla_block(q, k, v, la, reverse, st_ref):
    n = GLA_BLOCK
    row = lax.broadcasted_iota(i32, (n, n), 0)
    col = lax.broadcasted_iota(i32, (n, n), 1)
    same64 = (row >> 6) == (col >> 6)
    same32 = (row >> 5) == (col >> 5)
    same16 = (row >> 4) == (col >> 4)
    if not reverse:
        cum = (same64 & (col <= row)).astype(bf16)
        m1 = same64 & ((row & 63) >= 32) & ((col & 63) < 32)
        m2 = same32 & ((row & 31) >= 16) & ((col & 31) < 16)
        m3 = same16 & (col <= row)
        offs = (31, 15, 7, 63)
    else:
        cum = (same64 & (col >= row)).astype(bf16)
        m1 = same64 & ((row & 63) < 32) & ((col & 63) >= 32)
        m2 = same32 & ((row & 31) < 16) & ((col & 31) >= 16)
        m3 = same16 & (col >= row)
        offs = (32, 16, 8, 0)
    b = _dot_2x(cum, la)
    r1 = _bcast_rows(b, 64, offs[0])
    r2 = _bcast_rows(b, 32, offs[1])
    r3 = _bcast_rows(b, 16, offs[2])
    bl = _bcast_rows(b, 64, offs[3])
    q1 = q * jnp.exp(jnp.minimum(b - r1, 0.0))
    k1 = (k * jnp.exp(jnp.minimum(r1 - b, 0.0))).astype(bf16)
    q2 = q * jnp.exp(jnp.minimum(b - r2, 0.0))
    k2 = (k * jnp.exp(jnp.minimum(r2 - b, 0.0))).astype(bf16)
    q3 = q * jnp.exp(b - r3)
    k3 = (k * jnp.exp(r3 - b)).astype(bf16)
    qe = (q * jnp.exp(b)).astype(bf16)
    kl = (k * jnp.exp(bl - b)).astype(bf16)
    dec = jnp.exp(bl)
    vb = v.astype(bf16)
    zq = jnp.zeros((n, n), f32)
    intra = jnp.zeros((n, n), f32)
    for h in range(GLA_H):
        hm = (col >> 6) == h
        a1 = _dot_nt(jnp.where(hm, q1, zq).astype(bf16), k1)
        a2 = _dot_nt(jnp.where(hm, q2, zq).astype(bf16), k2)
        a3 = _dot_nt(jnp.where(hm, q3, zq).astype(bf16), k3)
        att = jnp.where(m1, a1, jnp.where(m2, a2, jnp.where(m3, a3, zq)))
        intra = intra + _dot(att.astype(bf16), jnp.where(hm, v, zq).astype(bf16))
    outs = [None] * 4
    for c in (range(4) if not reverse else range(3, -1, -1)):
        rs = slice(64 * c, 64 * c + 64)
        st = st_ref[...]
        inter = _dot_nt(qe[rs], st.astype(bf16))
        kv = _dot_tn(vb[rs], kl[rs])
        drow = dec[64 * c:64 * c + 1, :]
        st_ref[...] = st * drow + jnp.where(same64, kv, zq)
        outs[c] = intra[rs] + inter
    return jnp.concatenate(outs, axis=0)


def _gla_kernel(x_ref, s0_ref, wa_ref, ba_ref, g_ref, bm_ref, o_ref, fin_ref, st_s, o_s, *, t):
    nblk = t // GLA_BLOCK

    def block(j, reverse):
        rows = pl.ds(j * GLA_BLOCK if isinstance(j, int) else pl.multiple_of(j * GLA_BLOCK, GLA_BLOCK), GLA_BLOCK)
        q = x_ref[rows, 0:GLA_W].astype(f32) * GLA_DK ** -0.5
        k = x_ref[rows, GLA_W:2 * GLA_W].astype(f32)
        v = x_ref[rows, 2 * GLA_W:3 * GLA_W].astype(f32)
        ga = x_ref[rows, 4 * GLA_W:4 * GLA_W + LANES]
        d = 1 if reverse else 0
        z = _dot(ga.astype(bf16), wa_ref[:, d * GLA_W:(d + 1) * GLA_W]) + ba_ref[:, d * GLA_W:(d + 1) * GLA_W]
        la = -_softplus(-z) * (1.0 / GLA_TAU)
        o = _gla_block(q, k, v, la, reverse, st_s)
        if reverse:
            o_s[rows, :] = o_s[rows, :] + o
        else:
            o_s[rows, :] = o

    for d in range(2):
        if s0_ref is None:
            st_s[...] = jnp.zeros((GLA_W, GLA_W), f32)
        else:
            st_s[...] = s0_ref[0, d]
        if nblk == 1:
            block(0, d == 1)
        else:
            def body(i, carry, d=d):
                block(i if d == 0 else nblk - 1 - i, d == 1)
                return carry
            lax.fori_loop(0, nblk, body, 0)
        st_t = st_s[...].T
        for h in range(GLA_H):
            fin_ref[0, d, h] = st_t[h * GLA_DK:(h + 1) * GLA_DK, h * GLA_DK:(h + 1) * GLA_DK]
    gg = x_ref[:, 3 * GLA_W:4 * GLA_W].astype(f32)
    o_ref[...] = (_head_rms(o_s[...], g_ref[...], bm_ref[...]) * _silu(gg)).astype(bf16)


def _gla_call(pg, row0, n_seq, t, s0, wa, ba, g, bm):
    sb0 = row0 // t
    assert row0 % t == 0 and t % GLA_BLOCK == 0
    const = lambda b: (0, 0)
    in_specs = [pl.BlockSpec((t, PROJ_G), lambda b: (sb0 + b, 0))]
    args = [pg]
    if s0 is not None:
        in_specs.append(pl.BlockSpec((1, 2, GLA_W, GLA_W), lambda b: (b, 0, 0, 0)))
        args.append(s0)
    in_specs += [pl.BlockSpec((LANES, 2 * GLA_W), const), pl.BlockSpec((1, 2 * GLA_W), const),
                 pl.BlockSpec((1, GLA_W), const), pl.BlockSpec((GLA_W, GLA_W), const)]
    args += [wa, ba, g, bm]

    def kern(*refs):
        refs = list(refs)
        x_ref = refs.pop(0)
        s0_ref = refs.pop(0) if s0 is not None else None
        wa_ref, ba_ref, g_ref, bm_ref = refs[:4]
        _gla_kernel(x_ref, s0_ref, wa_ref, ba_ref, g_ref, bm_ref, *refs[4:], t=t)

    return pl.pallas_call(
        kern,
        grid=(n_seq,),
        in_specs=in_specs,
        out_specs=[pl.BlockSpec((t, GLA_W), lambda b: (b, 0)),
                   pl.BlockSpec((1, 2, GLA_H, GLA_DK, GLA_DK), lambda b: (b, 0, 0, 0, 0))],
        out_shape=[jax.ShapeDtypeStruct((n_seq * t, GLA_W), bf16),
                   jax.ShapeDtypeStruct((n_seq, 2, GLA_H, GLA_DK, GLA_DK), f32)],
        scratch_shapes=[pltpu.VMEM((GLA_W, GLA_W), f32), pltpu.VMEM((t, GLA_W), f32)],
        compiler_params=_cparams(("parallel",)),
        name="gla_t%d" % t,
    )(*args)


def _out_kernel(x_ref, ac_ref, lc_ref, gc_ref, al_ref, ll_ref, gl_ref, mod_ref, wo_ref, g2_ref, rwh_ref, rwl_ref,
                s1_ref, s3_ref, s2_ref, xb_ref, h2_ref, sc_ref, *, n_ctx_tiles):
    is_ctx = pl.program_id(0) < n_ctx_tiles
    pick = lambda c_ref, l_ref: jnp.where(is_ctx, c_ref[...], l_ref[...]).astype(bf16)
    mod = mod_ref[0]
    g1 = mod[:, 2 * D_MODEL:3 * D_MODEL]
    sh2 = mod[:, 3 * D_MODEL:4 * D_MODEL]
    sc2 = mod[:, 4 * D_MODEL:5 * D_MODEL]
    g2 = mod[:, 5 * D_MODEL:6 * D_MODEL]
    m = (_dot(pick(ac_ref, al_ref), wo_ref[0:ATTN_W, :])
         + _dot(pick(lc_ref, ll_ref), wo_ref[ATTN_W:ATTN_W + LRU_W, :])
         + _dot(pick(gc_ref, gl_ref), wo_ref[ATTN_W + LRU_W:ATTN_W + LRU_W + GLA_W, :]))
    x1 = x_ref[...] + g1 * m
    h2 = _rms(x1, g2_ref[...]) * (1.0 + sc2) + sh2
    h2_ref[...] = _pack_pairs(h2)
    h_hi, h_lo = _split(h2)
    logits_t = _dot_nt(rwh_ref[...], h_hi) + _dot_nt(rwh_ref[...], h_lo) + _dot_nt(rwl_ref[...], h_hi)
    sc_ref[...] = _sigmoid(logits_t)
    hb = h2.astype(bf16)
    act = (_silu(_dot(hb, s1_ref[...])) * _dot(hb, s3_ref[...])).astype(bf16)
    xb_ref[...] = x1 + g2 * _dot(act, s2_ref[...])


def _out_call(x, mix_ctx, mix_lat, mod, wo, g2, rwh, rwl, s1, s3, s2, seq_row):
    n = x.shape[0]
    nct = mix_ctx[0].shape[0] // PROJ_TILE
    const = lambda i: (0, 0)
    rowb = lambda w: pl.BlockSpec((PROJ_TILE, w), lambda i: (i, 0))
    ctxb = lambda w: pl.BlockSpec((PROJ_TILE, w), lambda i: (jnp.minimum(i, nct - 1), 0))
    latb = lambda w: pl.BlockSpec((PROJ_TILE, w), lambda i: (jnp.maximum(i - nct, 0), 0))
    return pl.pallas_call(
        functools.partial(_out_kernel, n_ctx_tiles=nct),
        grid=(n // PROJ_TILE,),
        in_specs=[rowb(D_MODEL), ctxb(ATTN_W), ctxb(LRU_W), ctxb(GLA_W), latb(ATTN_W), latb(LRU_W), latb(GLA_W),
                  pl.BlockSpec((1, 1, 6 * D_MODEL), lambda i: (seq_row(i), 0, 0)),
                  pl.BlockSpec((D_MODEL, D_MODEL), const), pl.BlockSpec((1, D_MODEL), const),
                  pl.BlockSpec((N_EXPERTS, D_MODEL), const), pl.BlockSpec((N_EXPERTS, D_MODEL), const),
                  pl.BlockSpec((D_MODEL, EXPERT_FF), const), pl.BlockSpec((D_MODEL, EXPERT_FF), const),
                  pl.BlockSpec((EXPERT_FF, D_MODEL), const)],
        out_specs=[rowb(D_MODEL),
                   rowb(D_PACK),
                   pl.BlockSpec((N_EXPERTS, PROJ_TILE), lambda i: (0, i))],
        out_shape=[jax.ShapeDtypeStruct((n, D_MODEL), f32),
                   jax.ShapeDtypeStruct((n, D_PACK), i32),
                   jax.ShapeDtypeStruct((N_EXPERTS, n), f32)],
        compiler_params=_cparams(("parallel",)),
        name="out_proj",
    )(x, *mix_ctx, *mix_lat, mod, wo, g2, rwh, rwl, s1, s3, s2)


def _set_row(acc, k, row):
    sub = lax.broadcasted_iota(i32, acc.shape, 0)
    return jnp.where(sub == k, jnp.broadcast_to(row, acc.shape), acc)


def _route_kernel(sc_ref, rb_ref, idx_ref, gate_ref, rank_ref, cnt_ref, run_s):
    tm = sc_ref.shape[1]

    @pl.when(pl.program_id(0) == 0)
    def _():
        run_s[...] = jnp.zeros_like(run_s)

    scores = sc_ref[...]
    sel = scores + rb_ref[...]
    erow = lax.broadcasted_iota(i32, (N_EXPERTS, tm), 0).astype(f32)
    neg = jnp.full((N_EXPERTS, tm), -jnp.inf, f32)
    hots = []
    idx_o = jnp.zeros((TOP_K, tm), f32)
    gate_o = jnp.zeros((TOP_K, tm), f32)
    gsum = jnp.zeros((1, tm), f32)
    chosen = jnp.zeros((N_EXPERTS, tm), f32)
    for k in range(TOP_K):
        m = jnp.max(sel, axis=0, keepdims=True)
        idx_f = jnp.min(jnp.where(sel == m, erow, float(N_EXPERTS)), axis=0, keepdims=True)
        hot = erow == idx_f
        gk = jnp.sum(jnp.where(hot, scores, 0.0), axis=0, keepdims=True)
        sel = jnp.where(hot, neg, sel)
        hots.append(hot)
        chosen = jnp.where(hot, 1.0, chosen)
        gsum = gsum + gk
        idx_o = _set_row(idx_o, k, idx_f)
        gate_o = _set_row(gate_o, k, gk)
    gate_ref[...] = gate_o / gsum * ROUTED_SCALE
    idx_ref[...] = idx_o.astype(i32)
    r = lax.broadcasted_iota(i32, (tm, tm), 0)
    c = lax.broadcasted_iota(i32, (tm, tm), 1)
    pos = _dot(chosen.astype(bf16), (r < c).astype(bf16)) + run_s[...]
    rank_o = jnp.zeros((TOP_K, tm), f32)
    for k in range(TOP_K):
        rank_o = _set_row(rank_o, k, jnp.sum(jnp.where(hots[k], pos, 0.0), axis=0, keepdims=True))
    rank_ref[...] = rank_o.astype(i32)
    run_s[...] = run_s[...] + jnp.sum(chosen, axis=1, keepdims=True)
    cnt_ref[...] = run_s[...]


def _route_call(scores_t, rb):
    n = scores_t.shape[1]
    colb = lambda r: pl.BlockSpec((r, ROW_TILE), lambda i: (0, i))
    cnt_spec = pl.BlockSpec((N_EXPERTS, 1), lambda i: (0, 0))
    return pl.pallas_call(
        _route_kernel,
        grid=(n // ROW_TILE,),
        in_specs=[colb(N_EXPERTS), cnt_spec],
        out_specs=[colb(TOP_K), colb(TOP_K), colb(TOP_K), cnt_spec],
        out_shape=[jax.ShapeDtypeStruct((TOP_K, n), i32), jax.ShapeDtypeStruct((TOP_K, n), f32),
                   jax.ShapeDtypeStruct((TOP_K, n), i32), jax.ShapeDtypeStruct((N_EXPERTS, 1), f32)],
        scratch_shapes=[pltpu.VMEM((N_EXPERTS, 1), f32)],
        compiler_params=_cparams(("arbitrary",)),
        name="route",
    )(scores_t, rb)


def _dest_kernel(idx_ref, rank_ref, start_ref, dest_ref):
    tm = idx_ref.shape[1]
    erow = lax.broadcasted_iota(i32, (N_EXPERTS, tm), 0)
    start = jnp.broadcast_to(start_ref[...], (N_EXPERTS, tm))
    idx = idx_ref[...]
    dest = jnp.zeros((TOP_K, tm), f32)
    for k in range(TOP_K):
        hot = erow == idx[k:k + 1, :]
        dest = _set_row(dest, k, jnp.sum(jnp.where(hot, start, 0.0), axis=0, keepdims=True))
    dest_ref[...] = dest.astype(i32) + rank_ref[...]


def _dest_call(idx, rank, start):
    n = idx.shape[1]
    colb = lambda r: pl.BlockSpec((r, ROW_TILE), lambda i: (0, i))
    return pl.pallas_call(
        _dest_kernel,
        grid=(n // ROW_TILE,),
        in_specs=[colb(TOP_K), colb(TOP_K), pl.BlockSpec((N_EXPERTS, 1), lambda i: (0, 0))],
        out_specs=colb(TOP_K),
        out_shape=jax.ShapeDtypeStruct((TOP_K, n), i32),
        compiler_params=_cparams(("parallel",)),
        name="dest",
    )(idx, rank, start)


def _sc_workers():
    info = plsc.get_sparse_core_info()
    mesh = plsc.VectorSubcoreMesh(core_axis_name="c", subcore_axis_name="s")
    worker_id = lambda: lax.axis_index("s") * info.num_cores + lax.axis_index("c")
    return mesh, info.num_cores * info.num_subcores, worker_id


def _scatter_rows_call(rows, dest_km, n_slots):
    n, d = rows.shape
    mesh, n_workers, worker_id = _sc_workers()
    per_worker = n // n_workers
    assert n % (n_workers * SC_CHUNK) == 0

    def body(rows_hbm, idx_hbm, out_hbm, *scratch):
        idx_vs, rows_v, sem = scratch[:TOP_K], scratch[TOP_K], scratch[TOP_K + 1]
        base = worker_id() * per_worker

        @pl.loop(0, per_worker // SC_CHUNK)
        def _(i):
            t0 = pl.multiple_of(base + i * SC_CHUNK, SC_CHUNK)
            pltpu.sync_copy(rows_hbm.at[pl.ds(t0, SC_CHUNK)], rows_v)
            for k in range(TOP_K):
                pltpu.sync_copy(idx_hbm.at[pl.ds(k * n + t0, SC_CHUNK)], idx_vs[k])
            copies = [pltpu.async_copy(rows_v, out_hbm.at[idx_vs[k]], sem) for k in range(TOP_K)]
            for cp in copies:
                cp.wait()

    return pl.kernel(
        body,
        out_type=jax.ShapeDtypeStruct((n_slots, d), rows.dtype),
        mesh=mesh,
        scratch_types=[pltpu.VMEM((SC_CHUNK,), i32)] * TOP_K + [pltpu.VMEM((SC_CHUNK, d), rows.dtype),
                                                                pltpu.SemaphoreType.DMA],
        name="scatter_rows",
    )(rows, dest_km.reshape(-1))


def _expert_kernel(nu_ref, bv_ref, ord_ref, eseq_ref, nex_ref, x_ref, w1_hbm, w3_hbm, w2_hbm, y_ref,
                   w1_f, w3_f, w2_f, w1_s, w3_s, w2_s, sems, *, layer):
    def weight_copies(j):
        e = eseq_ref[j]
        slot = j % W_RING
        return [pltpu.make_async_copy(w1_hbm.at[layer, e], w1_f.at[slot], sems.at[slot, 0]),
                pltpu.make_async_copy(w3_hbm.at[layer, e], w3_f.at[slot], sems.at[slot, 1]),
                pltpu.make_async_copy(w2_hbm.at[layer, e], w2_f.at[slot], sems.at[slot, 2])]

    def start_if_exists(j):
        @pl.when(j < nex_ref[0])
        def _():
            for cp in weight_copies(j):
                cp.start()

    for u in range(STEP_BLOCKS):
        i = pl.program_id(0) * STEP_BLOCKS + u
        rows = slice(u * SLOT_BLOCK, (u + 1) * SLOT_BLOCK)

        @pl.when(i < nu_ref[0])
        def _(i=i, rows=rows):
            j = ord_ref[i]

            @pl.when(i == 0)
            def _():
                for ahead in range(W_RING - 1):
                    start_if_exists(ahead)

            @pl.when((i == 0) | (j != ord_ref[jnp.maximum(i - 1, 0)]))
            def _():
                start_if_exists(j + W_RING - 1)
                for cp in weight_copies(j):
                    cp.wait()
                slot = j % W_RING
                w1_s[...] = w1_f[slot].astype(bf16)
                w3_s[...] = w3_f[slot].astype(bf16)
                w2_s[...] = w2_f[slot].astype(bf16)

            row = lax.broadcasted_iota(i32, (SLOT_BLOCK, D_PACK), 0)
            x_hi, x_lo = _unpack_pairs(jnp.where(row < bv_ref[i], x_ref[rows, :], 0))
            x = jnp.concatenate([x_hi, x_lo], axis=1).astype(bf16)
            act = (_silu(_dot(x, w1_s[...])) * _dot(x, w3_s[...])).astype(bf16)
            y_ref[rows, :] = _pack_pairs(_dot(act, w2_s[...]))

        @pl.when((i >= nu_ref[0]) & (pl.program_id(0) * STEP_BLOCKS < nu_ref[0]))
        def _(rows=rows):
            y_ref[rows, :] = x_ref[rows, :]


def _expert_call(n_used, blk_valid, blk_ord, expert_seq, n_seq_experts, slots, w1, w3, w2, layer):
    step_rows = STEP_BLOCKS * SLOT_BLOCK
    assert slots.shape[0] % step_rows == 0
    blk = lambda s, nu, *_: (jnp.minimum(s, (nu[0] - 1) // STEP_BLOCKS), 0)
    hbm = pl.BlockSpec(memory_space=pl.ANY)
    grid_spec = pltpu.PrefetchScalarGridSpec(
        num_scalar_prefetch=5,
        grid=(slots.shape[0] // step_rows,),
        in_specs=[pl.BlockSpec((step_rows, D_PACK), blk), hbm, hbm, hbm],
        out_specs=pl.BlockSpec((step_rows, D_PACK), blk),
        scratch_shapes=[pltpu.VMEM((W_RING, D_MODEL, EXPERT_FF), f32), pltpu.VMEM((W_RING, D_MODEL, EXPERT_FF), f32),
                        pltpu.VMEM((W_RING, EXPERT_FF, D_MODEL), f32),
                        pltpu.VMEM((D_MODEL, EXPERT_FF), bf16), pltpu.VMEM((D_MODEL, EXPERT_FF), bf16),
                        pltpu.VMEM((EXPERT_FF, D_MODEL), bf16),
                        pltpu.SemaphoreType.DMA((W_RING, 3))],
    )
    return pl.pallas_call(
        functools.partial(_expert_kernel, layer=layer),
        grid_spec=grid_spec,
        out_shape=jax.ShapeDtypeStruct(slots.shape, slots.dtype),
        input_output_aliases={5: 0},
        compiler_params=_cparams(("arbitrary",)),
        name="experts",
    )(n_used, blk_valid, blk_ord, expert_seq, n_seq_experts, slots, w1, w3, w2)


def _gather_rows_call(table, idx):
    n_idx = idx.shape[0]
    d = table.shape[1]
    mesh, n_workers, worker_id = _sc_workers()
    per_worker = n_idx // n_workers
    assert n_idx % (n_workers * SC_CHUNK) == 0

    n_chunks = per_worker // SC_CHUNK
    assert n_chunks % 2 == 0

    def body(table_hbm, idx_hbm, out_hbm, idx_a, idx_b, rows_a, rows_b, sem_a, sem_b):
        base = worker_id() * per_worker
        bufs = ((idx_a, rows_a, sem_a), (idx_b, rows_b, sem_b))

        def start_gather(c, buf):
            idx_v, rows_v, sem = buf
            off = pl.multiple_of(base + c * SC_CHUNK, SC_CHUNK)
            pltpu.sync_copy(idx_hbm.at[pl.ds(off, SC_CHUNK)], idx_v)
            pltpu.async_copy(table_hbm.at[idx_v], rows_v, sem)

        start_gather(0, bufs[0])

        @pl.loop(0, n_chunks, step=2)
        def _(c0):
            for b in range(2):
                c = c0 + b
                idx_v, rows_v, sem = bufs[b]

                @pl.when(c + 1 < n_chunks)
                def _():
                    start_gather(c + 1, bufs[1 - b])

                pltpu.make_async_copy(table_hbm.at[idx_v], rows_v, sem).wait()
                off = pl.multiple_of(base + c * SC_CHUNK, SC_CHUNK)
                pltpu.sync_copy(rows_v, out_hbm.at[pl.ds(off, SC_CHUNK)])

    return pl.kernel(
        body,
        out_type=jax.ShapeDtypeStruct((n_idx, d), table.dtype),
        mesh=mesh,
        scratch_types=[pltpu.VMEM((SC_CHUNK,), i32), pltpu.VMEM((SC_CHUNK,), i32),
                       pltpu.VMEM((SC_CHUNK, d), table.dtype), pltpu.VMEM((SC_CHUNK, d), table.dtype),
                       pltpu.SemaphoreType.DMA, pltpu.SemaphoreType.DMA],
        name="gather_rows",
    )(table, idx)


def _combine_kernel(gate_ref, xb_ref, mod_ref, y_ref, o_ref):
    g2 = mod_ref[0][:, 5 * D_MODEL:6 * D_MODEL]
    gates = gate_ref[...]
    acc_hi = jnp.zeros((gates.shape[0], D_PACK), f32)
    acc_lo = jnp.zeros((gates.shape[0], D_PACK), f32)
    for k in range(TOP_K):
        y_hi, y_lo = _unpack_pairs(y_ref[k])
        acc_hi = acc_hi + gates[:, k:k + 1] * y_hi
        acc_lo = acc_lo + gates[:, k:k + 1] * y_lo
    o_ref[:, 0:D_PACK] = xb_ref[:, 0:D_PACK] + g2[:, 0:D_PACK] * acc_hi
    o_ref[:, D_PACK:D_MODEL] = xb_ref[:, D_PACK:D_MODEL] + g2[:, D_PACK:D_MODEL] * acc_lo


def _combine_call(gates, xbase, mod, ygath, seq_row_c, row0, n_rows):
    tm = COMBINE_TILE
    assert row0 % tm == 0 and n_rows % tm == 0
    b0 = row0 // tm
    return pl.pallas_call(
        _combine_kernel,
        grid=(n_rows // tm,),
        in_specs=[pl.BlockSpec((tm, TOP_K), lambda i: (b0 + i, 0)),
                  pl.BlockSpec((tm, D_MODEL), lambda i: (b0 + i, 0)),
                  pl.BlockSpec((1, 1, 6 * D_MODEL), lambda i: (seq_row_c(b0 + i), 0, 0)),
                  pl.BlockSpec((TOP_K, tm, D_PACK), lambda i: (0, b0 + i, 0))],
        out_specs=pl.BlockSpec((tm, D_MODEL), lambda i: (i, 0)),
        out_shape=jax.ShapeDtypeStruct((n_rows, D_MODEL), f32),
        compiler_params=_cparams(("parallel",)),
        name="combine",
    )(gates, xbase, mod, ygath)


def _block_avg(width, group):
    r = np.arange(width)
    return jnp.asarray((r[:, None] // group == r[None, :] // group).astype(np.float32) / group, dtype=bf16)


def _kv_replicate():
    c = np.arange(ATTN_W)
    src = (c // (ATTN_W // N_KV_HEADS)) * HEAD_DIM + c % HEAD_DIM
    return jnp.asarray((np.arange(KV_W)[:, None] == src[None, :]).astype(np.float32), dtype=bf16)


def _rope_lane_tables(n_tok):
    rows = n_tok // GRID_W
    r = jnp.repeat(jnp.arange(rows, dtype=f32), GRID_W)
    col = jnp.tile(jnp.arange(GRID_W, dtype=f32), rows)
    inv = ROPE_THETA ** (-jnp.arange(ROPE_FREQ, dtype=f32) / ROPE_FREQ)
    ar = r[:, None] * inv
    ac = col[:, None] * inv
    cos_h = jnp.concatenate([jnp.cos(ar), jnp.cos(ar), jnp.cos(ac), jnp.cos(ac)], axis=-1)
    sin_h = jnp.concatenate([-jnp.sin(ar), jnp.sin(ar), -jnp.sin(ac), jnp.sin(ac)], axis=-1)
    return jnp.tile(cos_h, (1, N_HEADS)), jnp.tile(sin_h, (1, N_HEADS))


def _block_diag(w):
    nb, bw, _ = w.shape
    eye = jnp.eye(nb, dtype=w.dtype)
    return (w[:, :, None, :] * eye[:, None, :, None]).reshape(nb * bw, nb * bw)


def _gla_state_in(s):
    bsz = s.shape[0]
    eye = jnp.eye(GLA_H, dtype=s.dtype)
    st = jnp.swapaxes(s, -1, -2)
    big = st[:, :, :, :, None, :] * eye[None, None, :, None, :, None]
    return big.reshape(bsz, 2, GLA_W, GLA_W)


def kernel(x_prompt, x_sample, cache_k, cache_v, state_lru, state_gla, c, c_ctx, ada_w, ada_b, norm1_g, norm2_g, w_in, q_norm_g, k_norm_g, attn_out_g, conv_w, conv_b, lru_wa, lru_ba, lru_wi, lru_bi, lru_lambda, lru_out_g, gla_wa2, gla_ba, gla_out_g, w_out, router_w, router_b, exp_w1, exp_w3, exp_w2, sh_w1, sh_w3, sh_w2):
    bc, tc, _ = x_prompt.shape
    bl, tl, _ = x_sample.shape
    depth = w_in.shape[0]
    nc = bc * tc
    n = nc + bl * tl
    past = cache_k.shape[2]
    assert tc == ROW_TILE and tl % ROW_TILE == 0 and nc % tl == 0 and bl + 1 <= SUBLANES
    assert nc % PROJ_TILE == 0 and tl % PROJ_TILE == 0 and nc % COMBINE_TILE == 0 and tl % COMBINE_TILE == 0

    def seq_row_for(tile):
        def seq_row(i):
            return jnp.where(i < nc // tile, 0, 1 + (i - nc // tile) // (tl // tile))
        return seq_row

    seq_row = seq_row_for(PROJ_TILE)
    seq_row_c = seq_row_for(COMBINE_TILE)

    x = jnp.concatenate([x_prompt.reshape(nc, D_MODEL), x_sample.reshape(bl * tl, D_MODEL)], axis=0)
    cond = jnp.zeros((SUBLANES, D_MODEL), f32).at[0].set(c_ctx).at[1:1 + bl].set(c)
    mods = _ada_call(cond, ada_w, ada_b)

    bmq = _block_avg(ATTN_W, HEAD_DIM)
    bmk = _block_avg(KV_W, HEAD_DIM)
    bmg = _block_avg(GLA_W, GLA_DK)
    rep = _kv_replicate()
    cos_t, sin_t = _rope_lane_tables(tl)
    step_rows = STEP_BLOCKS * SLOT_BLOCK
    n_slots = -(-(n * TOP_K + N_EXPERTS * (SLOT_BLOCK - 1)) // step_rows) * step_rows
    tile8 = lambda v: jnp.tile(v, N_HEADS)[None, :]

    ks, vs, lrus, glas = [], [], [], []
    for l in range(depth):
        mod = mods[l].reshape(SUBLANES, 1, 6 * D_MODEL)
        w_in_p = jnp.pad(w_in[l].astype(bf16), ((0, 0), (0, LANES - 2 * GLA_RANK)))
        pa, pl_, pg = _proj_call(x, mod, norm1_g[l][None, :], w_in_p, seq_row)

        qg, kg, og = tile8(q_norm_g[l]), jnp.tile(k_norm_g[l], N_KV_HEADS)[None, :], attn_out_g[l][None, :]
        attn_c, k_new, v_new = _attn_ctx_call(pa, bc, tc, qg, kg, og, bmq, bmk, rep)
        attn_l = _attn_lat_call(pa, nc, bl, tl, cache_k[:, l].reshape(bl, past, KV_W),
                                cache_v[:, l].reshape(bl, past, KV_W), cos_t, sin_t, qg, kg, og, bmq, bmk, rep)
        ks.append(k_new.reshape(bc, tc, N_KV_HEADS, HEAD_DIM))
        vs.append(v_new.reshape(bc, tc, N_KV_HEADS, HEAD_DIM))

        wg = jnp.concatenate([_block_diag(lru_wa[l, 0]), _block_diag(lru_wi[l, 0]),
                              _block_diag(lru_wa[l, 1]), _block_diag(lru_wi[l, 1])], axis=1).astype(bf16)
        bg = jnp.concatenate([lru_ba[l, 0], lru_bi[l, 0], lru_ba[l, 1], lru_bi[l, 1]])[None, :]
        lru_args = (conv_w[l], conv_b[l][None, :], wg, bg, lru_lambda[l], lru_out_g[l][None, :])
        lru_c, lru_fin = _lru_call(pl_, 0, bc, tc, jnp.zeros((bc, 2, LRU_W), f32), *lru_args)
        lru_l, _ = _lru_call(pl_, nc, bl, tl, state_lru[:, l], *lru_args)
        lrus.append(lru_fin)

        wa = jnp.zeros((LANES, 2 * GLA_W), f32)
        wa = wa.at[0:GLA_RANK, 0:GLA_W].set(gla_wa2[l, 0]).at[GLA_RANK:2 * GLA_RANK, GLA_W:].set(gla_wa2[l, 1])
        gla_args = (wa.astype(bf16), gla_ba[l].reshape(1, 2 * GLA_W), gla_out_g[l].reshape(1, GLA_W), bmg)
        gla_c, gla_fin = _gla_call(pg, 0, bc, tc, None, *gla_args)
        gla_l, _ = _gla_call(pg, nc, bl, tl, _gla_state_in(state_gla[:, l]), *gla_args)
        glas.append(gla_fin)

        rw_t = router_w[l].T
        rw_hi = rw_t.astype(bf16)
        rw_lo = (rw_t - rw_hi.astype(f32)).astype(bf16)
        xbase, h2, scores_t = _out_call(x, (attn_c, lru_c, gla_c), (attn_l, lru_l, gla_l), mod, w_out[l].astype(bf16),
                                        norm2_g[l][None, :], rw_hi, rw_lo, sh_w1[l].astype(bf16),
                                        sh_w3[l].astype(bf16), sh_w2[l].astype(bf16), seq_row)

        idx, gates, rank, counts = _route_call(scores_t, router_b[l][:, None])
        cnt = counts[:, 0].astype(i32)
        padded = (cnt + SLOT_BLOCK - 1) // SLOT_BLOCK * SLOT_BLOCK
        padded_end = jnp.cumsum(padded)
        dest_km = _dest_call(idx, rank, (padded_end - padded).astype(f32)[:, None])
        nb = n_slots // SLOT_BLOCK
        blk_first = jnp.arange(nb, dtype=i32) * SLOT_BLOCK
        blk_e = jnp.minimum(jnp.sum((padded_end[None, :] <= blk_first[:, None]).astype(i32), axis=1), N_EXPERTS - 1)
        blk_valid = jnp.clip((padded_end - padded + cnt)[blk_e] - blk_first, 0, SLOT_BLOCK)
        n_used = padded_end[-1:] // SLOT_BLOCK
        owns = jnp.cumsum((cnt > 0).astype(i32))
        blk_ord = (owns - 1)[blk_e]
        expert_seq = jnp.minimum(jnp.sum((owns[None, :] <= jnp.arange(N_EXPERTS, dtype=i32)[:, None]).astype(i32),
                                         axis=1), N_EXPERTS - 1)
        slots = _scatter_rows_call(h2, dest_km, n_slots)
        slots = _expert_call(n_used, blk_valid, blk_ord, expert_seq, owns[-1:], slots, exp_w1, exp_w3, exp_w2, l)
        ygath = _gather_rows_call(slots, dest_km.reshape(-1)).reshape(TOP_K, n, D_PACK)
        gates_t = gates.T
        if l + 1 < depth:
            x = _combine_call(gates_t, xbase, mod, ygath, seq_row_c, 0, n)
        else:
            y_prompt = _combine_call(gates_t, xbase, mod, ygath, seq_row_c, 0, nc).reshape(bc, tc, D_MODEL)
            y_sample = _combine_call(gates_t, xbase, mod, ygath, seq_row_c, nc, n - nc).reshape(bl, tl, D_MODEL)

    return (y_prompt, y_sample, jnp.stack(ks, axis=1), jnp.stack(vs, axis=1),
            jnp.stack(lrus, axis=1), jnp.stack(glas, axis=1))
```

```python
import functools

import jax
import jax.numpy as jnp
import numpy as np
from jax import lax
from jax.experimental import pallas as pl
from jax.experimental.pallas import tpu as pltpu
from jax.experimental.pallas import tpu_sc as plsc

f32 = jnp.float32
bf16 = jnp.bfloat16
i32 = jnp.int32

D_MODEL = 1024
N_HEADS = 8
N_KV_HEADS = 2
HEAD_DIM = 64
ATTN_W = N_HEADS * HEAD_DIM
KV_W = N_KV_HEADS * HEAD_DIM
GRID_W = 64
ROPE_FREQ = HEAD_DIM // 4
ROPE_THETA = 10000.0
LRU_W = 256
LRU_BLOCKS = 4
LRU_C = 8.0
CONV_W = 4
CONV_LEFT = 2
GLA_H = 4
GLA_DK = 64
GLA_W = 256
GLA_RANK = 16
GLA_TAU = 16.0
N_EXPERTS = 256
TOP_K = 8
EXPERT_FF = 256
ROUTED_SCALE = 2.5
EPS = 1e-6

LANES = 128
SUBLANES = 8
ROW_TILE = 256
PROJ_TILE = 512
GLA_BLOCK = 256
SLOT_BLOCK = 256
STEP_BLOCKS = 8
W_RING = 3
COMBINE_TILE = 256
SC_CHUNK = 64
D_PACK = D_MODEL // 2
VMEM_LIMIT = 56 * 1024 * 1024


def _cparams(sem, vmem=VMEM_LIMIT):
    return pltpu.CompilerParams(dimension_semantics=sem, vmem_limit_bytes=vmem)


def _dot(a, b):
    return jnp.dot(a, b, preferred_element_type=f32)


def _dot_nt(a, b):
    return lax.dot_general(a, b, (((1,), (1,)), ((), ())), preferred_element_type=f32)


def _dot_tn(a, b):
    return lax.dot_general(a, b, (((0,), (0,)), ((), ())), preferred_element_type=f32)


def _split(x):
    hi = x.astype(bf16)
    lo = (x - hi.astype(f32)).astype(bf16)
    return hi, lo


def _dot_x2(x, w):
    hi, lo = _split(x)
    return _dot(hi, w) + _dot(lo, w)


def _dot_2x(m, x):
    hi, lo = _split(x)
    return _dot(m, hi) + _dot(m, lo)


def _dot3(a, b_hi, b_lo):
    a_hi, a_lo = _split(a)
    return _dot(a_hi, b_hi) + _dot(a_lo, b_hi) + _dot(a_hi, b_lo)


def _sigmoid(x):
    return 1.0 / (1.0 + jnp.exp(-x))


def _silu(x):
    return x * _sigmoid(x)


def _softplus(x):
    return jnp.maximum(x, 0.0) + jnp.log(1.0 + jnp.exp(-jnp.abs(x)))


def _gelu_tanh(x):
    return 0.5 * x * (1.0 + jnp.tanh(0.7978845608028654 * (x + 0.044715 * x * x * x)))


def _rms(x, g):
    return x * lax.rsqrt(jnp.mean(x * x, axis=-1, keepdims=True) + EPS) * g


def _pack_pairs(x):
    c = x.shape[1] // 2
    hi = lax.bitcast_convert_type(x[:, :c].astype(bf16).astype(f32), i32)
    lo = lax.bitcast_convert_type(x[:, c:].astype(bf16).astype(f32), i32)
    return hi | lax.shift_right_logical(lo, 16)


def _unpack_pairs(w):
    hi = lax.bitcast_convert_type(w & jnp.int32(-65536), f32)
    lo = lax.bitcast_convert_type(w << 16, f32)
    return hi, lo


def _head_rms(x, g, bm):
    ms = _dot_x2(x * x, bm)
    return x * lax.rsqrt(ms + EPS) * g


def _ada_kernel(c_ref, w_ref, b_ref, o_ref):
    s = _silu(c_ref[...])
    w = w_ref[0]
    w_hi, w_lo = _split(w)
    o_ref[0] = _dot3(s, w_hi, w_lo) + b_ref[0]


def _ada_call(cond, ada_w, ada_b):
    depth = ada_w.shape[0]
    nt = 1536
    return pl.pallas_call(
        _ada_kernel,
        grid=(depth, 6 * D_MODEL // nt),
        in_specs=[pl.BlockSpec((SUBLANES, D_MODEL), lambda l, j: (0, 0)),
                  pl.BlockSpec((1, D_MODEL, nt), lambda l, j: (l, 0, j)),
                  pl.BlockSpec((1, 1, nt), lambda l, j: (l, 0, j))],
        out_specs=pl.BlockSpec((1, SUBLANES, nt), lambda l, j: (l, 0, j)),
        out_shape=jax.ShapeDtypeStruct((depth, SUBLANES, 6 * D_MODEL), f32),
        compiler_params=_cparams(("parallel", "parallel")),
        name="ada_mod",
    )(cond, ada_w, ada_b.reshape(depth, 1, 6 * D_MODEL))


PROJ_A = ATTN_W + 2 * KV_W
PROJ_L = 2 * LRU_W
PROJ_G = 4 * GLA_W + LANES
IN_COLS = PROJ_A + PROJ_L + 4 * GLA_W + 2 * GLA_RANK


def _proj_kernel(x_ref, mod_ref, g_ref, w_ref, oa_ref, ol_ref, og_ref, w_s):
    @pl.when(pl.program_id(0) == 0)
    def _():
        w_s[:, 0:IN_COLS] = w_ref[0].astype(bf16)
        w_s[:, IN_COLS:] = jnp.zeros((D_MODEL, w_s.shape[1] - IN_COLS), bf16)

    mod = mod_ref[0]
    sh = mod[:, 0:D_MODEL]
    sc = mod[:, D_MODEL:2 * D_MODEL]
    h = (_rms(x_ref[...], g_ref[...]) * (1.0 + sc) + sh).astype(bf16)
    p = _dot(h, w_s[...])
    oa_ref[...] = p[:, 0:PROJ_A].astype(bf16)
    ol_ref[...] = p[:, PROJ_A:PROJ_A + PROJ_L].astype(bf16)
    og_ref[...] = p[:, PROJ_A + PROJ_L:PROJ_A + PROJ_L + PROJ_G].astype(bf16)


def _proj_call(x, mod, g, w_all, layer, seq_row):
    n = x.shape[0]
    cols = PROJ_A + PROJ_L + PROJ_G
    return pl.pallas_call(
        _proj_kernel,
        grid=(n // PROJ_TILE,),
        in_specs=[pl.BlockSpec((PROJ_TILE, D_MODEL), lambda i: (i, 0)),
                  pl.BlockSpec((1, 1, 6 * D_MODEL), lambda i: (seq_row(i), 0, 0)),
                  pl.BlockSpec((1, D_MODEL), lambda i: (0, 0)),
                  pl.BlockSpec((1, D_MODEL, IN_COLS), lambda i: (layer, 0, 0))],
        out_specs=[pl.BlockSpec((PROJ_TILE, PROJ_A), lambda i: (i, 0)),
                   pl.BlockSpec((PROJ_TILE, PROJ_L), lambda i: (i, 0)),
                   pl.BlockSpec((PROJ_TILE, PROJ_G), lambda i: (i, 0))],
        out_shape=[jax.ShapeDtypeStruct((n, PROJ_A), bf16),
                   jax.ShapeDtypeStruct((n, PROJ_L), bf16),
                   jax.ShapeDtypeStruct((n, PROJ_G), bf16)],
        scratch_shapes=[pltpu.VMEM((D_MODEL, cols), bf16)],
        compiler_params=_cparams(("arbitrary",)),
        name="in_proj",
    )(x, mod, g, w_all)


def _rope(x, cos_t, sin_t):
    w = x.shape[1]
    up = pltpu.roll(x, w - ROPE_FREQ, 1)
    dn = pltpu.roll(x, ROPE_FREQ, 1)
    lane = lax.broadcasted_iota(i32, x.shape, 1)
    partner = jnp.where((lane & (2 * ROPE_FREQ - 1)) < ROPE_FREQ, up, dn)
    return x * cos_t + partner * sin_t


def _attend(q, kk_ref, vv_ref, o_ref):
    tq = q.shape[0]
    gw = ATTN_W // N_KV_HEADS
    lane = lax.broadcasted_iota(i32, (tq, gw), 1)
    for g in range(N_KV_HEADS):
        qg = q[:, g * gw:(g + 1) * gw]
        kg = kk_ref[:, g * gw:(g + 1) * gw]
        vg = vv_ref[:, g * gw:(g + 1) * gw]
        acc = jnp.zeros((tq, gw), f32)
        for hh in range(N_HEADS // N_KV_HEADS):
            hm = (lane >> 6) == hh
            s = _dot_nt(jnp.where(hm, qg, 0.0).astype(bf16), kg)
            m = jnp.max(s, axis=-1, keepdims=True)
            p = jnp.exp(s - m)
            l = jnp.sum(p, axis=-1, keepdims=True)
            o = _dot(p.astype(bf16), vg) / l
            acc = jnp.where(hm, o, acc)
        o_ref[:, g * gw:(g + 1) * gw] = acc


def _attn_ctx_kernel(p_ref, qg_ref, kg_ref, og_ref, bmq_ref, bmk_ref, rep_ref,
                     o_ref, ko_ref, vo_ref, kk_s, vv_s, o_s):
    p = p_ref[...].astype(f32)
    q = _head_rms(p[:, 0:ATTN_W], qg_ref[...], bmq_ref[...])
    k = _head_rms(p[:, ATTN_W:ATTN_W + KV_W], kg_ref[...], bmk_ref[...])
    v = p[:, ATTN_W + KV_W:PROJ_A]
    ko_ref[...] = k
    vo_ref[...] = v
    kk_s[...] = _dot(k.astype(bf16), rep_ref[...]).astype(bf16)
    vv_s[...] = _dot(v.astype(bf16), rep_ref[...]).astype(bf16)
    _attend(q * HEAD_DIM ** -0.5, kk_s, vv_s, o_s)
    o_ref[...] = _rms(o_s[...], og_ref[...]).astype(bf16)


def _attn_ctx_call(pa, n_seq, t, qg, kg, og, bmq, bmk, rep):
    assert t == ROW_TILE
    const = lambda i: (0, 0)
    return pl.pallas_call(
        _attn_ctx_kernel,
        grid=(n_seq,),
        in_specs=[pl.BlockSpec((t, PROJ_A), lambda i: (i, 0)),
                  pl.BlockSpec((1, ATTN_W), const), pl.BlockSpec((1, KV_W), const), pl.BlockSpec((1, ATTN_W), const),
                  pl.BlockSpec((ATTN_W, ATTN_W), const), pl.BlockSpec((KV_W, KV_W), const),
                  pl.BlockSpec((KV_W, ATTN_W), const)],
        out_specs=[pl.BlockSpec((t, ATTN_W), lambda i: (i, 0)),
                   pl.BlockSpec((t, KV_W), lambda i: (i, 0)),
                   pl.BlockSpec((t, KV_W), lambda i: (i, 0))],
        out_shape=[jax.ShapeDtypeStruct((n_seq * t, ATTN_W), bf16),
                   jax.ShapeDtypeStruct((n_seq * t, KV_W), f32),
                   jax.ShapeDtypeStruct((n_seq * t, KV_W), f32)],
        scratch_shapes=[pltpu.VMEM((t, ATTN_W), bf16), pltpu.VMEM((t, ATTN_W), bf16), pltpu.VMEM((t, ATTN_W), f32)],
        compiler_params=_cparams(("parallel",)),
        name="attn_ctx",
    )(pa, qg, kg, og, bmq, bmk, rep)


def _attn_lat_kernel(q_ref, kv_ref, ck_ref, cv_ref, cq_ref, sq_ref, ckk_ref, skk_ref,
                     qg_ref, kg_ref, og_ref, bmq_ref, bmk_ref, rep_ref,
                     o_ref, kk_s, vv_s, o_s, *, past):
    @pl.when(pl.program_id(1) == 0)
    def _():
        kv = kv_ref[...].astype(f32)
        k = _head_rms(kv[:, 0:KV_W], kg_ref[...], bmk_ref[...])
        k = _rope(k, ckk_ref[...], skk_ref[...])
        v = kv[:, KV_W:2 * KV_W]
        kk_s[0:past, :] = _dot(ck_ref[0].astype(bf16), rep_ref[...]).astype(bf16)
        vv_s[0:past, :] = _dot(cv_ref[0].astype(bf16), rep_ref[...]).astype(bf16)
        kk_s[past:, :] = _dot(k.astype(bf16), rep_ref[...]).astype(bf16)
        vv_s[past:, :] = _dot(v.astype(bf16), rep_ref[...]).astype(bf16)

    q = _head_rms(q_ref[...].astype(f32), qg_ref[...], bmq_ref[...])
    q = _rope(q, cq_ref[...], sq_ref[...])
    _attend(q * HEAD_DIM ** -0.5, kk_s, vv_s, o_s)
    o_ref[...] = _rms(o_s[...], og_ref[...]).astype(bf16)


def _attn_lat_call(pa, row0, n_seq, t, cache_k, cache_v, cq, sq, qg, kg, og, bmq, bmk, rep):
    past = cache_k.shape[1]
    nq = t // ROW_TILE
    assert row0 % t == 0 and t % ROW_TILE == 0
    qb0 = row0 // ROW_TILE
    sb0 = row0 // t
    const = lambda b, j: (0, 0)
    return pl.pallas_call(
        functools.partial(_attn_lat_kernel, past=past),
        grid=(n_seq, nq),
        in_specs=[pl.BlockSpec((ROW_TILE, ATTN_W), lambda b, j: (qb0 + b * nq + j, 0)),
                  pl.BlockSpec((t, 2 * KV_W), lambda b, j: (sb0 + b, ATTN_W // (2 * KV_W))),
                  pl.BlockSpec((1, past, KV_W), lambda b, j: (b, 0, 0)),
                  pl.BlockSpec((1, past, KV_W), lambda b, j: (b, 0, 0)),
                  pl.BlockSpec((ROW_TILE, ATTN_W), lambda b, j: (j, 0)),
                  pl.BlockSpec((ROW_TILE, ATTN_W), lambda b, j: (j, 0)),
                  pl.BlockSpec((t, KV_W), const), pl.BlockSpec((t, KV_W), const),
                  pl.BlockSpec((1, ATTN_W), const), pl.BlockSpec((1, KV_W), const), pl.BlockSpec((1, ATTN_W), const),
                  pl.BlockSpec((ATTN_W, ATTN_W), const), pl.BlockSpec((KV_W, KV_W), const),
                  pl.BlockSpec((KV_W, ATTN_W), const)],
        out_specs=pl.BlockSpec((ROW_TILE, ATTN_W), lambda b, j: (b * nq + j, 0)),
        out_shape=jax.ShapeDtypeStruct((n_seq * t, ATTN_W), bf16),
        scratch_shapes=[pltpu.VMEM((past + t, ATTN_W), bf16), pltpu.VMEM((past + t, ATTN_W), bf16),
                        pltpu.VMEM((ROW_TILE, ATTN_W), f32)],
        compiler_params=_cparams(("parallel", "arbitrary")),
        name="attn_lat",
    )(pa, pa, cache_k, cache_v, cq, sq, cq, sq, qg, kg, og, bmq, bmk, rep)


def _lru_kernel(x_ref, h0_ref, cw_ref, cb_ref, wg_ref, bg_ref, lam_ref, g_ref,
                o_ref, fin_ref, xs, a_s, u_s, h_s, *, t):
    ng = t // SUBLANES
    lx = x_ref[:, 0:LRU_W].astype(f32)
    ly = x_ref[:, LRU_W:2 * LRU_W].astype(f32)
    xs[0:SUBLANES, :] = jnp.zeros((SUBLANES, LRU_W), f32)
    xs[SUBLANES + t:2 * SUBLANES + t, :] = jnp.zeros((SUBLANES, LRU_W), f32)
    xs[SUBLANES:SUBLANES + t, :] = lx
    xc = jnp.broadcast_to(cb_ref[...], (t, LRU_W))
    for j in range(CONV_W):
        off = SUBLANES + j - CONV_LEFT
        xc = xc + xs[off:off + t, :] * cw_ref[j:j + 1, :]
    gates = _sigmoid(_dot(xc.astype(bf16), wg_ref[...]) + bg_ref[...])
    sp = _softplus(-lam_ref[...])
    sub = lax.broadcasted_iota(i32, (t, LRU_W), 0) & (SUBLANES - 1)
    for d in range(2):
        r = gates[:, (2 * d) * LRU_W:(2 * d + 1) * LRU_W]
        gi = gates[:, (2 * d + 1) * LRU_W:(2 * d + 2) * LRU_W]
        log_a = -LRU_C * r * sp[d:d + 1, :]
        a = jnp.exp(log_a)
        th = jnp.tanh(log_a)
        u = jnp.sqrt(-2.0 * th / (1.0 - th)) * (gi * xc)
        for s in (1, 2, 4):
            if d == 0:
                a_sh = pltpu.roll(a, s, 0)
                u_sh = pltpu.roll(u, s, 0)
                ok = sub >= s
            else:
                a_sh = pltpu.roll(a, t - s, 0)
                u_sh = pltpu.roll(u, t - s, 0)
                ok = sub < SUBLANES - s
            u = jnp.where(ok, a * u_sh + u, u)
            a = jnp.where(ok, a * a_sh, a)
        a_s[...] = a
        u_s[...] = u
        h0 = jnp.broadcast_to(h0_ref[0, d:d + 1, :], (SUBLANES, LRU_W))
        edge = SUBLANES - 1 if d == 0 else 0

        def body(i, carry, d=d, edge=edge):
            g = i if d == 0 else ng - 1 - i
            rows = pl.ds(pl.multiple_of(g * SUBLANES, SUBLANES), SUBLANES)
            h = a_s[rows, :] * carry + u_s[rows, :]
            if d == 0:
                h_s[rows, :] = h
            else:
                h_s[rows, :] = h_s[rows, :] + h
            return jnp.broadcast_to(h[edge:edge + 1, :], (SUBLANES, LRU_W))

        last = lax.fori_loop(0, ng, body, h0)
        fin_ref[0, d:d + 1, :] = last[0:1, :]
    o_ref[...] = _rms(h_s[...] * _gelu_tanh(ly), g_ref[...]).astype(bf16)


def _lru_call(pl_, row0, n_seq, t, h0, cw, cb, wg, bg, lam, g):
    sb0 = row0 // t
    assert row0 % t == 0
    const = lambda b: (0, 0)
    in_specs = [pl.BlockSpec((t, PROJ_L), lambda b: (sb0 + b, 0)),
                pl.BlockSpec((1, 2, LRU_W), lambda b: (b, 0, 0)),
                pl.BlockSpec((CONV_W, LRU_W), const), pl.BlockSpec((1, LRU_W), const),
                pl.BlockSpec((LRU_W, 4 * LRU_W), const), pl.BlockSpec((1, 4 * LRU_W), const),
                pl.BlockSpec((2, LRU_W), const), pl.BlockSpec((1, LRU_W), const)]
    args = [pl_, h0, cw, cb, wg, bg, lam, g]
    return pl.pallas_call(
        functools.partial(_lru_kernel, t=t),
        grid=(n_seq,),
        in_specs=in_specs,
        out_specs=[pl.BlockSpec((t, LRU_W), lambda b: (b, 0)),
                   pl.BlockSpec((1, 2, LRU_W), lambda b: (b, 0, 0))],
        out_shape=[jax.ShapeDtypeStruct((n_seq * t, LRU_W), bf16), jax.ShapeDtypeStruct((n_seq, 2, LRU_W), f32)],
        scratch_shapes=[pltpu.VMEM((t + 2 * SUBLANES, LRU_W), f32), pltpu.VMEM((t, LRU_W), f32),
                        pltpu.VMEM((t, LRU_W), f32), pltpu.VMEM((t, LRU_W), f32)],
        compiler_params=_cparams(("parallel",)),
        name="rglru_t%d" % t,
    )(*args)


def _bcast_rows(b, period, off):
    w = b.shape[1]
    return jnp.concatenate(
        [jnp.broadcast_to(b[i * period + off:i * period + off + 1, :], (period, w)) for i in range(b.shape[0] // period)],
        axis=0)


def _gla_block(q, k, v, la, reverse, st_ref):
    n = GLA_BLOCK
    row = lax.broadcasted_iota(i32, (n, n), 0)
    col = lax.broadcasted_iota(i32, (n, n), 1)
    same64 = (row >> 6) == (col >> 6)
    same32 = (row >> 5) == (col >> 5)
    same16 = (row >> 4) == (col >> 4)
    if not reverse:
        cum = (same64 & (col <= row)).astype(bf16)
        m1 = same64 & ((row & 63) >= 32) & ((col & 63) < 32)
        m2 = same32 & ((row & 31) >= 16) & ((col & 31) < 16)
        m3 = same16 & (col <= row)
        offs = (31, 15, 7, 63)
    else:
        cum = (same64 & (col >= row)).astype(bf16)
        m1 = same64 & ((row & 63) < 32) & ((col & 63) >= 32)
        m2 = same32 & ((row & 31) < 16) & ((col & 31) >= 16)
        m3 = same16 & (col >= row)
        offs = (32, 16, 8, 0)
    b = _dot_2x(cum, la)
    r1 = _bcast_rows(b, 64, offs[0])
    r2 = _bcast_rows(b, 32, offs[1])
    r3 = _bcast_rows(b, 16, offs[2])
    bl = _bcast_rows(b, 64, offs[3])
    q1 = q * jnp.exp(jnp.minimum(b - r1, 0.0))
    k1 = (k * jnp.exp(jnp.minimum(r1 - b, 0.0))).astype(bf16)
    q2 = q * jnp.exp(jnp.minimum(b - r2, 0.0))
    k2 = (k * jnp.exp(jnp.minimum(r2 - b, 0.0))).astype(bf16)
    q3 = q * jnp.exp(b - r3)
    k3 = (k * jnp.exp(r3 - b)).astype(bf16)
    qe = (q * jnp.exp(b)).astype(bf16)
    kl = (k * jnp.exp(bl - b)).astype(bf16)
    dec = jnp.exp(bl)
    vb = v.astype(bf16)
    zq = jnp.zeros((n, n), f32)
    intra = jnp.zeros((n, n), f32)
    for h in range(GLA_H):
        hm = (col >> 6) == h
        a1 = _dot_nt(jnp.where(hm, q1, zq).astype(bf16), k1)
        a2 = _dot_nt(jnp.where(hm, q2, zq).astype(bf16), k2)
        a3 = _dot_nt(jnp.where(hm, q3, zq).astype(bf16), k3)
        att = jnp.where(m1, a1, jnp.where(m2, a2, jnp.where(m3, a3, zq)))
        intra = intra + _dot(att.astype(bf16), jnp.where(hm, v, zq).astype(bf16))
    outs = [None] * 4
    for c in (range(4) if not reverse else range(3, -1, -1)):
        rs = slice(64 * c, 64 * c + 64)
        st = st_ref[...]
        inter = _dot_nt(qe[rs], st.astype(bf16))
        kv = _dot_tn(vb[rs], kl[rs])
        drow = dec[64 * c:64 * c + 1, :]
        st_ref[...] = st * drow + jnp.where(same64, kv, zq)
        outs[c] = intra[rs] + inter
    return jnp.concatenate(outs, axis=0)


def _gla_kernel(x_ref, s0_ref, wa_ref, ba_ref, g_ref, bm_ref, o_ref, fin_ref, st_s, o_s, *, t):
    nblk = t // GLA_BLOCK

    def block(j, reverse):
        rows = pl.ds(j * GLA_BLOCK if isinstance(j, int) else pl.multiple_of(j * GLA_BLOCK, GLA_BLOCK), GLA_BLOCK)
        q = x_ref[rows, 0:GLA_W].astype(f32) * GLA_DK ** -0.5
        k = x_ref[rows, GLA_W:2 * GLA_W].astype(f32)
        v = x_ref[rows, 2 * GLA_W:3 * GLA_W].astype(f32)
        ga = x_ref[rows, 4 * GLA_W:4 * GLA_W + LANES]
        d = 1 if reverse else 0
        z = _dot(ga.astype(bf16), wa_ref[:, d * GLA_W:(d + 1) * GLA_W]) + ba_ref[:, d * GLA_W:(d + 1) * GLA_W]
        la = -_softplus(-z) * (1.0 / GLA_TAU)
        o = _gla_block(q, k, v, la, reverse, st_s)
        if reverse:
            o_s[rows, :] = o_s[rows, :] + o
        else:
            o_s[rows, :] = o

    for d in range(2):
        if s0_ref is None:
            st_s[...] = jnp.zeros((GLA_W, GLA_W), f32)
        else:
            st_s[...] = s0_ref[0, d]
        if nblk == 1:
            block(0, d == 1)
        else:
            def body(i, carry, d=d):
                block(i if d == 0 else nblk - 1 - i, d == 1)
                return carry
            lax.fori_loop(0, nblk, body, 0)
        st_t = st_s[...].T
        for h in range(GLA_H):
            fin_ref[0, d, h] = st_t[h * GLA_DK:(h + 1) * GLA_DK, h * GLA_DK:(h + 1) * GLA_DK]
    gg = x_ref[:, 3 * GLA_W:4 * GLA_W].astype(f32)
    o_ref[...] = (_head_rms(o_s[...], g_ref[...], bm_ref[...]) * _silu(gg)).astype(bf16)


def _gla_call(pg, row0, n_seq, t, s0, wa, ba, g, bm):
    sb0 = row0 // t
    assert row0 % t == 0 and t % GLA_BLOCK == 0
    const = lambda b: (0, 0)
    in_specs = [pl.BlockSpec((t, PROJ_G), lambda b: (sb0 + b, 0))]
    args = [pg]
    if s0 is not None:
        in_specs.append(pl.BlockSpec((1, 2, GLA_W, GLA_W), lambda b: (b, 0, 0, 0)))
        args.append(s0)
    in_specs += [pl.BlockSpec((LANES, 2 * GLA_W), const), pl.BlockSpec((1, 2 * GLA_W), const),
                 pl.BlockSpec((1, GLA_W), const), pl.BlockSpec((GLA_W, GLA_W), const)]
    args += [wa, ba, g, bm]

    def kern(*refs):
        refs = list(refs)
        x_ref = refs.pop(0)
        s0_ref = refs.pop(0) if s0 is not None else None
        wa_ref, ba_ref, g_ref, bm_ref = refs[:4]
        _gla_kernel(x_ref, s0_ref, wa_ref, ba_ref, g_ref, bm_ref, *refs[4:], t=t)

    return pl.pallas_call(
        kern,
        grid=(n_seq,),
        in_specs=in_specs,
        out_specs=[pl.BlockSpec((t, GLA_W), lambda b: (b, 0)),
                   pl.BlockSpec((1, 2, GLA_H, GLA_DK, GLA_DK), lambda b: (b, 0, 0, 0, 0))],
        out_shape=[jax.ShapeDtypeStruct((n_seq * t, GLA_W), bf16),
                   jax.ShapeDtypeStruct((n_seq, 2, GLA_H, GLA_DK, GLA_DK), f32)],
        scratch_shapes=[pltpu.VMEM((GLA_W, GLA_W), f32), pltpu.VMEM((t, GLA_W), f32)],
        compiler_params=_cparams(("parallel",)),
        name="gla_t%d" % t,
    )(*args)


def _out_kernel(x_ref, ac_ref, lc_ref, gc_ref, al_ref, ll_ref, gl_ref, mod_ref, wo_ref, g2_ref, rwh_ref, rwl_ref,
                s1_ref, s3_ref, s2_ref, xb_ref, h2_ref, sc_ref, *, n_ctx_tiles):
    is_ctx = pl.program_id(0) < n_ctx_tiles
    pick = lambda c_ref, l_ref: jnp.where(is_ctx, c_ref[...], l_ref[...]).astype(bf16)
    mod = mod_ref[0]
    g1 = mod[:, 2 * D_MODEL:3 * D_MODEL]
    sh2 = mod[:, 3 * D_MODEL:4 * D_MODEL]
    sc2 = mod[:, 4 * D_MODEL:5 * D_MODEL]
    g2 = mod[:, 5 * D_MODEL:6 * D_MODEL]
    m = (_dot(pick(ac_ref, al_ref), wo_ref[0:ATTN_W, :])
         + _dot(pick(lc_ref, ll_ref), wo_ref[ATTN_W:ATTN_W + LRU_W, :])
         + _dot(pick(gc_ref, gl_ref), wo_ref[ATTN_W + LRU_W:ATTN_W + LRU_W + GLA_W, :]))
    x1 = x_ref[...] + g1 * m
    h2 = _rms(x1, g2_ref[...]) * (1.0 + sc2) + sh2
    h2_ref[...] = _pack_pairs(h2)
    h_hi, h_lo = _split(h2)
    logits_t = _dot_nt(rwh_ref[...], h_hi) + _dot_nt(rwh_ref[...], h_lo) + _dot_nt(rwl_ref[...], h_hi)
    sc_ref[...] = _sigmoid(logits_t)
    hb = h2.astype(bf16)
    act = (_silu(_dot(hb, s1_ref[...])) * _dot(hb, s3_ref[...])).astype(bf16)
    xb_ref[...] = x1 + g2 * _dot(act, s2_ref[...])


def _out_call(x, mix_ctx, mix_lat, mod, wo, g2, rwh, rwl, s1, s3, s2, seq_row):
    n = x.shape[0]
    nct = mix_ctx[0].shape[0] // PROJ_TILE
    const = lambda i: (0, 0)
    rowb = lambda w: pl.BlockSpec((PROJ_TILE, w), lambda i: (i, 0))
    ctxb = lambda w: pl.BlockSpec((PROJ_TILE, w), lambda i: (jnp.minimum(i, nct - 1), 0))
    latb = lambda w: pl.BlockSpec((PROJ_TILE, w), lambda i: (jnp.maximum(i - nct, 0), 0))
    return pl.pallas_call(
        functools.partial(_out_kernel, n_ctx_tiles=nct),
        grid=(n // PROJ_TILE,),
        in_specs=[rowb(D_MODEL), ctxb(ATTN_W), ctxb(LRU_W), ctxb(GLA_W), latb(ATTN_W), latb(LRU_W), latb(GLA_W),
                  pl.BlockSpec((1, 1, 6 * D_MODEL), lambda i: (seq_row(i), 0, 0)),
                  pl.BlockSpec((D_MODEL, D_MODEL), const), pl.BlockSpec((1, D_MODEL), const),
                  pl.BlockSpec((N_EXPERTS, D_MODEL), const), pl.BlockSpec((N_EXPERTS, D_MODEL), const),
                  pl.BlockSpec((D_MODEL, EXPERT_FF), const), pl.BlockSpec((D_MODEL, EXPERT_FF), const),
                  pl.BlockSpec((EXPERT_FF, D_MODEL), const)],
        out_specs=[rowb(D_MODEL),
                   rowb(D_PACK),
                   pl.BlockSpec((N_EXPERTS, PROJ_TILE), lambda i: (0, i))],
        out_shape=[jax.ShapeDtypeStruct((n, D_MODEL), f32),
                   jax.ShapeDtypeStruct((n, D_PACK), i32),
                   jax.ShapeDtypeStruct((N_EXPERTS, n), f32)],
        compiler_params=_cparams(("parallel",)),
        name="out_proj",
    )(x, *mix_ctx, *mix_lat, mod, wo, g2, rwh, rwl, s1, s3, s2)


def _set_row(acc, k, row):
    sub = lax.broadcasted_iota(i32, acc.shape, 0)
    return jnp.where(sub == k, jnp.broadcast_to(row, acc.shape), acc)


def _route_kernel(sc_ref, rb_ref, idx_ref, gate_ref, rank_ref, cnt_ref, run_s):
    tm = sc_ref.shape[1]

    @pl.when(pl.program_id(0) == 0)
    def _():
        run_s[...] = jnp.zeros_like(run_s)

    scores = sc_ref[...]
    sel = scores + rb_ref[...]
    erow = lax.broadcasted_iota(i32, (N_EXPERTS, tm), 0).astype(f32)
    neg = jnp.full((N_EXPERTS, tm), -jnp.inf, f32)
    hots = []
    idx_o = jnp.zeros((TOP_K, tm), f32)
    gate_o = jnp.zeros((TOP_K, tm), f32)
    gsum = jnp.zeros((1, tm), f32)
    chosen = jnp.zeros((N_EXPERTS, tm), f32)
    for k in range(TOP_K):
        m = jnp.max(sel, axis=0, keepdims=True)
        idx_f = jnp.min(jnp.where(sel == m, erow, float(N_EXPERTS)), axis=0, keepdims=True)
        hot = erow == idx_f
        gk = jnp.sum(jnp.where(hot, scores, 0.0), axis=0, keepdims=True)
        sel = jnp.where(hot, neg, sel)
        hots.append(hot)
        chosen = jnp.where(hot, 1.0, chosen)
        gsum = gsum + gk
        idx_o = _set_row(idx_o, k, idx_f)
        gate_o = _set_row(gate_o, k, gk)
    gate_ref[...] = gate_o / gsum * ROUTED_SCALE
    idx_ref[...] = idx_o.astype(i32)
    r = lax.broadcasted_iota(i32, (tm, tm), 0)
    c = lax.broadcasted_iota(i32, (tm, tm), 1)
    pos = _dot(chosen.astype(bf16), (r < c).astype(bf16)) + run_s[...]
    rank_o = jnp.zeros((TOP_K, tm), f32)
    for k in range(TOP_K):
        rank_o = _set_row(rank_o, k, jnp.sum(jnp.where(hots[k], pos, 0.0), axis=0, keepdims=True))
    rank_ref[...] = rank_o.astype(i32)
    run_s[...] = run_s[...] + jnp.sum(chosen, axis=1, keepdims=True)
    cnt_ref[...] = run_s[...]


def _route_call(scores_t, rb):
    n = scores_t.shape[1]
    colb = lambda r: pl.BlockSpec((r, ROW_TILE), lambda i: (0, i))
    cnt_spec = pl.BlockSpec((N_EXPERTS, 1), lambda i: (0, 0))
    return pl.pallas_call(
        _route_kernel,
        grid=(n // ROW_TILE,),
        in_specs=[colb(N_EXPERTS), cnt_spec],
        out_specs=[colb(TOP_K), colb(TOP_K), colb(TOP_K), cnt_spec],
        out_shape=[jax.ShapeDtypeStruct((TOP_K, n), i32), jax.ShapeDtypeStruct((TOP_K, n), f32),
                   jax.ShapeDtypeStruct((TOP_K, n), i32), jax.ShapeDtypeStruct((N_EXPERTS, 1), f32)],
        scratch_shapes=[pltpu.VMEM((N_EXPERTS, 1), f32)],
        compiler_params=_cparams(("arbitrary",)),
        name="route",
    )(scores_t, rb)


def _dest_kernel(idx_ref, rank_ref, start_ref, dest_ref):
    tm = idx_ref.shape[1]
    erow = lax.broadcasted_iota(i32, (N_EXPERTS, tm), 0)
    start = jnp.broadcast_to(start_ref[...], (N_EXPERTS, tm))
    idx = idx_ref[...]
    dest = jnp.zeros((TOP_K, tm), f32)
    for k in range(TOP_K):
        hot = erow == idx[k:k + 1, :]
        dest = _set_row(dest, k, jnp.sum(jnp.where(hot, start, 0.0), axis=0, keepdims=True))
    dest_ref[...] = dest.astype(i32) + rank_ref[...]


def _dest_call(idx, rank, start):
    n = idx.shape[1]
    colb = lambda r: pl.BlockSpec((r, ROW_TILE), lambda i: (0, i))
    return pl.pallas_call(
        _dest_kernel,
        grid=(n // ROW_TILE,),
        in_specs=[colb(TOP_K), colb(TOP_K), pl.BlockSpec((N_EXPERTS, 1), lambda i: (0, 0))],
        out_specs=colb(TOP_K),
        out_shape=jax.ShapeDtypeStruct((TOP_K, n), i32),
        compiler_params=_cparams(("parallel",)),
        name="dest",
    )(idx, rank, start)


def _sc_workers():
    info = plsc.get_sparse_core_info()
    mesh = plsc.VectorSubcoreMesh(core_axis_name="c", subcore_axis_name="s")
    worker_id = lambda: lax.axis_index("s") * info.num_cores + lax.axis_index("c")
    return mesh, info.num_cores * info.num_subcores, worker_id


def _scatter_rows_call(rows, dest_km, n_slots):
    n, d = rows.shape
    mesh, n_workers, worker_id = _sc_workers()
    per_worker = n // n_workers
    assert n % (n_workers * SC_CHUNK) == 0

    def body(rows_hbm, idx_hbm, out_hbm, *scratch):
        idx_vs, rows_v, sem = scratch[:TOP_K], scratch[TOP_K], scratch[TOP_K + 1]
        base = worker_id() * per_worker

        @pl.loop(0, per_worker // SC_CHUNK)
        def _(i):
            t0 = pl.multiple_of(base + i * SC_CHUNK, SC_CHUNK)
            pltpu.sync_copy(rows_hbm.at[pl.ds(t0, SC_CHUNK)], rows_v)
            for k in range(TOP_K):
                pltpu.sync_copy(idx_hbm.at[pl.ds(k * n + t0, SC_CHUNK)], idx_vs[k])
            copies = [pltpu.async_copy(rows_v, out_hbm.at[idx_vs[k]], sem) for k in range(TOP_K)]
            for cp in copies:
                cp.wait()

    return pl.kernel(
        body,
        out_type=jax.ShapeDtypeStruct((n_slots, d), rows.dtype),
        mesh=mesh,
        scratch_types=[pltpu.VMEM((SC_CHUNK,), i32)] * TOP_K + [pltpu.VMEM((SC_CHUNK, d), rows.dtype),
                                                                pltpu.SemaphoreType.DMA],
        name="scatter_rows",
    )(rows, dest_km.reshape(-1))


def _expert_kernel(nu_ref, bv_ref, ord_ref, eseq_ref, nex_ref, x_ref, w1_hbm, w3_hbm, w2_hbm, y_ref,
                   w1_f, w3_f, w2_f, w1_s, w3_s, w2_s, sems, *, layer):
    def weight_copies(j):
        e = eseq_ref[j]
        slot = j % W_RING
        return [pltpu.make_async_copy(w1_hbm.at[layer, e], w1_f.at[slot], sems.at[slot, 0]),
                pltpu.make_async_copy(w3_hbm.at[layer, e], w3_f.at[slot], sems.at[slot, 1]),
                pltpu.make_async_copy(w2_hbm.at[layer, e], w2_f.at[slot], sems.at[slot, 2])]

    def start_if_exists(j):
        @pl.when(j < nex_ref[0])
        def _():
            for cp in weight_copies(j):
                cp.start()

    for u in range(STEP_BLOCKS):
        i = pl.program_id(0) * STEP_BLOCKS + u
        rows = slice(u * SLOT_BLOCK, (u + 1) * SLOT_BLOCK)

        @pl.when(i < nu_ref[0])
        def _(i=i, rows=rows):
            j = ord_ref[i]

            @pl.when(i == 0)
            def _():
                for ahead in range(W_RING - 1):
                    start_if_exists(ahead)

            @pl.when((i == 0) | (j != ord_ref[jnp.maximum(i - 1, 0)]))
            def _():
                start_if_exists(j + W_RING - 1)
                for cp in weight_copies(j):
                    cp.wait()
                slot = j % W_RING
                w1_s[...] = w1_f[slot].astype(bf16)
                w3_s[...] = w3_f[slot].astype(bf16)
                w2_s[...] = w2_f[slot].astype(bf16)

            row = lax.broadcasted_iota(i32, (SLOT_BLOCK, D_PACK), 0)
            x_hi, x_lo = _unpack_pairs(jnp.where(row < bv_ref[i], x_ref[rows, :], 0))
            x = jnp.concatenate([x_hi, x_lo], axis=1).astype(bf16)
            act = (_silu(_dot(x, w1_s[...])) * _dot(x, w3_s[...])).astype(bf16)
            y_ref[rows, :] = _pack_pairs(_dot(act, w2_s[...]))

        @pl.when((i >= nu_ref[0]) & (pl.program_id(0) * STEP_BLOCKS < nu_ref[0]))
        def _(rows=rows):
            y_ref[rows, :] = x_ref[rows, :]


def _expert_call(n_used, blk_valid, blk_ord, expert_seq, n_seq_experts, slots, w1, w3, w2, layer):
    step_rows = STEP_BLOCKS * SLOT_BLOCK
    assert slots.shape[0] % step_rows == 0
    blk = lambda s, nu, *_: (jnp.minimum(s, (nu[0] - 1) // STEP_BLOCKS), 0)
    hbm = pl.BlockSpec(memory_space=pl.ANY)
    grid_spec = pltpu.PrefetchScalarGridSpec(
        num_scalar_prefetch=5,
        grid=(slots.shape[0] // step_rows,),
        in_specs=[pl.BlockSpec((step_rows, D_PACK), blk), hbm, hbm, hbm],
        out_specs=pl.BlockSpec((step_rows, D_PACK), blk),
        scratch_shapes=[pltpu.VMEM((W_RING, D_MODEL, EXPERT_FF), f32), pltpu.VMEM((W_RING, D_MODEL, EXPERT_FF), f32),
                        pltpu.VMEM((W_RING, EXPERT_FF, D_MODEL), f32),
                        pltpu.VMEM((D_MODEL, EXPERT_FF), bf16), pltpu.VMEM((D_MODEL, EXPERT_FF), bf16),
                        pltpu.VMEM((EXPERT_FF, D_MODEL), bf16),
                        pltpu.SemaphoreType.DMA((W_RING, 3))],
    )
    return pl.pallas_call(
        functools.partial(_expert_kernel, layer=layer),
        grid_spec=grid_spec,
        out_shape=jax.ShapeDtypeStruct(slots.shape, slots.dtype),
        input_output_aliases={5: 0},
        compiler_params=_cparams(("arbitrary",)),
        name="experts",
    )(n_used, blk_valid, blk_ord, expert_seq, n_seq_experts, slots, w1, w3, w2)


def _gather_rows_call(table, idx):
    n_idx = idx.shape[0]
    d = table.shape[1]
    mesh, n_workers, worker_id = _sc_workers()
    per_worker = n_idx // n_workers
    assert n_idx % (n_workers * SC_CHUNK) == 0

    n_chunks = per_worker // SC_CHUNK
    assert n_chunks % 2 == 0

    def body(table_hbm, idx_hbm, out_hbm, idx_a, idx_b, rows_a, rows_b, sem_a, sem_b):
        base = worker_id() * per_worker
        bufs = ((idx_a, rows_a, sem_a), (idx_b, rows_b, sem_b))

        def start_gather(c, buf):
            idx_v, rows_v, sem = buf
            off = pl.multiple_of(base + c * SC_CHUNK, SC_CHUNK)
            pltpu.sync_copy(idx_hbm.at[pl.ds(off, SC_CHUNK)], idx_v)
            pltpu.async_copy(table_hbm.at[idx_v], rows_v, sem)

        start_gather(0, bufs[0])

        @pl.loop(0, n_chunks, step=2)
        def _(c0):
            for b in range(2):
                c = c0 + b
                idx_v, rows_v, sem = bufs[b]

                @pl.when(c + 1 < n_chunks)
                def _():
                    start_gather(c + 1, bufs[1 - b])

                pltpu.make_async_copy(table_hbm.at[idx_v], rows_v, sem).wait()
                off = pl.multiple_of(base + c * SC_CHUNK, SC_CHUNK)
                pltpu.sync_copy(rows_v, out_hbm.at[pl.ds(off, SC_CHUNK)])

    return pl.kernel(
        body,
        out_type=jax.ShapeDtypeStruct((n_idx, d), table.dtype),
        mesh=mesh,
        scratch_types=[pltpu.VMEM((SC_CHUNK,), i32), pltpu.VMEM((SC_CHUNK,), i32),
                       pltpu.VMEM((SC_CHUNK, d), table.dtype), pltpu.VMEM((SC_CHUNK, d), table.dtype),
                       pltpu.SemaphoreType.DMA, pltpu.SemaphoreType.DMA],
        name="gather_rows",
    )(table, idx)


def _combine_kernel(gate_ref, xb_ref, mod_ref, y_ref, o_ref):
    g2 = mod_ref[0][:, 5 * D_MODEL:6 * D_MODEL]
    gates = gate_ref[...]
    acc_hi = jnp.zeros((gates.shape[0], D_PACK), f32)
    acc_lo = jnp.zeros((gates.shape[0], D_PACK), f32)
    for k in range(TOP_K):
        y_hi, y_lo = _unpack_pairs(y_ref[k])
        acc_hi = acc_hi + gates[:, k:k + 1] * y_hi
        acc_lo = acc_lo + gates[:, k:k + 1] * y_lo
    o_ref[:, 0:D_PACK] = xb_ref[:, 0:D_PACK] + g2[:, 0:D_PACK] * acc_hi
    o_ref[:, D_PACK:D_MODEL] = xb_ref[:, D_PACK:D_MODEL] + g2[:, D_PACK:D_MODEL] * acc_lo


def _combine_call(gates, xbase, mod, ygath, seq_row_c, row0, n_rows):
    tm = COMBINE_TILE
    assert row0 % tm == 0 and n_rows % tm == 0
    b0 = row0 // tm
    return pl.pallas_call(
        _combine_kernel,
        grid=(n_rows // tm,),
        in_specs=[pl.BlockSpec((tm, TOP_K), lambda i: (b0 + i, 0)),
                  pl.BlockSpec((tm, D_MODEL), lambda i: (b0 + i, 0)),
                  pl.BlockSpec((1, 1, 6 * D_MODEL), lambda i: (seq_row_c(b0 + i), 0, 0)),
                  pl.BlockSpec((TOP_K, tm, D_PACK), lambda i: (0, b0 + i, 0))],
        out_specs=pl.BlockSpec((tm, D_MODEL), lambda i: (i, 0)),
        out_shape=jax.ShapeDtypeStruct((n_rows, D_MODEL), f32),
        compiler_params=_cparams(("parallel",)),
        name="combine",
    )(gates, xbase, mod, ygath)


def _block_avg(width, group):
    r = np.arange(width)
    return jnp.asarray((r[:, None] // group == r[None, :] // group).astype(np.float32) / group, dtype=bf16)


def _kv_replicate():
    c = np.arange(ATTN_W)
    src = (c // (ATTN_W // N_KV_HEADS)) * HEAD_DIM + c % HEAD_DIM
    return jnp.asarray((np.arange(KV_W)[:, None] == src[None, :]).astype(np.float32), dtype=bf16)


def _rope_lane_tables(n_tok):
    rows = n_tok // GRID_W
    r = jnp.repeat(jnp.arange(rows, dtype=f32), GRID_W)
    col = jnp.tile(jnp.arange(GRID_W, dtype=f32), rows)
    inv = ROPE_THETA ** (-jnp.arange(ROPE_FREQ, dtype=f32) / ROPE_FREQ)
    ar = r[:, None] * inv
    ac = col[:, None] * inv
    cos_h = jnp.concatenate([jnp.cos(ar), jnp.cos(ar), jnp.cos(ac), jnp.cos(ac)], axis=-1)
    sin_h = jnp.concatenate([-jnp.sin(ar), jnp.sin(ar), -jnp.sin(ac), jnp.sin(ac)], axis=-1)
    return jnp.tile(cos_h, (1, N_HEADS)), jnp.tile(sin_h, (1, N_HEADS))


def _block_diag(w):
    nb, bw, _ = w.shape
    eye = jnp.eye(nb, dtype=w.dtype)
    return (w[:, :, None, :] * eye[:, None, :, None]).reshape(nb * bw, nb * bw)


def _gla_state_in(s):
    bsz = s.shape[0]
    eye = jnp.eye(GLA_H, dtype=s.dtype)
    st = jnp.swapaxes(s, -1, -2)
    big = st[:, :, :, :, None, :] * eye[None, None, :, None, :, None]
    return big.reshape(bsz, 2, GLA_W, GLA_W)


def kernel(x_prompt, x_sample, cache_k, cache_v, state_lru, state_gla, c, c_ctx, ada_w, ada_b, norm1_g, norm2_g, w_in, q_norm_g, k_norm_g, attn_out_g, conv_w, conv_b, lru_wa, lru_ba, lru_wi, lru_bi, lru_lambda, lru_out_g, gla_wa2, gla_ba, gla_out_g, w_out, router_w, router_b, exp_w1, exp_w3, exp_w2, sh_w1, sh_w3, sh_w2):
    bc, tc, _ = x_prompt.shape
    bl, tl, _ = x_sample.shape
    depth = w_in.shape[0]
    nc = bc * tc
    n = nc + bl * tl
    past = cache_k.shape[2]
    assert tc == ROW_TILE and tl % ROW_TILE == 0 and nc % tl == 0 and bl + 1 <= SUBLANES
    assert nc % PROJ_TILE == 0 and tl % PROJ_TILE == 0 and nc % COMBINE_TILE == 0 and tl % COMBINE_TILE == 0

    def seq_row_for(tile):
        def seq_row(i):
            return jnp.where(i < nc // tile, 0, 1 + (i - nc // tile) // (tl // tile))
        return seq_row

    seq_row = seq_row_for(PROJ_TILE)
    seq_row_c = seq_row_for(COMBINE_TILE)

    x = jnp.concatenate([x_prompt.reshape(nc, D_MODEL), x_sample.reshape(bl * tl, D_MODEL)], axis=0)
    cond = jnp.zeros((SUBLANES, D_MODEL), f32).at[0].set(c_ctx).at[1:1 + bl].set(c)
    mods = _ada_call(cond, ada_w, ada_b)

    bmq = _block_avg(ATTN_W, HEAD_DIM)
    bmk = _block_avg(KV_W, HEAD_DIM)
    bmg = _block_avg(GLA_W, GLA_DK)
    rep = _kv_replicate()
    cos_t, sin_t = _rope_lane_tables(tl)
    step_rows = STEP_BLOCKS * SLOT_BLOCK
    n_slots = -(-(n * TOP_K + N_EXPERTS * (SLOT_BLOCK - 1)) // step_rows) * step_rows
    tile8 = lambda v: jnp.tile(v, N_HEADS)[None, :]

    ks, vs, lrus, glas = [], [], [], []
    for l in range(depth):
        mod = mods[l].reshape(SUBLANES, 1, 6 * D_MODEL)
        pa, pl_, pg = _proj_call(x, mod, norm1_g[l][None, :], w_in, l, seq_row)

        qg, kg, og = tile8(q_norm_g[l]), jnp.tile(k_norm_g[l], N_KV_HEADS)[None, :], attn_out_g[l][None, :]
        attn_c, k_new, v_new = _attn_ctx_call(pa, bc, tc, qg, kg, og, bmq, bmk, rep)
        attn_l = _attn_lat_call(pa, nc, bl, tl, cache_k[:, l].reshape(bl, past, KV_W),
                                cache_v[:, l].reshape(bl, past, KV_W), cos_t, sin_t, qg, kg, og, bmq, bmk, rep)
        ks.append(k_new.reshape(bc, tc, N_KV_HEADS, HEAD_DIM))
        vs.append(v_new.reshape(bc, tc, N_KV_HEADS, HEAD_DIM))

        wg = jnp.concatenate([_block_diag(lru_wa[l, 0]), _block_diag(lru_wi[l, 0]),
                              _block_diag(lru_wa[l, 1]), _block_diag(lru_wi[l, 1])], axis=1).astype(bf16)
        bg = jnp.concatenate([lru_ba[l, 0], lru_bi[l, 0], lru_ba[l, 1], lru_bi[l, 1]])[None, :]
        lru_args = (conv_w[l], conv_b[l][None, :], wg, bg, lru_lambda[l], lru_out_g[l][None, :])
        lru_c, lru_fin = _lru_call(pl_, 0, bc, tc, jnp.zeros((bc, 2, LRU_W), f32), *lru_args)
        lru_l, _ = _lru_call(pl_, nc, bl, tl, state_lru[:, l], *lru_args)
        lrus.append(lru_fin)

        wa = jnp.zeros((LANES, 2 * GLA_W), f32)
        wa = wa.at[0:GLA_RANK, 0:GLA_W].set(gla_wa2[l, 0]).at[GLA_RANK:2 * GLA_RANK, GLA_W:].set(gla_wa2[l, 1])
        gla_args = (wa.astype(bf16), gla_ba[l].reshape(1, 2 * GLA_W), gla_out_g[l].reshape(1, GLA_W), bmg)
        gla_c, gla_fin = _gla_call(pg, 0, bc, tc, None, *gla_args)
        gla_l, _ = _gla_call(pg, nc, bl, tl, _gla_state_in(state_gla[:, l]), *gla_args)
        glas.append(gla_fin)

        rw_t = router_w[l].T
        rw_hi = rw_t.astype(bf16)
        rw_lo = (rw_t - rw_hi.astype(f32)).astype(bf16)
        xbase, h2, scores_t = _out_call(x, (attn_c, lru_c, gla_c), (attn_l, lru_l, gla_l), mod, w_out[l].astype(bf16),
                                        norm2_g[l][None, :], rw_hi, rw_lo, sh_w1[l].astype(bf16),
                                        sh_w3[l].astype(bf16), sh_w2[l].astype(bf16), seq_row)

        idx, gates, rank, counts = _route_call(scores_t, router_b[l][:, None])
        cnt = counts[:, 0].astype(i32)
        padded = (cnt + SLOT_BLOCK - 1) // SLOT_BLOCK * SLOT_BLOCK
        padded_end = jnp.cumsum(padded)
        dest_km = _dest_call(idx, rank, (padded_end - padded).astype(f32)[:, None])
        nb = n_slots // SLOT_BLOCK
        blk_first = jnp.arange(nb, dtype=i32) * SLOT_BLOCK
        blk_e = jnp.minimum(jnp.sum((padded_end[None, :] <= blk_first[:, None]).astype(i32), axis=1), N_EXPERTS - 1)
        blk_valid = jnp.clip((padded_end - padded + cnt)[blk_e] - blk_first, 0, SLOT_BLOCK)
        n_used = padded_end[-1:] // SLOT_BLOCK
        owns = jnp.cumsum((cnt > 0).astype(i32))
        blk_ord = (owns - 1)[blk_e]
        expert_seq = jnp.minimum(jnp.sum((owns[None, :] <= jnp.arange(N_EXPERTS, dtype=i32)[:, None]).astype(i32),
                                         axis=1), N_EXPERTS - 1)
        slots = _scatter_rows_call(h2, dest_km, n_slots)
        slots = _expert_call(n_used, blk_valid, blk_ord, expert_seq, owns[-1:], slots, exp_w1, exp_w3, exp_w2, l)
        ygath = _gather_rows_call(slots, dest_km.reshape(-1)).reshape(TOP_K, n, D_PACK)
        gates_t = gates.T
        if l + 1 < depth:
            x = _combine_call(gates_t, xbase, mod, ygath, seq_row_c, 0, n)
        else:
            y_prompt = _combine_call(gates_t, xbase, mod, ygath, seq_row_c, 0, nc).reshape(bc, tc, D_MODEL)
            y_sample = _combine_call(gates_t, xbase, mod, ygath, seq_row_c, nc, n - nc).reshape(bl, tl, D_MODEL)

    return (y_prompt, y_sample, jnp.stack(ks, axis=1), jnp.stack(vs, axis=1),
            jnp.stack(lrus, axis=1), jnp.stack(glas, axis=1))
```

```python
import functools

import jax
import jax.numpy as jnp
import numpy as np
from jax import lax
from jax.experimental import pallas as pl
from jax.experimental.pallas import tpu as pltpu
from jax.experimental.pallas import tpu_sc as plsc

f32 = jnp.float32
bf16 = jnp.bfloat16
i32 = jnp.int32

D_MODEL = 1024
N_HEADS = 8
N_KV_HEADS = 2
HEAD_DIM = 64
ATTN_W = N_HEADS * HEAD_DIM
KV_W = N_KV_HEADS * HEAD_DIM
GRID_W = 64
ROPE_FREQ = HEAD_DIM // 4
ROPE_THETA = 10000.0
LRU_W = 256
LRU_BLOCKS = 4
LRU_C = 8.0
CONV_W = 4
CONV_LEFT = 2
GLA_H = 4
GLA_DK = 64
GLA_W = 256
GLA_RANK = 16
GLA_TAU = 16.0
N_EXPERTS = 256
TOP_K = 8
EXPERT_FF = 256
ROUTED_SCALE = 2.5
EPS = 1e-6

LANES = 128
SUBLANES = 8
ROW_TILE = 256
PROJ_TILE = 512
GLA_BLOCK = 256
SLOT_BLOCK = 256
STEP_BLOCKS = 4
W_RING = 3
COMBINE_TILE = 256
SC_CHUNK = 64
D_PACK = D_MODEL // 2
VMEM_LIMIT = 56 * 1024 * 1024


def _cparams(sem, vmem=VMEM_LIMIT):
    return pltpu.CompilerParams(dimension_semantics=sem, vmem_limit_bytes=vmem)


def _dot(a, b):
    return jnp.dot(a, b, preferred_element_type=f32)


def _dot_nt(a, b):
    return lax.dot_general(a, b, (((1,), (1,)), ((), ())), preferred_element_type=f32)


def _dot_tn(a, b):
    return lax.dot_general(a, b, (((0,), (0,)), ((), ())), preferred_element_type=f32)


def _split(x):
    hi = x.astype(bf16)
    lo = (x - hi.astype(f32)).astype(bf16)
    return hi, lo


def _dot_x2(x, w):
    hi, lo = _split(x)
    return _dot(hi, w) + _dot(lo, w)


def _dot_2x(m, x):
    hi, lo = _split(x)
    return _dot(m, hi) + _dot(m, lo)


def _dot3(a, b_hi, b_lo):
    a_hi, a_lo = _split(a)
    return _dot(a_hi, b_hi) + _dot(a_lo, b_hi) + _dot(a_hi, b_lo)


def _sigmoid(x):
    return 1.0 / (1.0 + jnp.exp(-x))


def _silu(x):
    return x * _sigmoid(x)


def _softplus(x):
    return jnp.maximum(x, 0.0) + jnp.log(1.0 + jnp.exp(-jnp.abs(x)))


def _gelu_tanh(x):
    return 0.5 * x * (1.0 + jnp.tanh(0.7978845608028654 * (x + 0.044715 * x * x * x)))


def _rms(x, g):
    return x * lax.rsqrt(jnp.mean(x * x, axis=-1, keepdims=True) + EPS) * g


def _pack_pairs(x):
    c = x.shape[1] // 2
    hi = lax.bitcast_convert_type(x[:, :c].astype(bf16).astype(f32), i32)
    lo = lax.bitcast_convert_type(x[:, c:].astype(bf16).astype(f32), i32)
    return hi | lax.shift_right_logical(lo, 16)


def _unpack_pairs(w):
    hi = lax.bitcast_convert_type(w & jnp.int32(-65536), f32)
    lo = lax.bitcast_convert_type(w << 16, f32)
    return hi, lo


def _head_rms(x, g, bm):
    ms = _dot_x2(x * x, bm)
    return x * lax.rsqrt(ms + EPS) * g


def _ada_kernel(c_ref, w_ref, b_ref, o_ref):
    s = _silu(c_ref[...])
    w = w_ref[0]
    w_hi, w_lo = _split(w)
    o_ref[0] = _dot3(s, w_hi, w_lo) + b_ref[0]


def _ada_call(cond, ada_w, ada_b):
    depth = ada_w.shape[0]
    nt = 1536
    return pl.pallas_call(
        _ada_kernel,
        grid=(depth, 6 * D_MODEL // nt),
        in_specs=[pl.BlockSpec((SUBLANES, D_MODEL), lambda l, j: (0, 0)),
                  pl.BlockSpec((1, D_MODEL, nt), lambda l, j: (l, 0, j)),
                  pl.BlockSpec((1, 1, nt), lambda l, j: (l, 0, j))],
        out_specs=pl.BlockSpec((1, SUBLANES, nt), lambda l, j: (l, 0, j)),
        out_shape=jax.ShapeDtypeStruct((depth, SUBLANES, 6 * D_MODEL), f32),
        compiler_params=_cparams(("parallel", "parallel")),
        name="ada_mod",
    )(cond, ada_w, ada_b.reshape(depth, 1, 6 * D_MODEL))


PROJ_A = ATTN_W + 2 * KV_W
PROJ_L = 2 * LRU_W
PROJ_G = 4 * GLA_W + LANES
IN_COLS = PROJ_A + PROJ_L + 4 * GLA_W + 2 * GLA_RANK


def _proj_kernel(xc_ref, xl_ref, mod_ref, g_ref, w_ref, oa_ref, ol_ref, og_ref, w_s, *, n_ctx_tiles):
    @pl.when(pl.program_id(0) == 0)
    def _():
        w_s[:, 0:IN_COLS] = w_ref[0].astype(bf16)
        w_s[:, IN_COLS:] = jnp.zeros((D_MODEL, w_s.shape[1] - IN_COLS), bf16)

    mod = mod_ref[0]
    sh = mod[:, 0:D_MODEL]
    sc = mod[:, D_MODEL:2 * D_MODEL]
    x = jnp.where(pl.program_id(0) < n_ctx_tiles, xc_ref[...], xl_ref[...])
    h = (_rms(x, g_ref[...]) * (1.0 + sc) + sh).astype(bf16)
    p = _dot(h, w_s[...])
    oa_ref[...] = p[:, 0:PROJ_A].astype(bf16)
    ol_ref[...] = p[:, PROJ_A:PROJ_A + PROJ_L].astype(bf16)
    og_ref[...] = p[:, PROJ_A + PROJ_L:PROJ_A + PROJ_L + PROJ_G].astype(bf16)


def _row_pair_specs(nct, width):
    return (pl.BlockSpec((PROJ_TILE, width), lambda i: (jnp.minimum(i, nct - 1), 0)),
            pl.BlockSpec((PROJ_TILE, width), lambda i: (jnp.maximum(i - nct, 0), 0)))


def _proj_call(xc, xl, mod, g, w_all, layer, seq_row):
    n = xc.shape[0] + xl.shape[0]
    nct = xc.shape[0] // PROJ_TILE
    cols = PROJ_A + PROJ_L + PROJ_G
    return pl.pallas_call(
        functools.partial(_proj_kernel, n_ctx_tiles=nct),
        grid=(n // PROJ_TILE,),
        in_specs=[*_row_pair_specs(nct, D_MODEL),
                  pl.BlockSpec((1, 1, 6 * D_MODEL), lambda i: (seq_row(i), 0, 0)),
                  pl.BlockSpec((1, D_MODEL), lambda i: (0, 0)),
                  pl.BlockSpec((1, D_MODEL, IN_COLS), lambda i: (layer, 0, 0))],
        out_specs=[pl.BlockSpec((PROJ_TILE, PROJ_A), lambda i: (i, 0)),
                   pl.BlockSpec((PROJ_TILE, PROJ_L), lambda i: (i, 0)),
                   pl.BlockSpec((PROJ_TILE, PROJ_G), lambda i: (i, 0))],
        out_shape=[jax.ShapeDtypeStruct((n, PROJ_A), bf16),
                   jax.ShapeDtypeStruct((n, PROJ_L), bf16),
                   jax.ShapeDtypeStruct((n, PROJ_G), bf16)],
        scratch_shapes=[pltpu.VMEM((D_MODEL, cols), bf16)],
        compiler_params=_cparams(("arbitrary",)),
        name="in_proj",
    )(xc, xl, mod, g, w_all)


def _rope(x, cos_t, sin_t):
    w = x.shape[1]
    up = pltpu.roll(x, w - ROPE_FREQ, 1)
    dn = pltpu.roll(x, ROPE_FREQ, 1)
    lane = lax.broadcasted_iota(i32, x.shape, 1)
    partner = jnp.where((lane & (2 * ROPE_FREQ - 1)) < ROPE_FREQ, up, dn)
    return x * cos_t + partner * sin_t


def _attend(q, kk_ref, vv_ref, o_ref):
    tq = q.shape[0]
    gw = ATTN_W // N_KV_HEADS
    lane = lax.broadcasted_iota(i32, (tq, gw), 1)
    for g in range(N_KV_HEADS):
        qg = q[:, g * gw:(g + 1) * gw]
        kg = kk_ref[:, g * gw:(g + 1) * gw]
        vg = vv_ref[:, g * gw:(g + 1) * gw]
        acc = jnp.zeros((tq, gw), f32)
        for hh in range(N_HEADS // N_KV_HEADS):
            hm = (lane >> 6) == hh
            s = _dot_nt(jnp.where(hm, qg, 0.0).astype(bf16), kg)
            m = jnp.max(s, axis=-1, keepdims=True)
            p = jnp.exp(s - m)
            l = jnp.sum(p, axis=-1, keepdims=True)
            o = _dot(p.astype(bf16), vg) / l
            acc = jnp.where(hm, o, acc)
        o_ref[:, g * gw:(g + 1) * gw] = acc


def _attn_ctx_kernel(p_ref, qg_ref, kg_ref, og_ref, bmq_ref, bmk_ref, rep_ref,
                     o_ref, ko_ref, vo_ref, kk_s, vv_s, o_s):
    p = p_ref[...].astype(f32)
    q = _head_rms(p[:, 0:ATTN_W], qg_ref[...], bmq_ref[...])
    k = _head_rms(p[:, ATTN_W:ATTN_W + KV_W], kg_ref[...], bmk_ref[...])
    v = p[:, ATTN_W + KV_W:PROJ_A]
    ko_ref[...] = k
    vo_ref[...] = v
    kk_s[...] = _dot(k.astype(bf16), rep_ref[...]).astype(bf16)
    vv_s[...] = _dot(v.astype(bf16), rep_ref[...]).astype(bf16)
    _attend(q * HEAD_DIM ** -0.5, kk_s, vv_s, o_s)
    o_ref[...] = _rms(o_s[...], og_ref[...]).astype(bf16)


def _attn_ctx_call(pa, n_seq, t, qg, kg, og, bmq, bmk, rep):
    assert t == ROW_TILE
    const = lambda i: (0, 0)
    return pl.pallas_call(
        _attn_ctx_kernel,
        grid=(n_seq,),
        in_specs=[pl.BlockSpec((t, PROJ_A), lambda i: (i, 0)),
                  pl.BlockSpec((1, ATTN_W), const), pl.BlockSpec((1, KV_W), const), pl.BlockSpec((1, ATTN_W), const),
                  pl.BlockSpec((ATTN_W, ATTN_W), const), pl.BlockSpec((KV_W, KV_W), const),
                  pl.BlockSpec((KV_W, ATTN_W), const)],
        out_specs=[pl.BlockSpec((t, ATTN_W), lambda i: (i, 0)),
                   pl.BlockSpec((t, KV_W), lambda i: (i, 0)),
                   pl.BlockSpec((t, KV_W), lambda i: (i, 0))],
        out_shape=[jax.ShapeDtypeStruct((n_seq * t, ATTN_W), bf16),
                   jax.ShapeDtypeStruct((n_seq * t, KV_W), f32),
                   jax.ShapeDtypeStruct((n_seq * t, KV_W), f32)],
        scratch_shapes=[pltpu.VMEM((t, ATTN_W), bf16), pltpu.VMEM((t, ATTN_W), bf16), pltpu.VMEM((t, ATTN_W), f32)],
        compiler_params=_cparams(("parallel",)),
        name="attn_ctx",
    )(pa, qg, kg, og, bmq, bmk, rep)


def _attn_lat_kernel(q_ref, kv_ref, ck_ref, cv_ref, cq_ref, sq_ref, ckk_ref, skk_ref,
                     qg_ref, kg_ref, og_ref, bmq_ref, bmk_ref, rep_ref,
                     o_ref, kk_s, vv_s, o_s, *, past):
    @pl.when(pl.program_id(1) == 0)
    def _():
        kv = kv_ref[...].astype(f32)
        k = _head_rms(kv[:, 0:KV_W], kg_ref[...], bmk_ref[...])
        k = _rope(k, ckk_ref[...], skk_ref[...])
        v = kv[:, KV_W:2 * KV_W]
        kk_s[0:past, :] = _dot(ck_ref[0].astype(bf16), rep_ref[...]).astype(bf16)
        vv_s[0:past, :] = _dot(cv_ref[0].astype(bf16), rep_ref[...]).astype(bf16)
        kk_s[past:, :] = _dot(k.astype(bf16), rep_ref[...]).astype(bf16)
        vv_s[past:, :] = _dot(v.astype(bf16), rep_ref[...]).astype(bf16)

    q = _head_rms(q_ref[...].astype(f32), qg_ref[...], bmq_ref[...])
    q = _rope(q, cq_ref[...], sq_ref[...])
    _attend(q * HEAD_DIM ** -0.5, kk_s, vv_s, o_s)
    o_ref[...] = _rms(o_s[...], og_ref[...]).astype(bf16)


def _attn_lat_call(pa, row0, n_seq, t, cache_k, cache_v, cq, sq, qg, kg, og, bmq, bmk, rep):
    past = cache_k.shape[1]
    nq = t // ROW_TILE
    assert row0 % t == 0 and t % ROW_TILE == 0
    qb0 = row0 // ROW_TILE
    sb0 = row0 // t
    const = lambda b, j: (0, 0)
    return pl.pallas_call(
        functools.partial(_attn_lat_kernel, past=past),
        grid=(n_seq, nq),
        in_specs=[pl.BlockSpec((ROW_TILE, ATTN_W), lambda b, j: (qb0 + b * nq + j, 0)),
                  pl.BlockSpec((t, 2 * KV_W), lambda b, j: (sb0 + b, ATTN_W // (2 * KV_W))),
                  pl.BlockSpec((1, past, KV_W), lambda b, j: (b, 0, 0)),
                  pl.BlockSpec((1, past, KV_W), lambda b, j: (b, 0, 0)),
                  pl.BlockSpec((ROW_TILE, ATTN_W), lambda b, j: (j, 0)),
                  pl.BlockSpec((ROW_TILE, ATTN_W), lambda b, j: (j, 0)),
                  pl.BlockSpec((t, KV_W), const), pl.BlockSpec((t, KV_W), const),
                  pl.BlockSpec((1, ATTN_W), const), pl.BlockSpec((1, KV_W), const), pl.BlockSpec((1, ATTN_W), const),
                  pl.BlockSpec((ATTN_W, ATTN_W), const), pl.BlockSpec((KV_W, KV_W), const),
                  pl.BlockSpec((KV_W, ATTN_W), const)],
        out_specs=pl.BlockSpec((ROW_TILE, ATTN_W), lambda b, j: (b * nq + j, 0)),
        out_shape=jax.ShapeDtypeStruct((n_seq * t, ATTN_W), bf16),
        scratch_shapes=[pltpu.VMEM((past + t, ATTN_W), bf16), pltpu.VMEM((past + t, ATTN_W), bf16),
                        pltpu.VMEM((ROW_TILE, ATTN_W), f32)],
        compiler_params=_cparams(("parallel", "arbitrary")),
        name="attn_lat",
    )(pa, pa, cache_k, cache_v, cq, sq, cq, sq, qg, kg, og, bmq, bmk, rep)


def _lru_kernel(x_ref, h0_ref, cw_ref, cb_ref, wg_ref, bg_ref, lam_ref, g_ref,
                o_ref, fin_ref, xs, a_s, u_s, h_s, *, t):
    ng = t // SUBLANES
    lx = x_ref[:, 0:LRU_W].astype(f32)
    ly = x_ref[:, LRU_W:2 * LRU_W].astype(f32)
    xs[0:SUBLANES, :] = jnp.zeros((SUBLANES, LRU_W), f32)
    xs[SUBLANES + t:2 * SUBLANES + t, :] = jnp.zeros((SUBLANES, LRU_W), f32)
    xs[SUBLANES:SUBLANES + t, :] = lx
    xc = jnp.broadcast_to(cb_ref[...], (t, LRU_W))
    for j in range(CONV_W):
        off = SUBLANES + j - CONV_LEFT
        xc = xc + xs[off:off + t, :] * cw_ref[j:j + 1, :]
    gates = _sigmoid(_dot(xc.astype(bf16), wg_ref[...]) + bg_ref[...])
    sp = _softplus(-lam_ref[...])
    sub = lax.broadcasted_iota(i32, (ng, SUBLANES, LRU_W), 1)
    for d in range(2):
        r = gates[:, (2 * d) * LRU_W:(2 * d + 1) * LRU_W]
        gi = gates[:, (2 * d + 1) * LRU_W:(2 * d + 2) * LRU_W]
        log_a = -LRU_C * r * sp[d:d + 1, :]
        a = jnp.exp(log_a)
        th = jnp.tanh(log_a)
        u = jnp.sqrt(-2.0 * th / (1.0 - th)) * (gi * xc)
        a = a.reshape(ng, SUBLANES, LRU_W)
        u = u.reshape(ng, SUBLANES, LRU_W)
        for s in (1, 2, 4):
            if d == 0:
                a_sh = pltpu.roll(a, s, 1)
                u_sh = pltpu.roll(u, s, 1)
                ok = sub >= s
            else:
                a_sh = pltpu.roll(a, SUBLANES - s, 1)
                u_sh = pltpu.roll(u, SUBLANES - s, 1)
                ok = sub < SUBLANES - s
            u = jnp.where(ok, a * u_sh + u, u)
            a = jnp.where(ok, a * a_sh, a)
        a_s[...] = a.reshape(t, LRU_W)
        u_s[...] = u.reshape(t, LRU_W)
        h0 = jnp.broadcast_to(h0_ref[0, d:d + 1, :], (SUBLANES, LRU_W))
        edge = SUBLANES - 1 if d == 0 else 0

        def body(i, carry, d=d, edge=edge):
            g = i if d == 0 else ng - 1 - i
            rows = pl.ds(pl.multiple_of(g * SUBLANES, SUBLANES), SUBLANES)
            h = a_s[rows, :] * carry + u_s[rows, :]
            if d == 0:
                h_s[rows, :] = h
            else:
                h_s[rows, :] = h_s[rows, :] + h
            return jnp.broadcast_to(h[edge:edge + 1, :], (SUBLANES, LRU_W))

        last = lax.fori_loop(0, ng, body, h0)
        fin_ref[0, d:d + 1, :] = last[0:1, :]
    o_ref[...] = _rms(h_s[...] * _gelu_tanh(ly), g_ref[...]).astype(bf16)


def _lru_call(pl_, row0, n_seq, t, h0, cw, cb, wg, bg, lam, g):
    sb0 = row0 // t
    assert row0 % t == 0
    const = lambda b: (0, 0)
    in_specs = [pl.BlockSpec((t, PROJ_L), lambda b: (sb0 + b, 0)),
                pl.BlockSpec((1, 2, LRU_W), lambda b: (b, 0, 0)),
                pl.BlockSpec((CONV_W, LRU_W), const), pl.BlockSpec((1, LRU_W), const),
                pl.BlockSpec((LRU_W, 4 * LRU_W), const), pl.BlockSpec((1, 4 * LRU_W), const),
                pl.BlockSpec((2, LRU_W), const), pl.BlockSpec((1, LRU_W), const)]
    args = [pl_, h0, cw, cb, wg, bg, lam, g]
    return pl.pallas_call(
        functools.partial(_lru_kernel, t=t),
        grid=(n_seq,),
        in_specs=in_specs,
        out_specs=[pl.BlockSpec((t, LRU_W), lambda b: (b, 0)),
                   pl.BlockSpec((1, 2, LRU_W), lambda b: (b, 0, 0))],
        out_shape=[jax.ShapeDtypeStruct((n_seq * t, LRU_W), bf16), jax.ShapeDtypeStruct((n_seq, 2, LRU_W), f32)],
        scratch_shapes=[pltpu.VMEM((t + 2 * SUBLANES, LRU_W), f32), pltpu.VMEM((t, LRU_W), f32),
                        pltpu.VMEM((t, LRU_W), f32), pltpu.VMEM((t, LRU_W), f32)],
        compiler_params=_cparams(("parallel",)),
        name="rglru_t%d" % t,
    )(*args)


def _bcast_rows(b, period, off):
    w = b.shape[1]
    return jnp.concatenate(
        [jnp.broadcast_to(b[i * period + off:i * period + off + 1, :], (period, w)) for i in range(b.shape[0] // period)],
        axis=0)


def _gla_block(q, k, v, la, reverse, st_ref):
    n = GLA_BLOCK
    row = lax.broadcasted_iota(i32, (n, n), 0)
    col = lax.broadcasted_iota(i32, (n, n), 1)
    same64 = (row >> 6) == (col >> 6)
    same32 = (row >> 5) == (col >> 5)
    same16 = (row >> 4) == (col >> 4)
    if not reverse:
        cum = (same64 & (col <= row)).astype(bf16)
        m1 = same64 & ((row & 63) >= 32) & ((col & 63) < 32)
        m2 = same32 & ((row & 31) >= 16) & ((col & 31) < 16)
        m3 = same16 & (col <= row)
        offs = (31, 15, 7, 63)
    else:
        cum = (same64 & (col >= row)).astype(bf16)
        m1 = same64 & ((row & 63) < 32) & ((col & 63) >= 32)
        m2 = same32 & ((row & 31) < 16) & ((col & 31) >= 16)
        m3 = same16 & (col >= row)
        offs = (32, 16, 8, 0)
    b = _dot_2x(cum, la)
    r1 = _bcast_rows(b, 64, offs[0])
    r2 = _bcast_rows(b, 32, offs[1])
    r3 = _bcast_rows(b, 16, offs[2])
    bl = _bcast_rows(b, 64, offs[3])
    q1 = q * jnp.exp(jnp.minimum(b - r1, 0.0))
    k1 = (k * jnp.exp(jnp.minimum(r1 - b, 0.0))).astype(bf16)
    q2 = q * jnp.exp(jnp.minimum(b - r2, 0.0))
    k2 = (k * jnp.exp(jnp.minimum(r2 - b, 0.0))).astype(bf16)
    q3 = q * jnp.exp(b - r3)
    k3 = (k * jnp.exp(r3 - b)).astype(bf16)
    qe = (q * jnp.exp(b)).astype(bf16)
    kl = (k * jnp.exp(bl - b)).astype(bf16)
    dec = jnp.exp(bl)
    vb = v.astype(bf16)
    zq = jnp.zeros((n, n), f32)
    intra = jnp.zeros((n, n), f32)
    for h in range(GLA_H):
        hm = (col >> 6) == h
        a1 = _dot_nt(jnp.where(hm, q1, zq).astype(bf16), k1)
        a2 = _dot_nt(jnp.where(hm, q2, zq).astype(bf16), k2)
        a3 = _dot_nt(jnp.where(hm, q3, zq).astype(bf16), k3)
        att = jnp.where(m1, a1, jnp.where(m2, a2, jnp.where(m3, a3, zq)))
        intra = intra + _dot(att.astype(bf16), jnp.where(hm, v, zq).astype(bf16))
    outs = [None] * 4
    for c in (range(4) if not reverse else range(3, -1, -1)):
        rs = slice(64 * c, 64 * c + 64)
        st = st_ref[...]
        inter = _dot_nt(qe[rs], st.astype(bf16))
        kv = _dot_tn(vb[rs], kl[rs])
        drow = dec[64 * c:64 * c + 1, :]
        st_ref[...] = st * drow + jnp.where(same64, kv, zq)
        outs[c] = intra[rs] + inter
    return jnp.concatenate(outs, axis=0)


def _gla_kernel(x_ref, s0_ref, wa_ref, ba_ref, g_ref, bm_ref, o_ref, fin_ref, st_s, o_s, *, t):
    nblk = t // GLA_BLOCK

    def block(j, reverse):
        rows = pl.ds(j * GLA_BLOCK if isinstance(j, int) else pl.multiple_of(j * GLA_BLOCK, GLA_BLOCK), GLA_BLOCK)
        q = x_ref[rows, 0:GLA_W].astype(f32) * GLA_DK ** -0.5
        k = x_ref[rows, GLA_W:2 * GLA_W].astype(f32)
        v = x_ref[rows, 2 * GLA_W:3 * GLA_W].astype(f32)
        ga = x_ref[rows, 4 * GLA_W:4 * GLA_W + LANES]
        d = 1 if reverse else 0
        z = _dot(ga.astype(bf16), wa_ref[:, d * GLA_W:(d + 1) * GLA_W]) + ba_ref[:, d * GLA_W:(d + 1) * GLA_W]
        la = -_softplus(-z) * (1.0 / GLA_TAU)
        o = _gla_block(q, k, v, la, reverse, st_s)
        if reverse:
            o_s[rows, :] = o_s[rows, :] + o
        else:
            o_s[rows, :] = o

    for d in range(2):
        if s0_ref is None:
            st_s[...] = jnp.zeros((GLA_W, GLA_W), f32)
        else:
            st_s[...] = s0_ref[0, d]
        if nblk == 1:
            block(0, d == 1)
        else:
            def body(i, carry, d=d):
                block(i if d == 0 else nblk - 1 - i, d == 1)
                return carry
            lax.fori_loop(0, nblk, body, 0)
        st_t = st_s[...].T
        for h in range(GLA_H):
            fin_ref[0, d, h] = st_t[h * GLA_DK:(h + 1) * GLA_DK, h * GLA_DK:(h + 1) * GLA_DK]
    gg = x_ref[:, 3 * GLA_W:4 * GLA_W].astype(f32)
    o_ref[...] = (_head_rms(o_s[...], g_ref[...], bm_ref[...]) * _silu(gg)).astype(bf16)


def _gla_call(pg, row0, n_seq, t, s0, wa, ba, g, bm):
    sb0 = row0 // t
    assert row0 % t == 0 and t % GLA_BLOCK == 0
    const = lambda b: (0, 0)
    in_specs = [pl.BlockSpec((t, PROJ_G), lambda b: (sb0 + b, 0))]
    args = [pg]
    if s0 is not None:
        in_specs.append(pl.BlockSpec((1, 2, GLA_W, GLA_W), lambda b: (b, 0, 0, 0)))
        args.append(s0)
    in_specs += [pl.BlockSpec((LANES, 2 * GLA_W), const), pl.BlockSpec((1, 2 * GLA_W), const),
                 pl.BlockSpec((1, GLA_W), const), pl.BlockSpec((GLA_W, GLA_W), const)]
    args += [wa, ba, g, bm]

    def kern(*refs):
        refs = list(refs)
        x_ref = refs.pop(0)
        s0_ref = refs.pop(0) if s0 is not None else None
        wa_ref, ba_ref, g_ref, bm_ref = refs[:4]
        _gla_kernel(x_ref, s0_ref, wa_ref, ba_ref, g_ref, bm_ref, *refs[4:], t=t)

    return pl.pallas_call(
        kern,
        grid=(n_seq,),
        in_specs=in_specs,
        out_specs=[pl.BlockSpec((t, GLA_W), lambda b: (b, 0)),
                   pl.BlockSpec((1, 2, GLA_H, GLA_DK, GLA_DK), lambda b: (b, 0, 0, 0, 0))],
        out_shape=[jax.ShapeDtypeStruct((n_seq * t, GLA_W), bf16),
                   jax.ShapeDtypeStruct((n_seq, 2, GLA_H, GLA_DK, GLA_DK), f32)],
        scratch_shapes=[pltpu.VMEM((GLA_W, GLA_W), f32), pltpu.VMEM((t, GLA_W), f32)],
        compiler_params=_cparams(("parallel",)),
        name="gla_t%d" % t,
    )(*args)


def _out_kernel(xc_ref, xl_ref, ac_ref, lc_ref, gc_ref, al_ref, ll_ref, gl_ref, mod_ref, wo_ref, g2_ref,
                rwh_ref, rwl_ref, s1_ref, s3_ref, s2_ref, xb_ref, h2_ref, sc_ref, *, n_ctx_tiles):
    is_ctx = pl.program_id(0) < n_ctx_tiles
    pick = lambda c_ref, l_ref: jnp.where(is_ctx, c_ref[...], l_ref[...]).astype(bf16)
    mod = mod_ref[0]
    g1 = mod[:, 2 * D_MODEL:3 * D_MODEL]
    sh2 = mod[:, 3 * D_MODEL:4 * D_MODEL]
    sc2 = mod[:, 4 * D_MODEL:5 * D_MODEL]
    g2 = mod[:, 5 * D_MODEL:6 * D_MODEL]
    m = (_dot(pick(ac_ref, al_ref), wo_ref[0:ATTN_W, :])
         + _dot(pick(lc_ref, ll_ref), wo_ref[ATTN_W:ATTN_W + LRU_W, :])
         + _dot(pick(gc_ref, gl_ref), wo_ref[ATTN_W + LRU_W:ATTN_W + LRU_W + GLA_W, :]))
    x1 = jnp.where(is_ctx, xc_ref[...], xl_ref[...]) + g1 * m
    h2 = _rms(x1, g2_ref[...]) * (1.0 + sc2) + sh2
    h2_ref[...] = _pack_pairs(h2)
    h_hi, h_lo = _split(h2)
    logits_t = _dot_nt(rwh_ref[...], h_hi) + _dot_nt(rwh_ref[...], h_lo) + _dot_nt(rwl_ref[...], h_hi)
    sc_ref[...] = _sigmoid(logits_t)
    hb = h2.astype(bf16)
    act = (_silu(_dot(hb, s1_ref[...])) * _dot(hb, s3_ref[...])).astype(bf16)
    xb_ref[...] = x1 + g2 * _dot(act, s2_ref[...])


def _out_call(xc, xl, mix_ctx, mix_lat, mod, wo, g2, rwh, rwl, s1, s3, s2, seq_row):
    n = xc.shape[0] + xl.shape[0]
    nct = xc.shape[0] // PROJ_TILE
    const = lambda i: (0, 0)
    rowb = lambda w: pl.BlockSpec((PROJ_TILE, w), lambda i: (i, 0))
    ctxb = lambda w: _row_pair_specs(nct, w)[0]
    latb = lambda w: _row_pair_specs(nct, w)[1]
    return pl.pallas_call(
        functools.partial(_out_kernel, n_ctx_tiles=nct),
        grid=(n // PROJ_TILE,),
        in_specs=[ctxb(D_MODEL), latb(D_MODEL),
                  ctxb(ATTN_W), ctxb(LRU_W), ctxb(GLA_W), latb(ATTN_W), latb(LRU_W), latb(GLA_W),
                  pl.BlockSpec((1, 1, 6 * D_MODEL), lambda i: (seq_row(i), 0, 0)),
                  pl.BlockSpec((D_MODEL, D_MODEL), const), pl.BlockSpec((1, D_MODEL), const),
                  pl.BlockSpec((N_EXPERTS, D_MODEL), const), pl.BlockSpec((N_EXPERTS, D_MODEL), const),
                  pl.BlockSpec((D_MODEL, EXPERT_FF), const), pl.BlockSpec((D_MODEL, EXPERT_FF), const),
                  pl.BlockSpec((EXPERT_FF, D_MODEL), const)],
        out_specs=[rowb(D_MODEL),
                   rowb(D_PACK),
                   pl.BlockSpec((N_EXPERTS, PROJ_TILE), lambda i: (0, i))],
        out_shape=[jax.ShapeDtypeStruct((n, D_MODEL), f32),
                   jax.ShapeDtypeStruct((n, D_PACK), i32),
                   jax.ShapeDtypeStruct((N_EXPERTS, n), f32)],
        compiler_params=_cparams(("parallel",)),
        name="out_proj",
    )(xc, xl, *mix_ctx, *mix_lat, mod, wo, g2, rwh, rwl, s1, s3, s2)


def _set_row(acc, k, row):
    sub = lax.broadcasted_iota(i32, acc.shape, 0)
    return jnp.where(sub == k, jnp.broadcast_to(row, acc.shape), acc)


def _route_kernel(sc_ref, rb_ref, idx_ref, gate_ref, rank_ref, cnt_ref, run_s):
    tm = sc_ref.shape[1]

    @pl.when(pl.program_id(0) == 0)
    def _():
        run_s[...] = jnp.zeros_like(run_s)

    scores = sc_ref[...]
    sel = scores + rb_ref[...]
    erow = lax.broadcasted_iota(i32, (N_EXPERTS, tm), 0).astype(f32)
    neg = jnp.full((N_EXPERTS, tm), -jnp.inf, f32)
    hots = []
    idx_o = jnp.zeros((TOP_K, tm), f32)
    gate_o = jnp.zeros((TOP_K, tm), f32)
    gsum = jnp.zeros((1, tm), f32)
    chosen = jnp.zeros((N_EXPERTS, tm), f32)
    for k in range(TOP_K):
        m = jnp.max(sel, axis=0, keepdims=True)
        idx_f = jnp.min(jnp.where(sel == m, erow, float(N_EXPERTS)), axis=0, keepdims=True)
        hot = erow == idx_f
        gk = jnp.sum(jnp.where(hot, scores, 0.0), axis=0, keepdims=True)
        sel = jnp.where(hot, neg, sel)
        hots.append(hot)
        chosen = jnp.where(hot, 1.0, chosen)
        gsum = gsum + gk
        idx_o = _set_row(idx_o, k, idx_f)
        gate_o = _set_row(gate_o, k, gk)
    gate_ref[...] = gate_o / gsum * ROUTED_SCALE
    idx_ref[...] = idx_o.astype(i32)
    r = lax.broadcasted_iota(i32, (tm, tm), 0)
    c = lax.broadcasted_iota(i32, (tm, tm), 1)
    pos = _dot(chosen.astype(bf16), (r < c).astype(bf16)) + run_s[...]
    rank_o = jnp.zeros((TOP_K, tm), f32)
    for k in range(TOP_K):
        rank_o = _set_row(rank_o, k, jnp.sum(jnp.where(hots[k], pos, 0.0), axis=0, keepdims=True))
    rank_ref[...] = rank_o.astype(i32)
    run_s[...] = run_s[...] + jnp.sum(chosen, axis=1, keepdims=True)
    cnt_ref[...] = run_s[...]


def _route_call(scores_t, rb):
    n = scores_t.shape[1]
    colb = lambda r: pl.BlockSpec((r, ROW_TILE), lambda i: (0, i))
    cnt_spec = pl.BlockSpec((N_EXPERTS, 1), lambda i: (0, 0))
    return pl.pallas_call(
        _route_kernel,
        grid=(n // ROW_TILE,),
        in_specs=[colb(N_EXPERTS), cnt_spec],
        out_specs=[colb(TOP_K), colb(TOP_K), colb(TOP_K), cnt_spec],
        out_shape=[jax.ShapeDtypeStruct((TOP_K, n), i32), jax.ShapeDtypeStruct((TOP_K, n), f32),
                   jax.ShapeDtypeStruct((TOP_K, n), i32), jax.ShapeDtypeStruct((N_EXPERTS, 1), f32)],
        scratch_shapes=[pltpu.VMEM((N_EXPERTS, 1), f32)],
        compiler_params=_cparams(("arbitrary",)),
        name="route",
    )(scores_t, rb)


def _dest_kernel(idx_ref, rank_ref, start_ref, dest_ref):
    tm = idx_ref.shape[1]
    erow = lax.broadcasted_iota(i32, (N_EXPERTS, tm), 0)
    start = jnp.broadcast_to(start_ref[...], (N_EXPERTS, tm))
    idx = idx_ref[...]
    dest = jnp.zeros((TOP_K, tm), f32)
    for k in range(TOP_K):
        hot = erow == idx[k:k + 1, :]
        dest = _set_row(dest, k, jnp.sum(jnp.where(hot, start, 0.0), axis=0, keepdims=True))
    dest_ref[...] = dest.astype(i32) + rank_ref[...]


def _dest_call(idx, rank, start):
    n = idx.shape[1]
    colb = lambda r: pl.BlockSpec((r, ROW_TILE), lambda i: (0, i))
    return pl.pallas_call(
        _dest_kernel,
        grid=(n // ROW_TILE,),
        in_specs=[colb(TOP_K), colb(TOP_K), pl.BlockSpec((N_EXPERTS, 1), lambda i: (0, 0))],
        out_specs=colb(TOP_K),
        out_shape=jax.ShapeDtypeStruct((TOP_K, n), i32),
        compiler_params=_cparams(("parallel",)),
        name="dest",
    )(idx, rank, start)


def _sc_workers():
    info = plsc.get_sparse_core_info()
    mesh = plsc.VectorSubcoreMesh(core_axis_name="c", subcore_axis_name="s")
    worker_id = lambda: lax.axis_index("s") * info.num_cores + lax.axis_index("c")
    return mesh, info.num_cores * info.num_subcores, worker_id


def _scatter_rows_call(rows, dest_km, n_slots):
    n, d = rows.shape
    mesh, n_workers, worker_id = _sc_workers()
    per_worker = n // n_workers
    assert n % (n_workers * SC_CHUNK) == 0

    def body(rows_hbm, idx_hbm, out_hbm, *scratch):
        idx_vs, rows_v, sem = scratch[:TOP_K], scratch[TOP_K], scratch[TOP_K + 1]
        base = worker_id() * per_worker

        @pl.loop(0, per_worker // SC_CHUNK)
        def _(i):
            t0 = pl.multiple_of(base + i * SC_CHUNK, SC_CHUNK)
            pltpu.sync_copy(rows_hbm.at[pl.ds(t0, SC_CHUNK)], rows_v)
            for k in range(TOP_K):
                pltpu.sync_copy(idx_hbm.at[pl.ds(k * n + t0, SC_CHUNK)], idx_vs[k])
            copies = [pltpu.async_copy(rows_v, out_hbm.at[idx_vs[k]], sem) for k in range(TOP_K)]
            for cp in copies:
                cp.wait()

    return pl.kernel(
        body,
        out_type=jax.ShapeDtypeStruct((n_slots, d), rows.dtype),
        mesh=mesh,
        scratch_types=[pltpu.VMEM((SC_CHUNK,), i32)] * TOP_K + [pltpu.VMEM((SC_CHUNK, d), rows.dtype),
                                                                pltpu.SemaphoreType.DMA],
        name="scatter_rows",
    )(rows, dest_km.reshape(-1))


def _expert_kernel(nu_ref, bv_ref, ord_ref, eseq_ref, nex_ref, x_ref, w1_hbm, w3_hbm, w2_hbm, y_ref,
                   w1_f, w3_f, w2_f, w1_s, w3_s, w2_s, sems, *, layer):
    def weight_copies(j):
        e = eseq_ref[j]
        slot = j % W_RING
        return [pltpu.make_async_copy(w1_hbm.at[layer, e], w1_f.at[slot], sems.at[slot, 0]),
                pltpu.make_async_copy(w3_hbm.at[layer, e], w3_f.at[slot], sems.at[slot, 1]),
                pltpu.make_async_copy(w2_hbm.at[layer, e], w2_f.at[slot], sems.at[slot, 2])]

    def start_if_exists(j):
        @pl.when(j < nex_ref[0])
        def _():
            for cp in weight_copies(j):
                cp.start()

    for u in range(STEP_BLOCKS):
        i = pl.program_id(0) * STEP_BLOCKS + u
        rows = slice(u * SLOT_BLOCK, (u + 1) * SLOT_BLOCK)

        @pl.when(i < nu_ref[0])
        def _(i=i, rows=rows):
            j = ord_ref[i]

            @pl.when(i == 0)
            def _():
                for ahead in range(W_RING - 1):
                    start_if_exists(ahead)

            @pl.when((i == 0) | (j != ord_ref[jnp.maximum(i - 1, 0)]))
            def _():
                start_if_exists(j + W_RING - 1)
                for cp in weight_copies(j):
                    cp.wait()
                slot = j % W_RING
                w1_s[...] = w1_f[slot].astype(bf16)
                w3_s[...] = w3_f[slot].astype(bf16)
                w2_s[...] = w2_f[slot].astype(bf16)

            row = lax.broadcasted_iota(i32, (SLOT_BLOCK, D_PACK), 0)
            x_hi, x_lo = _unpack_pairs(jnp.where(row < bv_ref[i], x_ref[rows, :], 0))
            x = jnp.concatenate([x_hi, x_lo], axis=1).astype(bf16)
            act = (_silu(_dot(x, w1_s[...])) * _dot(x, w3_s[...])).astype(bf16)
            y_ref[rows, :] = _pack_pairs(_dot(act, w2_s[...]))

        @pl.when((i >= nu_ref[0]) & (pl.program_id(0) * STEP_BLOCKS < nu_ref[0]))
        def _(rows=rows):
            y_ref[rows, :] = x_ref[rows, :]


def _expert_call(n_used, blk_valid, blk_ord, expert_seq, n_seq_experts, slots, w1, w3, w2, layer):
    step_rows = STEP_BLOCKS * SLOT_BLOCK
    assert slots.shape[0] % step_rows == 0
    blk = lambda s, nu, *_: (jnp.minimum(s, (nu[0] - 1) // STEP_BLOCKS), 0)
    hbm = pl.BlockSpec(memory_space=pl.ANY)
    grid_spec = pltpu.PrefetchScalarGridSpec(
        num_scalar_prefetch=5,
        grid=(slots.shape[0] // step_rows,),
        in_specs=[pl.BlockSpec((step_rows, D_PACK), blk), hbm, hbm, hbm],
        out_specs=pl.BlockSpec((step_rows, D_PACK), blk),
        scratch_shapes=[pltpu.VMEM((W_RING, D_MODEL, EXPERT_FF), f32), pltpu.VMEM((W_RING, D_MODEL, EXPERT_FF), f32),
                        pltpu.VMEM((W_RING, EXPERT_FF, D_MODEL), f32),
                        pltpu.VMEM((D_MODEL, EXPERT_FF), bf16), pltpu.VMEM((D_MODEL, EXPERT_FF), bf16),
                        pltpu.VMEM((EXPERT_FF, D_MODEL), bf16),
                        pltpu.SemaphoreType.DMA((W_RING, 3))],
    )
    return pl.pallas_call(
        functools.partial(_expert_kernel, layer=layer),
        grid_spec=grid_spec,
        out_shape=jax.ShapeDtypeStruct(slots.shape, slots.dtype),
        input_output_aliases={5: 0},
        compiler_params=_cparams(("arbitrary",)),
        name="experts",
    )(n_used, blk_valid, blk_ord, expert_seq, n_seq_experts, slots, w1, w3, w2)


def _gather_rows_call(table, idx):
    n_idx = idx.shape[0]
    d = table.shape[1]
    mesh, n_workers, worker_id = _sc_workers()
    per_worker = n_idx // n_workers
    assert n_idx % (n_workers * SC_CHUNK) == 0

    n_chunks = per_worker // SC_CHUNK
    assert n_chunks % 2 == 0

    def body(table_hbm, idx_hbm, out_hbm, idx_a, idx_b, rows_a, rows_b, sem_a, sem_b):
        base = worker_id() * per_worker
        bufs = ((idx_a, rows_a, sem_a), (idx_b, rows_b, sem_b))

        def start_gather(c, buf):
            idx_v, rows_v, sem = buf
            off = pl.multiple_of(base + c * SC_CHUNK, SC_CHUNK)
            pltpu.sync_copy(idx_hbm.at[pl.ds(off, SC_CHUNK)], idx_v)
            pltpu.async_copy(table_hbm.at[idx_v], rows_v, sem)

        start_gather(0, bufs[0])

        @pl.loop(0, n_chunks, step=2)
        def _(c0):
            for b in range(2):
                c = c0 + b
                idx_v, rows_v, sem = bufs[b]

                @pl.when(c + 1 < n_chunks)
                def _():
                    start_gather(c + 1, bufs[1 - b])

                pltpu.make_async_copy(table_hbm.at[idx_v], rows_v, sem).wait()
                off = pl.multiple_of(base + c * SC_CHUNK, SC_CHUNK)
                pltpu.sync_copy(rows_v, out_hbm.at[pl.ds(off, SC_CHUNK)])

    return pl.kernel(
        body,
        out_type=jax.ShapeDtypeStruct((n_idx, d), table.dtype),
        mesh=mesh,
        scratch_types=[pltpu.VMEM((SC_CHUNK,), i32), pltpu.VMEM((SC_CHUNK,), i32),
                       pltpu.VMEM((SC_CHUNK, d), table.dtype), pltpu.VMEM((SC_CHUNK, d), table.dtype),
                       pltpu.SemaphoreType.DMA, pltpu.SemaphoreType.DMA],
        name="gather_rows",
    )(table, idx)


def _combine_kernel(gate_ref, xb_ref, mod_ref, y_ref, o_ref):
    g2 = mod_ref[0][:, 5 * D_MODEL:6 * D_MODEL]
    gates = gate_ref[...]
    acc_hi = jnp.zeros((gates.shape[0], D_PACK), f32)
    acc_lo = jnp.zeros((gates.shape[0], D_PACK), f32)
    for k in range(TOP_K):
        y_hi, y_lo = _unpack_pairs(y_ref[k])
        acc_hi = acc_hi + gates[:, k:k + 1] * y_hi
        acc_lo = acc_lo + gates[:, k:k + 1] * y_lo
    o_ref[:, 0:D_PACK] = xb_ref[:, 0:D_PACK] + g2[:, 0:D_PACK] * acc_hi
    o_ref[:, D_PACK:D_MODEL] = xb_ref[:, D_PACK:D_MODEL] + g2[:, D_PACK:D_MODEL] * acc_lo


def _combine_call(gates, xbase, mod, ygath, seq_row_c, row0):
    tm = COMBINE_TILE
    n_rows = ygath.shape[1]
    assert row0 % tm == 0 and n_rows % tm == 0
    b0 = row0 // tm
    return pl.pallas_call(
        _combine_kernel,
        grid=(n_rows // tm,),
        in_specs=[pl.BlockSpec((tm, TOP_K), lambda i: (b0 + i, 0)),
                  pl.BlockSpec((tm, D_MODEL), lambda i: (b0 + i, 0)),
                  pl.BlockSpec((1, 1, 6 * D_MODEL), lambda i: (seq_row_c(b0 + i), 0, 0)),
                  pl.BlockSpec((TOP_K, tm, D_PACK), lambda i: (0, i, 0))],
        out_specs=pl.BlockSpec((tm, D_MODEL), lambda i: (i, 0)),
        out_shape=jax.ShapeDtypeStruct((n_rows, D_MODEL), f32),
        compiler_params=_cparams(("parallel",)),
        name="combine",
    )(gates, xbase, mod, ygath)


def _block_avg(width, group):
    r = np.arange(width)
    return jnp.asarray((r[:, None] // group == r[None, :] // group).astype(np.float32) / group, dtype=bf16)


def _kv_replicate():
    c = np.arange(ATTN_W)
    src = (c // (ATTN_W // N_KV_HEADS)) * HEAD_DIM + c % HEAD_DIM
    return jnp.asarray((np.arange(KV_W)[:, None] == src[None, :]).astype(np.float32), dtype=bf16)


def _rope_lane_tables(n_tok):
    rows = n_tok // GRID_W
    r = jnp.repeat(jnp.arange(rows, dtype=f32), GRID_W)
    col = jnp.tile(jnp.arange(GRID_W, dtype=f32), rows)
    inv = ROPE_THETA ** (-jnp.arange(ROPE_FREQ, dtype=f32) / ROPE_FREQ)
    ar = r[:, None] * inv
    ac = col[:, None] * inv
    cos_h = jnp.concatenate([jnp.cos(ar), jnp.cos(ar), jnp.cos(ac), jnp.cos(ac)], axis=-1)
    sin_h = jnp.concatenate([-jnp.sin(ar), jnp.sin(ar), -jnp.sin(ac), jnp.sin(ac)], axis=-1)
    return jnp.tile(cos_h, (1, N_HEADS)), jnp.tile(sin_h, (1, N_HEADS))


def _block_diag(w):
    nb, bw, _ = w.shape
    eye = jnp.eye(nb, dtype=w.dtype)
    return (w[:, :, None, :] * eye[:, None, :, None]).reshape(nb * bw, nb * bw)


def _gla_state_in(s):
    bsz = s.shape[0]
    eye = jnp.eye(GLA_H, dtype=s.dtype)
    st = jnp.swapaxes(s, -1, -2)
    big = st[:, :, :, :, None, :] * eye[None, None, :, None, :, None]
    return big.reshape(bsz, 2, GLA_W, GLA_W)


def kernel(x_prompt, x_sample, cache_k, cache_v, state_lru, state_gla, c, c_ctx, ada_w, ada_b, norm1_g, norm2_g, w_in, q_norm_g, k_norm_g, attn_out_g, conv_w, conv_b, lru_wa, lru_ba, lru_wi, lru_bi, lru_lambda, lru_out_g, gla_wa2, gla_ba, gla_out_g, w_out, router_w, router_b, exp_w1, exp_w3, exp_w2, sh_w1, sh_w3, sh_w2):
    bc, tc, _ = x_prompt.shape
    bl, tl, _ = x_sample.shape
    depth = w_in.shape[0]
    nc = bc * tc
    n = nc + bl * tl
    past = cache_k.shape[2]
    assert tc == ROW_TILE and tl % ROW_TILE == 0 and nc % tl == 0 and bl + 1 <= SUBLANES
    assert nc % PROJ_TILE == 0 and tl % PROJ_TILE == 0 and nc % COMBINE_TILE == 0 and tl % COMBINE_TILE == 0

    def seq_row_for(tile):
        def seq_row(i):
            return jnp.where(i < nc // tile, 0, 1 + (i - nc // tile) // (tl // tile))
        return seq_row

    seq_row = seq_row_for(PROJ_TILE)
    seq_row_c = seq_row_for(COMBINE_TILE)

    xc = x_prompt.reshape(nc, D_MODEL)
    xl = x_sample.reshape(bl * tl, D_MODEL)
    cond = jnp.zeros((SUBLANES, D_MODEL), f32).at[0].set(c_ctx).at[1:1 + bl].set(c)
    mods = _ada_call(cond, ada_w, ada_b)

    bmq = _block_avg(ATTN_W, HEAD_DIM)
    bmk = _block_avg(KV_W, HEAD_DIM)
    bmg = _block_avg(GLA_W, GLA_DK)
    rep = _kv_replicate()
    cos_t, sin_t = _rope_lane_tables(tl)
    step_rows = STEP_BLOCKS * SLOT_BLOCK
    n_slots = -(-(n * TOP_K + N_EXPERTS * (SLOT_BLOCK - 1)) // step_rows) * step_rows
    tile8 = lambda v: jnp.tile(v, N_HEADS)[None, :]

    ks, vs, lrus, glas = [], [], [], []
    for l in range(depth):
        mod = mods[l].reshape(SUBLANES, 1, 6 * D_MODEL)
        pa, pl_, pg = _proj_call(xc, xl, mod, norm1_g[l][None, :], w_in, l, seq_row)

        qg, kg, og = tile8(q_norm_g[l]), jnp.tile(k_norm_g[l], N_KV_HEADS)[None, :], attn_out_g[l][None, :]
        attn_c, k_new, v_new = _attn_ctx_call(pa, bc, tc, qg, kg, og, bmq, bmk, rep)
        attn_l = _attn_lat_call(pa, nc, bl, tl, cache_k[:, l].reshape(bl, past, KV_W),
                                cache_v[:, l].reshape(bl, past, KV_W), cos_t, sin_t, qg, kg, og, bmq, bmk, rep)
        ks.append(k_new.reshape(bc, tc, N_KV_HEADS, HEAD_DIM))
        vs.append(v_new.reshape(bc, tc, N_KV_HEADS, HEAD_DIM))

        wg = jnp.concatenate([_block_diag(lru_wa[l, 0]), _block_diag(lru_wi[l, 0]),
                              _block_diag(lru_wa[l, 1]), _block_diag(lru_wi[l, 1])], axis=1).astype(bf16)
        bg = jnp.concatenate([lru_ba[l, 0], lru_bi[l, 0], lru_ba[l, 1], lru_bi[l, 1]])[None, :]
        lru_args = (conv_w[l], conv_b[l][None, :], wg, bg, lru_lambda[l], lru_out_g[l][None, :])
        lru_c, lru_fin = _lru_call(pl_, 0, bc, tc, jnp.zeros((bc, 2, LRU_W), f32), *lru_args)
        lru_l, _ = _lru_call(pl_, nc, bl, tl, state_lru[:, l], *lru_args)
        lrus.append(lru_fin)

        wa = jnp.zeros((LANES, 2 * GLA_W), f32)
        wa = wa.at[0:GLA_RANK, 0:GLA_W].set(gla_wa2[l, 0]).at[GLA_RANK:2 * GLA_RANK, GLA_W:].set(gla_wa2[l, 1])
        gla_args = (wa.astype(bf16), gla_ba[l].reshape(1, 2 * GLA_W), gla_out_g[l].reshape(1, GLA_W), bmg)
        gla_c, gla_fin = _gla_call(pg, 0, bc, tc, None, *gla_args)
        gla_l, _ = _gla_call(pg, nc, bl, tl, _gla_state_in(state_gla[:, l]), *gla_args)
        glas.append(gla_fin)

        rw_t = router_w[l].T
        rw_hi = rw_t.astype(bf16)
        rw_lo = (rw_t - rw_hi.astype(f32)).astype(bf16)
        xbase, h2, scores_t = _out_call(xc, xl, (attn_c, lru_c, gla_c), (attn_l, lru_l, gla_l), mod, w_out[l].astype(bf16),
                                        norm2_g[l][None, :], rw_hi, rw_lo, sh_w1[l].astype(bf16),
                                        sh_w3[l].astype(bf16), sh_w2[l].astype(bf16), seq_row)

        idx, gates, rank, counts = _route_call(scores_t, router_b[l][:, None])
        cnt = counts[:, 0].astype(i32)
        padded = (cnt + SLOT_BLOCK - 1) // SLOT_BLOCK * SLOT_BLOCK
        padded_end = jnp.cumsum(padded)
        dest_km = _dest_call(idx, rank, (padded_end - padded).astype(f32)[:, None])
        nb = n_slots // SLOT_BLOCK
        blk_first = jnp.arange(nb, dtype=i32) * SLOT_BLOCK
        blk_e = jnp.minimum(jnp.sum((padded_end[None, :] <= blk_first[:, None]).astype(i32), axis=1), N_EXPERTS - 1)
        blk_valid = jnp.clip((padded_end - padded + cnt)[blk_e] - blk_first, 0, SLOT_BLOCK)
        n_used = padded_end[-1:] // SLOT_BLOCK
        owns = jnp.cumsum((cnt > 0).astype(i32))
        blk_ord = (owns - 1)[blk_e]
        expert_seq = jnp.minimum(jnp.sum((owns[None, :] <= jnp.arange(N_EXPERTS, dtype=i32)[:, None]).astype(i32),
                                         axis=1), N_EXPERTS - 1)
        slots = _scatter_rows_call(h2, dest_km, n_slots)
        slots = _expert_call(n_used, blk_valid, blk_ord, expert_seq, owns[-1:], slots, exp_w1, exp_w3, exp_w2, l)
        gates_t = gates.T
        yg_c = _gather_rows_call(slots, dest_km[:, :nc].reshape(-1)).reshape(TOP_K, nc, D_PACK)
        yg_l = _gather_rows_call(slots, dest_km[:, nc:].reshape(-1)).reshape(TOP_K, n - nc, D_PACK)
        xc = _combine_call(gates_t, xbase, mod, yg_c, seq_row_c, 0)
        xl = _combine_call(gates_t, xbase, mod, yg_l, seq_row_c, nc)

    y_prompt = xc.reshape(bc, tc, D_MODEL)
    y_sample = xl.reshape(bl, tl, D_MODEL)
    return (y_prompt, y_sample, jnp.stack(ks, axis=1), jnp.stack(vs, axis=1),
            jnp.stack(lrus, axis=1), jnp.stack(glas, axis=1))
```

```python
import functools

import jax
import jax.numpy as jnp
import numpy as np
from jax import lax
from jax.experimental import pallas as pl
from jax.experimental.pallas import tpu as pltpu
from jax.experimental.pallas import tpu_sc as plsc

f32 = jnp.float32
bf16 = jnp.bfloat16
i32 = jnp.int32

D_MODEL = 1024
N_HEADS = 8
N_KV_HEADS = 2
HEAD_DIM = 64
ATTN_W = N_HEADS * HEAD_DIM
KV_W = N_KV_HEADS * HEAD_DIM
GRID_W = 64
ROPE_FREQ = HEAD_DIM // 4
ROPE_THETA = 10000.0
LRU_W = 256
LRU_BLOCKS = 4
LRU_C = 8.0
CONV_W = 4
CONV_LEFT = 2
GLA_H = 4
GLA_DK = 64
GLA_W = 256
GLA_RANK = 16
GLA_TAU = 16.0
N_EXPERTS = 256
TOP_K = 8
EXPERT_FF = 256
ROUTED_SCALE = 2.5
EPS = 1e-6

LANES = 128
SUBLANES = 8
ROW_TILE = 256
PROJ_TILE = 512
GLA_BLOCK = 256
SLOT_BLOCK = 128
STEP_BLOCKS = 8
W_RING = 3
COMBINE_TILE = 256
SC_CHUNK = 64
D_PACK = D_MODEL // 2
VMEM_LIMIT = 56 * 1024 * 1024


def _cparams(sem, vmem=VMEM_LIMIT):
    return pltpu.CompilerParams(dimension_semantics=sem, vmem_limit_bytes=vmem)


def _dot(a, b):
    return jnp.dot(a, b, preferred_element_type=f32)


def _dot_nt(a, b):
    return lax.dot_general(a, b, (((1,), (1,)), ((), ())), preferred_element_type=f32)


def _dot_tn(a, b):
    return lax.dot_general(a, b, (((0,), (0,)), ((), ())), preferred_element_type=f32)


def _split(x):
    hi = x.astype(bf16)
    lo = (x - hi.astype(f32)).astype(bf16)
    return hi, lo


def _dot_x2(x, w):
    hi, lo = _split(x)
    return _dot(hi, w) + _dot(lo, w)


def _dot_2x(m, x):
    hi, lo = _split(x)
    return _dot(m, hi) + _dot(m, lo)


def _dot3(a, b_hi, b_lo):
    a_hi, a_lo = _split(a)
    return _dot(a_hi, b_hi) + _dot(a_lo, b_hi) + _dot(a_hi, b_lo)


def _sigmoid(x):
    return 1.0 / (1.0 + jnp.exp(-x))


def _silu(x):
    return x * _sigmoid(x)


def _softplus(x):
    return jnp.maximum(x, 0.0) + jnp.log(1.0 + jnp.exp(-jnp.abs(x)))


def _gelu_tanh(x):
    return 0.5 * x * (1.0 + jnp.tanh(0.7978845608028654 * (x + 0.044715 * x * x * x)))


def _rms(x, g):
    return x * lax.rsqrt(jnp.mean(x * x, axis=-1, keepdims=True) + EPS) * g


def _pack_pairs(x):
    c = x.shape[1] // 2
    hi = lax.bitcast_convert_type(x[:, :c].astype(bf16).astype(f32), i32)
    lo = lax.bitcast_convert_type(x[:, c:].astype(bf16).astype(f32), i32)
    return hi | lax.shift_right_logical(lo, 16)


def _unpack_pairs(w):
    hi = lax.bitcast_convert_type(w & jnp.int32(-65536), f32)
    lo = lax.bitcast_convert_type(w << 16, f32)
    return hi, lo


def _head_rms(x, g, bm):
    ms = _dot_x2(x * x, bm)
    return x * lax.rsqrt(ms + EPS) * g


def _ada_kernel(c_ref, w_ref, b_ref, o_ref):
    s = _silu(c_ref[...])
    w = w_ref[0]
    w_hi, w_lo = _split(w)
    o_ref[0] = _dot3(s, w_hi, w_lo) + b_ref[0]


def _ada_call(cond, ada_w, ada_b):
    depth = ada_w.shape[0]
    nt = 1536
    return pl.pallas_call(
        _ada_kernel,
        grid=(depth, 6 * D_MODEL // nt),
        in_specs=[pl.BlockSpec((SUBLANES, D_MODEL), lambda l, j: (0, 0)),
                  pl.BlockSpec((1, D_MODEL, nt), lambda l, j: (l, 0, j)),
                  pl.BlockSpec((1, 1, nt), lambda l, j: (l, 0, j))],
        out_specs=pl.BlockSpec((1, SUBLANES, nt), lambda l, j: (l, 0, j)),
        out_shape=jax.ShapeDtypeStruct((depth, SUBLANES, 6 * D_MODEL), f32),
        compiler_params=_cparams(("parallel", "parallel")),
        name="ada_mod",
    )(cond, ada_w, ada_b.reshape(depth, 1, 6 * D_MODEL))


PROJ_A = ATTN_W + 2 * KV_W
PROJ_L = 2 * LRU_W
PROJ_G = 4 * GLA_W + LANES
IN_COLS = PROJ_A + PROJ_L + 4 * GLA_W + 2 * GLA_RANK


def _proj_kernel(xc_ref, xl_ref, mod_ref, g_ref, w_ref, oa_ref, ol_ref, og_ref, w_s, *, n_ctx_tiles):
    @pl.when(pl.program_id(0) == 0)
    def _():
        w_s[:, 0:IN_COLS] = w_ref[0].astype(bf16)
        w_s[:, IN_COLS:] = jnp.zeros((D_MODEL, w_s.shape[1] - IN_COLS), bf16)

    mod = mod_ref[0]
    sh = mod[:, 0:D_MODEL]
    sc = mod[:, D_MODEL:2 * D_MODEL]
    x = jnp.where(pl.program_id(0) < n_ctx_tiles, xc_ref[...], xl_ref[...])
    h = (_rms(x, g_ref[...]) * (1.0 + sc) + sh).astype(bf16)
    p = _dot(h, w_s[...])
    oa_ref[...] = p[:, 0:PROJ_A].astype(bf16)
    ol_ref[...] = p[:, PROJ_A:PROJ_A + PROJ_L].astype(bf16)
    og_ref[...] = p[:, PROJ_A + PROJ_L:PROJ_A + PROJ_L + PROJ_G].astype(bf16)


def _row_pair_specs(nct, width):
    return (pl.BlockSpec((PROJ_TILE, width), lambda i: (jnp.minimum(i, nct - 1), 0)),
            pl.BlockSpec((PROJ_TILE, width), lambda i: (jnp.maximum(i - nct, 0), 0)))


def _proj_call(xc, xl, mod, g, w_all, layer, seq_row):
    n = xc.shape[0] + xl.shape[0]
    nct = xc.shape[0] // PROJ_TILE
    cols = PROJ_A + PROJ_L + PROJ_G
    return pl.pallas_call(
        functools.partial(_proj_kernel, n_ctx_tiles=nct),
        grid=(n // PROJ_TILE,),
        in_specs=[*_row_pair_specs(nct, D_MODEL),
                  pl.BlockSpec((1, 1, 6 * D_MODEL), lambda i: (seq_row(i), 0, 0)),
                  pl.BlockSpec((1, D_MODEL), lambda i: (0, 0)),
                  pl.BlockSpec((1, D_MODEL, IN_COLS), lambda i: (layer, 0, 0))],
        out_specs=[pl.BlockSpec((PROJ_TILE, PROJ_A), lambda i: (i, 0)),
                   pl.BlockSpec((PROJ_TILE, PROJ_L), lambda i: (i, 0)),
                   pl.BlockSpec((PROJ_TILE, PROJ_G), lambda i: (i, 0))],
        out_shape=[jax.ShapeDtypeStruct((n, PROJ_A), bf16),
                   jax.ShapeDtypeStruct((n, PROJ_L), bf16),
                   jax.ShapeDtypeStruct((n, PROJ_G), bf16)],
        scratch_shapes=[pltpu.VMEM((D_MODEL, cols), bf16)],
        compiler_params=_cparams(("arbitrary",)),
        name="in_proj",
    )(xc, xl, mod, g, w_all)


def _rope(x, cos_t, sin_t):
    w = x.shape[1]
    up = pltpu.roll(x, w - ROPE_FREQ, 1)
    dn = pltpu.roll(x, ROPE_FREQ, 1)
    lane = lax.broadcasted_iota(i32, x.shape, 1)
    partner = jnp.where((lane & (2 * ROPE_FREQ - 1)) < ROPE_FREQ, up, dn)
    return x * cos_t + partner * sin_t


def _attend(q, kk_ref, vv_ref, o_ref):
    tq = q.shape[0]
    gw = ATTN_W // N_KV_HEADS
    lane = lax.broadcasted_iota(i32, (tq, gw), 1)
    for g in range(N_KV_HEADS):
        qg = q[:, g * gw:(g + 1) * gw]
        kg = kk_ref[:, g * gw:(g + 1) * gw]
        vg = vv_ref[:, g * gw:(g + 1) * gw]
        acc = jnp.zeros((tq, gw), f32)
        for hh in range(N_HEADS // N_KV_HEADS):
            hm = (lane >> 6) == hh
            s = _dot_nt(jnp.where(hm, qg, 0.0).astype(bf16), kg)
            m = jnp.max(s, axis=-1, keepdims=True)
            p = jnp.exp(s - m)
            l = jnp.sum(p, axis=-1, keepdims=True)
            o = _dot(p.astype(bf16), vg) / l
            acc = jnp.where(hm, o, acc)
        o_ref[:, g * gw:(g + 1) * gw] = acc


def _attn_ctx_kernel(p_ref, qg_ref, kg_ref, og_ref, bmq_ref, bmk_ref, rep_ref,
                     o_ref, ko_ref, vo_ref, kk_s, vv_s, o_s):
    p = p_ref[...].astype(f32)
    q = _head_rms(p[:, 0:ATTN_W], qg_ref[...], bmq_ref[...])
    k = _head_rms(p[:, ATTN_W:ATTN_W + KV_W], kg_ref[...], bmk_ref[...])
    v = p[:, ATTN_W + KV_W:PROJ_A]
    ko_ref[...] = k
    vo_ref[...] = v
    kk_s[...] = _dot(k.astype(bf16), rep_ref[...]).astype(bf16)
    vv_s[...] = _dot(v.astype(bf16), rep_ref[...]).astype(bf16)
    _attend(q * HEAD_DIM ** -0.5, kk_s, vv_s, o_s)
    o_ref[...] = _rms(o_s[...], og_ref[...]).astype(bf16)


def _attn_ctx_call(pa, n_seq, t, qg, kg, og, bmq, bmk, rep):
    assert t == ROW_TILE
    const = lambda i: (0, 0)
    return pl.pallas_call(
        _attn_ctx_kernel,
        grid=(n_seq,),
        in_specs=[pl.BlockSpec((t, PROJ_A), lambda i: (i, 0)),
                  pl.BlockSpec((1, ATTN_W), const), pl.BlockSpec((1, KV_W), const), pl.BlockSpec((1, ATTN_W), const),
                  pl.BlockSpec((ATTN_W, ATTN_W), const), pl.BlockSpec((KV_W, KV_W), const),
                  pl.BlockSpec((KV_W, ATTN_W), const)],
        out_specs=[pl.BlockSpec((t, ATTN_W), lambda i: (i, 0)),
                   pl.BlockSpec((t, KV_W), lambda i: (i, 0)),
                   pl.BlockSpec((t, KV_W), lambda i: (i, 0))],
        out_shape=[jax.ShapeDtypeStruct((n_seq * t, ATTN_W), bf16),
                   jax.ShapeDtypeStruct((n_seq * t, KV_W), f32),
                   jax.ShapeDtypeStruct((n_seq * t, KV_W), f32)],
        scratch_shapes=[pltpu.VMEM((t, ATTN_W), bf16), pltpu.VMEM((t, ATTN_W), bf16), pltpu.VMEM((t, ATTN_W), f32)],
        compiler_params=_cparams(("parallel",)),
        name="attn_ctx",
    )(pa, qg, kg, og, bmq, bmk, rep)


def _attn_lat_kernel(q_ref, kv_ref, ck_ref, cv_ref, cq_ref, sq_ref, ckk_ref, skk_ref,
                     qg_ref, kg_ref, og_ref, bmq_ref, bmk_ref, rep_ref,
                     o_ref, kk_s, vv_s, o_s, *, past):
    @pl.when(pl.program_id(1) == 0)
    def _():
        kv = kv_ref[...].astype(f32)
        k = _head_rms(kv[:, 0:KV_W], kg_ref[...], bmk_ref[...])
        k = _rope(k, ckk_ref[...], skk_ref[...])
        v = kv[:, KV_W:2 * KV_W]
        kk_s[0:past, :] = _dot(ck_ref[0].astype(bf16), rep_ref[...]).astype(bf16)
        vv_s[0:past, :] = _dot(cv_ref[0].astype(bf16), rep_ref[...]).astype(bf16)
        kk_s[past:, :] = _dot(k.astype(bf16), rep_ref[...]).astype(bf16)
        vv_s[past:, :] = _dot(v.astype(bf16), rep_ref[...]).astype(bf16)

    q = _head_rms(q_ref[...].astype(f32), qg_ref[...], bmq_ref[...])
    q = _rope(q, cq_ref[...], sq_ref[...])
    _attend(q * HEAD_DIM ** -0.5, kk_s, vv_s, o_s)
    o_ref[...] = _rms(o_s[...], og_ref[...]).astype(bf16)


def _attn_lat_call(pa, row0, n_seq, t, cache_k, cache_v, cq, sq, qg, kg, og, bmq, bmk, rep):
    past = cache_k.shape[1]
    nq = t // ROW_TILE
    assert row0 % t == 0 and t % ROW_TILE == 0
    qb0 = row0 // ROW_TILE
    sb0 = row0 // t
    const = lambda b, j: (0, 0)
    return pl.pallas_call(
        functools.partial(_attn_lat_kernel, past=past),
        grid=(n_seq, nq),
        in_specs=[pl.BlockSpec((ROW_TILE, ATTN_W), lambda b, j: (qb0 + b * nq + j, 0)),
                  pl.BlockSpec((t, 2 * KV_W), lambda b, j: (sb0 + b, ATTN_W // (2 * KV_W))),
                  pl.BlockSpec((1, past, KV_W), lambda b, j: (b, 0, 0)),
                  pl.BlockSpec((1, past, KV_W), lambda b, j: (b, 0, 0)),
                  pl.BlockSpec((ROW_TILE, ATTN_W), lambda b, j: (j, 0)),
                  pl.BlockSpec((ROW_TILE, ATTN_W), lambda b, j: (j, 0)),
                  pl.BlockSpec((t, KV_W), const), pl.BlockSpec((t, KV_W), const),
                  pl.BlockSpec((1, ATTN_W), const), pl.BlockSpec((1, KV_W), const), pl.BlockSpec((1, ATTN_W), const),
                  pl.BlockSpec((ATTN_W, ATTN_W), const), pl.BlockSpec((KV_W, KV_W), const),
                  pl.BlockSpec((KV_W, ATTN_W), const)],
        out_specs=pl.BlockSpec((ROW_TILE, ATTN_W), lambda b, j: (b * nq + j, 0)),
        out_shape=jax.ShapeDtypeStruct((n_seq * t, ATTN_W), bf16),
        scratch_shapes=[pltpu.VMEM((past + t, ATTN_W), bf16), pltpu.VMEM((past + t, ATTN_W), bf16),
                        pltpu.VMEM((ROW_TILE, ATTN_W), f32)],
        compiler_params=_cparams(("parallel", "arbitrary")),
        name="attn_lat",
    )(pa, pa, cache_k, cache_v, cq, sq, cq, sq, qg, kg, og, bmq, bmk, rep)


def _lru_kernel(x_ref, h0_ref, cw_ref, cb_ref, wg_ref, bg_ref, lam_ref, g_ref,
                o_ref, fin_ref, xs, a_s, u_s, h_s, *, t):
    ng = t // SUBLANES
    lx = x_ref[:, 0:LRU_W].astype(f32)
    ly = x_ref[:, LRU_W:2 * LRU_W].astype(f32)
    xs[0:SUBLANES, :] = jnp.zeros((SUBLANES, LRU_W), f32)
    xs[SUBLANES + t:2 * SUBLANES + t, :] = jnp.zeros((SUBLANES, LRU_W), f32)
    xs[SUBLANES:SUBLANES + t, :] = lx
    xc = jnp.broadcast_to(cb_ref[...], (t, LRU_W))
    for j in range(CONV_W):
        off = SUBLANES + j - CONV_LEFT
        xc = xc + xs[off:off + t, :] * cw_ref[j:j + 1, :]
    gates = _sigmoid(_dot(xc.astype(bf16), wg_ref[...]) + bg_ref[...])
    sp = _softplus(-lam_ref[...])
    sub = lax.broadcasted_iota(i32, (ng, SUBLANES, LRU_W), 1)
    for d in range(2):
        r = gates[:, (2 * d) * LRU_W:(2 * d + 1) * LRU_W]
        gi = gates[:, (2 * d + 1) * LRU_W:(2 * d + 2) * LRU_W]
        log_a = -LRU_C * r * sp[d:d + 1, :]
        a = jnp.exp(log_a)
        th = jnp.tanh(log_a)
        u = jnp.sqrt(-2.0 * th / (1.0 - th)) * (gi * xc)
        a = a.reshape(ng, SUBLANES, LRU_W)
        u = u.reshape(ng, SUBLANES, LRU_W)
        for s in (1, 2, 4):
            if d == 0:
                a_sh = pltpu.roll(a, s, 1)
                u_sh = pltpu.roll(u, s, 1)
                ok = sub >= s
            else:
                a_sh = pltpu.roll(a, SUBLANES - s, 1)
                u_sh = pltpu.roll(u, SUBLANES - s, 1)
                ok = sub < SUBLANES - s
            u = jnp.where(ok, a * u_sh + u, u)
            a = jnp.where(ok, a * a_sh, a)
        a_s[...] = a.reshape(t, LRU_W)
        u_s[...] = u.reshape(t, LRU_W)
        h0 = jnp.broadcast_to(h0_ref[0, d:d + 1, :], (SUBLANES, LRU_W))
        edge = SUBLANES - 1 if d == 0 else 0

        def body(i, carry, d=d, edge=edge):
            g = i if d == 0 else ng - 1 - i
            rows = pl.ds(pl.multiple_of(g * SUBLANES, SUBLANES), SUBLANES)
            h = a_s[rows, :] * carry + u_s[rows, :]
            if d == 0:
                h_s[rows, :] = h
            else:
                h_s[rows, :] = h_s[rows, :] + h
            return jnp.broadcast_to(h[edge:edge + 1, :], (SUBLANES, LRU_W))

        last = lax.fori_loop(0, ng, body, h0)
        fin_ref[0, d:d + 1, :] = last[0:1, :]
    o_ref[...] = _rms(h_s[...] * _gelu_tanh(ly), g_ref[...]).astype(bf16)


def _lru_call(pl_, row0, n_seq, t, h0, cw, cb, wg, bg, lam, g):
    sb0 = row0 // t
    assert row0 % t == 0
    const = lambda b: (0, 0)
    in_specs = [pl.BlockSpec((t, PROJ_L), lambda b: (sb0 + b, 0)),
                pl.BlockSpec((1, 2, LRU_W), lambda b: (b, 0, 0)),
                pl.BlockSpec((CONV_W, LRU_W), const), pl.BlockSpec((1, LRU_W), const),
                pl.BlockSpec((LRU_W, 4 * LRU_W), const), pl.BlockSpec((1, 4 * LRU_W), const),
                pl.BlockSpec((2, LRU_W), const), pl.BlockSpec((1, LRU_W), const)]
    args = [pl_, h0, cw, cb, wg, bg, lam, g]
    return pl.pallas_call(
        functools.partial(_lru_kernel, t=t),
        grid=(n_seq,),
        in_specs=in_specs,
        out_specs=[pl.BlockSpec((t, LRU_W), lambda b: (b, 0)),
                   pl.BlockSpec((1, 2, LRU_W), lambda b: (b, 0, 0))],
        out_shape=[jax.ShapeDtypeStruct((n_seq * t, LRU_W), bf16), jax.ShapeDtypeStruct((n_seq, 2, LRU_W), f32)],
        scratch_shapes=[pltpu.VMEM((t + 2 * SUBLANES, LRU_W), f32), pltpu.VMEM((t, LRU_W), f32),
                        pltpu.VMEM((t, LRU_W), f32), pltpu.VMEM((t, LRU_W), f32)],
        compiler_params=_cparams(("parallel",)),
        name="rglru_t%d" % t,
    )(*args)


def _bcast_rows(b, period, off):
    w = b.shape[1]
    return jnp.concatenate(
        [jnp.broadcast_to(b[i * period + off:i * period + off + 1, :], (period, w)) for i in range(b.shape[0] // period)],
        axis=0)


def _gla_block(q, k, v, la, reverse, st_ref):
    n = GLA_BLOCK
    row = lax.broadcasted_iota(i32, (n, n), 0)
    col = lax.broadcasted_iota(i32, (n, n), 1)
    same64 = (row >> 6) == (col >> 6)
    same32 = (row >> 5) == (col >> 5)
    same16 = (row >> 4) == (col >> 4)
    if not reverse:
        cum = (same64 & (col <= row)).astype(bf16)
        m1 = same64 & ((row & 63) >= 32) & ((col & 63) < 32)
        m2 = same32 & ((row & 31) >= 16) & ((col & 31) < 16)
        m3 = same16 & (col <= row)
        offs = (31, 15, 7, 63)
    else:
        cum = (same64 & (col >= row)).astype(bf16)
        m1 = same64 & ((row & 63) < 32) & ((col & 63) >= 32)
        m2 = same32 & ((row & 31) < 16) & ((col & 31) >= 16)
        m3 = same16 & (col >= row)
        offs = (32, 16, 8, 0)
    b = _dot_2x(cum, la)
    r1 = _bcast_rows(b, 64, offs[0])
    r2 = _bcast_rows(b, 32, offs[1])
    r3 = _bcast_rows(b, 16, offs[2])
    bl = _bcast_rows(b, 64, offs[3])
    q1 = q * jnp.exp(jnp.minimum(b - r1, 0.0))
    k1 = (k * jnp.exp(jnp.minimum(r1 - b, 0.0))).astype(bf16)
    q2 = q * jnp.exp(jnp.minimum(b - r2, 0.0))
    k2 = (k * jnp.exp(jnp.minimum(r2 - b, 0.0))).astype(bf16)
    q3 = q * jnp.exp(b - r3)
    k3 = (k * jnp.exp(r3 - b)).astype(bf16)
    qe = (q * jnp.exp(b)).astype(bf16)
    kl = (k * jnp.exp(bl - b)).astype(bf16)
    dec = jnp.exp(bl)
    vb = v.astype(bf16)
    zq = jnp.zeros((n, n), f32)
    intra = jnp.zeros((n, n), f32)
    for h in range(GLA_H):
        hm = (col >> 6) == h
        a1 = _dot_nt(jnp.where(hm, q1, zq).astype(bf16), k1)
        a2 = _dot_nt(jnp.where(hm, q2, zq).astype(bf16), k2)
        a3 = _dot_nt(jnp.where(hm, q3, zq).astype(bf16), k3)
        att = jnp.where(m1, a1, jnp.where(m2, a2, jnp.where(m3, a3, zq)))
        intra = intra + _dot(att.astype(bf16), jnp.where(hm, v, zq).astype(bf16))
    outs = [None] * 4
    for c in (range(4) if not reverse else range(3, -1, -1)):
        rs = slice(64 * c, 64 * c + 64)
        st = st_ref[...]
        inter = _dot_nt(qe[rs], st.astype(bf16))
        kv = _dot_tn(vb[rs], kl[rs])
        drow = dec[64 * c:64 * c + 1, :]
        st_ref[...] = st * drow + jnp.where(same64, kv, zq)
        outs[c] = intra[rs] + inter
    return jnp.concatenate(outs, axis=0)


def _gla_kernel(x_ref, s0_ref, wa_ref, ba_ref, g_ref, bm_ref, o_ref, fin_ref, st_s, o_s, *, t):
    nblk = t // GLA_BLOCK

    def block(j, reverse):
        rows = pl.ds(j * GLA_BLOCK if isinstance(j, int) else pl.multiple_of(j * GLA_BLOCK, GLA_BLOCK), GLA_BLOCK)
        q = x_ref[rows, 0:GLA_W].astype(f32) * GLA_DK ** -0.5
        k = x_ref[rows, GLA_W:2 * GLA_W].astype(f32)
        v = x_ref[rows, 2 * GLA_W:3 * GLA_W].astype(f32)
        ga = x_ref[rows, 4 * GLA_W:4 * GLA_W + LANES]
        d = 1 if reverse else 0
        z = _dot(ga.astype(bf16), wa_ref[:, d * GLA_W:(d + 1) * GLA_W]) + ba_ref[:, d * GLA_W:(d + 1) * GLA_W]
        la = -_softplus(-z) * (1.0 / GLA_TAU)
        o = _gla_block(q, k, v, la, reverse, st_s)
        if reverse:
            o_s[rows, :] = o_s[rows, :] + o
        else:
            o_s[rows, :] = o

    for d in range(2):
        if s0_ref is None:
            st_s[...] = jnp.zeros((GLA_W, GLA_W), f32)
        else:
            st_s[...] = s0_ref[0, d]
        if nblk == 1:
            block(0, d == 1)
        else:
            def body(i, carry, d=d):
                block(i if d == 0 else nblk - 1 - i, d == 1)
                return carry
            lax.fori_loop(0, nblk, body, 0)
        st_t = st_s[...].T
        for h in range(GLA_H):
            fin_ref[0, d, h] = st_t[h * GLA_DK:(h + 1) * GLA_DK, h * GLA_DK:(h + 1) * GLA_DK]
    gg = x_ref[:, 3 * GLA_W:4 * GLA_W].astype(f32)
    o_ref[...] = (_head_rms(o_s[...], g_ref[...], bm_ref[...]) * _silu(gg)).astype(bf16)


def _gla_call(pg, row0, n_seq, t, s0, wa, ba, g, bm):
    sb0 = row0 // t
    assert row0 % t == 0 and t % GLA_BLOCK == 0
    const = lambda b: (0, 0)
    in_specs = [pl.BlockSpec((t, PROJ_G), lambda b: (sb0 + b, 0))]
    args = [pg]
    if s0 is not None:
        in_specs.append(pl.BlockSpec((1, 2, GLA_W, GLA_W), lambda b: (b, 0, 0, 0)))
        args.append(s0)
    in_specs += [pl.BlockSpec((LANES, 2 * GLA_W), const), pl.BlockSpec((1, 2 * GLA_W), const),
                 pl.BlockSpec((1, GLA_W), const), pl.BlockSpec((GLA_W, GLA_W), const)]
    args += [wa, ba, g, bm]

    def kern(*refs):
        refs = list(refs)
        x_ref = refs.pop(0)
        s0_ref = refs.pop(0) if s0 is not None else None
        wa_ref, ba_ref, g_ref, bm_ref = refs[:4]
        _gla_kernel(x_ref, s0_ref, wa_ref, ba_ref, g_ref, bm_ref, *refs[4:], t=t)

    return pl.pallas_call(
        kern,
        grid=(n_seq,),
        in_specs=in_specs,
        out_specs=[pl.BlockSpec((t, GLA_W), lambda b: (b, 0)),
                   pl.BlockSpec((1, 2, GLA_H, GLA_DK, GLA_DK), lambda b: (b, 0, 0, 0, 0))],
        out_shape=[jax.ShapeDtypeStruct((n_seq * t, GLA_W), bf16),
                   jax.ShapeDtypeStruct((n_seq, 2, GLA_H, GLA_DK, GLA_DK), f32)],
        scratch_shapes=[pltpu.VMEM((GLA_W, GLA_W), f32), pltpu.VMEM((t, GLA_W), f32)],
        compiler_params=_cparams(("parallel",)),
        name="gla_t%d" % t,
    )(*args)


def _out_kernel(xc_ref, xl_ref, ac_ref, lc_ref, gc_ref, al_ref, ll_ref, gl_ref, mod_ref, wo_ref, g2_ref,
                rwh_ref, rwl_ref, s1_ref, s3_ref, s2_ref, xb_ref, h2_ref, sc_ref, *, n_ctx_tiles):
    is_ctx = pl.program_id(0) < n_ctx_tiles
    pick = lambda c_ref, l_ref: jnp.where(is_ctx, c_ref[...], l_ref[...]).astype(bf16)
    mod = mod_ref[0]
    g1 = mod[:, 2 * D_MODEL:3 * D_MODEL]
    sh2 = mod[:, 3 * D_MODEL:4 * D_MODEL]
    sc2 = mod[:, 4 * D_MODEL:5 * D_MODEL]
    g2 = mod[:, 5 * D_MODEL:6 * D_MODEL]
    m = (_dot(pick(ac_ref, al_ref), wo_ref[0:ATTN_W, :])
         + _dot(pick(lc_ref, ll_ref), wo_ref[ATTN_W:ATTN_W + LRU_W, :])
         + _dot(pick(gc_ref, gl_ref), wo_ref[ATTN_W + LRU_W:ATTN_W + LRU_W + GLA_W, :]))
    x1 = jnp.where(is_ctx, xc_ref[...], xl_ref[...]) + g1 * m
    h2 = _rms(x1, g2_ref[...]) * (1.0 + sc2) + sh2
    h2_ref[...] = _pack_pairs(h2)
    h_hi, h_lo = _split(h2)
    logits_t = _dot_nt(rwh_ref[...], h_hi) + _dot_nt(rwh_ref[...], h_lo) + _dot_nt(rwl_ref[...], h_hi)
    sc_ref[...] = _sigmoid(logits_t)
    hb = h2.astype(bf16)
    act = (_silu(_dot(hb, s1_ref[...])) * _dot(hb, s3_ref[...])).astype(bf16)
    xb_ref[...] = x1 + g2 * _dot(act, s2_ref[...])


def _out_call(xc, xl, mix_ctx, mix_lat, mod, wo, g2, rwh, rwl, s1, s3, s2, seq_row):
    n = xc.shape[0] + xl.shape[0]
    nct = xc.shape[0] // PROJ_TILE
    const = lambda i: (0, 0)
    rowb = lambda w: pl.BlockSpec((PROJ_TILE, w), lambda i: (i, 0))
    ctxb = lambda w: _row_pair_specs(nct, w)[0]
    latb = lambda w: _row_pair_specs(nct, w)[1]
    return pl.pallas_call(
        functools.partial(_out_kernel, n_ctx_tiles=nct),
        grid=(n // PROJ_TILE,),
        in_specs=[ctxb(D_MODEL), latb(D_MODEL),
                  ctxb(ATTN_W), ctxb(LRU_W), ctxb(GLA_W), latb(ATTN_W), latb(LRU_W), latb(GLA_W),
                  pl.BlockSpec((1, 1, 6 * D_MODEL), lambda i: (seq_row(i), 0, 0)),
                  pl.BlockSpec((D_MODEL, D_MODEL), const), pl.BlockSpec((1, D_MODEL), const),
                  pl.BlockSpec((N_EXPERTS, D_MODEL), const), pl.BlockSpec((N_EXPERTS, D_MODEL), const),
                  pl.BlockSpec((D_MODEL, EXPERT_FF), const), pl.BlockSpec((D_MODEL, EXPERT_FF), const),
                  pl.BlockSpec((EXPERT_FF, D_MODEL), const)],
        out_specs=[rowb(D_MODEL),
                   rowb(D_PACK),
                   pl.BlockSpec((N_EXPERTS, PROJ_TILE), lambda i: (0, i))],
        out_shape=[jax.ShapeDtypeStruct((n, D_MODEL), f32),
                   jax.ShapeDtypeStruct((n, D_PACK), i32),
                   jax.ShapeDtypeStruct((N_EXPERTS, n), f32)],
        compiler_params=_cparams(("parallel",)),
        name="out_proj",
    )(xc, xl, *mix_ctx, *mix_lat, mod, wo, g2, rwh, rwl, s1, s3, s2)


def _set_row(acc, k, row):
    sub = lax.broadcasted_iota(i32, acc.shape, 0)
    return jnp.where(sub == k, jnp.broadcast_to(row, acc.shape), acc)


def _route_kernel(sc_ref, rb_ref, idx_ref, gate_ref, rank_ref, cnt_ref, run_s):
    tm = sc_ref.shape[1]

    @pl.when(pl.program_id(0) == 0)
    def _():
        run_s[...] = jnp.zeros_like(run_s)

    scores = sc_ref[...]
    sel = scores + rb_ref[...]
    erow = lax.broadcasted_iota(i32, (N_EXPERTS, tm), 0).astype(f32)
    neg = jnp.full((N_EXPERTS, tm), -jnp.inf, f32)
    hots = []
    idx_o = jnp.zeros((TOP_K, tm), f32)
    gate_o = jnp.zeros((TOP_K, tm), f32)
    gsum = jnp.zeros((1, tm), f32)
    chosen = jnp.zeros((N_EXPERTS, tm), f32)
    for k in range(TOP_K):
        m = jnp.max(sel, axis=0, keepdims=True)
        idx_f = jnp.min(jnp.where(sel == m, erow, float(N_EXPERTS)), axis=0, keepdims=True)
        hot = erow == idx_f
        gk = jnp.sum(jnp.where(hot, scores, 0.0), axis=0, keepdims=True)
        sel = jnp.where(hot, neg, sel)
        hots.append(hot)
        chosen = jnp.where(hot, 1.0, chosen)
        gsum = gsum + gk
        idx_o = _set_row(idx_o, k, idx_f)
        gate_o = _set_row(gate_o, k, gk)
    gate_ref[...] = gate_o / gsum * ROUTED_SCALE
    idx_ref[...] = idx_o.astype(i32)
    r = lax.broadcasted_iota(i32, (tm, tm), 0)
    c = lax.broadcasted_iota(i32, (tm, tm), 1)
    pos = _dot(chosen.astype(bf16), (r < c).astype(bf16)) + run_s[...]
    rank_o = jnp.zeros((TOP_K, tm), f32)
    for k in range(TOP_K):
        rank_o = _set_row(rank_o, k, jnp.sum(jnp.where(hots[k], pos, 0.0), axis=0, keepdims=True))
    rank_ref[...] = rank_o.astype(i32)
    run_s[...] = run_s[...] + jnp.sum(chosen, axis=1, keepdims=True)
    cnt_ref[...] = run_s[...]


def _route_call(scores_t, rb):
    n = scores_t.shape[1]
    colb = lambda r: pl.BlockSpec((r, ROW_TILE), lambda i: (0, i))
    cnt_spec = pl.BlockSpec((N_EXPERTS, 1), lambda i: (0, 0))
    return pl.pallas_call(
        _route_kernel,
        grid=(n // ROW_TILE,),
        in_specs=[colb(N_EXPERTS), cnt_spec],
        out_specs=[colb(TOP_K), colb(TOP_K), colb(TOP_K), cnt_spec],
        out_shape=[jax.ShapeDtypeStruct((TOP_K, n), i32), jax.ShapeDtypeStruct((TOP_K, n), f32),
                   jax.ShapeDtypeStruct((TOP_K, n), i32), jax.ShapeDtypeStruct((N_EXPERTS, 1), f32)],
        scratch_shapes=[pltpu.VMEM((N_EXPERTS, 1), f32)],
        compiler_params=_cparams(("arbitrary",)),
        name="route",
    )(scores_t, rb)


def _dest_kernel(idx_ref, rank_ref, start_ref, dest_ref):
    tm = idx_ref.shape[1]
    erow = lax.broadcasted_iota(i32, (N_EXPERTS, tm), 0)
    start = jnp.broadcast_to(start_ref[...], (N_EXPERTS, tm))
    idx = idx_ref[...]
    dest = jnp.zeros((TOP_K, tm), f32)
    for k in range(TOP_K):
        hot = erow == idx[k:k + 1, :]
        dest = _set_row(dest, k, jnp.sum(jnp.where(hot, start, 0.0), axis=0, keepdims=True))
    dest_ref[...] = dest.astype(i32) + rank_ref[...]


def _dest_call(idx, rank, start):
    n = idx.shape[1]
    colb = lambda r: pl.BlockSpec((r, ROW_TILE), lambda i: (0, i))
    return pl.pallas_call(
        _dest_kernel,
        grid=(n // ROW_TILE,),
        in_specs=[colb(TOP_K), colb(TOP_K), pl.BlockSpec((N_EXPERTS, 1), lambda i: (0, 0))],
        out_specs=colb(TOP_K),
        out_shape=jax.ShapeDtypeStruct((TOP_K, n), i32),
        compiler_params=_cparams(("parallel",)),
        name="dest",
    )(idx, rank, start)


def _sc_workers():
    info = plsc.get_sparse_core_info()
    mesh = plsc.VectorSubcoreMesh(core_axis_name="c", subcore_axis_name="s")
    worker_id = lambda: lax.axis_index("s") * info.num_cores + lax.axis_index("c")
    return mesh, info.num_cores * info.num_subcores, worker_id


def _scatter_rows_call(rows, dest_km, n_slots):
    n, d = rows.shape
    mesh, n_workers, worker_id = _sc_workers()
    per_worker = n // n_workers
    assert n % (n_workers * SC_CHUNK) == 0

    def body(rows_hbm, idx_hbm, out_hbm, *scratch):
        idx_vs, rows_v, sem = scratch[:TOP_K], scratch[TOP_K], scratch[TOP_K + 1]
        base = worker_id() * per_worker

        @pl.loop(0, per_worker // SC_CHUNK)
        def _(i):
            t0 = pl.multiple_of(base + i * SC_CHUNK, SC_CHUNK)
            pltpu.sync_copy(rows_hbm.at[pl.ds(t0, SC_CHUNK)], rows_v)
            for k in range(TOP_K):
                pltpu.sync_copy(idx_hbm.at[pl.ds(k * n + t0, SC_CHUNK)], idx_vs[k])
            copies = [pltpu.async_copy(rows_v, out_hbm.at[idx_vs[k]], sem) for k in range(TOP_K)]
            for cp in copies:
                cp.wait()

    return pl.kernel(
        body,
        out_type=jax.ShapeDtypeStruct((n_slots, d), rows.dtype),
        mesh=mesh,
        scratch_types=[pltpu.VMEM((SC_CHUNK,), i32)] * TOP_K + [pltpu.VMEM((SC_CHUNK, d), rows.dtype),
                                                                pltpu.SemaphoreType.DMA],
        name="scatter_rows",
    )(rows, dest_km.reshape(-1))


def _expert_kernel(nu_ref, bv_ref, ord_ref, eseq_ref, nex_ref, x_ref, w1_hbm, w3_hbm, w2_hbm, y_ref,
                   w1_f, w3_f, w2_f, w1_s, w3_s, w2_s, sems, *, layer):
    def weight_copies(j):
        e = eseq_ref[j]
        slot = j % W_RING
        return [pltpu.make_async_copy(w1_hbm.at[layer, e], w1_f.at[slot], sems.at[slot, 0]),
                pltpu.make_async_copy(w3_hbm.at[layer, e], w3_f.at[slot], sems.at[slot, 1]),
                pltpu.make_async_copy(w2_hbm.at[layer, e], w2_f.at[slot], sems.at[slot, 2])]

    def start_if_exists(j):
        @pl.when(j < nex_ref[0])
        def _():
            for cp in weight_copies(j):
                cp.start()

    for u in range(STEP_BLOCKS):
        i = pl.program_id(0) * STEP_BLOCKS + u
        rows = slice(u * SLOT_BLOCK, (u + 1) * SLOT_BLOCK)

        @pl.when(i < nu_ref[0])
        def _(i=i, rows=rows):
            j = ord_ref[i]

            @pl.when(i == 0)
            def _():
                for ahead in range(W_RING - 1):
                    start_if_exists(ahead)

            @pl.when((i == 0) | (j != ord_ref[jnp.maximum(i - 1, 0)]))
            def _():
                start_if_exists(j + W_RING - 1)
                for cp in weight_copies(j):
                    cp.wait()
                slot = j % W_RING
                w1_s[...] = w1_f[slot].astype(bf16)
                w3_s[...] = w3_f[slot].astype(bf16)
                w2_s[...] = w2_f[slot].astype(bf16)

            row = lax.broadcasted_iota(i32, (SLOT_BLOCK, D_PACK), 0)
            x_hi, x_lo = _unpack_pairs(jnp.where(row < bv_ref[i], x_ref[rows, :], 0))
            x = jnp.concatenate([x_hi, x_lo], axis=1).astype(bf16)
            act = (_silu(_dot(x, w1_s[...])) * _dot(x, w3_s[...])).astype(bf16)
            y_ref[rows, :] = _pack_pairs(_dot(act, w2_s[...]))

        @pl.when((i >= nu_ref[0]) & (pl.program_id(0) * STEP_BLOCKS < nu_ref[0]))
        def _(rows=rows):
            y_ref[rows, :] = x_ref[rows, :]


def _expert_call(n_used, blk_valid, blk_ord, expert_seq, n_seq_experts, slots, w1, w3, w2, layer):
    step_rows = STEP_BLOCKS * SLOT_BLOCK
    assert slots.shape[0] % step_rows == 0
    blk = lambda s, nu, *_: (jnp.minimum(s, (nu[0] - 1) // STEP_BLOCKS), 0)
    hbm = pl.BlockSpec(memory_space=pl.ANY)
    grid_spec = pltpu.PrefetchScalarGridSpec(
        num_scalar_prefetch=5,
        grid=(slots.shape[0] // step_rows,),
        in_specs=[pl.BlockSpec((step_rows, D_PACK), blk), hbm, hbm, hbm],
        out_specs=pl.BlockSpec((step_rows, D_PACK), blk),
        scratch_shapes=[pltpu.VMEM((W_RING, D_MODEL, EXPERT_FF), f32), pltpu.VMEM((W_RING, D_MODEL, EXPERT_FF), f32),
                        pltpu.VMEM((W_RING, EXPERT_FF, D_MODEL), f32),
                        pltpu.VMEM((D_MODEL, EXPERT_FF), bf16), pltpu.VMEM((D_MODEL, EXPERT_FF), bf16),
                        pltpu.VMEM((EXPERT_FF, D_MODEL), bf16),
                        pltpu.SemaphoreType.DMA((W_RING, 3))],
    )
    return pl.pallas_call(
        functools.partial(_expert_kernel, layer=layer),
        grid_spec=grid_spec,
        out_shape=jax.ShapeDtypeStruct(slots.shape, slots.dtype),
        input_output_aliases={5: 0},
        compiler_params=_cparams(("arbitrary",)),
        name="experts",
    )(n_used, blk_valid, blk_ord, expert_seq, n_seq_experts, slots, w1, w3, w2)


def _gather_rows_call(table, idx):
    n_idx = idx.shape[0]
    d = table.shape[1]
    mesh, n_workers, worker_id = _sc_workers()
    per_worker = n_idx // n_workers
    assert n_idx % (n_workers * SC_CHUNK) == 0

    n_chunks = per_worker // SC_CHUNK
    assert n_chunks % 2 == 0

    def body(table_hbm, idx_hbm, out_hbm, idx_a, idx_b, rows_a, rows_b, sem_a, sem_b):
        base = worker_id() * per_worker
        bufs = ((idx_a, rows_a, sem_a), (idx_b, rows_b, sem_b))

        def start_gather(c, buf):
            idx_v, rows_v, sem = buf
            off = pl.multiple_of(base + c * SC_CHUNK, SC_CHUNK)
            pltpu.sync_copy(idx_hbm.at[pl.ds(off, SC_CHUNK)], idx_v)
            pltpu.async_copy(table_hbm.at[idx_v], rows_v, sem)

        start_gather(0, bufs[0])

        @pl.loop(0, n_chunks, step=2)
        def _(c0):
            for b in range(2):
                c = c0 + b
                idx_v, rows_v, sem = bufs[b]

                @pl.when(c + 1 < n_chunks)
                def _():
                    start_gather(c + 1, bufs[1 - b])

                pltpu.make_async_copy(table_hbm.at[idx_v], rows_v, sem).wait()
                off = pl.multiple_of(base + c * SC_CHUNK, SC_CHUNK)
                pltpu.sync_copy(rows_v, out_hbm.at[pl.ds(off, SC_CHUNK)])

    return pl.kernel(
        body,
        out_type=jax.ShapeDtypeStruct((n_idx, d), table.dtype),
        mesh=mesh,
        scratch_types=[pltpu.VMEM((SC_CHUNK,), i32), pltpu.VMEM((SC_CHUNK,), i32),
                       pltpu.VMEM((SC_CHUNK, d), table.dtype), pltpu.VMEM((SC_CHUNK, d), table.dtype),
                       pltpu.SemaphoreType.DMA, pltpu.SemaphoreType.DMA],
        name="gather_rows",
    )(table, idx)


def _combine_kernel(gate_ref, xb_ref, mod_ref, y_ref, o_ref):
    g2 = mod_ref[0][:, 5 * D_MODEL:6 * D_MODEL]
    gates = gate_ref[...]
    acc_hi = jnp.zeros((gates.shape[0], D_PACK), f32)
    acc_lo = jnp.zeros((gates.shape[0], D_PACK), f32)
    for k in range(TOP_K):
        y_hi, y_lo = _unpack_pairs(y_ref[k])
        acc_hi = acc_hi + gates[:, k:k + 1] * y_hi
        acc_lo = acc_lo + gates[:, k:k + 1] * y_lo
    o_ref[:, 0:D_PACK] = xb_ref[:, 0:D_PACK] + g2[:, 0:D_PACK] * acc_hi
    o_ref[:, D_PACK:D_MODEL] = xb_ref[:, D_PACK:D_MODEL] + g2[:, D_PACK:D_MODEL] * acc_lo


def _combine_call(gates, xbase, mod, ygath, seq_row_c, row0):
    tm = COMBINE_TILE
    n_rows = ygath.shape[1]
    assert row0 % tm == 0 and n_rows % tm == 0
    b0 = row0 // tm
    return pl.pallas_call(
        _combine_kernel,
        grid=(n_rows // tm,),
        in_specs=[pl.BlockSpec((tm, TOP_K), lambda i: (b0 + i, 0)),
                  pl.BlockSpec((tm, D_MODEL), lambda i: (b0 + i, 0)),
                  pl.BlockSpec((1, 1, 6 * D_MODEL), lambda i: (seq_row_c(b0 + i), 0, 0)),
                  pl.BlockSpec((TOP_K, tm, D_PACK), lambda i: (0, i, 0))],
        out_specs=pl.BlockSpec((tm, D_MODEL), lambda i: (i, 0)),
        out_shape=jax.ShapeDtypeStruct((n_rows, D_MODEL), f32),
        compiler_params=_cparams(("parallel",)),
        name="combine",
    )(gates, xbase, mod, ygath)


def _block_avg(width, group):
    r = np.arange(width)
    return jnp.asarray((r[:, None] // group == r[None, :] // group).astype(np.float32) / group, dtype=bf16)


def _kv_replicate():
    c = np.arange(ATTN_W)
    src = (c // (ATTN_W // N_KV_HEADS)) * HEAD_DIM + c % HEAD_DIM
    return jnp.asarray((np.arange(KV_W)[:, None] == src[None, :]).astype(np.float32), dtype=bf16)


def _rope_lane_tables(n_tok):
    rows = n_tok // GRID_W
    r = jnp.repeat(jnp.arange(rows, dtype=f32), GRID_W)
    col = jnp.tile(jnp.arange(GRID_W, dtype=f32), rows)
    inv = ROPE_THETA ** (-jnp.arange(ROPE_FREQ, dtype=f32) / ROPE_FREQ)
    ar = r[:, None] * inv
    ac = col[:, None] * inv
    cos_h = jnp.concatenate([jnp.cos(ar), jnp.cos(ar), jnp.cos(ac), jnp.cos(ac)], axis=-1)
    sin_h = jnp.concatenate([-jnp.sin(ar), jnp.sin(ar), -jnp.sin(ac), jnp.sin(ac)], axis=-1)
    return jnp.tile(cos_h, (1, N_HEADS)), jnp.tile(sin_h, (1, N_HEADS))


def _block_diag(w):
    nb, bw, _ = w.shape
    eye = jnp.eye(nb, dtype=w.dtype)
    return (w[:, :, None, :] * eye[:, None, :, None]).reshape(nb * bw, nb * bw)


def _gla_state_in(s):
    bsz = s.shape[0]
    eye = jnp.eye(GLA_H, dtype=s.dtype)
    st = jnp.swapaxes(s, -1, -2)
    big = st[:, :, :, :, None, :] * eye[None, None, :, None, :, None]
    return big.reshape(bsz, 2, GLA_W, GLA_W)


def kernel(x_prompt, x_sample, cache_k, cache_v, state_lru, state_gla, c, c_ctx, ada_w, ada_b, norm1_g, norm2_g, w_in, q_norm_g, k_norm_g, attn_out_g, conv_w, conv_b, lru_wa, lru_ba, lru_wi, lru_bi, lru_lambda, lru_out_g, gla_wa2, gla_ba, gla_out_g, w_out, router_w, router_b, exp_w1, exp_w3, exp_w2, sh_w1, sh_w3, sh_w2):
    bc, tc, _ = x_prompt.shape
    bl, tl, _ = x_sample.shape
    depth = w_in.shape[0]
    nc = bc * tc
    n = nc + bl * tl
    past = cache_k.shape[2]
    assert tc == ROW_TILE and tl % ROW_TILE == 0 and nc % tl == 0 and bl + 1 <= SUBLANES
    assert nc % PROJ_TILE == 0 and tl % PROJ_TILE == 0 and nc % COMBINE_TILE == 0 and tl % COMBINE_TILE == 0

    def seq_row_for(tile):
        def seq_row(i):
            return jnp.where(i < nc // tile, 0, 1 + (i - nc // tile) // (tl // tile))
        return seq_row

    seq_row = seq_row_for(PROJ_TILE)
    seq_row_c = seq_row_for(COMBINE_TILE)

    xc = x_prompt.reshape(nc, D_MODEL)
    xl = x_sample.reshape(bl * tl, D_MODEL)
    cond = jnp.zeros((SUBLANES, D_MODEL), f32).at[0].set(c_ctx).at[1:1 + bl].set(c)
    mods = _ada_call(cond, ada_w, ada_b)

    bmq = _block_avg(ATTN_W, HEAD_DIM)
    bmk = _block_avg(KV_W, HEAD_DIM)
    bmg = _block_avg(GLA_W, GLA_DK)
    rep = _kv_replicate()
    cos_t, sin_t = _rope_lane_tables(tl)
    step_rows = STEP_BLOCKS * SLOT_BLOCK
    n_slots = -(-(n * TOP_K + N_EXPERTS * (SLOT_BLOCK - 1)) // step_rows) * step_rows
    tile8 = lambda v: jnp.tile(v, N_HEADS)[None, :]

    ks, vs, lrus, glas = [], [], [], []
    for l in range(depth):
        mod = mods[l].reshape(SUBLANES, 1, 6 * D_MODEL)
        pa, pl_, pg = _proj_call(xc, xl, mod, norm1_g[l][None, :], w_in, l, seq_row)

        qg, kg, og = tile8(q_norm_g[l]), jnp.tile(k_norm_g[l], N_KV_HEADS)[None, :], attn_out_g[l][None, :]
        attn_c, k_new, v_new = _attn_ctx_call(pa, bc, tc, qg, kg, og, bmq, bmk, rep)
        attn_l = _attn_lat_call(pa, nc, bl, tl, cache_k[:, l].reshape(bl, past, KV_W),
                                cache_v[:, l].reshape(bl, past, KV_W), cos_t, sin_t, qg, kg, og, bmq, bmk, rep)
        ks.append(k_new.reshape(bc, tc, N_KV_HEADS, HEAD_DIM))
        vs.append(v_new.reshape(bc, tc, N_KV_HEADS, HEAD_DIM))

        wg = jnp.concatenate([_block_diag(lru_wa[l, 0]), _block_diag(lru_wi[l, 0]),
                              _block_diag(lru_wa[l, 1]), _block_diag(lru_wi[l, 1])], axis=1).astype(bf16)
        bg = jnp.concatenate([lru_ba[l, 0], lru_bi[l, 0], lru_ba[l, 1], lru_bi[l, 1]])[None, :]
        lru_args = (conv_w[l], conv_b[l][None, :], wg, bg, lru_lambda[l], lru_out_g[l][None, :])
        lru_c, lru_fin = _lru_call(pl_, 0, bc, tc, jnp.zeros((bc, 2, LRU_W), f32), *lru_args)
        lru_l, _ = _lru_call(pl_, nc, bl, tl, state_lru[:, l], *lru_args)
        lrus.append(lru_fin)

        wa = jnp.zeros((LANES, 2 * GLA_W), f32)
        wa = wa.at[0:GLA_RANK, 0:GLA_W].set(gla_wa2[l, 0]).at[GLA_RANK:2 * GLA_RANK, GLA_W:].set(gla_wa2[l, 1])
        gla_args = (wa.astype(bf16), gla_ba[l].reshape(1, 2 * GLA_W), gla_out_g[l].reshape(1, GLA_W), bmg)
        gla_c, gla_fin = _gla_call(pg, 0, bc, tc, None, *gla_args)
        gla_l, _ = _gla_call(pg, nc, bl, tl, _gla_state_in(state_gla[:, l]), *gla_args)
        glas.append(gla_fin)

        rw_t = router_w[l].T
        rw_hi = rw_t.astype(bf16)
        rw_lo = (rw_t - rw_hi.astype(f32)).astype(bf16)
        xbase, h2, scores_t = _out_call(xc, xl, (attn_c, lru_c, gla_c), (attn_l, lru_l, gla_l), mod, w_out[l].astype(bf16),
                                        norm2_g[l][None, :], rw_hi, rw_lo, sh_w1[l].astype(bf16),
                                        sh_w3[l].astype(bf16), sh_w2[l].astype(bf16), seq_row)

        idx, gates, rank, counts = _route_call(scores_t, router_b[l][:, None])
        cnt = counts[:, 0].astype(i32)
        padded = (cnt + SLOT_BLOCK - 1) // SLOT_BLOCK * SLOT_BLOCK
        padded_end = jnp.cumsum(padded)
        dest_km = _dest_call(idx, rank, (padded_end - padded).astype(f32)[:, None])
        nb = n_slots // SLOT_BLOCK
        blk_first = jnp.arange(nb, dtype=i32) * SLOT_BLOCK
        blk_e = jnp.minimum(jnp.sum((padded_end[None, :] <= blk_first[:, None]).astype(i32), axis=1), N_EXPERTS - 1)
        blk_valid = jnp.clip((padded_end - padded + cnt)[blk_e] - blk_first, 0, SLOT_BLOCK)
        n_used = padded_end[-1:] // SLOT_BLOCK
        owns = jnp.cumsum((cnt > 0).astype(i32))
        blk_ord = (owns - 1)[blk_e]
        expert_seq = jnp.minimum(jnp.sum((owns[None, :] <= jnp.arange(N_EXPERTS, dtype=i32)[:, None]).astype(i32),
                                         axis=1), N_EXPERTS - 1)
        slots = _scatter_rows_call(h2, dest_km, n_slots)
        slots = _expert_call(n_used, blk_valid, blk_ord, expert_seq, owns[-1:], slots, exp_w1, exp_w3, exp_w2, l)
        gates_t = gates.T
        yg_c = _gather_rows_call(slots, dest_km[:, :nc].reshape(-1)).reshape(TOP_K, nc, D_PACK)
        yg_l = _gather_rows_call(slots, dest_km[:, nc:].reshape(-1)).reshape(TOP_K, n - nc, D_PACK)
        xc = _combine_call(gates_t, xbase, mod, yg_c, seq_row_c, 0)
        xl = _combine_call(gates_t, xbase, mod, yg_l, seq_row_c, nc)

    y_prompt = xc.reshape(bc, tc, D_MODEL)
    y_sample = xl.reshape(bl, tl, D_MODEL)
    return (y_prompt, y_sample, jnp.stack(ks, axis=1), jnp.stack(vs, axis=1),
            jnp.stack(lrus, axis=1), jnp.stack(glas, axis=1))
```

```python
import functools

import jax
import jax.numpy as jnp
import numpy as np
from jax import lax
from jax.experimental import pallas as pl
from jax.experimental.pallas import tpu as pltpu
from jax.experimental.pallas import tpu_sc as plsc

f32 = jnp.float32
bf16 = jnp.bfloat16
i32 = jnp.int32

D_MODEL = 1024
N_HEADS = 8
N_KV_HEADS = 2
HEAD_DIM = 64
ATTN_W = N_HEADS * HEAD_DIM
KV_W = N_KV_HEADS * HEAD_DIM
GRID_W = 64
ROPE_FREQ = HEAD_DIM // 4
ROPE_THETA = 10000.0
LRU_W = 256
LRU_BLOCKS = 4
LRU_C = 8.0
CONV_W = 4
CONV_LEFT = 2
GLA_H = 4
GLA_DK = 64
GLA_W = 256
GLA_RANK = 16
GLA_TAU = 16.0
N_EXPERTS = 256
TOP_K = 8
EXPERT_FF = 256
ROUTED_SCALE = 2.5
EPS = 1e-6

LANES = 128
SUBLANES = 8
ROW_TILE = 256
PROJ_TILE = 512
GLA_BLOCK = 256
SLOT_BLOCK = 256
STEP_BLOCKS = 4
W_RING = 4
COMBINE_TILE = 256
SC_CHUNK = 64
D_PACK = D_MODEL // 2
VMEM_LIMIT = 56 * 1024 * 1024


def _cparams(sem, vmem=VMEM_LIMIT):
    return pltpu.CompilerParams(dimension_semantics=sem, vmem_limit_bytes=vmem)


def _dot(a, b):
    return jnp.dot(a, b, preferred_element_type=f32)


def _dot_nt(a, b):
    return lax.dot_general(a, b, (((1,), (1,)), ((), ())), preferred_element_type=f32)


def _dot_tn(a, b):
    return lax.dot_general(a, b, (((0,), (0,)), ((), ())), preferred_element_type=f32)


def _split(x):
    hi = x.astype(bf16)
    lo = (x - hi.astype(f32)).astype(bf16)
    return hi, lo


def _dot_x2(x, w):
    hi, lo = _split(x)
    return _dot(hi, w) + _dot(lo, w)


def _dot_2x(m, x):
    hi, lo = _split(x)
    return _dot(m, hi) + _dot(m, lo)


def _dot3(a, b_hi, b_lo):
    a_hi, a_lo = _split(a)
    return _dot(a_hi, b_hi) + _dot(a_lo, b_hi) + _dot(a_hi, b_lo)


def _sigmoid(x):
    return 1.0 / (1.0 + jnp.exp(-x))


def _silu(x):
    return x * _sigmoid(x)


def _softplus(x):
    return jnp.maximum(x, 0.0) + jnp.log(1.0 + jnp.exp(-jnp.abs(x)))


def _gelu_tanh(x):
    return 0.5 * x * (1.0 + jnp.tanh(0.7978845608028654 * (x + 0.044715 * x * x * x)))


def _rms(x, g):
    return x * lax.rsqrt(jnp.mean(x * x, axis=-1, keepdims=True) + EPS) * g


def _pack_pairs(x):
    c = x.shape[1] // 2
    hi = lax.bitcast_convert_type(x[:, :c].astype(bf16).astype(f32), i32)
    lo = lax.bitcast_convert_type(x[:, c:].astype(bf16).astype(f32), i32)
    return hi | lax.shift_right_logical(lo, 16)


def _unpack_pairs(w):
    hi = lax.bitcast_convert_type(w & jnp.int32(-65536), f32)
    lo = lax.bitcast_convert_type(w << 16, f32)
    return hi, lo


def _head_rms(x, g, bm):
    ms = _dot_x2(x * x, bm)
    return x * lax.rsqrt(ms + EPS) * g


def _ada_kernel(c_ref, w_ref, b_ref, o_ref):
    s = _silu(c_ref[...])
    w = w_ref[0]
    w_hi, w_lo = _split(w)
    o_ref[0] = _dot3(s, w_hi, w_lo) + b_ref[0]


def _ada_call(cond, ada_w, ada_b):
    depth = ada_w.shape[0]
    nt = 1536
    return pl.pallas_call(
        _ada_kernel,
        grid=(depth, 6 * D_MODEL // nt),
        in_specs=[pl.BlockSpec((SUBLANES, D_MODEL), lambda l, j: (0, 0)),
                  pl.BlockSpec((1, D_MODEL, nt), lambda l, j: (l, 0, j)),
                  pl.BlockSpec((1, 1, nt), lambda l, j: (l, 0, j))],
        out_specs=pl.BlockSpec((1, SUBLANES, nt), lambda l, j: (l, 0, j)),
        out_shape=jax.ShapeDtypeStruct((depth, SUBLANES, 6 * D_MODEL), f32),
        compiler_params=_cparams(("parallel", "parallel")),
        name="ada_mod",
    )(cond, ada_w, ada_b.reshape(depth, 1, 6 * D_MODEL))


PROJ_A = ATTN_W + 2 * KV_W
PROJ_L = 2 * LRU_W
PROJ_G = 4 * GLA_W + LANES
IN_COLS = PROJ_A + PROJ_L + 4 * GLA_W + 2 * GLA_RANK


def _proj_kernel(xc_ref, xl_ref, mod_ref, g_ref, w_ref, oa_ref, ol_ref, og_ref, w_s, *, n_ctx_tiles):
    @pl.when(pl.program_id(0) == 0)
    def _():
        w_s[:, 0:IN_COLS] = w_ref[0].astype(bf16)
        w_s[:, IN_COLS:] = jnp.zeros((D_MODEL, w_s.shape[1] - IN_COLS), bf16)

    mod = mod_ref[0]
    sh = mod[:, 0:D_MODEL]
    sc = mod[:, D_MODEL:2 * D_MODEL]
    x = jnp.where(pl.program_id(0) < n_ctx_tiles, xc_ref[...], xl_ref[...])
    h = (_rms(x, g_ref[...]) * (1.0 + sc) + sh).astype(bf16)
    p = _dot(h, w_s[...])
    oa_ref[...] = p[:, 0:PROJ_A].astype(bf16)
    ol_ref[...] = p[:, PROJ_A:PROJ_A + PROJ_L].astype(bf16)
    og_ref[...] = p[:, PROJ_A + PROJ_L:PROJ_A + PROJ_L + PROJ_G].astype(bf16)


def _row_pair_specs(nct, width):
    return (pl.BlockSpec((PROJ_TILE, width), lambda i: (jnp.minimum(i, nct - 1), 0)),
            pl.BlockSpec((PROJ_TILE, width), lambda i: (jnp.maximum(i - nct, 0), 0)))


def _proj_call(xc, xl, mod, g, w_all, layer, seq_row):
    n = xc.shape[0] + xl.shape[0]
    nct = xc.shape[0] // PROJ_TILE
    cols = PROJ_A + PROJ_L + PROJ_G
    return pl.pallas_call(
        functools.partial(_proj_kernel, n_ctx_tiles=nct),
        grid=(n // PROJ_TILE,),
        in_specs=[*_row_pair_specs(nct, D_MODEL),
                  pl.BlockSpec((1, 1, 6 * D_MODEL), lambda i: (seq_row(i), 0, 0)),
                  pl.BlockSpec((1, D_MODEL), lambda i: (0, 0)),
                  pl.BlockSpec((1, D_MODEL, IN_COLS), lambda i: (layer, 0, 0))],
        out_specs=[pl.BlockSpec((PROJ_TILE, PROJ_A), lambda i: (i, 0)),
                   pl.BlockSpec((PROJ_TILE, PROJ_L), lambda i: (i, 0)),
                   pl.BlockSpec((PROJ_TILE, PROJ_G), lambda i: (i, 0))],
        out_shape=[jax.ShapeDtypeStruct((n, PROJ_A), bf16),
                   jax.ShapeDtypeStruct((n, PROJ_L), bf16),
                   jax.ShapeDtypeStruct((n, PROJ_G), bf16)],
        scratch_shapes=[pltpu.VMEM((D_MODEL, cols), bf16)],
        compiler_params=_cparams(("arbitrary",)),
        name="in_proj",
    )(xc, xl, mod, g, w_all)


def _rope(x, cos_t, sin_t):
    w = x.shape[1]
    up = pltpu.roll(x, w - ROPE_FREQ, 1)
    dn = pltpu.roll(x, ROPE_FREQ, 1)
    lane = lax.broadcasted_iota(i32, x.shape, 1)
    partner = jnp.where((lane & (2 * ROPE_FREQ - 1)) < ROPE_FREQ, up, dn)
    return x * cos_t + partner * sin_t


def _attend(q, kk_ref, vv_ref, o_ref):
    tq = q.shape[0]
    gw = ATTN_W // N_KV_HEADS
    lane = lax.broadcasted_iota(i32, (tq, gw), 1)
    for g in range(N_KV_HEADS):
        qg = q[:, g * gw:(g + 1) * gw]
        kg = kk_ref[:, g * gw:(g + 1) * gw]
        vg = vv_ref[:, g * gw:(g + 1) * gw]
        acc = jnp.zeros((tq, gw), f32)
        for hh in range(N_HEADS // N_KV_HEADS):
            hm = (lane >> 6) == hh
            s = _dot_nt(jnp.where(hm, qg, 0.0).astype(bf16), kg)
            m = jnp.max(s, axis=-1, keepdims=True)
            p = jnp.exp(s - m)
            l = jnp.sum(p, axis=-1, keepdims=True)
            o = _dot(p.astype(bf16), vg) / l
            acc = jnp.where(hm, o, acc)
        o_ref[:, g * gw:(g + 1) * gw] = acc


def _attn_ctx_kernel(p_ref, qg_ref, kg_ref, og_ref, bmq_ref, bmk_ref, rep_ref,
                     o_ref, ko_ref, vo_ref, kk_s, vv_s, o_s):
    p = p_ref[...].astype(f32)
    q = _head_rms(p[:, 0:ATTN_W], qg_ref[...], bmq_ref[...])
    k = _head_rms(p[:, ATTN_W:ATTN_W + KV_W], kg_ref[...], bmk_ref[...])
    v = p[:, ATTN_W + KV_W:PROJ_A]
    ko_ref[...] = k
    vo_ref[...] = v
    kk_s[...] = _dot(k.astype(bf16), rep_ref[...]).astype(bf16)
    vv_s[...] = _dot(v.astype(bf16), rep_ref[...]).astype(bf16)
    _attend(q * HEAD_DIM ** -0.5, kk_s, vv_s, o_s)
    o_ref[...] = _rms(o_s[...], og_ref[...]).astype(bf16)


def _attn_ctx_call(pa, n_seq, t, qg, kg, og, bmq, bmk, rep):
    assert t == ROW_TILE
    const = lambda i: (0, 0)
    return pl.pallas_call(
        _attn_ctx_kernel,
        grid=(n_seq,),
        in_specs=[pl.BlockSpec((t, PROJ_A), lambda i: (i, 0)),
                  pl.BlockSpec((1, ATTN_W), const), pl.BlockSpec((1, KV_W), const), pl.BlockSpec((1, ATTN_W), const),
                  pl.BlockSpec((ATTN_W, ATTN_W), const), pl.BlockSpec((KV_W, KV_W), const),
                  pl.BlockSpec((KV_W, ATTN_W), const)],
        out_specs=[pl.BlockSpec((t, ATTN_W), lambda i: (i, 0)),
                   pl.BlockSpec((t, KV_W), lambda i: (i, 0)),
                   pl.BlockSpec((t, KV_W), lambda i: (i, 0))],
        out_shape=[jax.ShapeDtypeStruct((n_seq * t, ATTN_W), bf16),
                   jax.ShapeDtypeStruct((n_seq * t, KV_W), f32),
                   jax.ShapeDtypeStruct((n_seq * t, KV_W), f32)],
        scratch_shapes=[pltpu.VMEM((t, ATTN_W), bf16), pltpu.VMEM((t, ATTN_W), bf16), pltpu.VMEM((t, ATTN_W), f32)],
        compiler_params=_cparams(("parallel",)),
        name="attn_ctx",
    )(pa, qg, kg, og, bmq, bmk, rep)


def _attn_lat_kernel(q_ref, kv_ref, ck_ref, cv_ref, cq_ref, sq_ref, ckk_ref, skk_ref,
                     qg_ref, kg_ref, og_ref, bmq_ref, bmk_ref, rep_ref,
                     o_ref, kk_s, vv_s, o_s, *, past):
    @pl.when(pl.program_id(1) == 0)
    def _():
        kv = kv_ref[...].astype(f32)
        k = _head_rms(kv[:, 0:KV_W], kg_ref[...], bmk_ref[...])
        k = _rope(k, ckk_ref[...], skk_ref[...])
        v = kv[:, KV_W:2 * KV_W]
        kk_s[0:past, :] = _dot(ck_ref[0].astype(bf16), rep_ref[...]).astype(bf16)
        vv_s[0:past, :] = _dot(cv_ref[0].astype(bf16), rep_ref[...]).astype(bf16)
        kk_s[past:, :] = _dot(k.astype(bf16), rep_ref[...]).astype(bf16)
        vv_s[past:, :] = _dot(v.astype(bf16), rep_ref[...]).astype(bf16)

    q = _head_rms(q_ref[...].astype(f32), qg_ref[...], bmq_ref[...])
    q = _rope(q, cq_ref[...], sq_ref[...])
    _attend(q * HEAD_DIM ** -0.5, kk_s, vv_s, o_s)
    o_ref[...] = _rms(o_s[...], og_ref[...]).astype(bf16)


def _attn_lat_call(pa, row0, n_seq, t, cache_k, cache_v, cq, sq, qg, kg, og, bmq, bmk, rep):
    past = cache_k.shape[1]
    nq = t // ROW_TILE
    assert row0 % t == 0 and t % ROW_TILE == 0
    qb0 = row0 // ROW_TILE
    sb0 = row0 // t
    const = lambda b, j: (0, 0)
    return pl.pallas_call(
        functools.partial(_attn_lat_kernel, past=past),
        grid=(n_seq, nq),
        in_specs=[pl.BlockSpec((ROW_TILE, ATTN_W), lambda b, j: (qb0 + b * nq + j, 0)),
                  pl.BlockSpec((t, 2 * KV_W), lambda b, j: (sb0 + b, ATTN_W // (2 * KV_W))),
                  pl.BlockSpec((1, past, KV_W), lambda b, j: (b, 0, 0)),
                  pl.BlockSpec((1, past, KV_W), lambda b, j: (b, 0, 0)),
                  pl.BlockSpec((ROW_TILE, ATTN_W), lambda b, j: (j, 0)),
                  pl.BlockSpec((ROW_TILE, ATTN_W), lambda b, j: (j, 0)),
                  pl.BlockSpec((t, KV_W), const), pl.BlockSpec((t, KV_W), const),
                  pl.BlockSpec((1, ATTN_W), const), pl.BlockSpec((1, KV_W), const), pl.BlockSpec((1, ATTN_W), const),
                  pl.BlockSpec((ATTN_W, ATTN_W), const), pl.BlockSpec((KV_W, KV_W), const),
                  pl.BlockSpec((KV_W, ATTN_W), const)],
        out_specs=pl.BlockSpec((ROW_TILE, ATTN_W), lambda b, j: (b * nq + j, 0)),
        out_shape=jax.ShapeDtypeStruct((n_seq * t, ATTN_W), bf16),
        scratch_shapes=[pltpu.VMEM((past + t, ATTN_W), bf16), pltpu.VMEM((past + t, ATTN_W), bf16),
                        pltpu.VMEM((ROW_TILE, ATTN_W), f32)],
        compiler_params=_cparams(("parallel", "arbitrary")),
        name="attn_lat",
    )(pa, pa, cache_k, cache_v, cq, sq, cq, sq, qg, kg, og, bmq, bmk, rep)


def _lru_kernel(x_ref, h0_ref, cw_ref, cb_ref, wg_ref, bg_ref, lam_ref, g_ref,
                o_ref, fin_ref, xs, a_s, u_s, h_s, *, t):
    ng = t // SUBLANES
    lx = x_ref[:, 0:LRU_W].astype(f32)
    ly = x_ref[:, LRU_W:2 * LRU_W].astype(f32)
    xs[0:SUBLANES, :] = jnp.zeros((SUBLANES, LRU_W), f32)
    xs[SUBLANES + t:2 * SUBLANES + t, :] = jnp.zeros((SUBLANES, LRU_W), f32)
    xs[SUBLANES:SUBLANES + t, :] = lx
    xc = jnp.broadcast_to(cb_ref[...], (t, LRU_W))
    for j in range(CONV_W):
        off = SUBLANES + j - CONV_LEFT
        xc = xc + xs[off:off + t, :] * cw_ref[j:j + 1, :]
    gates = _sigmoid(_dot(xc.astype(bf16), wg_ref[...]) + bg_ref[...])
    sp = _softplus(-lam_ref[...])
    sub = lax.broadcasted_iota(i32, (ng, SUBLANES, LRU_W), 1)
    for d in range(2):
        r = gates[:, (2 * d) * LRU_W:(2 * d + 1) * LRU_W]
        gi = gates[:, (2 * d + 1) * LRU_W:(2 * d + 2) * LRU_W]
        log_a = -LRU_C * r * sp[d:d + 1, :]
        a = jnp.exp(log_a)
        th = jnp.tanh(log_a)
        u = jnp.sqrt(-2.0 * th / (1.0 - th)) * (gi * xc)
        a = a.reshape(ng, SUBLANES, LRU_W)
        u = u.reshape(ng, SUBLANES, LRU_W)
        for s in (1, 2, 4):
            if d == 0:
                a_sh = pltpu.roll(a, s, 1)
                u_sh = pltpu.roll(u, s, 1)
                ok = sub >= s
            else:
                a_sh = pltpu.roll(a, SUBLANES - s, 1)
                u_sh = pltpu.roll(u, SUBLANES - s, 1)
                ok = sub < SUBLANES - s
            u = jnp.where(ok, a * u_sh + u, u)
            a = jnp.where(ok, a * a_sh, a)
        a_s[...] = a.reshape(t, LRU_W)
        u_s[...] = u.reshape(t, LRU_W)
        h0 = jnp.broadcast_to(h0_ref[0, d:d + 1, :], (SUBLANES, LRU_W))
        edge = SUBLANES - 1 if d == 0 else 0

        def body(i, carry, d=d, edge=edge):
            g = i if d == 0 else ng - 1 - i
            rows = pl.ds(pl.multiple_of(g * SUBLANES, SUBLANES), SUBLANES)
            h = a_s[rows, :] * carry + u_s[rows, :]
            if d == 0:
                h_s[rows, :] = h
            else:
                h_s[rows, :] = h_s[rows, :] + h
            return jnp.broadcast_to(h[edge:edge + 1, :], (SUBLANES, LRU_W))

        last = lax.fori_loop(0, ng, body, h0)
        fin_ref[0, d:d + 1, :] = last[0:1, :]
    o_ref[...] = _rms(h_s[...] * _gelu_tanh(ly), g_ref[...]).astype(bf16)


def _lru_call(pl_, row0, n_seq, t, h0, cw, cb, wg, bg, lam, g):
    sb0 = row0 // t
    assert row0 % t == 0
    const = lambda b: (0, 0)
    in_specs = [pl.BlockSpec((t, PROJ_L), lambda b: (sb0 + b, 0)),
                pl.BlockSpec((1, 2, LRU_W), lambda b: (b, 0, 0)),
                pl.BlockSpec((CONV_W, LRU_W), const), pl.BlockSpec((1, LRU_W), const),
                pl.BlockSpec((LRU_W, 4 * LRU_W), const), pl.BlockSpec((1, 4 * LRU_W), const),
                pl.BlockSpec((2, LRU_W), const), pl.BlockSpec((1, LRU_W), const)]
    args = [pl_, h0, cw, cb, wg, bg, lam, g]
    return pl.pallas_call(
        functools.partial(_lru_kernel, t=t),
        grid=(n_seq,),
        in_specs=in_specs,
        out_specs=[pl.BlockSpec((t, LRU_W), lambda b: (b, 0)),
                   pl.BlockSpec((1, 2, LRU_W), lambda b: (b, 0, 0))],
        out_shape=[jax.ShapeDtypeStruct((n_seq * t, LRU_W), bf16), jax.ShapeDtypeStruct((n_seq, 2, LRU_W), f32)],
        scratch_shapes=[pltpu.VMEM((t + 2 * SUBLANES, LRU_W), f32), pltpu.VMEM((t, LRU_W), f32),
                        pltpu.VMEM((t, LRU_W), f32), pltpu.VMEM((t, LRU_W), f32)],
        compiler_params=_cparams(("parallel",)),
        name="rglru_t%d" % t,
    )(*args)


def _bcast_rows(b, period, off):
    w = b.shape[1]
    return jnp.concatenate(
        [jnp.broadcast_to(b[i * period + off:i * period + off + 1, :], (period, w)) for i in range(b.shape[0] // period)],
        axis=0)


def _gla_block(q, k, v, la, reverse, st_ref):
    n = GLA_BLOCK
    row = lax.broadcasted_iota(i32, (n, n), 0)
    col = lax.broadcasted_iota(i32, (n, n), 1)
    same64 = (row >> 6) == (col >> 6)
    same32 = (row >> 5) == (col >> 5)
    same16 = (row >> 4) == (col >> 4)
    if not reverse:
        cum = (same64 & (col <= row)).astype(bf16)
        m1 = same64 & ((row & 63) >= 32) & ((col & 63) < 32)
        m2 = same32 & ((row & 31) >= 16) & ((col & 31) < 16)
        m3 = same16 & (col <= row)
        offs = (31, 15, 7, 63)
    else:
        cum = (same64 & (col >= row)).astype(bf16)
        m1 = same64 & ((row & 63) < 32) & ((col & 63) >= 32)
        m2 = same32 & ((row & 31) < 16) & ((col & 31) >= 16)
        m3 = same16 & (col >= row)
        offs = (32, 16, 8, 0)
    b = _dot_2x(cum, la)
    r1 = _bcast_rows(b, 64, offs[0])
    r2 = _bcast_rows(b, 32, offs[1])
    r3 = _bcast_rows(b, 16, offs[2])
    bl = _bcast_rows(b, 64, offs[3])
    q1 = q * jnp.exp(jnp.minimum(b - r1, 0.0))
    k1 = (k * jnp.exp(jnp.minimum(r1 - b, 0.0))).astype(bf16)
    q2 = q * jnp.exp(jnp.minimum(b - r2, 0.0))
    k2 = (k * jnp.exp(jnp.minimum(r2 - b, 0.0))).astype(bf16)
    q3 = q * jnp.exp(b - r3)
    k3 = (k * jnp.exp(r3 - b)).astype(bf16)
    qe = (q * jnp.exp(b)).astype(bf16)
    kl = (k * jnp.exp(bl - b)).astype(bf16)
    dec = jnp.exp(bl)
    vb = v.astype(bf16)
    zq = jnp.zeros((n, n), f32)
    intra = jnp.zeros((n, n), f32)
    for h in range(GLA_H):
        hm = (col >> 6) == h
        a1 = _dot_nt(jnp.where(hm, q1, zq).astype(bf16), k1)
        a2 = _dot_nt(jnp.where(hm, q2, zq).astype(bf16), k2)
        a3 = _dot_nt(jnp.where(hm, q3, zq).astype(bf16), k3)
        att = jnp.where(m1, a1, jnp.where(m2, a2, jnp.where(m3, a3, zq)))
        intra = intra + _dot(att.astype(bf16), jnp.where(hm, v, zq).astype(bf16))
    outs = [None] * 4
    for c in (range(4) if not reverse else range(3, -1, -1)):
        rs = slice(64 * c, 64 * c + 64)
        st = st_ref[...]
        inter = _dot_nt(qe[rs], st.astype(bf16))
        kv = _dot_tn(vb[rs], kl[rs])
        drow = dec[64 * c:64 * c + 1, :]
        st_ref[...] = st * drow + jnp.where(same64, kv, zq)
        outs[c] = intra[rs] + inter
    return jnp.concatenate(outs, axis=0)


def _gla_kernel(x_ref, s0_ref, wa_ref, ba_ref, g_ref, bm_ref, o_ref, fin_ref, st_s, o_s, *, t):
    nblk = t // GLA_BLOCK

    def block(j, reverse):
        rows = pl.ds(j * GLA_BLOCK if isinstance(j, int) else pl.multiple_of(j * GLA_BLOCK, GLA_BLOCK), GLA_BLOCK)
        q = x_ref[rows, 0:GLA_W].astype(f32) * GLA_DK ** -0.5
        k = x_ref[rows, GLA_W:2 * GLA_W].astype(f32)
        v = x_ref[rows, 2 * GLA_W:3 * GLA_W].astype(f32)
        ga = x_ref[rows, 4 * GLA_W:4 * GLA_W + LANES]
        d = 1 if reverse else 0
        z = _dot(ga.astype(bf16), wa_ref[:, d * GLA_W:(d + 1) * GLA_W]) + ba_ref[:, d * GLA_W:(d + 1) * GLA_W]
        la = -_softplus(-z) * (1.0 / GLA_TAU)
        o = _gla_block(q, k, v, la, reverse, st_s)
        if reverse:
            o_s[rows, :] = o_s[rows, :] + o
        else:
            o_s[rows, :] = o

    for d in range(2):
        if s0_ref is None:
            st_s[...] = jnp.zeros((GLA_W, GLA_W), f32)
        else:
            st_s[...] = s0_ref[0, d]
        if nblk == 1:
            block(0, d == 1)
        else:
            def body(i, carry, d=d):
                block(i if d == 0 else nblk - 1 - i, d == 1)
                return carry
            lax.fori_loop(0, nblk, body, 0)
        st_t = st_s[...].T
        for h in range(GLA_H):
            fin_ref[0, d, h] = st_t[h * GLA_DK:(h + 1) * GLA_DK, h * GLA_DK:(h + 1) * GLA_DK]
    gg = x_ref[:, 3 * GLA_W:4 * GLA_W].astype(f32)
    o_ref[...] = (_head_rms(o_s[...], g_ref[...], bm_ref[...]) * _silu(gg)).astype(bf16)


def _gla_call(pg, row0, n_seq, t, s0, wa, ba, g, bm):
    sb0 = row0 // t
    assert row0 % t == 0 and t % GLA_BLOCK == 0
    const = lambda b: (0, 0)
    in_specs = [pl.BlockSpec((t, PROJ_G), lambda b: (sb0 + b, 0))]
    args = [pg]
    if s0 is not None:
        in_specs.append(pl.BlockSpec((1, 2, GLA_W, GLA_W), lambda b: (b, 0, 0, 0)))
        args.append(s0)
    in_specs += [pl.BlockSpec((LANES, 2 * GLA_W), const), pl.BlockSpec((1, 2 * GLA_W), const),
                 pl.BlockSpec((1, GLA_W), const), pl.BlockSpec((GLA_W, GLA_W), const)]
    args += [wa, ba, g, bm]

    def kern(*refs):
        refs = list(refs)
        x_ref = refs.pop(0)
        s0_ref = refs.pop(0) if s0 is not None else None
        wa_ref, ba_ref, g_ref, bm_ref = refs[:4]
        _gla_kernel(x_ref, s0_ref, wa_ref, ba_ref, g_ref, bm_ref, *refs[4:], t=t)

    return pl.pallas_call(
        kern,
        grid=(n_seq,),
        in_specs=in_specs,
        out_specs=[pl.BlockSpec((t, GLA_W), lambda b: (b, 0)),
                   pl.BlockSpec((1, 2, GLA_H, GLA_DK, GLA_DK), lambda b: (b, 0, 0, 0, 0))],
        out_shape=[jax.ShapeDtypeStruct((n_seq * t, GLA_W), bf16),
                   jax.ShapeDtypeStruct((n_seq, 2, GLA_H, GLA_DK, GLA_DK), f32)],
        scratch_shapes=[pltpu.VMEM((GLA_W, GLA_W), f32), pltpu.VMEM((t, GLA_W), f32)],
        compiler_params=_cparams(("parallel",)),
        name="gla_t%d" % t,
    )(*args)


def _out_kernel(xc_ref, xl_ref, ac_ref, lc_ref, gc_ref, al_ref, ll_ref, gl_ref, mod_ref, wo_ref, g2_ref,
                rwh_ref, rwl_ref, s1_ref, s3_ref, s2_ref, xb_ref, h2_ref, sc_ref, *, n_ctx_tiles):
    is_ctx = pl.program_id(0) < n_ctx_tiles
    pick = lambda c_ref, l_ref: jnp.where(is_ctx, c_ref[...], l_ref[...]).astype(bf16)
    mod = mod_ref[0]
    g1 = mod[:, 2 * D_MODEL:3 * D_MODEL]
    sh2 = mod[:, 3 * D_MODEL:4 * D_MODEL]
    sc2 = mod[:, 4 * D_MODEL:5 * D_MODEL]
    g2 = mod[:, 5 * D_MODEL:6 * D_MODEL]
    m = (_dot(pick(ac_ref, al_ref), wo_ref[0:ATTN_W, :])
         + _dot(pick(lc_ref, ll_ref), wo_ref[ATTN_W:ATTN_W + LRU_W, :])
         + _dot(pick(gc_ref, gl_ref), wo_ref[ATTN_W + LRU_W:ATTN_W + LRU_W + GLA_W, :]))
    x1 = jnp.where(is_ctx, xc_ref[...], xl_ref[...]) + g1 * m
    h2 = _rms(x1, g2_ref[...]) * (1.0 + sc2) + sh2
    h2_ref[...] = _pack_pairs(h2)
    h_hi, h_lo = _split(h2)
    logits_t = _dot_nt(rwh_ref[...], h_hi) + _dot_nt(rwh_ref[...], h_lo) + _dot_nt(rwl_ref[...], h_hi)
    sc_ref[...] = _sigmoid(logits_t)
    hb = h2.astype(bf16)
    act = (_silu(_dot(hb, s1_ref[...])) * _dot(hb, s3_ref[...])).astype(bf16)
    xb_ref[...] = x1 + g2 * _dot(act, s2_ref[...])


def _out_call(xc, xl, mix_ctx, mix_lat, mod, wo, g2, rwh, rwl, s1, s3, s2, seq_row):
    n = xc.shape[0] + xl.shape[0]
    nct = xc.shape[0] // PROJ_TILE
    const = lambda i: (0, 0)
    rowb = lambda w: pl.BlockSpec((PROJ_TILE, w), lambda i: (i, 0))
    ctxb = lambda w: _row_pair_specs(nct, w)[0]
    latb = lambda w: _row_pair_specs(nct, w)[1]
    return pl.pallas_call(
        functools.partial(_out_kernel, n_ctx_tiles=nct),
        grid=(n // PROJ_TILE,),
        in_specs=[ctxb(D_MODEL), latb(D_MODEL),
                  ctxb(ATTN_W), ctxb(LRU_W), ctxb(GLA_W), latb(ATTN_W), latb(LRU_W), latb(GLA_W),
                  pl.BlockSpec((1, 1, 6 * D_MODEL), lambda i: (seq_row(i), 0, 0)),
                  pl.BlockSpec((D_MODEL, D_MODEL), const), pl.BlockSpec((1, D_MODEL), const),
                  pl.BlockSpec((N_EXPERTS, D_MODEL), const), pl.BlockSpec((N_EXPERTS, D_MODEL), const),
                  pl.BlockSpec((D_MODEL, EXPERT_FF), const), pl.BlockSpec((D_MODEL, EXPERT_FF), const),
                  pl.BlockSpec((EXPERT_FF, D_MODEL), const)],
        out_specs=[rowb(D_MODEL),
                   rowb(D_PACK),
                   pl.BlockSpec((N_EXPERTS, PROJ_TILE), lambda i: (0, i))],
        out_shape=[jax.ShapeDtypeStruct((n, D_MODEL), f32),
                   jax.ShapeDtypeStruct((n, D_PACK), i32),
                   jax.ShapeDtypeStruct((N_EXPERTS, n), f32)],
        compiler_params=_cparams(("parallel",)),
        name="out_proj",
    )(xc, xl, *mix_ctx, *mix_lat, mod, wo, g2, rwh, rwl, s1, s3, s2)


def _set_row(acc, k, row):
    sub = lax.broadcasted_iota(i32, acc.shape, 0)
    return jnp.where(sub == k, jnp.broadcast_to(row, acc.shape), acc)


def _route_kernel(sc_ref, rb_ref, idx_ref, gate_ref, rank_ref, cnt_ref, run_s):
    tm = sc_ref.shape[1]

    @pl.when(pl.program_id(0) == 0)
    def _():
        run_s[...] = jnp.zeros_like(run_s)

    scores = sc_ref[...]
    sel = scores + rb_ref[...]
    erow = lax.broadcasted_iota(i32, (N_EXPERTS, tm), 0).astype(f32)
    neg = jnp.full((N_EXPERTS, tm), -jnp.inf, f32)
    hots = []
    idx_o = jnp.zeros((TOP_K, tm), f32)
    gate_o = jnp.zeros((TOP_K, tm), f32)
    gsum = jnp.zeros((1, tm), f32)
    for k in range(TOP_K):
        m = jnp.max(sel, axis=0, keepdims=True)
        idx_f = jnp.min(jnp.where(sel == m, erow, float(N_EXPERTS)), axis=0, keepdims=True)
        hot = erow == idx_f
        gk = jnp.sum(jnp.where(hot, scores, 0.0), axis=0, keepdims=True)
        sel = jnp.where(hot, neg, sel)
        hots.append(hot)
        gsum = gsum + gk
        idx_o = _set_row(idx_o, k, idx_f)
        gate_o = _set_row(gate_o, k, gk)
    chosen = jnp.where(sel == neg, 1.0, 0.0)
    gate_ref[...] = gate_o / gsum * ROUTED_SCALE
    idx_ref[...] = idx_o.astype(i32)
    r = lax.broadcasted_iota(i32, (tm, tm), 0)
    c = lax.broadcasted_iota(i32, (tm, tm), 1)
    pos = _dot(chosen.astype(bf16), (r < c).astype(bf16)) + run_s[...]
    rank_o = jnp.zeros((TOP_K, tm), f32)
    for k in range(TOP_K):
        rank_o = _set_row(rank_o, k, jnp.sum(jnp.where(hots[k], pos, 0.0), axis=0, keepdims=True))
    rank_ref[...] = rank_o.astype(i32)
    run_s[...] = run_s[...] + jnp.sum(chosen, axis=1, keepdims=True)
    cnt_ref[...] = run_s[...]


def _route_call(scores_t, rb):
    n = scores_t.shape[1]
    colb = lambda r: pl.BlockSpec((r, ROW_TILE), lambda i: (0, i))
    cnt_spec = pl.BlockSpec((N_EXPERTS, 1), lambda i: (0, 0))
    return pl.pallas_call(
        _route_kernel,
        grid=(n // ROW_TILE,),
        in_specs=[colb(N_EXPERTS), cnt_spec],
        out_specs=[colb(TOP_K), colb(TOP_K), colb(TOP_K), cnt_spec],
        out_shape=[jax.ShapeDtypeStruct((TOP_K, n), i32), jax.ShapeDtypeStruct((TOP_K, n), f32),
                   jax.ShapeDtypeStruct((TOP_K, n), i32), jax.ShapeDtypeStruct((N_EXPERTS, 1), f32)],
        scratch_shapes=[pltpu.VMEM((N_EXPERTS, 1), f32)],
        compiler_params=_cparams(("arbitrary",)),
        name="route",
    )(scores_t, rb)


def _dest_kernel(idx_ref, rank_ref, start_ref, dest_ref):
    tm = idx_ref.shape[1]
    erow = lax.broadcasted_iota(i32, (N_EXPERTS, tm), 0)
    start = jnp.broadcast_to(start_ref[...], (N_EXPERTS, tm))
    idx = idx_ref[...]
    dest = jnp.zeros((TOP_K, tm), f32)
    for k in range(TOP_K):
        hot = erow == idx[k:k + 1, :]
        dest = _set_row(dest, k, jnp.sum(jnp.where(hot, start, 0.0), axis=0, keepdims=True))
    dest_ref[...] = dest.astype(i32) + rank_ref[...]


def _dest_call(idx, rank, start):
    n = idx.shape[1]
    colb = lambda r: pl.BlockSpec((r, ROW_TILE), lambda i: (0, i))
    return pl.pallas_call(
        _dest_kernel,
        grid=(n // ROW_TILE,),
        in_specs=[colb(TOP_K), colb(TOP_K), pl.BlockSpec((N_EXPERTS, 1), lambda i: (0, 0))],
        out_specs=colb(TOP_K),
        out_shape=jax.ShapeDtypeStruct((TOP_K, n), i32),
        compiler_params=_cparams(("parallel",)),
        name="dest",
    )(idx, rank, start)


def _sc_workers():
    info = plsc.get_sparse_core_info()
    mesh = plsc.VectorSubcoreMesh(core_axis_name="c", subcore_axis_name="s")
    worker_id = lambda: lax.axis_index("s") * info.num_cores + lax.axis_index("c")
    return mesh, info.num_cores * info.num_subcores, worker_id


def _scatter_rows_call(rows, dest_km, n_slots):
    n, d = rows.shape
    mesh, n_workers, worker_id = _sc_workers()
    per_worker = n // n_workers
    assert n % (n_workers * SC_CHUNK) == 0

    def body(rows_hbm, idx_hbm, out_hbm, *scratch):
        idx_vs, rows_v, sem = scratch[:TOP_K], scratch[TOP_K], scratch[TOP_K + 1]
        base = worker_id() * per_worker

        @pl.loop(0, per_worker // SC_CHUNK)
        def _(i):
            t0 = pl.multiple_of(base + i * SC_CHUNK, SC_CHUNK)
            pltpu.sync_copy(rows_hbm.at[pl.ds(t0, SC_CHUNK)], rows_v)
            for k in range(TOP_K):
                pltpu.sync_copy(idx_hbm.at[pl.ds(k * n + t0, SC_CHUNK)], idx_vs[k])
            copies = [pltpu.async_copy(rows_v, out_hbm.at[idx_vs[k]], sem) for k in range(TOP_K)]
            for cp in copies:
                cp.wait()

    return pl.kernel(
        body,
        out_type=jax.ShapeDtypeStruct((n_slots, d), rows.dtype),
        mesh=mesh,
        scratch_types=[pltpu.VMEM((SC_CHUNK,), i32)] * TOP_K + [pltpu.VMEM((SC_CHUNK, d), rows.dtype),
                                                                pltpu.SemaphoreType.DMA],
        name="scatter_rows",
    )(rows, dest_km.reshape(-1))


def _expert_kernel(nu_ref, bv_ref, ord_ref, eseq_ref, nex_ref, x_ref, w1_hbm, w3_hbm, w2_hbm, y_ref,
                   w1_f, w3_f, w2_f, w1_s, w3_s, w2_s, sems, *, layer):
    def weight_copies(j):
        e = eseq_ref[j]
        slot = j % W_RING
        return [pltpu.make_async_copy(w1_hbm.at[layer, e], w1_f.at[slot], sems.at[slot, 0]),
                pltpu.make_async_copy(w3_hbm.at[layer, e], w3_f.at[slot], sems.at[slot, 1]),
                pltpu.make_async_copy(w2_hbm.at[layer, e], w2_f.at[slot], sems.at[slot, 2])]

    def start_if_exists(j):
        @pl.when(j < nex_ref[0])
        def _():
            for cp in weight_copies(j):
                cp.start()

    for u in range(STEP_BLOCKS):
        i = pl.program_id(0) * STEP_BLOCKS + u
        rows = slice(u * SLOT_BLOCK, (u + 1) * SLOT_BLOCK)

        @pl.when(i < nu_ref[0])
        def _(i=i, rows=rows):
            j = ord_ref[i]

            @pl.when(i == 0)
            def _():
                for ahead in range(W_RING - 1):
                    start_if_exists(ahead)

            @pl.when((i == 0) | (j != ord_ref[jnp.maximum(i - 1, 0)]))
            def _():
                start_if_exists(j + W_RING - 1)
                for cp in weight_copies(j):
                    cp.wait()
                slot = j % W_RING
                w1_s[...] = w1_f[slot].astype(bf16)
                w3_s[...] = w3_f[slot].astype(bf16)
                w2_s[...] = w2_f[slot].astype(bf16)

            row = lax.broadcasted_iota(i32, (SLOT_BLOCK, D_PACK), 0)
            x_hi, x_lo = _unpack_pairs(jnp.where(row < bv_ref[i], x_ref[rows, :], 0))
            x = jnp.concatenate([x_hi, x_lo], axis=1).astype(bf16)
            act = (_silu(_dot(x, w1_s[...])) * _dot(x, w3_s[...])).astype(bf16)
            y_ref[rows, :] = _pack_pairs(_dot(act, w2_s[...]))

        @pl.when((i >= nu_ref[0]) & (pl.program_id(0) * STEP_BLOCKS < nu_ref[0]))
        def _(rows=rows):
            y_ref[rows, :] = x_ref[rows, :]


def _expert_call(n_used, blk_valid, blk_ord, expert_seq, n_seq_experts, slots, w1, w3, w2, layer):
    step_rows = STEP_BLOCKS * SLOT_BLOCK
    assert slots.shape[0] % step_rows == 0
    blk = lambda s, nu, *_: (jnp.minimum(s, (nu[0] - 1) // STEP_BLOCKS), 0)
    hbm = pl.BlockSpec(memory_space=pl.ANY)
    grid_spec = pltpu.PrefetchScalarGridSpec(
        num_scalar_prefetch=5,
        grid=(slots.shape[0] // step_rows,),
        in_specs=[pl.BlockSpec((step_rows, D_PACK), blk), hbm, hbm, hbm],
        out_specs=pl.BlockSpec((step_rows, D_PACK), blk),
        scratch_shapes=[pltpu.VMEM((W_RING, D_MODEL, EXPERT_FF), f32), pltpu.VMEM((W_RING, D_MODEL, EXPERT_FF), f32),
                        pltpu.VMEM((W_RING, EXPERT_FF, D_MODEL), f32),
                        pltpu.VMEM((D_MODEL, EXPERT_FF), bf16), pltpu.VMEM((D_MODEL, EXPERT_FF), bf16),
                        pltpu.VMEM((EXPERT_FF, D_MODEL), bf16),
                        pltpu.SemaphoreType.DMA((W_RING, 3))],
    )
    return pl.pallas_call(
        functools.partial(_expert_kernel, layer=layer),
        grid_spec=grid_spec,
        out_shape=jax.ShapeDtypeStruct(slots.shape, slots.dtype),
        input_output_aliases={5: 0},
        compiler_params=_cparams(("arbitrary",)),
        name="experts",
    )(n_used, blk_valid, blk_ord, expert_seq, n_seq_experts, slots, w1, w3, w2)


def _gather_rows_call(table, idx):
    n_idx = idx.shape[0]
    d = table.shape[1]
    mesh, n_workers, worker_id = _sc_workers()
    per_worker = n_idx // n_workers
    assert n_idx % (n_workers * SC_CHUNK) == 0

    n_chunks = per_worker // SC_CHUNK
    assert n_chunks % 2 == 0

    def body(table_hbm, idx_hbm, out_hbm, idx_a, idx_b, rows_a, rows_b, sem_a, sem_b):
        base = worker_id() * per_worker
        bufs = ((idx_a, rows_a, sem_a), (idx_b, rows_b, sem_b))

        def start_gather(c, buf):
            idx_v, rows_v, sem = buf
            off = pl.multiple_of(base + c * SC_CHUNK, SC_CHUNK)
            pltpu.sync_copy(idx_hbm.at[pl.ds(off, SC_CHUNK)], idx_v)
            pltpu.async_copy(table_hbm.at[idx_v], rows_v, sem)

        start_gather(0, bufs[0])

        @pl.loop(0, n_chunks, step=2)
        def _(c0):
            for b in range(2):
                c = c0 + b
                idx_v, rows_v, sem = bufs[b]

                @pl.when(c + 1 < n_chunks)
                def _():
                    start_gather(c + 1, bufs[1 - b])

                pltpu.make_async_copy(table_hbm.at[idx_v], rows_v, sem).wait()
                off = pl.multiple_of(base + c * SC_CHUNK, SC_CHUNK)
                pltpu.sync_copy(rows_v, out_hbm.at[pl.ds(off, SC_CHUNK)])

    return pl.kernel(
        body,
        out_type=jax.ShapeDtypeStruct((n_idx, d), table.dtype),
        mesh=mesh,
        scratch_types=[pltpu.VMEM((SC_CHUNK,), i32), pltpu.VMEM((SC_CHUNK,), i32),
                       pltpu.VMEM((SC_CHUNK, d), table.dtype), pltpu.VMEM((SC_CHUNK, d), table.dtype),
                       pltpu.SemaphoreType.DMA, pltpu.SemaphoreType.DMA],
        name="gather_rows",
    )(table, idx)


def _combine_kernel(gate_ref, xb_ref, mod_ref, y_ref, o_ref):
    g2 = mod_ref[0][:, 5 * D_MODEL:6 * D_MODEL]
    gates = gate_ref[...]
    acc_hi = jnp.zeros((gates.shape[0], D_PACK), f32)
    acc_lo = jnp.zeros((gates.shape[0], D_PACK), f32)
    for k in range(TOP_K):
        y_hi, y_lo = _unpack_pairs(y_ref[k])
        acc_hi = acc_hi + gates[:, k:k + 1] * y_hi
        acc_lo = acc_lo + gates[:, k:k + 1] * y_lo
    o_ref[:, 0:D_PACK] = xb_ref[:, 0:D_PACK] + g2[:, 0:D_PACK] * acc_hi
    o_ref[:, D_PACK:D_MODEL] = xb_ref[:, D_PACK:D_MODEL] + g2[:, D_PACK:D_MODEL] * acc_lo


def _combine_call(gates, xbase, mod, ygath, seq_row_c, row0):
    tm = COMBINE_TILE
    n_rows = ygath.shape[1]
    assert row0 % tm == 0 and n_rows % tm == 0
    b0 = row0 // tm
    return pl.pallas_call(
        _combine_kernel,
        grid=(n_rows // tm,),
        in_specs=[pl.BlockSpec((tm, TOP_K), lambda i: (b0 + i, 0)),
                  pl.BlockSpec((tm, D_MODEL), lambda i: (b0 + i, 0)),
                  pl.BlockSpec((1, 1, 6 * D_MODEL), lambda i: (seq_row_c(b0 + i), 0, 0)),
                  pl.BlockSpec((TOP_K, tm, D_PACK), lambda i: (0, i, 0))],
        out_specs=pl.BlockSpec((tm, D_MODEL), lambda i: (i, 0)),
        out_shape=jax.ShapeDtypeStruct((n_rows, D_MODEL), f32),
        compiler_params=_cparams(("parallel",)),
        name="combine",
    )(gates, xbase, mod, ygath)


def _block_avg(width, group):
    r = np.arange(width)
    return jnp.asarray((r[:, None] // group == r[None, :] // group).astype(np.float32) / group, dtype=bf16)


def _kv_replicate():
    c = np.arange(ATTN_W)
    src = (c // (ATTN_W // N_KV_HEADS)) * HEAD_DIM + c % HEAD_DIM
    return jnp.asarray((np.arange(KV_W)[:, None] == src[None, :]).astype(np.float32), dtype=bf16)


def _rope_lane_tables(n_tok):
    rows = n_tok // GRID_W
    r = jnp.repeat(jnp.arange(rows, dtype=f32), GRID_W)
    col = jnp.tile(jnp.arange(GRID_W, dtype=f32), rows)
    inv = ROPE_THETA ** (-jnp.arange(ROPE_FREQ, dtype=f32) / ROPE_FREQ)
    ar = r[:, None] * inv
    ac = col[:, None] * inv
    cos_h = jnp.concatenate([jnp.cos(ar), jnp.cos(ar), jnp.cos(ac), jnp.cos(ac)], axis=-1)
    sin_h = jnp.concatenate([-jnp.sin(ar), jnp.sin(ar), -jnp.sin(ac), jnp.sin(ac)], axis=-1)
    return jnp.tile(cos_h, (1, N_HEADS)), jnp.tile(sin_h, (1, N_HEADS))


def _block_diag(w):
    nb, bw, _ = w.shape
    eye = jnp.eye(nb, dtype=w.dtype)
    return (w[:, :, None, :] * eye[:, None, :, None]).reshape(nb * bw, nb * bw)


def _gla_state_in(s):
    bsz = s.shape[0]
    eye = jnp.eye(GLA_H, dtype=s.dtype)
    st = jnp.swapaxes(s, -1, -2)
    big = st[:, :, :, :, None, :] * eye[None, None, :, None, :, None]
    return big.reshape(bsz, 2, GLA_W, GLA_W)


def kernel(x_prompt, x_sample, cache_k, cache_v, state_lru, state_gla, c, c_ctx, ada_w, ada_b, norm1_g, norm2_g, w_in, q_norm_g, k_norm_g, attn_out_g, conv_w, conv_b, lru_wa, lru_ba, lru_wi, lru_bi, lru_lambda, lru_out_g, gla_wa2, gla_ba, gla_out_g, w_out, router_w, router_b, exp_w1, exp_w3, exp_w2, sh_w1, sh_w3, sh_w2):
    bc, tc, _ = x_prompt.shape
    bl, tl, _ = x_sample.shape
    depth = w_in.shape[0]
    nc = bc * tc
    n = nc + bl * tl
    past = cache_k.shape[2]
    assert tc == ROW_TILE and tl % ROW_TILE == 0 and nc % tl == 0 and bl + 1 <= SUBLANES
    assert nc % PROJ_TILE == 0 and tl % PROJ_TILE == 0 and nc % COMBINE_TILE == 0 and tl % COMBINE_TILE == 0

    def seq_row_for(tile):
        def seq_row(i):
            return jnp.where(i < nc // tile, 0, 1 + (i - nc // tile) // (tl // tile))
        return seq_row

    seq_row = seq_row_for(PROJ_TILE)
    seq_row_c = seq_row_for(COMBINE_TILE)

    xc = x_prompt.reshape(nc, D_MODEL)
    xl = x_sample.reshape(bl * tl, D_MODEL)
    cond = jnp.zeros((SUBLANES, D_MODEL), f32).at[0].set(c_ctx).at[1:1 + bl].set(c)
    mods = _ada_call(cond, ada_w, ada_b)

    bmq = _block_avg(ATTN_W, HEAD_DIM)
    bmk = _block_avg(KV_W, HEAD_DIM)
    bmg = _block_avg(GLA_W, GLA_DK)
    rep = _kv_replicate()
    cos_t, sin_t = _rope_lane_tables(tl)
    step_rows = STEP_BLOCKS * SLOT_BLOCK
    n_slots = -(-(n * TOP_K + N_EXPERTS * (SLOT_BLOCK - 1)) // step_rows) * step_rows
    tile8 = lambda v: jnp.tile(v, N_HEADS)[None, :]

    ks, vs, lrus, glas = [], [], [], []
    for l in range(depth):
        mod = mods[l].reshape(SUBLANES, 1, 6 * D_MODEL)
        pa, pl_, pg = _proj_call(xc, xl, mod, norm1_g[l][None, :], w_in, l, seq_row)

        qg, kg, og = tile8(q_norm_g[l]), jnp.tile(k_norm_g[l], N_KV_HEADS)[None, :], attn_out_g[l][None, :]
        attn_c, k_new, v_new = _attn_ctx_call(pa, bc, tc, qg, kg, og, bmq, bmk, rep)
        attn_l = _attn_lat_call(pa, nc, bl, tl, cache_k[:, l].reshape(bl, past, KV_W),
                                cache_v[:, l].reshape(bl, past, KV_W), cos_t, sin_t, qg, kg, og, bmq, bmk, rep)
        ks.append(k_new.reshape(bc, tc, N_KV_HEADS, HEAD_DIM))
        vs.append(v_new.reshape(bc, tc, N_KV_HEADS, HEAD_DIM))

        wg = jnp.concatenate([_block_diag(lru_wa[l, 0]), _block_diag(lru_wi[l, 0]),
                              _block_diag(lru_wa[l, 1]), _block_diag(lru_wi[l, 1])], axis=1).astype(bf16)
        bg = jnp.concatenate([lru_ba[l, 0], lru_bi[l, 0], lru_ba[l, 1], lru_bi[l, 1]])[None, :]
        lru_args = (conv_w[l], conv_b[l][None, :], wg, bg, lru_lambda[l], lru_out_g[l][None, :])
        lru_c, lru_fin = _lru_call(pl_, 0, bc, tc, jnp.zeros((bc, 2, LRU_W), f32), *lru_args)
        lru_l, _ = _lru_call(pl_, nc, bl, tl, state_lru[:, l], *lru_args)
        lrus.append(lru_fin)

        wa = jnp.zeros((LANES, 2 * GLA_W), f32)
        wa = wa.at[0:GLA_RANK, 0:GLA_W].set(gla_wa2[l, 0]).at[GLA_RANK:2 * GLA_RANK, GLA_W:].set(gla_wa2[l, 1])
        gla_args = (wa.astype(bf16), gla_ba[l].reshape(1, 2 * GLA_W), gla_out_g[l].reshape(1, GLA_W), bmg)
        gla_c, gla_fin = _gla_call(pg, 0, bc, tc, None, *gla_args)
        gla_l, _ = _gla_call(pg, nc, bl, tl, _gla_state_in(state_gla[:, l]), *gla_args)
        glas.append(gla_fin)

        rw_t = router_w[l].T
        rw_hi = rw_t.astype(bf16)
        rw_lo = (rw_t - rw_hi.astype(f32)).astype(bf16)
        xbase, h2, scores_t = _out_call(xc, xl, (attn_c, lru_c, gla_c), (attn_l, lru_l, gla_l), mod, w_out[l].astype(bf16),
                                        norm2_g[l][None, :], rw_hi, rw_lo, sh_w1[l].astype(bf16),
                                        sh_w3[l].astype(bf16), sh_w2[l].astype(bf16), seq_row)

        idx, gates, rank, counts = _route_call(scores_t, router_b[l][:, None])
        cnt = counts[:, 0].astype(i32)
        padded = (cnt + SLOT_BLOCK - 1) // SLOT_BLOCK * SLOT_BLOCK
        padded_end = jnp.cumsum(padded)
        dest_km = _dest_call(idx, rank, (padded_end - padded).astype(f32)[:, None])
        nb = n_slots // SLOT_BLOCK
        blk_first = jnp.arange(nb, dtype=i32) * SLOT_BLOCK
        blk_e = jnp.minimum(jnp.sum((padded_end[None, :] <= blk_first[:, None]).astype(i32), axis=1), N_EXPERTS - 1)
        blk_valid = jnp.clip((padded_end - padded + cnt)[blk_e] - blk_first, 0, SLOT_BLOCK)
        n_used = padded_end[-1:] // SLOT_BLOCK
        owns = jnp.cumsum((cnt > 0).astype(i32))
        blk_ord = (owns - 1)[blk_e]
        expert_seq = jnp.minimum(jnp.sum((owns[None, :] <= jnp.arange(N_EXPERTS, dtype=i32)[:, None]).astype(i32),
                                         axis=1), N_EXPERTS - 1)
        slots = _scatter_rows_call(h2, dest_km, n_slots)
        slots = _expert_call(n_used, blk_valid, blk_ord, expert_seq, owns[-1:], slots, exp_w1, exp_w3, exp_w2, l)
        gates_t = gates.T
        yg_c = _gather_rows_call(slots, dest_km[:, :nc].reshape(-1)).reshape(TOP_K, nc, D_PACK)
        yg_l = _gather_rows_call(slots, dest_km[:, nc:].reshape(-1)).reshape(TOP_K, n - nc, D_PACK)
        xc = _combine_call(gates_t, xbase, mod, yg_c, seq_row_c, 0)
        xl = _combine_call(gates_t, xbase, mod, yg_l, seq_row_c, nc)

    y_prompt = xc.reshape(bc, tc, D_MODEL)
    y_sample = xl.reshape(bl, tl, D_MODEL)
    return (y_prompt, y_sample, jnp.stack(ks, axis=1), jnp.stack(vs, axis=1),
            jnp.stack(lrus, axis=1), jnp.stack(glas, axis=1))
```

```python
import functools

import jax
import jax.numpy as jnp
import numpy as np
from jax import lax
from jax.experimental import pallas as pl
from jax.experimental.pallas import tpu as pltpu
from jax.experimental.pallas import tpu_sc as plsc

f32 = jnp.float32
bf16 = jnp.bfloat16
i32 = jnp.int32

D_MODEL = 1024
N_HEADS = 8
N_KV_HEADS = 2
HEAD_DIM = 64
ATTN_W = N_HEADS * HEAD_DIM
KV_W = N_KV_HEADS * HEAD_DIM
GRID_W = 64
ROPE_FREQ = HEAD_DIM // 4
ROPE_THETA = 10000.0
LRU_W = 256
LRU_BLOCKS = 4
LRU_C = 8.0
CONV_W = 4
CONV_LEFT = 2
GLA_H = 4
GLA_DK = 64
GLA_W = 256
GLA_RANK = 16
GLA_TAU = 16.0
N_EXPERTS = 256
TOP_K = 8
EXPERT_FF = 256
ROUTED_SCALE = 2.5
EPS = 1e-6

LANES = 128
SUBLANES = 8
ROW_TILE = 256
ROUTE_TILE = 512
PROJ_TILE = 512
GLA_BLOCK = 256
SLOT_BLOCK = 256
STEP_BLOCKS = 4
W_RING = 4
COMBINE_TILE = 512
SC_CHUNK = 64
D_PACK = D_MODEL // 2
VMEM_LIMIT = 56 * 1024 * 1024


def _cparams(sem, vmem=VMEM_LIMIT):
    return pltpu.CompilerParams(dimension_semantics=sem, vmem_limit_bytes=vmem)


def _dot(a, b):
    return jnp.dot(a, b, preferred_element_type=f32)


def _dot_nt(a, b):
    return lax.dot_general(a, b, (((1,), (1,)), ((), ())), preferred_element_type=f32)


def _dot_tn(a, b):
    return lax.dot_general(a, b, (((0,), (0,)), ((), ())), preferred_element_type=f32)


def _split(x):
    hi = x.astype(bf16)
    lo = (x - hi.astype(f32)).astype(bf16)
    return hi, lo


def _dot_x2(x, w):
    hi, lo = _split(x)
    return _dot(hi, w) + _dot(lo, w)


def _dot_2x(m, x):
    hi, lo = _split(x)
    return _dot(m, hi) + _dot(m, lo)


def _dot3(a, b_hi, b_lo):
    a_hi, a_lo = _split(a)
    return _dot(a_hi, b_hi) + _dot(a_lo, b_hi) + _dot(a_hi, b_lo)


def _sigmoid(x):
    return 1.0 / (1.0 + jnp.exp(-x))


def _silu(x):
    return x * _sigmoid(x)


def _softplus(x):
    return jnp.maximum(x, 0.0) + jnp.log(1.0 + jnp.exp(-jnp.abs(x)))


def _gelu_tanh(x):
    return 0.5 * x * (1.0 + jnp.tanh(0.7978845608028654 * (x + 0.044715 * x * x * x)))


def _rms(x, g):
    return x * lax.rsqrt(jnp.mean(x * x, axis=-1, keepdims=True) + EPS) * g


def _pack_pairs(x):
    c = x.shape[1] // 2
    hi = lax.bitcast_convert_type(x[:, :c].astype(bf16).astype(f32), i32)
    lo = lax.bitcast_convert_type(x[:, c:].astype(bf16).astype(f32), i32)
    return hi | lax.shift_right_logical(lo, 16)


def _unpack_pairs(w):
    hi = lax.bitcast_convert_type(w & jnp.int32(-65536), f32)
    lo = lax.bitcast_convert_type(w << 16, f32)
    return hi, lo


def _head_rms(x, g, bm):
    ms = _dot_x2(x * x, bm)
    return x * lax.rsqrt(ms + EPS) * g


def _ada_kernel(c_ref, w_ref, b_ref, o_ref):
    s = _silu(c_ref[...])
    w = w_ref[0]
    w_hi, w_lo = _split(w)
    o_ref[0] = _dot3(s, w_hi, w_lo) + b_ref[0]


def _ada_call(cond, ada_w, ada_b):
    depth = ada_w.shape[0]
    nt = 1536
    return pl.pallas_call(
        _ada_kernel,
        grid=(depth, 6 * D_MODEL // nt),
        in_specs=[pl.BlockSpec((SUBLANES, D_MODEL), lambda l, j: (0, 0)),
                  pl.BlockSpec((1, D_MODEL, nt), lambda l, j: (l, 0, j)),
                  pl.BlockSpec((1, 1, nt), lambda l, j: (l, 0, j))],
        out_specs=pl.BlockSpec((1, SUBLANES, nt), lambda l, j: (l, 0, j)),
        out_shape=jax.ShapeDtypeStruct((depth, SUBLANES, 6 * D_MODEL), f32),
        compiler_params=_cparams(("parallel", "parallel")),
        name="ada_mod",
    )(cond, ada_w, ada_b.reshape(depth, 1, 6 * D_MODEL))


PROJ_A = ATTN_W + 2 * KV_W
PROJ_L = 2 * LRU_W
PROJ_G = 4 * GLA_W + LANES
IN_COLS = PROJ_A + PROJ_L + 4 * GLA_W + 2 * GLA_RANK


def _proj_kernel(xc_ref, xl_ref, mod_ref, g_ref, w_ref, oa_ref, ol_ref, og_ref, w_s, *, n_ctx_tiles):
    @pl.when(pl.program_id(0) == 0)
    def _():
        w_s[:, 0:IN_COLS] = w_ref[0].astype(bf16)
        w_s[:, IN_COLS:] = jnp.zeros((D_MODEL, w_s.shape[1] - IN_COLS), bf16)

    mod = mod_ref[0]
    sh = mod[:, 0:D_MODEL]
    sc = mod[:, D_MODEL:2 * D_MODEL]
    x = jnp.where(pl.program_id(0) < n_ctx_tiles, xc_ref[...], xl_ref[...])
    h = (_rms(x, g_ref[...]) * (1.0 + sc) + sh).astype(bf16)
    p = _dot(h, w_s[...])
    oa_ref[...] = p[:, 0:PROJ_A].astype(bf16)
    ol_ref[...] = p[:, PROJ_A:PROJ_A + PROJ_L].astype(bf16)
    og_ref[...] = p[:, PROJ_A + PROJ_L:PROJ_A + PROJ_L + PROJ_G].astype(bf16)


def _row_pair_specs(nct, width):
    return (pl.BlockSpec((PROJ_TILE, width), lambda i: (jnp.minimum(i, nct - 1), 0)),
            pl.BlockSpec((PROJ_TILE, width), lambda i: (jnp.maximum(i - nct, 0), 0)))


def _proj_call(xc, xl, mod, g, w_all, layer, seq_row):
    n = xc.shape[0] + xl.shape[0]
    nct = xc.shape[0] // PROJ_TILE
    cols = PROJ_A + PROJ_L + PROJ_G
    return pl.pallas_call(
        functools.partial(_proj_kernel, n_ctx_tiles=nct),
        grid=(n // PROJ_TILE,),
        in_specs=[*_row_pair_specs(nct, D_MODEL),
                  pl.BlockSpec((1, 1, 6 * D_MODEL), lambda i: (seq_row(i), 0, 0)),
                  pl.BlockSpec((1, D_MODEL), lambda i: (0, 0)),
                  pl.BlockSpec((1, D_MODEL, IN_COLS), lambda i: (layer, 0, 0))],
        out_specs=[pl.BlockSpec((PROJ_TILE, PROJ_A), lambda i: (i, 0)),
                   pl.BlockSpec((PROJ_TILE, PROJ_L), lambda i: (i, 0)),
                   pl.BlockSpec((PROJ_TILE, PROJ_G), lambda i: (i, 0))],
        out_shape=[jax.ShapeDtypeStruct((n, PROJ_A), bf16),
                   jax.ShapeDtypeStruct((n, PROJ_L), bf16),
                   jax.ShapeDtypeStruct((n, PROJ_G), bf16)],
        scratch_shapes=[pltpu.VMEM((D_MODEL, cols), bf16)],
        compiler_params=_cparams(("arbitrary",)),
        name="in_proj",
    )(xc, xl, mod, g, w_all)


def _rope(x, cos_t, sin_t):
    w = x.shape[1]
    up = pltpu.roll(x, w - ROPE_FREQ, 1)
    dn = pltpu.roll(x, ROPE_FREQ, 1)
    lane = lax.broadcasted_iota(i32, x.shape, 1)
    partner = jnp.where((lane & (2 * ROPE_FREQ - 1)) < ROPE_FREQ, up, dn)
    return x * cos_t + partner * sin_t


def _attend(q, kk_ref, vv_ref, o_ref):
    tq = q.shape[0]
    gw = ATTN_W // N_KV_HEADS
    lane = lax.broadcasted_iota(i32, (tq, gw), 1)
    for g in range(N_KV_HEADS):
        qg = q[:, g * gw:(g + 1) * gw]
        kg = kk_ref[:, g * gw:(g + 1) * gw]
        vg = vv_ref[:, g * gw:(g + 1) * gw]
        acc = jnp.zeros((tq, gw), f32)
        for hh in range(N_HEADS // N_KV_HEADS):
            hm = (lane >> 6) == hh
            s = _dot_nt(jnp.where(hm, qg, 0.0).astype(bf16), kg)
            m = jnp.max(s, axis=-1, keepdims=True)
            p = jnp.exp(s - m)
            l = jnp.sum(p, axis=-1, keepdims=True)
            o = _dot(p.astype(bf16), vg) / l
            acc = jnp.where(hm, o, acc)
        o_ref[:, g * gw:(g + 1) * gw] = acc


def _attn_ctx_kernel(p_ref, qg_ref, kg_ref, og_ref, bmq_ref, bmk_ref, rep_ref,
                     o_ref, ko_ref, vo_ref, kk_s, vv_s, o_s):
    p = p_ref[...].astype(f32)
    q = _head_rms(p[:, 0:ATTN_W], qg_ref[...], bmq_ref[...])
    k = _head_rms(p[:, ATTN_W:ATTN_W + KV_W], kg_ref[...], bmk_ref[...])
    v = p[:, ATTN_W + KV_W:PROJ_A]
    ko_ref[...] = k
    vo_ref[...] = v
    kk_s[...] = _dot(k.astype(bf16), rep_ref[...]).astype(bf16)
    vv_s[...] = _dot(v.astype(bf16), rep_ref[...]).astype(bf16)
    _attend(q * HEAD_DIM ** -0.5, kk_s, vv_s, o_s)
    o_ref[...] = _rms(o_s[...], og_ref[...]).astype(bf16)


def _attn_ctx_call(pa, n_seq, t, qg, kg, og, bmq, bmk, rep):
    assert t == ROW_TILE
    const = lambda i: (0, 0)
    return pl.pallas_call(
        _attn_ctx_kernel,
        grid=(n_seq,),
        in_specs=[pl.BlockSpec((t, PROJ_A), lambda i: (i, 0)),
                  pl.BlockSpec((1, ATTN_W), const), pl.BlockSpec((1, KV_W), const), pl.BlockSpec((1, ATTN_W), const),
                  pl.BlockSpec((ATTN_W, ATTN_W), const), pl.BlockSpec((KV_W, KV_W), const),
                  pl.BlockSpec((KV_W, ATTN_W), const)],
        out_specs=[pl.BlockSpec((t, ATTN_W), lambda i: (i, 0)),
                   pl.BlockSpec((t, KV_W), lambda i: (i, 0)),
                   pl.BlockSpec((t, KV_W), lambda i: (i, 0))],
        out_shape=[jax.ShapeDtypeStruct((n_seq * t, ATTN_W), bf16),
                   jax.ShapeDtypeStruct((n_seq * t, KV_W), f32),
                   jax.ShapeDtypeStruct((n_seq * t, KV_W), f32)],
        scratch_shapes=[pltpu.VMEM((t, ATTN_W), bf16), pltpu.VMEM((t, ATTN_W), bf16), pltpu.VMEM((t, ATTN_W), f32)],
        compiler_params=_cparams(("parallel",)),
        name="attn_ctx",
    )(pa, qg, kg, og, bmq, bmk, rep)


def _attn_lat_kernel(q_ref, kv_ref, ck_ref, cv_ref, cq_ref, sq_ref, ckk_ref, skk_ref,
                     qg_ref, kg_ref, og_ref, bmq_ref, bmk_ref, rep_ref,
                     o_ref, kk_s, vv_s, o_s, *, past):
    @pl.when(pl.program_id(1) == 0)
    def _():
        kv = kv_ref[...].astype(f32)
        k = _head_rms(kv[:, 0:KV_W], kg_ref[...], bmk_ref[...])
        k = _rope(k, ckk_ref[...], skk_ref[...])
        v = kv[:, KV_W:2 * KV_W]
        kk_s[0:past, :] = _dot(ck_ref[0].astype(bf16), rep_ref[...]).astype(bf16)
        vv_s[0:past, :] = _dot(cv_ref[0].astype(bf16), rep_ref[...]).astype(bf16)
        kk_s[past:, :] = _dot(k.astype(bf16), rep_ref[...]).astype(bf16)
        vv_s[past:, :] = _dot(v.astype(bf16), rep_ref[...]).astype(bf16)

    q = _head_rms(q_ref[...].astype(f32), qg_ref[...], bmq_ref[...])
    q = _rope(q, cq_ref[...], sq_ref[...])
    _attend(q * HEAD_DIM ** -0.5, kk_s, vv_s, o_s)
    o_ref[...] = _rms(o_s[...], og_ref[...]).astype(bf16)


def _attn_lat_call(pa, row0, n_seq, t, cache_k, cache_v, cq, sq, qg, kg, og, bmq, bmk, rep):
    past = cache_k.shape[1]
    nq = t // ROW_TILE
    assert row0 % t == 0 and t % ROW_TILE == 0
    qb0 = row0 // ROW_TILE
    sb0 = row0 // t
    const = lambda b, j: (0, 0)
    return pl.pallas_call(
        functools.partial(_attn_lat_kernel, past=past),
        grid=(n_seq, nq),
        in_specs=[pl.BlockSpec((ROW_TILE, ATTN_W), lambda b, j: (qb0 + b * nq + j, 0)),
                  pl.BlockSpec((t, 2 * KV_W), lambda b, j: (sb0 + b, ATTN_W // (2 * KV_W))),
                  pl.BlockSpec((1, past, KV_W), lambda b, j: (b, 0, 0)),
                  pl.BlockSpec((1, past, KV_W), lambda b, j: (b, 0, 0)),
                  pl.BlockSpec((ROW_TILE, ATTN_W), lambda b, j: (j, 0)),
                  pl.BlockSpec((ROW_TILE, ATTN_W), lambda b, j: (j, 0)),
                  pl.BlockSpec((t, KV_W), const), pl.BlockSpec((t, KV_W), const),
                  pl.BlockSpec((1, ATTN_W), const), pl.BlockSpec((1, KV_W), const), pl.BlockSpec((1, ATTN_W), const),
                  pl.BlockSpec((ATTN_W, ATTN_W), const), pl.BlockSpec((KV_W, KV_W), const),
                  pl.BlockSpec((KV_W, ATTN_W), const)],
        out_specs=pl.BlockSpec((ROW_TILE, ATTN_W), lambda b, j: (b * nq + j, 0)),
        out_shape=jax.ShapeDtypeStruct((n_seq * t, ATTN_W), bf16),
        scratch_shapes=[pltpu.VMEM((past + t, ATTN_W), bf16), pltpu.VMEM((past + t, ATTN_W), bf16),
                        pltpu.VMEM((ROW_TILE, ATTN_W), f32)],
        compiler_params=_cparams(("parallel", "arbitrary")),
        name="attn_lat",
    )(pa, pa, cache_k, cache_v, cq, sq, cq, sq, qg, kg, og, bmq, bmk, rep)


def _lru_kernel(x_ref, h0_ref, cw_ref, cb_ref, wg_ref, bg_ref, lam_ref, g_ref,
                o_ref, fin_ref, xs, a_s, u_s, h_s, *, t):
    ng = t // SUBLANES
    lx = x_ref[:, 0:LRU_W].astype(f32)
    ly = x_ref[:, LRU_W:2 * LRU_W].astype(f32)
    xs[0:SUBLANES, :] = jnp.zeros((SUBLANES, LRU_W), f32)
    xs[SUBLANES + t:2 * SUBLANES + t, :] = jnp.zeros((SUBLANES, LRU_W), f32)
    xs[SUBLANES:SUBLANES + t, :] = lx
    xc = jnp.broadcast_to(cb_ref[...], (t, LRU_W))
    for j in range(CONV_W):
        off = SUBLANES + j - CONV_LEFT
        xc = xc + xs[off:off + t, :] * cw_ref[j:j + 1, :]
    gates = _sigmoid(_dot(xc.astype(bf16), wg_ref[...]) + bg_ref[...])
    sp = _softplus(-lam_ref[...])
    sub = lax.broadcasted_iota(i32, (ng, SUBLANES, LRU_W), 1)
    for d in range(2):
        r = gates[:, (2 * d) * LRU_W:(2 * d + 1) * LRU_W]
        gi = gates[:, (2 * d + 1) * LRU_W:(2 * d + 2) * LRU_W]
        log_a = -LRU_C * r * sp[d:d + 1, :]
        a = jnp.exp(log_a)
        th = jnp.tanh(log_a)
        u = jnp.sqrt(-2.0 * th / (1.0 - th)) * (gi * xc)
        a = a.reshape(ng, SUBLANES, LRU_W)
        u = u.reshape(ng, SUBLANES, LRU_W)
        for s in (1, 2, 4):
            if d == 0:
                a_sh = pltpu.roll(a, s, 1)
                u_sh = pltpu.roll(u, s, 1)
                ok = sub >= s
            else:
                a_sh = pltpu.roll(a, SUBLANES - s, 1)
                u_sh = pltpu.roll(u, SUBLANES - s, 1)
                ok = sub < SUBLANES - s
            u = jnp.where(ok, a * u_sh + u, u)
            a = jnp.where(ok, a * a_sh, a)
        a_s[...] = a.reshape(t, LRU_W)
        u_s[...] = u.reshape(t, LRU_W)
        h0 = jnp.broadcast_to(h0_ref[0, d:d + 1, :], (SUBLANES, LRU_W))
        edge = SUBLANES - 1 if d == 0 else 0

        def body(i, carry, d=d, edge=edge):
            g = i if d == 0 else ng - 1 - i
            rows = pl.ds(pl.multiple_of(g * SUBLANES, SUBLANES), SUBLANES)
            h = a_s[rows, :] * carry + u_s[rows, :]
            if d == 0:
                h_s[rows, :] = h
            else:
                h_s[rows, :] = h_s[rows, :] + h
            return jnp.broadcast_to(h[edge:edge + 1, :], (SUBLANES, LRU_W))

        last = lax.fori_loop(0, ng, body, h0)
        fin_ref[0, d:d + 1, :] = last[0:1, :]
    o_ref[...] = _rms(h_s[...] * _gelu_tanh(ly), g_ref[...]).astype(bf16)


def _lru_call(pl_, row0, n_seq, t, h0, cw, cb, wg, bg, lam, g):
    sb0 = row0 // t
    assert row0 % t == 0
    const = lambda b: (0, 0)
    in_specs = [pl.BlockSpec((t, PROJ_L), lambda b: (sb0 + b, 0)),
                pl.BlockSpec((1, 2, LRU_W), lambda b: (b, 0, 0)),
                pl.BlockSpec((CONV_W, LRU_W), const), pl.BlockSpec((1, LRU_W), const),
                pl.BlockSpec((LRU_W, 4 * LRU_W), const), pl.BlockSpec((1, 4 * LRU_W), const),
                pl.BlockSpec((2, LRU_W), const), pl.BlockSpec((1, LRU_W), const)]
    args = [pl_, h0, cw, cb, wg, bg, lam, g]
    return pl.pallas_call(
        functools.partial(_lru_kernel, t=t),
        grid=(n_seq,),
        in_specs=in_specs,
        out_specs=[pl.BlockSpec((t, LRU_W), lambda b: (b, 0)),
                   pl.BlockSpec((1, 2, LRU_W), lambda b: (b, 0, 0))],
        out_shape=[jax.ShapeDtypeStruct((n_seq * t, LRU_W), bf16), jax.ShapeDtypeStruct((n_seq, 2, LRU_W), f32)],
        scratch_shapes=[pltpu.VMEM((t + 2 * SUBLANES, LRU_W), f32), pltpu.VMEM((t, LRU_W), f32),
                        pltpu.VMEM((t, LRU_W), f32), pltpu.VMEM((t, LRU_W), f32)],
        compiler_params=_cparams(("parallel",)),
        name="rglru_t%d" % t,
    )(*args)


def _bcast_rows(b, period, off):
    w = b.shape[1]
    return jnp.concatenate(
        [jnp.broadcast_to(b[i * period + off:i * period + off + 1, :], (period, w)) for i in range(b.shape[0] // period)],
        axis=0)


def _gla_block(q, k, v, la, reverse, st_ref):
    n = GLA_BLOCK
    row = lax.broadcasted_iota(i32, (n, n), 0)
    col = lax.broadcasted_iota(i32, (n, n), 1)
    same64 = (row >> 6) == (col >> 6)
    same32 = (row >> 5) == (col >> 5)
    same16 = (row >> 4) == (col >> 4)
    if not reverse:
        cum = (same64 & (col <= row)).astype(bf16)
        m1 = same64 & ((row & 63) >= 32) & ((col & 63) < 32)
        m2 = same32 & ((row & 31) >= 16) & ((col & 31) < 16)
        m3 = same16 & (col <= row)
        offs = (31, 15, 7, 63)
    else:
        cum = (same64 & (col >= row)).astype(bf16)
        m1 = same64 & ((row & 63) < 32) & ((col & 63) >= 32)
        m2 = same32 & ((row & 31) < 16) & ((col & 31) >= 16)
        m3 = same16 & (col >= row)
        offs = (32, 16, 8, 0)
    b = _dot_2x(cum, la)
    r1 = _bcast_rows(b, 64, offs[0])
    r2 = _bcast_rows(b, 32, offs[1])
    r3 = _bcast_rows(b, 16, offs[2])
    bl = _bcast_rows(b, 64, offs[3])
    q1 = q * jnp.exp(jnp.minimum(b - r1, 0.0))
    k1 = (k * jnp.exp(jnp.minimum(r1 - b, 0.0))).astype(bf16)
    q2 = q * jnp.exp(jnp.minimum(b - r2, 0.0))
    k2 = (k * jnp.exp(jnp.minimum(r2 - b, 0.0))).astype(bf16)
    q3 = q * jnp.exp(b - r3)
    k3 = (k * jnp.exp(r3 - b)).astype(bf16)
    qe = (q * jnp.exp(b)).astype(bf16)
    kl = (k * jnp.exp(bl - b)).astype(bf16)
    dec = jnp.exp(bl)
    vb = v.astype(bf16)
    zq = jnp.zeros((n, n), f32)
    intra = jnp.zeros((n, n), f32)
    for h in range(GLA_H):
        hm = (col >> 6) == h
        a1 = _dot_nt(jnp.where(hm, q1, zq).astype(bf16), k1)
        a2 = _dot_nt(jnp.where(hm, q2, zq).astype(bf16), k2)
        a3 = _dot_nt(jnp.where(hm, q3, zq).astype(bf16), k3)
        att = jnp.where(m1, a1, jnp.where(m2, a2, jnp.where(m3, a3, zq)))
        intra = intra + _dot(att.astype(bf16), jnp.where(hm, v, zq).astype(bf16))
    outs = [None] * 4
    for c in (range(4) if not reverse else range(3, -1, -1)):
        rs = slice(64 * c, 64 * c + 64)
        st = st_ref[...]
        inter = _dot_nt(qe[rs], st.astype(bf16))
        kv = _dot_tn(vb[rs], kl[rs])
        drow = dec[64 * c:64 * c + 1, :]
        st_ref[...] = st * drow + jnp.where(same64, kv, zq)
        outs[c] = intra[rs] + inter
    return jnp.concatenate(outs, axis=0)


def _gla_kernel(x_ref, s0_ref, wa_ref, ba_ref, g_ref, bm_ref, o_ref, fin_ref, st_s, o_s, *, t):
    nblk = t // GLA_BLOCK

    def block(j, reverse):
        rows = pl.ds(j * GLA_BLOCK if isinstance(j, int) else pl.multiple_of(j * GLA_BLOCK, GLA_BLOCK), GLA_BLOCK)
        q = x_ref[rows, 0:GLA_W].astype(f32) * GLA_DK ** -0.5
        k = x_ref[rows, GLA_W:2 * GLA_W].astype(f32)
        v = x_ref[rows, 2 * GLA_W:3 * GLA_W].astype(f32)
        ga = x_ref[rows, 4 * GLA_W:4 * GLA_W + LANES]
        d = 1 if reverse else 0
        z = _dot(ga.astype(bf16), wa_ref[:, d * GLA_W:(d + 1) * GLA_W]) + ba_ref[:, d * GLA_W:(d + 1) * GLA_W]
        la = -_softplus(-z) * (1.0 / GLA_TAU)
        o = _gla_block(q, k, v, la, reverse, st_s)
        if reverse:
            o_s[rows, :] = o_s[rows, :] + o
        else:
            o_s[rows, :] = o

    for d in range(2):
        if s0_ref is None:
            st_s[...] = jnp.zeros((GLA_W, GLA_W), f32)
        else:
            st_s[...] = s0_ref[0, d]
        if nblk == 1:
            block(0, d == 1)
        else:
            def body(i, carry, d=d):
                block(i if d == 0 else nblk - 1 - i, d == 1)
                return carry
            lax.fori_loop(0, nblk, body, 0)
        st_t = st_s[...].T
        for h in range(GLA_H):
            fin_ref[0, d, h] = st_t[h * GLA_DK:(h + 1) * GLA_DK, h * GLA_DK:(h + 1) * GLA_DK]
    gg = x_ref[:, 3 * GLA_W:4 * GLA_W].astype(f32)
    o_ref[...] = (_head_rms(o_s[...], g_ref[...], bm_ref[...]) * _silu(gg)).astype(bf16)


def _gla_call(pg, row0, n_seq, t, s0, wa, ba, g, bm):
    sb0 = row0 // t
    assert row0 % t == 0 and t % GLA_BLOCK == 0
    const = lambda b: (0, 0)
    in_specs = [pl.BlockSpec((t, PROJ_G), lambda b: (sb0 + b, 0))]
    args = [pg]
    if s0 is not None:
        in_specs.append(pl.BlockSpec((1, 2, GLA_W, GLA_W), lambda b: (b, 0, 0, 0)))
        args.append(s0)
    in_specs += [pl.BlockSpec((LANES, 2 * GLA_W), const), pl.BlockSpec((1, 2 * GLA_W), const),
                 pl.BlockSpec((1, GLA_W), const), pl.BlockSpec((GLA_W, GLA_W), const)]
    args += [wa, ba, g, bm]

    def kern(*refs):
        refs = list(refs)
        x_ref = refs.pop(0)
        s0_ref = refs.pop(0) if s0 is not None else None
        wa_ref, ba_ref, g_ref, bm_ref = refs[:4]
        _gla_kernel(x_ref, s0_ref, wa_ref, ba_ref, g_ref, bm_ref, *refs[4:], t=t)

    return pl.pallas_call(
        kern,
        grid=(n_seq,),
        in_specs=in_specs,
        out_specs=[pl.BlockSpec((t, GLA_W), lambda b: (b, 0)),
                   pl.BlockSpec((1, 2, GLA_H, GLA_DK, GLA_DK), lambda b: (b, 0, 0, 0, 0))],
        out_shape=[jax.ShapeDtypeStruct((n_seq * t, GLA_W), bf16),
                   jax.ShapeDtypeStruct((n_seq, 2, GLA_H, GLA_DK, GLA_DK), f32)],
        scratch_shapes=[pltpu.VMEM((GLA_W, GLA_W), f32), pltpu.VMEM((t, GLA_W), f32)],
        compiler_params=_cparams(("parallel",)),
        name="gla_t%d" % t,
    )(*args)


def _out_kernel(xc_ref, xl_ref, ac_ref, lc_ref, gc_ref, al_ref, ll_ref, gl_ref, mod_ref, wo_ref, g2_ref,
                rwh_ref, rwl_ref, s1_ref, s3_ref, s2_ref, xb_ref, h2_ref, sc_ref, *, n_ctx_tiles):
    is_ctx = pl.program_id(0) < n_ctx_tiles
    pick = lambda c_ref, l_ref: jnp.where(is_ctx, c_ref[...], l_ref[...]).astype(bf16)
    mod = mod_ref[0]
    g1 = mod[:, 2 * D_MODEL:3 * D_MODEL]
    sh2 = mod[:, 3 * D_MODEL:4 * D_MODEL]
    sc2 = mod[:, 4 * D_MODEL:5 * D_MODEL]
    g2 = mod[:, 5 * D_MODEL:6 * D_MODEL]
    m = (_dot(pick(ac_ref, al_ref), wo_ref[0:ATTN_W, :])
         + _dot(pick(lc_ref, ll_ref), wo_ref[ATTN_W:ATTN_W + LRU_W, :])
         + _dot(pick(gc_ref, gl_ref), wo_ref[ATTN_W + LRU_W:ATTN_W + LRU_W + GLA_W, :]))
    x1 = jnp.where(is_ctx, xc_ref[...], xl_ref[...]) + g1 * m
    h2 = _rms(x1, g2_ref[...]) * (1.0 + sc2) + sh2
    h2_ref[...] = _pack_pairs(h2)
    h_hi, h_lo = _split(h2)
    logits_t = _dot_nt(rwh_ref[...], h_hi) + _dot_nt(rwh_ref[...], h_lo) + _dot_nt(rwl_ref[...], h_hi)
    sc_ref[...] = _sigmoid(logits_t)
    hb = h2.astype(bf16)
    act = (_silu(_dot(hb, s1_ref[...])) * _dot(hb, s3_ref[...])).astype(bf16)
    xb_ref[...] = x1 + g2 * _dot(act, s2_ref[...])


def _out_call(xc, xl, mix_ctx, mix_lat, mod, wo, g2, rwh, rwl, s1, s3, s2, seq_row):
    n = xc.shape[0] + xl.shape[0]
    nct = xc.shape[0] // PROJ_TILE
    const = lambda i: (0, 0)
    rowb = lambda w: pl.BlockSpec((PROJ_TILE, w), lambda i: (i, 0))
    ctxb = lambda w: _row_pair_specs(nct, w)[0]
    latb = lambda w: _row_pair_specs(nct, w)[1]
    return pl.pallas_call(
        functools.partial(_out_kernel, n_ctx_tiles=nct),
        grid=(n // PROJ_TILE,),
        in_specs=[ctxb(D_MODEL), latb(D_MODEL),
                  ctxb(ATTN_W), ctxb(LRU_W), ctxb(GLA_W), latb(ATTN_W), latb(LRU_W), latb(GLA_W),
                  pl.BlockSpec((1, 1, 6 * D_MODEL), lambda i: (seq_row(i), 0, 0)),
                  pl.BlockSpec((D_MODEL, D_MODEL), const), pl.BlockSpec((1, D_MODEL), const),
                  pl.BlockSpec((N_EXPERTS, D_MODEL), const), pl.BlockSpec((N_EXPERTS, D_MODEL), const),
                  pl.BlockSpec((D_MODEL, EXPERT_FF), const), pl.BlockSpec((D_MODEL, EXPERT_FF), const),
                  pl.BlockSpec((EXPERT_FF, D_MODEL), const)],
        out_specs=[rowb(D_MODEL),
                   rowb(D_PACK),
                   pl.BlockSpec((N_EXPERTS, PROJ_TILE), lambda i: (0, i))],
        out_shape=[jax.ShapeDtypeStruct((n, D_MODEL), f32),
                   jax.ShapeDtypeStruct((n, D_PACK), i32),
                   jax.ShapeDtypeStruct((N_EXPERTS, n), f32)],
        compiler_params=_cparams(("parallel",)),
        name="out_proj",
    )(xc, xl, *mix_ctx, *mix_lat, mod, wo, g2, rwh, rwl, s1, s3, s2)


def _set_row(acc, k, row):
    sub = lax.broadcasted_iota(i32, acc.shape, 0)
    return jnp.where(sub == k, jnp.broadcast_to(row, acc.shape), acc)


def _route_kernel(sc_ref, rb_ref, idx_ref, gate_ref, rank_ref, cnt_ref, run_s):
    tm = sc_ref.shape[1]

    @pl.when(pl.program_id(0) == 0)
    def _():
        run_s[...] = jnp.zeros_like(run_s)

    scores = sc_ref[...]
    sel = scores + rb_ref[...]
    erow = lax.broadcasted_iota(i32, (N_EXPERTS, tm), 0).astype(f32)
    neg = jnp.full((N_EXPERTS, tm), -jnp.inf, f32)
    hots = []
    idx_o = jnp.zeros((TOP_K, tm), f32)
    gate_o = jnp.zeros((TOP_K, tm), f32)
    gsum = jnp.zeros((1, tm), f32)
    for k in range(TOP_K):
        m = jnp.max(sel, axis=0, keepdims=True)
        idx_f = jnp.min(jnp.where(sel == m, erow, float(N_EXPERTS)), axis=0, keepdims=True)
        hot = erow == idx_f
        gk = jnp.sum(jnp.where(hot, scores, 0.0), axis=0, keepdims=True)
        sel = jnp.where(hot, neg, sel)
        hots.append(hot)
        gsum = gsum + gk
        idx_o = _set_row(idx_o, k, idx_f)
        gate_o = _set_row(gate_o, k, gk)
    chosen = jnp.where(sel == neg, 1.0, 0.0)
    gate_ref[...] = gate_o / gsum * ROUTED_SCALE
    idx_ref[...] = idx_o.astype(i32)
    r = lax.broadcasted_iota(i32, (tm, tm), 0)
    c = lax.broadcasted_iota(i32, (tm, tm), 1)
    pos = _dot(chosen.astype(bf16), (r < c).astype(bf16)) + run_s[...]
    rank_o = jnp.zeros((TOP_K, tm), f32)
    for k in range(TOP_K):
        rank_o = _set_row(rank_o, k, jnp.sum(jnp.where(hots[k], pos, 0.0), axis=0, keepdims=True))
    rank_ref[...] = rank_o.astype(i32)
    run_s[...] = run_s[...] + jnp.sum(chosen, axis=1, keepdims=True)
    cnt_ref[...] = run_s[...]


def _route_call(scores_t, rb):
    n = scores_t.shape[1]
    colb = lambda r: pl.BlockSpec((r, ROUTE_TILE), lambda i: (0, i))
    cnt_spec = pl.BlockSpec((N_EXPERTS, 1), lambda i: (0, 0))
    return pl.pallas_call(
        _route_kernel,
        grid=(n // ROUTE_TILE,),
        in_specs=[colb(N_EXPERTS), cnt_spec],
        out_specs=[colb(TOP_K), colb(TOP_K), colb(TOP_K), cnt_spec],
        out_shape=[jax.ShapeDtypeStruct((TOP_K, n), i32), jax.ShapeDtypeStruct((TOP_K, n), f32),
                   jax.ShapeDtypeStruct((TOP_K, n), i32), jax.ShapeDtypeStruct((N_EXPERTS, 1), f32)],
        scratch_shapes=[pltpu.VMEM((N_EXPERTS, 1), f32)],
        compiler_params=_cparams(("arbitrary",)),
        name="route",
    )(scores_t, rb)


def _dest_kernel(idx_ref, rank_ref, start_ref, dest_ref):
    tm = idx_ref.shape[1]
    erow = lax.broadcasted_iota(i32, (N_EXPERTS, tm), 0)
    start = jnp.broadcast_to(start_ref[...], (N_EXPERTS, tm))
    idx = idx_ref[...]
    dest = jnp.zeros((TOP_K, tm), f32)
    for k in range(TOP_K):
        hot = erow == idx[k:k + 1, :]
        dest = _set_row(dest, k, jnp.sum(jnp.where(hot, start, 0.0), axis=0, keepdims=True))
    dest_ref[...] = dest.astype(i32) + rank_ref[...]


def _dest_call(idx, rank, start):
    n = idx.shape[1]
    colb = lambda r: pl.BlockSpec((r, ROUTE_TILE), lambda i: (0, i))
    return pl.pallas_call(
        _dest_kernel,
        grid=(n // ROUTE_TILE,),
        in_specs=[colb(TOP_K), colb(TOP_K), pl.BlockSpec((N_EXPERTS, 1), lambda i: (0, 0))],
        out_specs=colb(TOP_K),
        out_shape=jax.ShapeDtypeStruct((TOP_K, n), i32),
        compiler_params=_cparams(("parallel",)),
        name="dest",
    )(idx, rank, start)


def _sc_workers():
    info = plsc.get_sparse_core_info()
    mesh = plsc.VectorSubcoreMesh(core_axis_name="c", subcore_axis_name="s")
    worker_id = lambda: lax.axis_index("s") * info.num_cores + lax.axis_index("c")
    return mesh, info.num_cores * info.num_subcores, worker_id


def _scatter_rows_call(rows, dest_km, n_slots):
    n, d = rows.shape
    mesh, n_workers, worker_id = _sc_workers()
    per_worker = n // n_workers
    assert n % (n_workers * SC_CHUNK) == 0

    def body(rows_hbm, idx_hbm, out_hbm, *scratch):
        idx_vs, rows_v, sem = scratch[:TOP_K], scratch[TOP_K], scratch[TOP_K + 1]
        base = worker_id() * per_worker

        @pl.loop(0, per_worker // SC_CHUNK)
        def _(i):
            t0 = pl.multiple_of(base + i * SC_CHUNK, SC_CHUNK)
            pltpu.sync_copy(rows_hbm.at[pl.ds(t0, SC_CHUNK)], rows_v)
            for k in range(TOP_K):
                pltpu.sync_copy(idx_hbm.at[pl.ds(k * n + t0, SC_CHUNK)], idx_vs[k])
            copies = [pltpu.async_copy(rows_v, out_hbm.at[idx_vs[k]], sem) for k in range(TOP_K)]
            for cp in copies:
                cp.wait()

    return pl.kernel(
        body,
        out_type=jax.ShapeDtypeStruct((n_slots, d), rows.dtype),
        mesh=mesh,
        scratch_types=[pltpu.VMEM((SC_CHUNK,), i32)] * TOP_K + [pltpu.VMEM((SC_CHUNK, d), rows.dtype),
                                                                pltpu.SemaphoreType.DMA],
        name="scatter_rows",
    )(rows, dest_km.reshape(-1))


def _expert_kernel(nu_ref, bv_ref, ord_ref, eseq_ref, nex_ref, x_ref, w1_hbm, w3_hbm, w2_hbm, y_ref,
                   w1_f, w3_f, w2_f, w1_s, w3_s, w2_s, sems, *, layer):
    def weight_copies(j):
        e = eseq_ref[j]
        slot = j % W_RING
        return [pltpu.make_async_copy(w1_hbm.at[layer, e], w1_f.at[slot], sems.at[slot, 0]),
                pltpu.make_async_copy(w3_hbm.at[layer, e], w3_f.at[slot], sems.at[slot, 1]),
                pltpu.make_async_copy(w2_hbm.at[layer, e], w2_f.at[slot], sems.at[slot, 2])]

    def start_if_exists(j):
        @pl.when(j < nex_ref[0])
        def _():
            for cp in weight_copies(j):
                cp.start()

    for u in range(STEP_BLOCKS):
        i = pl.program_id(0) * STEP_BLOCKS + u
        rows = slice(u * SLOT_BLOCK, (u + 1) * SLOT_BLOCK)

        @pl.when(i < nu_ref[0])
        def _(i=i, rows=rows):
            j = ord_ref[i]

            @pl.when(i == 0)
            def _():
                for ahead in range(W_RING - 1):
                    start_if_exists(ahead)

            @pl.when((i == 0) | (j != ord_ref[jnp.maximum(i - 1, 0)]))
            def _():
                start_if_exists(j + W_RING - 1)
                for cp in weight_copies(j):
                    cp.wait()
                slot = j % W_RING
                w1_s[...] = w1_f[slot].astype(bf16)
                w3_s[...] = w3_f[slot].astype(bf16)
                w2_s[...] = w2_f[slot].astype(bf16)

            row = lax.broadcasted_iota(i32, (SLOT_BLOCK, D_PACK), 0)
            x_hi, x_lo = _unpack_pairs(jnp.where(row < bv_ref[i], x_ref[rows, :], 0))
            x = jnp.concatenate([x_hi, x_lo], axis=1).astype(bf16)
            act = (_silu(_dot(x, w1_s[...])) * _dot(x, w3_s[...])).astype(bf16)
            y_ref[rows, :] = _pack_pairs(_dot(act, w2_s[...]))

        @pl.when((i >= nu_ref[0]) & (pl.program_id(0) * STEP_BLOCKS < nu_ref[0]))
        def _(rows=rows):
            y_ref[rows, :] = x_ref[rows, :]


def _expert_call(n_used, blk_valid, blk_ord, expert_seq, n_seq_experts, slots, w1, w3, w2, layer):
    step_rows = STEP_BLOCKS * SLOT_BLOCK
    assert slots.shape[0] % step_rows == 0
    blk = lambda s, nu, *_: (jnp.minimum(s, (nu[0] - 1) // STEP_BLOCKS), 0)
    hbm = pl.BlockSpec(memory_space=pl.ANY)
    grid_spec = pltpu.PrefetchScalarGridSpec(
        num_scalar_prefetch=5,
        grid=(slots.shape[0] // step_rows,),
        in_specs=[pl.BlockSpec((step_rows, D_PACK), blk), hbm, hbm, hbm],
        out_specs=pl.BlockSpec((step_rows, D_PACK), blk),
        scratch_shapes=[pltpu.VMEM((W_RING, D_MODEL, EXPERT_FF), f32), pltpu.VMEM((W_RING, D_MODEL, EXPERT_FF), f32),
                        pltpu.VMEM((W_RING, EXPERT_FF, D_MODEL), f32),
                        pltpu.VMEM((D_MODEL, EXPERT_FF), bf16), pltpu.VMEM((D_MODEL, EXPERT_FF), bf16),
                        pltpu.VMEM((EXPERT_FF, D_MODEL), bf16),
                        pltpu.SemaphoreType.DMA((W_RING, 3))],
    )
    return pl.pallas_call(
        functools.partial(_expert_kernel, layer=layer),
        grid_spec=grid_spec,
        out_shape=jax.ShapeDtypeStruct(slots.shape, slots.dtype),
        input_output_aliases={5: 0},
        compiler_params=_cparams(("arbitrary",)),
        name="experts",
    )(n_used, blk_valid, blk_ord, expert_seq, n_seq_experts, slots, w1, w3, w2)


def _gather_rows_call(table, idx):
    n_idx = idx.shape[0]
    d = table.shape[1]
    mesh, n_workers, worker_id = _sc_workers()
    per_worker = n_idx // n_workers
    assert n_idx % (n_workers * SC_CHUNK) == 0

    n_chunks = per_worker // SC_CHUNK
    assert n_chunks % 2 == 0

    def body(table_hbm, idx_hbm, out_hbm, idx_a, idx_b, rows_a, rows_b, sem_a, sem_b):
        base = worker_id() * per_worker
        bufs = ((idx_a, rows_a, sem_a), (idx_b, rows_b, sem_b))

        def start_gather(c, buf):
            idx_v, rows_v, sem = buf
            off = pl.multiple_of(base + c * SC_CHUNK, SC_CHUNK)
            pltpu.sync_copy(idx_hbm.at[pl.ds(off, SC_CHUNK)], idx_v)
            pltpu.async_copy(table_hbm.at[idx_v], rows_v, sem)

        start_gather(0, bufs[0])

        @pl.loop(0, n_chunks, step=2)
        def _(c0):
            for b in range(2):
                c = c0 + b
                idx_v, rows_v, sem = bufs[b]

                @pl.when(c + 1 < n_chunks)
                def _():
                    start_gather(c + 1, bufs[1 - b])

                pltpu.make_async_copy(table_hbm.at[idx_v], rows_v, sem).wait()
                off = pl.multiple_of(base + c * SC_CHUNK, SC_CHUNK)
                pltpu.sync_copy(rows_v, out_hbm.at[pl.ds(off, SC_CHUNK)])

    return pl.kernel(
        body,
        out_type=jax.ShapeDtypeStruct((n_idx, d), table.dtype),
        mesh=mesh,
        scratch_types=[pltpu.VMEM((SC_CHUNK,), i32), pltpu.VMEM((SC_CHUNK,), i32),
                       pltpu.VMEM((SC_CHUNK, d), table.dtype), pltpu.VMEM((SC_CHUNK, d), table.dtype),
                       pltpu.SemaphoreType.DMA, pltpu.SemaphoreType.DMA],
        name="gather_rows",
    )(table, idx)


def _combine_kernel(gate_ref, xb_ref, mod_ref, y_ref, o_ref):
    g2 = mod_ref[0][:, 5 * D_MODEL:6 * D_MODEL]
    gates = gate_ref[...]
    acc_hi = jnp.zeros((gates.shape[0], D_PACK), f32)
    acc_lo = jnp.zeros((gates.shape[0], D_PACK), f32)
    for k in range(TOP_K):
        y_hi, y_lo = _unpack_pairs(y_ref[k])
        acc_hi = acc_hi + gates[:, k:k + 1] * y_hi
        acc_lo = acc_lo + gates[:, k:k + 1] * y_lo
    o_ref[:, 0:D_PACK] = xb_ref[:, 0:D_PACK] + g2[:, 0:D_PACK] * acc_hi
    o_ref[:, D_PACK:D_MODEL] = xb_ref[:, D_PACK:D_MODEL] + g2[:, D_PACK:D_MODEL] * acc_lo


def _combine_call(gates, xbase, mod, ygath, seq_row_c, row0):
    tm = COMBINE_TILE
    n_rows = ygath.shape[1]
    assert row0 % tm == 0 and n_rows % tm == 0
    b0 = row0 // tm
    return pl.pallas_call(
        _combine_kernel,
        grid=(n_rows // tm,),
        in_specs=[pl.BlockSpec((tm, TOP_K), lambda i: (b0 + i, 0)),
                  pl.BlockSpec((tm, D_MODEL), lambda i: (b0 + i, 0)),
                  pl.BlockSpec((1, 1, 6 * D_MODEL), lambda i: (seq_row_c(b0 + i), 0, 0)),
                  pl.BlockSpec((TOP_K, tm, D_PACK), lambda i: (0, i, 0))],
        out_specs=pl.BlockSpec((tm, D_MODEL), lambda i: (i, 0)),
        out_shape=jax.ShapeDtypeStruct((n_rows, D_MODEL), f32),
        compiler_params=_cparams(("parallel",)),
        name="combine",
    )(gates, xbase, mod, ygath)


def _block_avg(width, group):
    r = np.arange(width)
    return jnp.asarray((r[:, None] // group == r[None, :] // group).astype(np.float32) / group, dtype=bf16)


def _kv_replicate():
    c = np.arange(ATTN_W)
    src = (c // (ATTN_W // N_KV_HEADS)) * HEAD_DIM + c % HEAD_DIM
    return jnp.asarray((np.arange(KV_W)[:, None] == src[None, :]).astype(np.float32), dtype=bf16)


def _rope_lane_tables(n_tok):
    rows = n_tok // GRID_W
    r = jnp.repeat(jnp.arange(rows, dtype=f32), GRID_W)
    col = jnp.tile(jnp.arange(GRID_W, dtype=f32), rows)
    inv = ROPE_THETA ** (-jnp.arange(ROPE_FREQ, dtype=f32) / ROPE_FREQ)
    ar = r[:, None] * inv
    ac = col[:, None] * inv
    cos_h = jnp.concatenate([jnp.cos(ar), jnp.cos(ar), jnp.cos(ac), jnp.cos(ac)], axis=-1)
    sin_h = jnp.concatenate([-jnp.sin(ar), jnp.sin(ar), -jnp.sin(ac), jnp.sin(ac)], axis=-1)
    return jnp.tile(cos_h, (1, N_HEADS)), jnp.tile(sin_h, (1, N_HEADS))


def _block_diag(w):
    nb, bw, _ = w.shape
    eye = jnp.eye(nb, dtype=w.dtype)
    return (w[:, :, None, :] * eye[:, None, :, None]).reshape(nb * bw, nb * bw)


def _gla_state_in(s):
    bsz = s.shape[0]
    eye = jnp.eye(GLA_H, dtype=s.dtype)
    st = jnp.swapaxes(s, -1, -2)
    big = st[:, :, :, :, None, :] * eye[None, None, :, None, :, None]
    return big.reshape(bsz, 2, GLA_W, GLA_W)


def kernel(x_prompt, x_sample, cache_k, cache_v, state_lru, state_gla, c, c_ctx, ada_w, ada_b, norm1_g, norm2_g, w_in, q_norm_g, k_norm_g, attn_out_g, conv_w, conv_b, lru_wa, lru_ba, lru_wi, lru_bi, lru_lambda, lru_out_g, gla_wa2, gla_ba, gla_out_g, w_out, router_w, router_b, exp_w1, exp_w3, exp_w2, sh_w1, sh_w3, sh_w2):
    bc, tc, _ = x_prompt.shape
    bl, tl, _ = x_sample.shape
    depth = w_in.shape[0]
    nc = bc * tc
    n = nc + bl * tl
    past = cache_k.shape[2]
    assert tc == ROW_TILE and tl % ROW_TILE == 0 and nc % tl == 0 and bl + 1 <= SUBLANES
    assert nc % PROJ_TILE == 0 and tl % PROJ_TILE == 0 and nc % COMBINE_TILE == 0 and tl % COMBINE_TILE == 0

    def seq_row_for(tile):
        def seq_row(i):
            return jnp.where(i < nc // tile, 0, 1 + (i - nc // tile) // (tl // tile))
        return seq_row

    seq_row = seq_row_for(PROJ_TILE)
    seq_row_c = seq_row_for(COMBINE_TILE)

    xc = x_prompt.reshape(nc, D_MODEL)
    xl = x_sample.reshape(bl * tl, D_MODEL)
    cond = jnp.zeros((SUBLANES, D_MODEL), f32).at[0].set(c_ctx).at[1:1 + bl].set(c)
    mods = _ada_call(cond, ada_w, ada_b)

    bmq = _block_avg(ATTN_W, HEAD_DIM)
    bmk = _block_avg(KV_W, HEAD_DIM)
    bmg = _block_avg(GLA_W, GLA_DK)
    rep = _kv_replicate()
    cos_t, sin_t = _rope_lane_tables(tl)
    step_rows = STEP_BLOCKS * SLOT_BLOCK
    n_slots = -(-(n * TOP_K + N_EXPERTS * (SLOT_BLOCK - 1)) // step_rows) * step_rows
    tile8 = lambda v: jnp.tile(v, N_HEADS)[None, :]

    ks, vs, lrus, glas = [], [], [], []
    for l in range(depth):
        mod = mods[l].reshape(SUBLANES, 1, 6 * D_MODEL)
        pa, pl_, pg = _proj_call(xc, xl, mod, norm1_g[l][None, :], w_in, l, seq_row)

        qg, kg, og = tile8(q_norm_g[l]), jnp.tile(k_norm_g[l], N_KV_HEADS)[None, :], attn_out_g[l][None, :]
        attn_c, k_new, v_new = _attn_ctx_call(pa, bc, tc, qg, kg, og, bmq, bmk, rep)
        attn_l = _attn_lat_call(pa, nc, bl, tl, cache_k[:, l].reshape(bl, past, KV_W),
                                cache_v[:, l].reshape(bl, past, KV_W), cos_t, sin_t, qg, kg, og, bmq, bmk, rep)
        ks.append(k_new.reshape(bc, tc, N_KV_HEADS, HEAD_DIM))
        vs.append(v_new.reshape(bc, tc, N_KV_HEADS, HEAD_DIM))

        wg = jnp.concatenate([_block_diag(lru_wa[l, 0]), _block_diag(lru_wi[l, 0]),
                              _block_diag(lru_wa[l, 1]), _block_diag(lru_wi[l, 1])], axis=1).astype(bf16)
        bg = jnp.concatenate([lru_ba[l, 0], lru_bi[l, 0], lru_ba[l, 1], lru_bi[l, 1]])[None, :]
        lru_args = (conv_w[l], conv_b[l][None, :], wg, bg, lru_lambda[l], lru_out_g[l][None, :])
        lru_c, lru_fin = _lru_call(pl_, 0, bc, tc, jnp.zeros((bc, 2, LRU_W), f32), *lru_args)
        lru_l, _ = _lru_call(pl_, nc, bl, tl, state_lru[:, l], *lru_args)
        lrus.append(lru_fin)

        wa = jnp.zeros((LANES, 2 * GLA_W), f32)
        wa = wa.at[0:GLA_RANK, 0:GLA_W].set(gla_wa2[l, 0]).at[GLA_RANK:2 * GLA_RANK, GLA_W:].set(gla_wa2[l, 1])
        gla_args = (wa.astype(bf16), gla_ba[l].reshape(1, 2 * GLA_W), gla_out_g[l].reshape(1, GLA_W), bmg)
        gla_c, gla_fin = _gla_call(pg, 0, bc, tc, None, *gla_args)
        gla_l, _ = _gla_call(pg, nc, bl, tl, _gla_state_in(state_gla[:, l]), *gla_args)
        glas.append(gla_fin)

        rw_t = router_w[l].T
        rw_hi = rw_t.astype(bf16)
        rw_lo = (rw_t - rw_hi.astype(f32)).astype(bf16)
        xbase, h2, scores_t = _out_call(xc, xl, (attn_c, lru_c, gla_c), (attn_l, lru_l, gla_l), mod, w_out[l].astype(bf16),
                                        norm2_g[l][None, :], rw_hi, rw_lo, sh_w1[l].astype(bf16),
                                        sh_w3[l].astype(bf16), sh_w2[l].astype(bf16), seq_row)

        idx, gates, rank, counts = _route_call(scores_t, router_b[l][:, None])
        cnt = counts[:, 0].astype(i32)
        padded = (cnt + SLOT_BLOCK - 1) // SLOT_BLOCK * SLOT_BLOCK
        padded_end = jnp.cumsum(padded)
        dest_km = _dest_call(idx, rank, (padded_end - padded).astype(f32)[:, None])
        nb = n_slots // SLOT_BLOCK
        blk_first = jnp.arange(nb, dtype=i32) * SLOT_BLOCK
        blk_e = jnp.minimum(jnp.sum((padded_end[None, :] <= blk_first[:, None]).astype(i32), axis=1), N_EXPERTS - 1)
        blk_valid = jnp.clip((padded_end - padded + cnt)[blk_e] - blk_first, 0, SLOT_BLOCK)
        n_used = padded_end[-1:] // SLOT_BLOCK
        owns = jnp.cumsum((cnt > 0).astype(i32))
        blk_ord = (owns - 1)[blk_e]
        expert_seq = jnp.minimum(jnp.sum((owns[None, :] <= jnp.arange(N_EXPERTS, dtype=i32)[:, None]).astype(i32),
                                         axis=1), N_EXPERTS - 1)
        slots = _scatter_rows_call(h2, dest_km, n_slots)
        slots = _expert_call(n_used, blk_valid, blk_ord, expert_seq, owns[-1:], slots, exp_w1, exp_w3, exp_w2, l)
        gates_t = gates.T
        yg_c = _gather_rows_call(slots, dest_km[:, :nc].reshape(-1)).reshape(TOP_K, nc, D_PACK)
        yg_l = _gather_rows_call(slots, dest_km[:, nc:].reshape(-1)).reshape(TOP_K, n - nc, D_PACK)
        xc = _combine_call(gates_t, xbase, mod, yg_c, seq_row_c, 0)
        xl = _combine_call(gates_t, xbase, mod, yg_l, seq_row_c, nc)

    y_prompt = xc.reshape(bc, tc, D_MODEL)
    y_sample = xl.reshape(bl, tl, D_MODEL)
    return (y_prompt, y_sample, jnp.stack(ks, axis=1), jnp.stack(vs, axis=1),
            jnp.stack(lrus, axis=1), jnp.stack(glas, axis=1))
```

```python
import functools

import jax
import jax.numpy as jnp
import numpy as np
from jax import lax
from jax.experimental import pallas as pl
from jax.experimental.pallas import tpu as pltpu
from jax.experimental.pallas import tpu_sc as plsc

f32 = jnp.float32
bf16 = jnp.bfloat16
i32 = jnp.int32

D_MODEL = 1024
N_HEADS = 8
N_KV_HEADS = 2
HEAD_DIM = 64
ATTN_W = N_HEADS * HEAD_DIM
KV_W = N_KV_HEADS * HEAD_DIM
GRID_W = 64
ROPE_FREQ = HEAD_DIM // 4
ROPE_THETA = 10000.0
LRU_W = 256
LRU_BLOCKS = 4
LRU_C = 8.0
CONV_W = 4
CONV_LEFT = 2
GLA_H = 4
GLA_DK = 64
GLA_W = 256
GLA_RANK = 16
GLA_TAU = 16.0
N_EXPERTS = 256
TOP_K = 8
EXPERT_FF = 256
ROUTED_SCALE = 2.5
EPS = 1e-6

LANES = 128
SUBLANES = 8
ROW_TILE = 256
ROUTE_TILE = 512
PROJ_TILE = 512
GLA_BLOCK = 256
SLOT_BLOCK = 256
STEP_BLOCKS = 4
W_RING = 4
COMBINE_TILE = 512
SC_SCATTER_CHUNK = 128
SC_CHUNK = 64
D_PACK = D_MODEL // 2
VMEM_LIMIT = 56 * 1024 * 1024


def _cparams(sem, vmem=VMEM_LIMIT):
    return pltpu.CompilerParams(dimension_semantics=sem, vmem_limit_bytes=vmem)


def _dot(a, b):
    return jnp.dot(a, b, preferred_element_type=f32)


def _dot_nt(a, b):
    return lax.dot_general(a, b, (((1,), (1,)), ((), ())), preferred_element_type=f32)


def _dot_tn(a, b):
    return lax.dot_general(a, b, (((0,), (0,)), ((), ())), preferred_element_type=f32)


def _split(x):
    hi = x.astype(bf16)
    lo = (x - hi.astype(f32)).astype(bf16)
    return hi, lo


def _dot_x2(x, w):
    hi, lo = _split(x)
    return _dot(hi, w) + _dot(lo, w)


def _dot_2x(m, x):
    hi, lo = _split(x)
    return _dot(m, hi) + _dot(m, lo)


def _dot3(a, b_hi, b_lo):
    a_hi, a_lo = _split(a)
    return _dot(a_hi, b_hi) + _dot(a_lo, b_hi) + _dot(a_hi, b_lo)


def _sigmoid(x):
    return 1.0 / (1.0 + jnp.exp(-x))


def _silu(x):
    return x * _sigmoid(x)


def _softplus(x):
    return jnp.maximum(x, 0.0) + jnp.log(1.0 + jnp.exp(-jnp.abs(x)))


def _gelu_tanh(x):
    return 0.5 * x * (1.0 + jnp.tanh(0.7978845608028654 * (x + 0.044715 * x * x * x)))


def _rms(x, g):
    return x * lax.rsqrt(jnp.mean(x * x, axis=-1, keepdims=True) + EPS) * g


def _pack_pairs(x):
    c = x.shape[1] // 2
    hi = lax.bitcast_convert_type(x[:, :c].astype(bf16).astype(f32), i32)
    lo = lax.bitcast_convert_type(x[:, c:].astype(bf16).astype(f32), i32)
    return hi | lax.shift_right_logical(lo, 16)


def _unpack_pairs(w):
    hi = lax.bitcast_convert_type(w & jnp.int32(-65536), f32)
    lo = lax.bitcast_convert_type(w << 16, f32)
    return hi, lo


def _head_rms(x, g, bm):
    ms = _dot_x2(x * x, bm)
    return x * lax.rsqrt(ms + EPS) * g


def _ada_kernel(c_ref, w_ref, b_ref, o_ref):
    s = _silu(c_ref[...])
    w = w_ref[0]
    w_hi, w_lo = _split(w)
    o_ref[0] = _dot3(s, w_hi, w_lo) + b_ref[0]


def _ada_call(cond, ada_w, ada_b):
    depth = ada_w.shape[0]
    nt = 1536
    return pl.pallas_call(
        _ada_kernel,
        grid=(depth, 6 * D_MODEL // nt),
        in_specs=[pl.BlockSpec((SUBLANES, D_MODEL), lambda l, j: (0, 0)),
                  pl.BlockSpec((1, D_MODEL, nt), lambda l, j: (l, 0, j)),
                  pl.BlockSpec((1, 1, nt), lambda l, j: (l, 0, j))],
        out_specs=pl.BlockSpec((1, SUBLANES, nt), lambda l, j: (l, 0, j)),
        out_shape=jax.ShapeDtypeStruct((depth, SUBLANES, 6 * D_MODEL), f32),
        compiler_params=_cparams(("parallel", "parallel")),
        name="ada_mod",
    )(cond, ada_w, ada_b.reshape(depth, 1, 6 * D_MODEL))


PROJ_A = ATTN_W + 2 * KV_W
PROJ_L = 2 * LRU_W
PROJ_G = 4 * GLA_W + LANES
IN_COLS = PROJ_A + PROJ_L + 4 * GLA_W + 2 * GLA_RANK


def _proj_kernel(xc_ref, xl_ref, mod_ref, g_ref, w_ref, oa_ref, ol_ref, og_ref, w_s, *, n_ctx_tiles):
    @pl.when(pl.program_id(0) == 0)
    def _():
        w_s[:, 0:IN_COLS] = w_ref[0].astype(bf16)
        w_s[:, IN_COLS:] = jnp.zeros((D_MODEL, w_s.shape[1] - IN_COLS), bf16)

    mod = mod_ref[0]
    sh = mod[:, 0:D_MODEL]
    sc = mod[:, D_MODEL:2 * D_MODEL]
    x = jnp.where(pl.program_id(0) < n_ctx_tiles, xc_ref[...], xl_ref[...])
    h = (_rms(x, g_ref[...]) * (1.0 + sc) + sh).astype(bf16)
    p = _dot(h, w_s[...])
    oa_ref[...] = p[:, 0:PROJ_A].astype(bf16)
    ol_ref[...] = p[:, PROJ_A:PROJ_A + PROJ_L].astype(bf16)
    og_ref[...] = p[:, PROJ_A + PROJ_L:PROJ_A + PROJ_L + PROJ_G].astype(bf16)


def _row_pair_specs(nct, width):
    return (pl.BlockSpec((PROJ_TILE, width), lambda i: (jnp.minimum(i, nct - 1), 0)),
            pl.BlockSpec((PROJ_TILE, width), lambda i: (jnp.maximum(i - nct, 0), 0)))


def _proj_call(xc, xl, mod, g, w_all, layer, seq_row):
    n = xc.shape[0] + xl.shape[0]
    nct = xc.shape[0] // PROJ_TILE
    cols = PROJ_A + PROJ_L + PROJ_G
    return pl.pallas_call(
        functools.partial(_proj_kernel, n_ctx_tiles=nct),
        grid=(n // PROJ_TILE,),
        in_specs=[*_row_pair_specs(nct, D_MODEL),
                  pl.BlockSpec((1, 1, 6 * D_MODEL), lambda i: (seq_row(i), 0, 0)),
                  pl.BlockSpec((1, D_MODEL), lambda i: (0, 0)),
                  pl.BlockSpec((1, D_MODEL, IN_COLS), lambda i: (layer, 0, 0))],
        out_specs=[pl.BlockSpec((PROJ_TILE, PROJ_A), lambda i: (i, 0)),
                   pl.BlockSpec((PROJ_TILE, PROJ_L), lambda i: (i, 0)),
                   pl.BlockSpec((PROJ_TILE, PROJ_G), lambda i: (i, 0))],
        out_shape=[jax.ShapeDtypeStruct((n, PROJ_A), bf16),
                   jax.ShapeDtypeStruct((n, PROJ_L), bf16),
                   jax.ShapeDtypeStruct((n, PROJ_G), bf16)],
        scratch_shapes=[pltpu.VMEM((D_MODEL, cols), bf16)],
        compiler_params=_cparams(("arbitrary",)),
        name="in_proj",
    )(xc, xl, mod, g, w_all)


def _rope(x, cos_t, sin_t):
    w = x.shape[1]
    up = pltpu.roll(x, w - ROPE_FREQ, 1)
    dn = pltpu.roll(x, ROPE_FREQ, 1)
    lane = lax.broadcasted_iota(i32, x.shape, 1)
    partner = jnp.where((lane & (2 * ROPE_FREQ - 1)) < ROPE_FREQ, up, dn)
    return x * cos_t + partner * sin_t


def _attend(q, kk_ref, vv_ref, o_ref):
    tq = q.shape[0]
    gw = ATTN_W // N_KV_HEADS
    lane = lax.broadcasted_iota(i32, (tq, gw), 1)
    for g in range(N_KV_HEADS):
        qg = q[:, g * gw:(g + 1) * gw]
        kg = kk_ref[:, g * gw:(g + 1) * gw]
        vg = vv_ref[:, g * gw:(g + 1) * gw]
        acc = jnp.zeros((tq, gw), f32)
        for hh in range(N_HEADS // N_KV_HEADS):
            hm = (lane >> 6) == hh
            s = _dot_nt(jnp.where(hm, qg, 0.0).astype(bf16), kg)
            m = jnp.max(s, axis=-1, keepdims=True)
            p = jnp.exp(s - m)
            l = jnp.sum(p, axis=-1, keepdims=True)
            o = _dot(p.astype(bf16), vg) / l
            acc = jnp.where(hm, o, acc)
        o_ref[:, g * gw:(g + 1) * gw] = acc


def _attn_ctx_kernel(p_ref, qg_ref, kg_ref, og_ref, bmq_ref, bmk_ref, rep_ref,
                     o_ref, ko_ref, vo_ref, kk_s, vv_s, o_s):
    p = p_ref[...].astype(f32)
    q = _head_rms(p[:, 0:ATTN_W], qg_ref[...], bmq_ref[...])
    k = _head_rms(p[:, ATTN_W:ATTN_W + KV_W], kg_ref[...], bmk_ref[...])
    v = p[:, ATTN_W + KV_W:PROJ_A]
    ko_ref[...] = k
    vo_ref[...] = v
    kk_s[...] = _dot(k.astype(bf16), rep_ref[...]).astype(bf16)
    vv_s[...] = _dot(v.astype(bf16), rep_ref[...]).astype(bf16)
    _attend(q * HEAD_DIM ** -0.5, kk_s, vv_s, o_s)
    o_ref[...] = _rms(o_s[...], og_ref[...]).astype(bf16)


def _attn_ctx_call(pa, n_seq, t, qg, kg, og, bmq, bmk, rep):
    assert t == ROW_TILE
    const = lambda i: (0, 0)
    return pl.pallas_call(
        _attn_ctx_kernel,
        grid=(n_seq,),
        in_specs=[pl.BlockSpec((t, PROJ_A), lambda i: (i, 0)),
                  pl.BlockSpec((1, ATTN_W), const), pl.BlockSpec((1, KV_W), const), pl.BlockSpec((1, ATTN_W), const),
                  pl.BlockSpec((ATTN_W, ATTN_W), const), pl.BlockSpec((KV_W, KV_W), const),
                  pl.BlockSpec((KV_W, ATTN_W), const)],
        out_specs=[pl.BlockSpec((t, ATTN_W), lambda i: (i, 0)),
                   pl.BlockSpec((t, KV_W), lambda i: (i, 0)),
                   pl.BlockSpec((t, KV_W), lambda i: (i, 0))],
        out_shape=[jax.ShapeDtypeStruct((n_seq * t, ATTN_W), bf16),
                   jax.ShapeDtypeStruct((n_seq * t, KV_W), f32),
                   jax.ShapeDtypeStruct((n_seq * t, KV_W), f32)],
        scratch_shapes=[pltpu.VMEM((t, ATTN_W), bf16), pltpu.VMEM((t, ATTN_W), bf16), pltpu.VMEM((t, ATTN_W), f32)],
        compiler_params=_cparams(("parallel",)),
        name="attn_ctx",
    )(pa, qg, kg, og, bmq, bmk, rep)


def _attn_lat_kernel(q_ref, kv_ref, ck_ref, cv_ref, cq_ref, sq_ref, ckk_ref, skk_ref,
                     qg_ref, kg_ref, og_ref, bmq_ref, bmk_ref, rep_ref,
                     o_ref, kk_s, vv_s, o_s, *, past):
    @pl.when(pl.program_id(1) == 0)
    def _():
        kv = kv_ref[...].astype(f32)
        k = _head_rms(kv[:, 0:KV_W], kg_ref[...], bmk_ref[...])
        k = _rope(k, ckk_ref[...], skk_ref[...])
        v = kv[:, KV_W:2 * KV_W]
        kk_s[0:past, :] = _dot(ck_ref[0].astype(bf16), rep_ref[...]).astype(bf16)
        vv_s[0:past, :] = _dot(cv_ref[0].astype(bf16), rep_ref[...]).astype(bf16)
        kk_s[past:, :] = _dot(k.astype(bf16), rep_ref[...]).astype(bf16)
        vv_s[past:, :] = _dot(v.astype(bf16), rep_ref[...]).astype(bf16)

    q = _head_rms(q_ref[...].astype(f32), qg_ref[...], bmq_ref[...])
    q = _rope(q, cq_ref[...], sq_ref[...])
    _attend(q * HEAD_DIM ** -0.5, kk_s, vv_s, o_s)
    o_ref[...] = _rms(o_s[...], og_ref[...]).astype(bf16)


def _attn_lat_call(pa, row0, n_seq, t, cache_k, cache_v, cq, sq, qg, kg, og, bmq, bmk, rep):
    past = cache_k.shape[1]
    nq = t // ROW_TILE
    assert row0 % t == 0 and t % ROW_TILE == 0
    qb0 = row0 // ROW_TILE
    sb0 = row0 // t
    const = lambda b, j: (0, 0)
    return pl.pallas_call(
        functools.partial(_attn_lat_kernel, past=past),
        grid=(n_seq, nq),
        in_specs=[pl.BlockSpec((ROW_TILE, ATTN_W), lambda b, j: (qb0 + b * nq + j, 0)),
                  pl.BlockSpec((t, 2 * KV_W), lambda b, j: (sb0 + b, ATTN_W // (2 * KV_W))),
                  pl.BlockSpec((1, past, KV_W), lambda b, j: (b, 0, 0)),
                  pl.BlockSpec((1, past, KV_W), lambda b, j: (b, 0, 0)),
                  pl.BlockSpec((ROW_TILE, ATTN_W), lambda b, j: (j, 0)),
                  pl.BlockSpec((ROW_TILE, ATTN_W), lambda b, j: (j, 0)),
                  pl.BlockSpec((t, KV_W), const), pl.BlockSpec((t, KV_W), const),
                  pl.BlockSpec((1, ATTN_W), const), pl.BlockSpec((1, KV_W), const), pl.BlockSpec((1, ATTN_W), const),
                  pl.BlockSpec((ATTN_W, ATTN_W), const), pl.BlockSpec((KV_W, KV_W), const),
                  pl.BlockSpec((KV_W, ATTN_W), const)],
        out_specs=pl.BlockSpec((ROW_TILE, ATTN_W), lambda b, j: (b * nq + j, 0)),
        out_shape=jax.ShapeDtypeStruct((n_seq * t, ATTN_W), bf16),
        scratch_shapes=[pltpu.VMEM((past + t, ATTN_W), bf16), pltpu.VMEM((past + t, ATTN_W), bf16),
                        pltpu.VMEM((ROW_TILE, ATTN_W), f32)],
        compiler_params=_cparams(("parallel", "arbitrary")),
        name="attn_lat",
    )(pa, pa, cache_k, cache_v, cq, sq, cq, sq, qg, kg, og, bmq, bmk, rep)


def _lru_kernel(x_ref, h0_ref, cw_ref, cb_ref, wg_ref, bg_ref, lam_ref, g_ref,
                o_ref, fin_ref, xs, a_s, u_s, h_s, *, t):
    ng = t // SUBLANES
    lx = x_ref[:, 0:LRU_W].astype(f32)
    ly = x_ref[:, LRU_W:2 * LRU_W].astype(f32)
    xs[0:SUBLANES, :] = jnp.zeros((SUBLANES, LRU_W), f32)
    xs[SUBLANES + t:2 * SUBLANES + t, :] = jnp.zeros((SUBLANES, LRU_W), f32)
    xs[SUBLANES:SUBLANES + t, :] = lx
    xc = jnp.broadcast_to(cb_ref[...], (t, LRU_W))
    for j in range(CONV_W):
        off = SUBLANES + j - CONV_LEFT
        xc = xc + xs[off:off + t, :] * cw_ref[j:j + 1, :]
    gates = _sigmoid(_dot(xc.astype(bf16), wg_ref[...]) + bg_ref[...])
    sp = _softplus(-lam_ref[...])
    sub = lax.broadcasted_iota(i32, (ng, SUBLANES, LRU_W), 1)
    for d in range(2):
        r = gates[:, (2 * d) * LRU_W:(2 * d + 1) * LRU_W]
        gi = gates[:, (2 * d + 1) * LRU_W:(2 * d + 2) * LRU_W]
        log_a = -LRU_C * r * sp[d:d + 1, :]
        a = jnp.exp(log_a)
        th = jnp.tanh(log_a)
        u = jnp.sqrt(-2.0 * th / (1.0 - th)) * (gi * xc)
        a = a.reshape(ng, SUBLANES, LRU_W)
        u = u.reshape(ng, SUBLANES, LRU_W)
        for s in (1, 2, 4):
            if d == 0:
                a_sh = pltpu.roll(a, s, 1)
                u_sh = pltpu.roll(u, s, 1)
                ok = sub >= s
            else:
                a_sh = pltpu.roll(a, SUBLANES - s, 1)
                u_sh = pltpu.roll(u, SUBLANES - s, 1)
                ok = sub < SUBLANES - s
            u = jnp.where(ok, a * u_sh + u, u)
            a = jnp.where(ok, a * a_sh, a)
        a_s[...] = a.reshape(t, LRU_W)
        u_s[...] = u.reshape(t, LRU_W)
        h0 = jnp.broadcast_to(h0_ref[0, d:d + 1, :], (SUBLANES, LRU_W))
        edge = SUBLANES - 1 if d == 0 else 0

        def body(i, carry, d=d, edge=edge):
            g = i if d == 0 else ng - 1 - i
            rows = pl.ds(pl.multiple_of(g * SUBLANES, SUBLANES), SUBLANES)
            h = a_s[rows, :] * carry + u_s[rows, :]
            if d == 0:
                h_s[rows, :] = h
            else:
                h_s[rows, :] = h_s[rows, :] + h
            return jnp.broadcast_to(h[edge:edge + 1, :], (SUBLANES, LRU_W))

        last = lax.fori_loop(0, ng, body, h0)
        fin_ref[0, d:d + 1, :] = last[0:1, :]
    o_ref[...] = _rms(h_s[...] * _gelu_tanh(ly), g_ref[...]).astype(bf16)


def _lru_call(pl_, row0, n_seq, t, h0, cw, cb, wg, bg, lam, g):
    sb0 = row0 // t
    assert row0 % t == 0
    const = lambda b: (0, 0)
    in_specs = [pl.BlockSpec((t, PROJ_L), lambda b: (sb0 + b, 0)),
                pl.BlockSpec((1, 2, LRU_W), lambda b: (b, 0, 0)),
                pl.BlockSpec((CONV_W, LRU_W), const), pl.BlockSpec((1, LRU_W), const),
                pl.BlockSpec((LRU_W, 4 * LRU_W), const), pl.BlockSpec((1, 4 * LRU_W), const),
                pl.BlockSpec((2, LRU_W), const), pl.BlockSpec((1, LRU_W), const)]
    args = [pl_, h0, cw, cb, wg, bg, lam, g]
    return pl.pallas_call(
        functools.partial(_lru_kernel, t=t),
        grid=(n_seq,),
        in_specs=in_specs,
        out_specs=[pl.BlockSpec((t, LRU_W), lambda b: (b, 0)),
                   pl.BlockSpec((1, 2, LRU_W), lambda b: (b, 0, 0))],
        out_shape=[jax.ShapeDtypeStruct((n_seq * t, LRU_W), bf16), jax.ShapeDtypeStruct((n_seq, 2, LRU_W), f32)],
        scratch_shapes=[pltpu.VMEM((t + 2 * SUBLANES, LRU_W), f32), pltpu.VMEM((t, LRU_W), f32),
                        pltpu.VMEM((t, LRU_W), f32), pltpu.VMEM((t, LRU_W), f32)],
        compiler_params=_cparams(("parallel",)),
        name="rglru_t%d" % t,
    )(*args)


def _bcast_rows(b, period, off):
    w = b.shape[1]
    return jnp.concatenate(
        [jnp.broadcast_to(b[i * period + off:i * period + off + 1, :], (period, w)) for i in range(b.shape[0] // period)],
        axis=0)


def _gla_block(q, k, v, la, reverse, st_ref):
    n = GLA_BLOCK
    row = lax.broadcasted_iota(i32, (n, n), 0)
    col = lax.broadcasted_iota(i32, (n, n), 1)
    same64 = (row >> 6) == (col >> 6)
    same32 = (row >> 5) == (col >> 5)
    same16 = (row >> 4) == (col >> 4)
    if not reverse:
        cum = (same64 & (col <= row)).astype(bf16)
        m1 = same64 & ((row & 63) >= 32) & ((col & 63) < 32)
        m2 = same32 & ((row & 31) >= 16) & ((col & 31) < 16)
        m3 = same16 & (col <= row)
        offs = (31, 15, 7, 63)
    else:
        cum = (same64 & (col >= row)).astype(bf16)
        m1 = same64 & ((row & 63) < 32) & ((col & 63) >= 32)
        m2 = same32 & ((row & 31) < 16) & ((col & 31) >= 16)
        m3 = same16 & (col >= row)
        offs = (32, 16, 8, 0)
    b = _dot_2x(cum, la)
    r1 = _bcast_rows(b, 64, offs[0])
    r2 = _bcast_rows(b, 32, offs[1])
    r3 = _bcast_rows(b, 16, offs[2])
    bl = _bcast_rows(b, 64, offs[3])
    q1 = q * jnp.exp(jnp.minimum(b - r1, 0.0))
    k1 = (k * jnp.exp(jnp.minimum(r1 - b, 0.0))).astype(bf16)
    q2 = q * jnp.exp(jnp.minimum(b - r2, 0.0))
    k2 = (k * jnp.exp(jnp.minimum(r2 - b, 0.0))).astype(bf16)
    q3 = q * jnp.exp(b - r3)
    k3 = (k * jnp.exp(r3 - b)).astype(bf16)
    qe = (q * jnp.exp(b)).astype(bf16)
    kl = (k * jnp.exp(bl - b)).astype(bf16)
    dec = jnp.exp(bl)
    vb = v.astype(bf16)
    zq = jnp.zeros((n, n), f32)
    intra = jnp.zeros((n, n), f32)
    for h in range(GLA_H):
        hm = (col >> 6) == h
        a1 = _dot_nt(jnp.where(hm, q1, zq).astype(bf16), k1)
        a2 = _dot_nt(jnp.where(hm, q2, zq).astype(bf16), k2)
        a3 = _dot_nt(jnp.where(hm, q3, zq).astype(bf16), k3)
        att = jnp.where(m1, a1, jnp.where(m2, a2, jnp.where(m3, a3, zq)))
        intra = intra + _dot(att.astype(bf16), jnp.where(hm, v, zq).astype(bf16))
    outs = [None] * 4
    for c in (range(4) if not reverse else range(3, -1, -1)):
        rs = slice(64 * c, 64 * c + 64)
        st = st_ref[...]
        inter = _dot_nt(qe[rs], st.astype(bf16))
        kv = _dot_tn(vb[rs], kl[rs])
        drow = dec[64 * c:64 * c + 1, :]
        st_ref[...] = st * drow + jnp.where(same64, kv, zq)
        outs[c] = intra[rs] + inter
    return jnp.concatenate(outs, axis=0)


def _gla_kernel(x_ref, s0_ref, wa_ref, ba_ref, g_ref, bm_ref, o_ref, fin_ref, st_s, o_s, *, t):
    nblk = t // GLA_BLOCK

    def block(j, reverse):
        rows = pl.ds(j * GLA_BLOCK if isinstance(j, int) else pl.multiple_of(j * GLA_BLOCK, GLA_BLOCK), GLA_BLOCK)
        q = x_ref[rows, 0:GLA_W].astype(f32) * GLA_DK ** -0.5
        k = x_ref[rows, GLA_W:2 * GLA_W].astype(f32)
        v = x_ref[rows, 2 * GLA_W:3 * GLA_W].astype(f32)
        ga = x_ref[rows, 4 * GLA_W:4 * GLA_W + LANES]
        d = 1 if reverse else 0
        z = _dot(ga.astype(bf16), wa_ref[:, d * GLA_W:(d + 1) * GLA_W]) + ba_ref[:, d * GLA_W:(d + 1) * GLA_W]
        la = -_softplus(-z) * (1.0 / GLA_TAU)
        o = _gla_block(q, k, v, la, reverse, st_s)
        if reverse:
            o_s[rows, :] = o_s[rows, :] + o
        else:
            o_s[rows, :] = o

    for d in range(2):
        if s0_ref is None:
            st_s[...] = jnp.zeros((GLA_W, GLA_W), f32)
        else:
            st_s[...] = s0_ref[0, d]
        if nblk == 1:
            block(0, d == 1)
        else:
            def body(i, carry, d=d):
                block(i if d == 0 else nblk - 1 - i, d == 1)
                return carry
            lax.fori_loop(0, nblk, body, 0)
        st_t = st_s[...].T
        for h in range(GLA_H):
            fin_ref[0, d, h] = st_t[h * GLA_DK:(h + 1) * GLA_DK, h * GLA_DK:(h + 1) * GLA_DK]
    gg = x_ref[:, 3 * GLA_W:4 * GLA_W].astype(f32)
    o_ref[...] = (_head_rms(o_s[...], g_ref[...], bm_ref[...]) * _silu(gg)).astype(bf16)


def _gla_call(pg, row0, n_seq, t, s0, wa, ba, g, bm):
    sb0 = row0 // t
    assert row0 % t == 0 and t % GLA_BLOCK == 0
    const = lambda b: (0, 0)
    in_specs = [pl.BlockSpec((t, PROJ_G), lambda b: (sb0 + b, 0))]
    args = [pg]
    if s0 is not None:
        in_specs.append(pl.BlockSpec((1, 2, GLA_W, GLA_W), lambda b: (b, 0, 0, 0)))
        args.append(s0)
    in_specs += [pl.BlockSpec((LANES, 2 * GLA_W), const), pl.BlockSpec((1, 2 * GLA_W), const),
                 pl.BlockSpec((1, GLA_W), const), pl.BlockSpec((GLA_W, GLA_W), const)]
    args += [wa, ba, g, bm]

    def kern(*refs):
        refs = list(refs)
        x_ref = refs.pop(0)
        s0_ref = refs.pop(0) if s0 is not None else None
        wa_ref, ba_ref, g_ref, bm_ref = refs[:4]
        _gla_kernel(x_ref, s0_ref, wa_ref, ba_ref, g_ref, bm_ref, *refs[4:], t=t)

    return pl.pallas_call(
        kern,
        grid=(n_seq,),
        in_specs=in_specs,
        out_specs=[pl.BlockSpec((t, GLA_W), lambda b: (b, 0)),
                   pl.BlockSpec((1, 2, GLA_H, GLA_DK, GLA_DK), lambda b: (b, 0, 0, 0, 0))],
        out_shape=[jax.ShapeDtypeStruct((n_seq * t, GLA_W), bf16),
                   jax.ShapeDtypeStruct((n_seq, 2, GLA_H, GLA_DK, GLA_DK), f32)],
        scratch_shapes=[pltpu.VMEM((GLA_W, GLA_W), f32), pltpu.VMEM((t, GLA_W), f32)],
        compiler_params=_cparams(("parallel",)),
        name="gla_t%d" % t,
    )(*args)


def _out_kernel(xc_ref, xl_ref, ac_ref, lc_ref, gc_ref, al_ref, ll_ref, gl_ref, mod_ref, wo_ref, g2_ref,
                rwh_ref, rwl_ref, s1_ref, s3_ref, s2_ref, xb_ref, h2_ref, sc_ref, *, n_ctx_tiles):
    is_ctx = pl.program_id(0) < n_ctx_tiles
    pick = lambda c_ref, l_ref: jnp.where(is_ctx, c_ref[...], l_ref[...]).astype(bf16)
    mod = mod_ref[0]
    g1 = mod[:, 2 * D_MODEL:3 * D_MODEL]
    sh2 = mod[:, 3 * D_MODEL:4 * D_MODEL]
    sc2 = mod[:, 4 * D_MODEL:5 * D_MODEL]
    g2 = mod[:, 5 * D_MODEL:6 * D_MODEL]
    m = (_dot(pick(ac_ref, al_ref), wo_ref[0:ATTN_W, :])
         + _dot(pick(lc_ref, ll_ref), wo_ref[ATTN_W:ATTN_W + LRU_W, :])
         + _dot(pick(gc_ref, gl_ref), wo_ref[ATTN_W + LRU_W:ATTN_W + LRU_W + GLA_W, :]))
    x1 = jnp.where(is_ctx, xc_ref[...], xl_ref[...]) + g1 * m
    h2 = _rms(x1, g2_ref[...]) * (1.0 + sc2) + sh2
    h2_ref[...] = _pack_pairs(h2)
    h_hi, h_lo = _split(h2)
    logits_t = _dot_nt(rwh_ref[...], h_hi) + _dot_nt(rwh_ref[...], h_lo) + _dot_nt(rwl_ref[...], h_hi)
    sc_ref[...] = _sigmoid(logits_t)
    hb = h2.astype(bf16)
    act = (_silu(_dot(hb, s1_ref[...])) * _dot(hb, s3_ref[...])).astype(bf16)
    xb_ref[...] = x1 + g2 * _dot(act, s2_ref[...])


def _out_call(xc, xl, mix_ctx, mix_lat, mod, wo, g2, rwh, rwl, s1, s3, s2, seq_row):
    n = xc.shape[0] + xl.shape[0]
    nct = xc.shape[0] // PROJ_TILE
    const = lambda i: (0, 0)
    rowb = lambda w: pl.BlockSpec((PROJ_TILE, w), lambda i: (i, 0))
    ctxb = lambda w: _row_pair_specs(nct, w)[0]
    latb = lambda w: _row_pair_specs(nct, w)[1]
    return pl.pallas_call(
        functools.partial(_out_kernel, n_ctx_tiles=nct),
        grid=(n // PROJ_TILE,),
        in_specs=[ctxb(D_MODEL), latb(D_MODEL),
                  ctxb(ATTN_W), ctxb(LRU_W), ctxb(GLA_W), latb(ATTN_W), latb(LRU_W), latb(GLA_W),
                  pl.BlockSpec((1, 1, 6 * D_MODEL), lambda i: (seq_row(i), 0, 0)),
                  pl.BlockSpec((D_MODEL, D_MODEL), const), pl.BlockSpec((1, D_MODEL), const),
                  pl.BlockSpec((N_EXPERTS, D_MODEL), const), pl.BlockSpec((N_EXPERTS, D_MODEL), const),
                  pl.BlockSpec((D_MODEL, EXPERT_FF), const), pl.BlockSpec((D_MODEL, EXPERT_FF), const),
                  pl.BlockSpec((EXPERT_FF, D_MODEL), const)],
        out_specs=[rowb(D_MODEL),
                   rowb(D_PACK),
                   pl.BlockSpec((N_EXPERTS, PROJ_TILE), lambda i: (0, i))],
        out_shape=[jax.ShapeDtypeStruct((n, D_MODEL), f32),
                   jax.ShapeDtypeStruct((n, D_PACK), i32),
                   jax.ShapeDtypeStruct((N_EXPERTS, n), f32)],
        compiler_params=_cparams(("parallel",)),
        name="out_proj",
    )(xc, xl, *mix_ctx, *mix_lat, mod, wo, g2, rwh, rwl, s1, s3, s2)


def _set_row(acc, k, row):
    sub = lax.broadcasted_iota(i32, acc.shape, 0)
    return jnp.where(sub == k, jnp.broadcast_to(row, acc.shape), acc)


def _route_kernel(sc_ref, rb_ref, idx_ref, gate_ref, rank_ref, cnt_ref, run_s):
    tm = sc_ref.shape[1]

    @pl.when(pl.program_id(0) == 0)
    def _():
        run_s[...] = jnp.zeros_like(run_s)

    scores = sc_ref[...]
    sel = scores + rb_ref[...]
    erow = lax.broadcasted_iota(i32, (N_EXPERTS, tm), 0).astype(f32)
    neg = jnp.full((N_EXPERTS, tm), -jnp.inf, f32)
    hots = []
    idx_o = jnp.zeros((TOP_K, tm), f32)
    gate_o = jnp.zeros((TOP_K, tm), f32)
    gsum = jnp.zeros((1, tm), f32)
    for k in range(TOP_K):
        m = jnp.max(sel, axis=0, keepdims=True)
        idx_f = jnp.min(jnp.where(sel == m, erow, float(N_EXPERTS)), axis=0, keepdims=True)
        hot = erow == idx_f
        gk = jnp.sum(jnp.where(hot, scores, 0.0), axis=0, keepdims=True)
        sel = jnp.where(hot, neg, sel)
        hots.append(hot)
        gsum = gsum + gk
        idx_o = _set_row(idx_o, k, idx_f)
        gate_o = _set_row(gate_o, k, gk)
    chosen = jnp.where(sel == neg, 1.0, 0.0)
    gate_ref[...] = gate_o / gsum * ROUTED_SCALE
    idx_ref[...] = idx_o.astype(i32)
    r = lax.broadcasted_iota(i32, (tm, tm), 0)
    c = lax.broadcasted_iota(i32, (tm, tm), 1)
    pos = _dot(chosen.astype(bf16), (r < c).astype(bf16)) + run_s[...]
    rank_o = jnp.zeros((TOP_K, tm), f32)
    for k in range(TOP_K):
        rank_o = _set_row(rank_o, k, jnp.sum(jnp.where(hots[k], pos, 0.0), axis=0, keepdims=True))
    rank_ref[...] = rank_o.astype(i32)
    run_s[...] = run_s[...] + jnp.sum(chosen, axis=1, keepdims=True)
    cnt_ref[...] = run_s[...]


def _route_call(scores_t, rb):
    n = scores_t.shape[1]
    colb = lambda r: pl.BlockSpec((r, ROUTE_TILE), lambda i: (0, i))
    cnt_spec = pl.BlockSpec((N_EXPERTS, 1), lambda i: (0, 0))
    return pl.pallas_call(
        _route_kernel,
        grid=(n // ROUTE_TILE,),
        in_specs=[colb(N_EXPERTS), cnt_spec],
        out_specs=[colb(TOP_K), colb(TOP_K), colb(TOP_K), cnt_spec],
        out_shape=[jax.ShapeDtypeStruct((TOP_K, n), i32), jax.ShapeDtypeStruct((TOP_K, n), f32),
                   jax.ShapeDtypeStruct((TOP_K, n), i32), jax.ShapeDtypeStruct((N_EXPERTS, 1), f32)],
        scratch_shapes=[pltpu.VMEM((N_EXPERTS, 1), f32)],
        compiler_params=_cparams(("arbitrary",)),
        name="route",
    )(scores_t, rb)


def _dest_kernel(idx_ref, rank_ref, start_ref, dest_ref):
    tm = idx_ref.shape[1]
    erow = lax.broadcasted_iota(i32, (N_EXPERTS, tm), 0)
    start = jnp.broadcast_to(start_ref[...], (N_EXPERTS, tm))
    idx = idx_ref[...]
    dest = jnp.zeros((TOP_K, tm), f32)
    for k in range(TOP_K):
        hot = erow == idx[k:k + 1, :]
        dest = _set_row(dest, k, jnp.sum(jnp.where(hot, start, 0.0), axis=0, keepdims=True))
    dest_ref[...] = dest.astype(i32) + rank_ref[...]


def _dest_call(idx, rank, start):
    n = idx.shape[1]
    colb = lambda r: pl.BlockSpec((r, ROUTE_TILE), lambda i: (0, i))
    return pl.pallas_call(
        _dest_kernel,
        grid=(n // ROUTE_TILE,),
        in_specs=[colb(TOP_K), colb(TOP_K), pl.BlockSpec((N_EXPERTS, 1), lambda i: (0, 0))],
        out_specs=colb(TOP_K),
        out_shape=jax.ShapeDtypeStruct((TOP_K, n), i32),
        compiler_params=_cparams(("parallel",)),
        name="dest",
    )(idx, rank, start)


def _sc_workers():
    info = plsc.get_sparse_core_info()
    mesh = plsc.VectorSubcoreMesh(core_axis_name="c", subcore_axis_name="s")
    worker_id = lambda: lax.axis_index("s") * info.num_cores + lax.axis_index("c")
    return mesh, info.num_cores * info.num_subcores, worker_id


def _scatter_rows_call(rows, dest_km, n_slots):
    n, d = rows.shape
    ch = SC_SCATTER_CHUNK
    mesh, n_workers, worker_id = _sc_workers()
    chunks_per_worker = n // (n_workers * ch)
    assert n % (n_workers * ch) == 0
    idx_cm = dest_km.reshape(TOP_K, n // ch, ch).transpose(1, 0, 2)

    def body(rows_hbm, idx_hbm, out_hbm, idx_v, rows_v, sem):
        base = worker_id() * chunks_per_worker

        @pl.loop(0, chunks_per_worker)
        def _(i):
            c = base + i
            pltpu.sync_copy(rows_hbm.at[pl.ds(pl.multiple_of(c * ch, ch), ch)], rows_v)
            pltpu.sync_copy(idx_hbm.at[c], idx_v)
            copies = [pltpu.async_copy(rows_v, out_hbm.at[idx_v.at[k]], sem) for k in range(TOP_K)]
            for cp in copies:
                cp.wait()

    return pl.kernel(
        body,
        out_type=jax.ShapeDtypeStruct((n_slots, d), rows.dtype),
        mesh=mesh,
        scratch_types=[pltpu.VMEM((TOP_K, ch), i32), pltpu.VMEM((ch, d), rows.dtype), pltpu.SemaphoreType.DMA],
        name="scatter_rows",
    )(rows, idx_cm)


def _expert_kernel(nu_ref, bv_ref, ord_ref, eseq_ref, nex_ref, x_ref, w1_hbm, w3_hbm, w2_hbm, y_ref,
                   w1_f, w3_f, w2_f, w1_s, w3_s, w2_s, sems, *, layer):
    def weight_copies(j):
        e = eseq_ref[j]
        slot = j % W_RING
        return [pltpu.make_async_copy(w1_hbm.at[layer, e], w1_f.at[slot], sems.at[slot, 0]),
                pltpu.make_async_copy(w3_hbm.at[layer, e], w3_f.at[slot], sems.at[slot, 1]),
                pltpu.make_async_copy(w2_hbm.at[layer, e], w2_f.at[slot], sems.at[slot, 2])]

    def start_if_exists(j):
        @pl.when(j < nex_ref[0])
        def _():
            for cp in weight_copies(j):
                cp.start()

    for u in range(STEP_BLOCKS):
        i = pl.program_id(0) * STEP_BLOCKS + u
        rows = slice(u * SLOT_BLOCK, (u + 1) * SLOT_BLOCK)

        @pl.when(i < nu_ref[0])
        def _(i=i, rows=rows):
            j = ord_ref[i]

            @pl.when(i == 0)
            def _():
                for ahead in range(W_RING - 1):
                    start_if_exists(ahead)

            @pl.when((i == 0) | (j != ord_ref[jnp.maximum(i - 1, 0)]))
            def _():
                start_if_exists(j + W_RING - 1)
                for cp in weight_copies(j):
                    cp.wait()
                slot = j % W_RING
                w1_s[...] = w1_f[slot].astype(bf16)
                w3_s[...] = w3_f[slot].astype(bf16)
                w2_s[...] = w2_f[slot].astype(bf16)

            row = lax.broadcasted_iota(i32, (SLOT_BLOCK, D_PACK), 0)
            x_hi, x_lo = _unpack_pairs(jnp.where(row < bv_ref[i], x_ref[rows, :], 0))
            x = jnp.concatenate([x_hi, x_lo], axis=1).astype(bf16)
            act = (_silu(_dot(x, w1_s[...])) * _dot(x, w3_s[...])).astype(bf16)
            y_ref[rows, :] = _pack_pairs(_dot(act, w2_s[...]))

        @pl.when((i >= nu_ref[0]) & (pl.program_id(0) * STEP_BLOCKS < nu_ref[0]))
        def _(rows=rows):
            y_ref[rows, :] = x_ref[rows, :]


def _expert_call(n_used, blk_valid, blk_ord, expert_seq, n_seq_experts, slots, w1, w3, w2, layer):
    step_rows = STEP_BLOCKS * SLOT_BLOCK
    assert slots.shape[0] % step_rows == 0
    blk = lambda s, nu, *_: (jnp.minimum(s, (nu[0] - 1) // STEP_BLOCKS), 0)
    hbm = pl.BlockSpec(memory_space=pl.ANY)
    grid_spec = pltpu.PrefetchScalarGridSpec(
        num_scalar_prefetch=5,
        grid=(slots.shape[0] // step_rows,),
        in_specs=[pl.BlockSpec((step_rows, D_PACK), blk), hbm, hbm, hbm],
        out_specs=pl.BlockSpec((step_rows, D_PACK), blk),
        scratch_shapes=[pltpu.VMEM((W_RING, D_MODEL, EXPERT_FF), f32), pltpu.VMEM((W_RING, D_MODEL, EXPERT_FF), f32),
                        pltpu.VMEM((W_RING, EXPERT_FF, D_MODEL), f32),
                        pltpu.VMEM((D_MODEL, EXPERT_FF), bf16), pltpu.VMEM((D_MODEL, EXPERT_FF), bf16),
                        pltpu.VMEM((EXPERT_FF, D_MODEL), bf16),
                        pltpu.SemaphoreType.DMA((W_RING, 3))],
    )
    return pl.pallas_call(
        functools.partial(_expert_kernel, layer=layer),
        grid_spec=grid_spec,
        out_shape=jax.ShapeDtypeStruct(slots.shape, slots.dtype),
        input_output_aliases={5: 0},
        compiler_params=_cparams(("arbitrary",)),
        name="experts",
    )(n_used, blk_valid, blk_ord, expert_seq, n_seq_experts, slots, w1, w3, w2)


def _gather_rows_call(table, idx):
    n_idx = idx.shape[0]
    d = table.shape[1]
    mesh, n_workers, worker_id = _sc_workers()
    per_worker = n_idx // n_workers
    assert n_idx % (n_workers * SC_CHUNK) == 0

    n_chunks = per_worker // SC_CHUNK
    assert n_chunks % 2 == 0

    def body(table_hbm, idx_hbm, out_hbm, idx_a, idx_b, rows_a, rows_b, sem_a, sem_b):
        base = worker_id() * per_worker
        bufs = ((idx_a, rows_a, sem_a), (idx_b, rows_b, sem_b))

        def start_gather(c, buf):
            idx_v, rows_v, sem = buf
            off = pl.multiple_of(base + c * SC_CHUNK, SC_CHUNK)
            pltpu.sync_copy(idx_hbm.at[pl.ds(off, SC_CHUNK)], idx_v)
            pltpu.async_copy(table_hbm.at[idx_v], rows_v, sem)

        start_gather(0, bufs[0])

        @pl.loop(0, n_chunks, step=2)
        def _(c0):
            for b in range(2):
                c = c0 + b
                idx_v, rows_v, sem = bufs[b]

                @pl.when(c + 1 < n_chunks)
                def _():
                    start_gather(c + 1, bufs[1 - b])

                pltpu.make_async_copy(table_hbm.at[idx_v], rows_v, sem).wait()
                off = pl.multiple_of(base + c * SC_CHUNK, SC_CHUNK)
                pltpu.sync_copy(rows_v, out_hbm.at[pl.ds(off, SC_CHUNK)])

    return pl.kernel(
        body,
        out_type=jax.ShapeDtypeStruct((n_idx, d), table.dtype),
        mesh=mesh,
        scratch_types=[pltpu.VMEM((SC_CHUNK,), i32), pltpu.VMEM((SC_CHUNK,), i32),
                       pltpu.VMEM((SC_CHUNK, d), table.dtype), pltpu.VMEM((SC_CHUNK, d), table.dtype),
                       pltpu.SemaphoreType.DMA, pltpu.SemaphoreType.DMA],
        name="gather_rows",
    )(table, idx)


def _combine_kernel(gate_ref, xb_ref, mod_ref, y_ref, o_ref):
    g2 = mod_ref[0][:, 5 * D_MODEL:6 * D_MODEL]
    gates = gate_ref[...]
    acc_hi = jnp.zeros((gates.shape[0], D_PACK), f32)
    acc_lo = jnp.zeros((gates.shape[0], D_PACK), f32)
    for k in range(TOP_K):
        y_hi, y_lo = _unpack_pairs(y_ref[k])
        acc_hi = acc_hi + gates[:, k:k + 1] * y_hi
        acc_lo = acc_lo + gates[:, k:k + 1] * y_lo
    o_ref[:, 0:D_PACK] = xb_ref[:, 0:D_PACK] + g2[:, 0:D_PACK] * acc_hi
    o_ref[:, D_PACK:D_MODEL] = xb_ref[:, D_PACK:D_MODEL] + g2[:, D_PACK:D_MODEL] * acc_lo


def _combine_call(gates, xbase, mod, ygath, seq_row_c, row0):
    tm = COMBINE_TILE
    n_rows = ygath.shape[1]
    assert row0 % tm == 0 and n_rows % tm == 0
    b0 = row0 // tm
    return pl.pallas_call(
        _combine_kernel,
        grid=(n_rows // tm,),
        in_specs=[pl.BlockSpec((tm, TOP_K), lambda i: (b0 + i, 0)),
                  pl.BlockSpec((tm, D_MODEL), lambda i: (b0 + i, 0)),
                  pl.BlockSpec((1, 1, 6 * D_MODEL), lambda i: (seq_row_c(b0 + i), 0, 0)),
                  pl.BlockSpec((TOP_K, tm, D_PACK), lambda i: (0, i, 0))],
        out_specs=pl.BlockSpec((tm, D_MODEL), lambda i: (i, 0)),
        out_shape=jax.ShapeDtypeStruct((n_rows, D_MODEL), f32),
        compiler_params=_cparams(("parallel",)),
        name="combine",
    )(gates, xbase, mod, ygath)


def _block_avg(width, group):
    r = np.arange(width)
    return jnp.asarray((r[:, None] // group == r[None, :] // group).astype(np.float32) / group, dtype=bf16)


def _kv_replicate():
    c = np.arange(ATTN_W)
    src = (c // (ATTN_W // N_KV_HEADS)) * HEAD_DIM + c % HEAD_DIM
    return jnp.asarray((np.arange(KV_W)[:, None] == src[None, :]).astype(np.float32), dtype=bf16)


def _rope_lane_tables(n_tok):
    rows = n_tok // GRID_W
    r = jnp.repeat(jnp.arange(rows, dtype=f32), GRID_W)
    col = jnp.tile(jnp.arange(GRID_W, dtype=f32), rows)
    inv = ROPE_THETA ** (-jnp.arange(ROPE_FREQ, dtype=f32) / ROPE_FREQ)
    ar = r[:, None] * inv
    ac = col[:, None] * inv
    cos_h = jnp.concatenate([jnp.cos(ar), jnp.cos(ar), jnp.cos(ac), jnp.cos(ac)], axis=-1)
    sin_h = jnp.concatenate([-jnp.sin(ar), jnp.sin(ar), -jnp.sin(ac), jnp.sin(ac)], axis=-1)
    return jnp.tile(cos_h, (1, N_HEADS)), jnp.tile(sin_h, (1, N_HEADS))


def _block_diag(w):
    nb, bw, _ = w.shape
    eye = jnp.eye(nb, dtype=w.dtype)
    return (w[:, :, None, :] * eye[:, None, :, None]).reshape(nb * bw, nb * bw)


def _gla_state_in(s):
    bsz = s.shape[0]
    eye = jnp.eye(GLA_H, dtype=s.dtype)
    st = jnp.swapaxes(s, -1, -2)
    big = st[:, :, :, :, None, :] * eye[None, None, :, None, :, None]
    return big.reshape(bsz, 2, GLA_W, GLA_W)


def kernel(x_prompt, x_sample, cache_k, cache_v, state_lru, state_gla, c, c_ctx, ada_w, ada_b, norm1_g, norm2_g, w_in, q_norm_g, k_norm_g, attn_out_g, conv_w, conv_b, lru_wa, lru_ba, lru_wi, lru_bi, lru_lambda, lru_out_g, gla_wa2, gla_ba, gla_out_g, w_out, router_w, router_b, exp_w1, exp_w3, exp_w2, sh_w1, sh_w3, sh_w2):
    bc, tc, _ = x_prompt.shape
    bl, tl, _ = x_sample.shape
    depth = w_in.shape[0]
    nc = bc * tc
    n = nc + bl * tl
    past = cache_k.shape[2]
    assert tc == ROW_TILE and tl % ROW_TILE == 0 and nc % tl == 0 and bl + 1 <= SUBLANES
    assert nc % PROJ_TILE == 0 and tl % PROJ_TILE == 0 and nc % COMBINE_TILE == 0 and tl % COMBINE_TILE == 0

    def seq_row_for(tile):
        def seq_row(i):
            return jnp.where(i < nc // tile, 0, 1 + (i - nc // tile) // (tl // tile))
        return seq_row

    seq_row = seq_row_for(PROJ_TILE)
    seq_row_c = seq_row_for(COMBINE_TILE)

    xc = x_prompt.reshape(nc, D_MODEL)
    xl = x_sample.reshape(bl * tl, D_MODEL)
    cond = jnp.zeros((SUBLANES, D_MODEL), f32).at[0].set(c_ctx).at[1:1 + bl].set(c)
    mods = _ada_call(cond, ada_w, ada_b)

    bmq = _block_avg(ATTN_W, HEAD_DIM)
    bmk = _block_avg(KV_W, HEAD_DIM)
    bmg = _block_avg(GLA_W, GLA_DK)
    rep = _kv_replicate()
    cos_t, sin_t = _rope_lane_tables(tl)
    step_rows = STEP_BLOCKS * SLOT_BLOCK
    n_slots = -(-(n * TOP_K + N_EXPERTS * (SLOT_BLOCK - 1)) // step_rows) * step_rows
    tile8 = lambda v: jnp.tile(v, N_HEADS)[None, :]

    ks, vs, lrus, glas = [], [], [], []
    for l in range(depth):
        mod = mods[l].reshape(SUBLANES, 1, 6 * D_MODEL)
        pa, pl_, pg = _proj_call(xc, xl, mod, norm1_g[l][None, :], w_in, l, seq_row)

        qg, kg, og = tile8(q_norm_g[l]), jnp.tile(k_norm_g[l], N_KV_HEADS)[None, :], attn_out_g[l][None, :]
        attn_c, k_new, v_new = _attn_ctx_call(pa, bc, tc, qg, kg, og, bmq, bmk, rep)
        attn_l = _attn_lat_call(pa, nc, bl, tl, cache_k[:, l].reshape(bl, past, KV_W),
                                cache_v[:, l].reshape(bl, past, KV_W), cos_t, sin_t, qg, kg, og, bmq, bmk, rep)
        ks.append(k_new.reshape(bc, tc, N_KV_HEADS, HEAD_DIM))
        vs.append(v_new.reshape(bc, tc, N_KV_HEADS, HEAD_DIM))

        wg = jnp.concatenate([_block_diag(lru_wa[l, 0]), _block_diag(lru_wi[l, 0]),
                              _block_diag(lru_wa[l, 1]), _block_diag(lru_wi[l, 1])], axis=1).astype(bf16)
        bg = jnp.concatenate([lru_ba[l, 0], lru_bi[l, 0], lru_ba[l, 1], lru_bi[l, 1]])[None, :]
        lru_args = (conv_w[l], conv_b[l][None, :], wg, bg, lru_lambda[l], lru_out_g[l][None, :])
        lru_c, lru_fin = _lru_call(pl_, 0, bc, tc, jnp.zeros((bc, 2, LRU_W), f32), *lru_args)
        lru_l, _ = _lru_call(pl_, nc, bl, tl, state_lru[:, l], *lru_args)
        lrus.append(lru_fin)

        wa = jnp.zeros((LANES, 2 * GLA_W), f32)
        wa = wa.at[0:GLA_RANK, 0:GLA_W].set(gla_wa2[l, 0]).at[GLA_RANK:2 * GLA_RANK, GLA_W:].set(gla_wa2[l, 1])
        gla_args = (wa.astype(bf16), gla_ba[l].reshape(1, 2 * GLA_W), gla_out_g[l].reshape(1, GLA_W), bmg)
        gla_c, gla_fin = _gla_call(pg, 0, bc, tc, None, *gla_args)
        gla_l, _ = _gla_call(pg, nc, bl, tl, _gla_state_in(state_gla[:, l]), *gla_args)
        glas.append(gla_fin)

        rw_t = router_w[l].T
        rw_hi = rw_t.astype(bf16)
        rw_lo = (rw_t - rw_hi.astype(f32)).astype(bf16)
        xbase, h2, scores_t = _out_call(xc, xl, (attn_c, lru_c, gla_c), (attn_l, lru_l, gla_l), mod, w_out[l].astype(bf16),
                                        norm2_g[l][None, :], rw_hi, rw_lo, sh_w1[l].astype(bf16),
                                        sh_w3[l].astype(bf16), sh_w2[l].astype(bf16), seq_row)

        idx, gates, rank, counts = _route_call(scores_t, router_b[l][:, None])
        cnt = counts[:, 0].astype(i32)
        padded = (cnt + SLOT_BLOCK - 1) // SLOT_BLOCK * SLOT_BLOCK
        padded_end = jnp.cumsum(padded)
        dest_km = _dest_call(idx, rank, (padded_end - padded).astype(f32)[:, None])
        nb = n_slots // SLOT_BLOCK
        blk_first = jnp.arange(nb, dtype=i32) * SLOT_BLOCK
        blk_e = jnp.minimum(jnp.sum((padded_end[None, :] <= blk_first[:, None]).astype(i32), axis=1), N_EXPERTS - 1)
        blk_valid = jnp.clip((padded_end - padded + cnt)[blk_e] - blk_first, 0, SLOT_BLOCK)
        n_used = padded_end[-1:] // SLOT_BLOCK
        owns = jnp.cumsum((cnt > 0).astype(i32))
        blk_ord = (owns - 1)[blk_e]
        expert_seq = jnp.minimum(jnp.sum((owns[None, :] <= jnp.arange(N_EXPERTS, dtype=i32)[:, None]).astype(i32),
                                         axis=1), N_EXPERTS - 1)
        slots = _scatter_rows_call(h2, dest_km, n_slots)
        slots = _expert_call(n_used, blk_valid, blk_ord, expert_seq, owns[-1:], slots, exp_w1, exp_w3, exp_w2, l)
        gates_t = gates.T
        yg_c = _gather_rows_call(slots, dest_km[:, :nc].reshape(-1)).reshape(TOP_K, nc, D_PACK)
        yg_l = _gather_rows_call(slots, dest_km[:, nc:].reshape(-1)).reshape(TOP_K, n - nc, D_PACK)
        xc = _combine_call(gates_t, xbase, mod, yg_c, seq_row_c, 0)
        xl = _combine_call(gates_t, xbase, mod, yg_l, seq_row_c, nc)

    y_prompt = xc.reshape(bc, tc, D_MODEL)
    y_sample = xl.reshape(bl, tl, D_MODEL)
    return (y_prompt, y_sample, jnp.stack(ks, axis=1), jnp.stack(vs, axis=1),
            jnp.stack(lrus, axis=1), jnp.stack(glas, axis=1))
```

```python
import functools

import jax
import jax.numpy as jnp
import numpy as np
from jax import lax
from jax.experimental import pallas as pl
from jax.experimental.pallas import tpu as pltpu
from jax.experimental.pallas import tpu_sc as plsc

f32 = jnp.float32
bf16 = jnp.bfloat16
i32 = jnp.int32

D_MODEL = 1024
N_HEADS = 8
N_KV_HEADS = 2
HEAD_DIM = 64
ATTN_W = N_HEADS * HEAD_DIM
KV_W = N_KV_HEADS * HEAD_DIM
GRID_W = 64
ROPE_FREQ = HEAD_DIM // 4
ROPE_THETA = 10000.0
LRU_W = 256
LRU_BLOCKS = 4
LRU_C = 8.0
CONV_W = 4
CONV_LEFT = 2
GLA_H = 4
GLA_DK = 64
GLA_W = 256
GLA_RANK = 16
GLA_TAU = 16.0
N_EXPERTS = 256
TOP_K = 8
EXPERT_FF = 256
ROUTED_SCALE = 2.5
EPS = 1e-6

LANES = 128
SUBLANES = 8
ROW_TILE = 256
ROUTE_TILE = 512
PROJ_TILE = 512
GLA_BLOCK = 256
SLOT_BLOCK = 256
STEP_BLOCKS = 4
W_RING = 4
COMBINE_TILE = 512
SC_SCATTER_CHUNK = 128
SC_CHUNK = 32
SC_GATHER_BUFS = 4
D_PACK = D_MODEL // 2
VMEM_LIMIT = 56 * 1024 * 1024


def _cparams(sem, vmem=VMEM_LIMIT):
    return pltpu.CompilerParams(dimension_semantics=sem, vmem_limit_bytes=vmem)


def _dot(a, b):
    return jnp.dot(a, b, preferred_element_type=f32)


def _dot_nt(a, b):
    return lax.dot_general(a, b, (((1,), (1,)), ((), ())), preferred_element_type=f32)


def _dot_tn(a, b):
    return lax.dot_general(a, b, (((0,), (0,)), ((), ())), preferred_element_type=f32)


def _split(x):
    hi = x.astype(bf16)
    lo = (x - hi.astype(f32)).astype(bf16)
    return hi, lo


def _dot_x2(x, w):
    hi, lo = _split(x)
    return _dot(hi, w) + _dot(lo, w)


def _dot_2x(m, x):
    hi, lo = _split(x)
    return _dot(m, hi) + _dot(m, lo)


def _dot3(a, b_hi, b_lo):
    a_hi, a_lo = _split(a)
    return _dot(a_hi, b_hi) + _dot(a_lo, b_hi) + _dot(a_hi, b_lo)


def _sigmoid(x):
    return 1.0 / (1.0 + jnp.exp(-x))


def _silu(x):
    return x * _sigmoid(x)


def _softplus(x):
    return jnp.maximum(x, 0.0) + jnp.log(1.0 + jnp.exp(-jnp.abs(x)))


def _gelu_tanh(x):
    return 0.5 * x * (1.0 + jnp.tanh(0.7978845608028654 * (x + 0.044715 * x * x * x)))


def _rms(x, g):
    return x * lax.rsqrt(jnp.mean(x * x, axis=-1, keepdims=True) + EPS) * g


def _pack_pairs(x):
    c = x.shape[1] // 2
    hi = lax.bitcast_convert_type(x[:, :c].astype(bf16).astype(f32), i32)
    lo = lax.bitcast_convert_type(x[:, c:].astype(bf16).astype(f32), i32)
    return hi | lax.shift_right_logical(lo, 16)


def _unpack_pairs(w):
    hi = lax.bitcast_convert_type(w & jnp.int32(-65536), f32)
    lo = lax.bitcast_convert_type(w << 16, f32)
    return hi, lo


def _head_rms(x, g, bm):
    ms = _dot_x2(x * x, bm)
    return x * lax.rsqrt(ms + EPS) * g


def _ada_kernel(c_ref, w_ref, b_ref, o_ref):
    s = _silu(c_ref[...])
    w = w_ref[0]
    w_hi, w_lo = _split(w)
    o_ref[0] = _dot3(s, w_hi, w_lo) + b_ref[0]


def _ada_call(cond, ada_w, ada_b):
    depth = ada_w.shape[0]
    nt = 1536
    return pl.pallas_call(
        _ada_kernel,
        grid=(depth, 6 * D_MODEL // nt),
        in_specs=[pl.BlockSpec((SUBLANES, D_MODEL), lambda l, j: (0, 0)),
                  pl.BlockSpec((1, D_MODEL, nt), lambda l, j: (l, 0, j)),
                  pl.BlockSpec((1, 1, nt), lambda l, j: (l, 0, j))],
        out_specs=pl.BlockSpec((1, SUBLANES, nt), lambda l, j: (l, 0, j)),
        out_shape=jax.ShapeDtypeStruct((depth, SUBLANES, 6 * D_MODEL), f32),
        compiler_params=_cparams(("parallel", "parallel")),
        name="ada_mod",
    )(cond, ada_w, ada_b.reshape(depth, 1, 6 * D_MODEL))


PROJ_A = ATTN_W + 2 * KV_W
PROJ_L = 2 * LRU_W
PROJ_G = 4 * GLA_W + LANES
IN_COLS = PROJ_A + PROJ_L + 4 * GLA_W + 2 * GLA_RANK


def _proj_kernel(xc_ref, xl_ref, mod_ref, g_ref, w_ref, oa_ref, ol_ref, og_ref, w_s, *, n_ctx_tiles):
    @pl.when(pl.program_id(0) == 0)
    def _():
        w_s[:, 0:IN_COLS] = w_ref[0].astype(bf16)
        w_s[:, IN_COLS:] = jnp.zeros((D_MODEL, w_s.shape[1] - IN_COLS), bf16)

    mod = mod_ref[0]
    sh = mod[:, 0:D_MODEL]
    sc = mod[:, D_MODEL:2 * D_MODEL]
    x = jnp.where(pl.program_id(0) < n_ctx_tiles, xc_ref[...], xl_ref[...])
    h = (_rms(x, g_ref[...]) * (1.0 + sc) + sh).astype(bf16)
    p = _dot(h, w_s[...])
    oa_ref[...] = p[:, 0:PROJ_A].astype(bf16)
    ol_ref[...] = p[:, PROJ_A:PROJ_A + PROJ_L].astype(bf16)
    og_ref[...] = p[:, PROJ_A + PROJ_L:PROJ_A + PROJ_L + PROJ_G].astype(bf16)


def _row_pair_specs(nct, width):
    return (pl.BlockSpec((PROJ_TILE, width), lambda i: (jnp.minimum(i, nct - 1), 0)),
            pl.BlockSpec((PROJ_TILE, width), lambda i: (jnp.maximum(i - nct, 0), 0)))


def _proj_call(xc, xl, mod, g, w_all, layer, seq_row):
    n = xc.shape[0] + xl.shape[0]
    nct = xc.shape[0] // PROJ_TILE
    cols = PROJ_A + PROJ_L + PROJ_G
    return pl.pallas_call(
        functools.partial(_proj_kernel, n_ctx_tiles=nct),
        grid=(n // PROJ_TILE,),
        in_specs=[*_row_pair_specs(nct, D_MODEL),
                  pl.BlockSpec((1, 1, 6 * D_MODEL), lambda i: (seq_row(i), 0, 0)),
                  pl.BlockSpec((1, D_MODEL), lambda i: (0, 0)),
                  pl.BlockSpec((1, D_MODEL, IN_COLS), lambda i: (layer, 0, 0))],
        out_specs=[pl.BlockSpec((PROJ_TILE, PROJ_A), lambda i: (i, 0)),
                   pl.BlockSpec((PROJ_TILE, PROJ_L), lambda i: (i, 0)),
                   pl.BlockSpec((PROJ_TILE, PROJ_G), lambda i: (i, 0))],
        out_shape=[jax.ShapeDtypeStruct((n, PROJ_A), bf16),
                   jax.ShapeDtypeStruct((n, PROJ_L), bf16),
                   jax.ShapeDtypeStruct((n, PROJ_G), bf16)],
        scratch_shapes=[pltpu.VMEM((D_MODEL, cols), bf16)],
        compiler_params=_cparams(("arbitrary",)),
        name="in_proj",
    )(xc, xl, mod, g, w_all)


def _rope(x, cos_t, sin_t):
    w = x.shape[1]
    up = pltpu.roll(x, w - ROPE_FREQ, 1)
    dn = pltpu.roll(x, ROPE_FREQ, 1)
    lane = lax.broadcasted_iota(i32, x.shape, 1)
    partner = jnp.where((lane & (2 * ROPE_FREQ - 1)) < ROPE_FREQ, up, dn)
    return x * cos_t + partner * sin_t


def _attend(q, kk_ref, vv_ref, o_ref):
    tq = q.shape[0]
    gw = ATTN_W // N_KV_HEADS
    lane = lax.broadcasted_iota(i32, (tq, gw), 1)
    for g in range(N_KV_HEADS):
        qg = q[:, g * gw:(g + 1) * gw]
        kg = kk_ref[:, g * gw:(g + 1) * gw]
        vg = vv_ref[:, g * gw:(g + 1) * gw]
        acc = jnp.zeros((tq, gw), f32)
        for hh in range(N_HEADS // N_KV_HEADS):
            hm = (lane >> 6) == hh
            s = _dot_nt(jnp.where(hm, qg, 0.0).astype(bf16), kg)
            m = jnp.max(s, axis=-1, keepdims=True)
            p = jnp.exp(s - m)
            l = jnp.sum(p, axis=-1, keepdims=True)
            o = _dot(p.astype(bf16), vg) / l
            acc = jnp.where(hm, o, acc)
        o_ref[:, g * gw:(g + 1) * gw] = acc


def _attn_ctx_kernel(p_ref, qg_ref, kg_ref, og_ref, bmq_ref, bmk_ref, rep_ref,
                     o_ref, ko_ref, vo_ref, kk_s, vv_s, o_s):
    p = p_ref[...].astype(f32)
    q = _head_rms(p[:, 0:ATTN_W], qg_ref[...], bmq_ref[...])
    k = _head_rms(p[:, ATTN_W:ATTN_W + KV_W], kg_ref[...], bmk_ref[...])
    v = p[:, ATTN_W + KV_W:PROJ_A]
    ko_ref[...] = k
    vo_ref[...] = v
    kk_s[...] = _dot(k.astype(bf16), rep_ref[...]).astype(bf16)
    vv_s[...] = _dot(v.astype(bf16), rep_ref[...]).astype(bf16)
    _attend(q * HEAD_DIM ** -0.5, kk_s, vv_s, o_s)
    o_ref[...] = _rms(o_s[...], og_ref[...]).astype(bf16)


def _attn_ctx_call(pa, n_seq, t, qg, kg, og, bmq, bmk, rep):
    assert t == ROW_TILE
    const = lambda i: (0, 0)
    return pl.pallas_call(
        _attn_ctx_kernel,
        grid=(n_seq,),
        in_specs=[pl.BlockSpec((t, PROJ_A), lambda i: (i, 0)),
                  pl.BlockSpec((1, ATTN_W), const), pl.BlockSpec((1, KV_W), const), pl.BlockSpec((1, ATTN_W), const),
                  pl.BlockSpec((ATTN_W, ATTN_W), const), pl.BlockSpec((KV_W, KV_W), const),
                  pl.BlockSpec((KV_W, ATTN_W), const)],
        out_specs=[pl.BlockSpec((t, ATTN_W), lambda i: (i, 0)),
                   pl.BlockSpec((t, KV_W), lambda i: (i, 0)),
                   pl.BlockSpec((t, KV_W), lambda i: (i, 0))],
        out_shape=[jax.ShapeDtypeStruct((n_seq * t, ATTN_W), bf16),
                   jax.ShapeDtypeStruct((n_seq * t, KV_W), f32),
                   jax.ShapeDtypeStruct((n_seq * t, KV_W), f32)],
        scratch_shapes=[pltpu.VMEM((t, ATTN_W), bf16), pltpu.VMEM((t, ATTN_W), bf16), pltpu.VMEM((t, ATTN_W), f32)],
        compiler_params=_cparams(("parallel",)),
        name="attn_ctx",
    )(pa, qg, kg, og, bmq, bmk, rep)


def _attn_lat_kernel(q_ref, kv_ref, ck_ref, cv_ref, cq_ref, sq_ref, ckk_ref, skk_ref,
                     qg_ref, kg_ref, og_ref, bmq_ref, bmk_ref, rep_ref,
                     o_ref, kk_s, vv_s, o_s, *, past):
    @pl.when(pl.program_id(1) == 0)
    def _():
        kv = kv_ref[...].astype(f32)
        k = _head_rms(kv[:, 0:KV_W], kg_ref[...], bmk_ref[...])
        k = _rope(k, ckk_ref[...], skk_ref[...])
        v = kv[:, KV_W:2 * KV_W]
        kk_s[0:past, :] = _dot(ck_ref[0].astype(bf16), rep_ref[...]).astype(bf16)
        vv_s[0:past, :] = _dot(cv_ref[0].astype(bf16), rep_ref[...]).astype(bf16)
        kk_s[past:, :] = _dot(k.astype(bf16), rep_ref[...]).astype(bf16)
        vv_s[past:, :] = _dot(v.astype(bf16), rep_ref[...]).astype(bf16)

    q = _head_rms(q_ref[...].astype(f32), qg_ref[...], bmq_ref[...])
    q = _rope(q, cq_ref[...], sq_ref[...])
    _attend(q * HEAD_DIM ** -0.5, kk_s, vv_s, o_s)
    o_ref[...] = _rms(o_s[...], og_ref[...]).astype(bf16)


def _attn_lat_call(pa, row0, n_seq, t, cache_k, cache_v, cq, sq, qg, kg, og, bmq, bmk, rep):
    past = cache_k.shape[1]
    nq = t // ROW_TILE
    assert row0 % t == 0 and t % ROW_TILE == 0
    qb0 = row0 // ROW_TILE
    sb0 = row0 // t
    const = lambda b, j: (0, 0)
    return pl.pallas_call(
        functools.partial(_attn_lat_kernel, past=past),
        grid=(n_seq, nq),
        in_specs=[pl.BlockSpec((ROW_TILE, ATTN_W), lambda b, j: (qb0 + b * nq + j, 0)),
                  pl.BlockSpec((t, 2 * KV_W), lambda b, j: (sb0 + b, ATTN_W // (2 * KV_W))),
                  pl.BlockSpec((1, past, KV_W), lambda b, j: (b, 0, 0)),
                  pl.BlockSpec((1, past, KV_W), lambda b, j: (b, 0, 0)),
                  pl.BlockSpec((ROW_TILE, ATTN_W), lambda b, j: (j, 0)),
                  pl.BlockSpec((ROW_TILE, ATTN_W), lambda b, j: (j, 0)),
                  pl.BlockSpec((t, KV_W), const), pl.BlockSpec((t, KV_W), const),
                  pl.BlockSpec((1, ATTN_W), const), pl.BlockSpec((1, KV_W), const), pl.BlockSpec((1, ATTN_W), const),
                  pl.BlockSpec((ATTN_W, ATTN_W), const), pl.BlockSpec((KV_W, KV_W), const),
                  pl.BlockSpec((KV_W, ATTN_W), const)],
        out_specs=pl.BlockSpec((ROW_TILE, ATTN_W), lambda b, j: (b * nq + j, 0)),
        out_shape=jax.ShapeDtypeStruct((n_seq * t, ATTN_W), bf16),
        scratch_shapes=[pltpu.VMEM((past + t, ATTN_W), bf16), pltpu.VMEM((past + t, ATTN_W), bf16),
                        pltpu.VMEM((ROW_TILE, ATTN_W), f32)],
        compiler_params=_cparams(("parallel", "arbitrary")),
        name="attn_lat",
    )(pa, pa, cache_k, cache_v, cq, sq, cq, sq, qg, kg, og, bmq, bmk, rep)


def _lru_kernel(x_ref, h0_ref, cw_ref, cb_ref, wg_ref, bg_ref, lam_ref, g_ref,
                o_ref, fin_ref, xs, a_s, u_s, h_s, *, t):
    ng = t // SUBLANES
    lx = x_ref[:, 0:LRU_W].astype(f32)
    ly = x_ref[:, LRU_W:2 * LRU_W].astype(f32)
    xs[0:SUBLANES, :] = jnp.zeros((SUBLANES, LRU_W), f32)
    xs[SUBLANES + t:2 * SUBLANES + t, :] = jnp.zeros((SUBLANES, LRU_W), f32)
    xs[SUBLANES:SUBLANES + t, :] = lx
    xc = jnp.broadcast_to(cb_ref[...], (t, LRU_W))
    for j in range(CONV_W):
        off = SUBLANES + j - CONV_LEFT
        xc = xc + xs[off:off + t, :] * cw_ref[j:j + 1, :]
    gates = _sigmoid(_dot(xc.astype(bf16), wg_ref[...]) + bg_ref[...])
    sp = _softplus(-lam_ref[...])
    sub = lax.broadcasted_iota(i32, (ng, SUBLANES, LRU_W), 1)
    for d in range(2):
        r = gates[:, (2 * d) * LRU_W:(2 * d + 1) * LRU_W]
        gi = gates[:, (2 * d + 1) * LRU_W:(2 * d + 2) * LRU_W]
        log_a = -LRU_C * r * sp[d:d + 1, :]
        a = jnp.exp(log_a)
        th = jnp.tanh(log_a)
        u = jnp.sqrt(-2.0 * th / (1.0 - th)) * (gi * xc)
        a = a.reshape(ng, SUBLANES, LRU_W)
        u = u.reshape(ng, SUBLANES, LRU_W)
        for s in (1, 2, 4):
            if d == 0:
                a_sh = pltpu.roll(a, s, 1)
                u_sh = pltpu.roll(u, s, 1)
                ok = sub >= s
            else:
                a_sh = pltpu.roll(a, SUBLANES - s, 1)
                u_sh = pltpu.roll(u, SUBLANES - s, 1)
                ok = sub < SUBLANES - s
            u = jnp.where(ok, a * u_sh + u, u)
            a = jnp.where(ok, a * a_sh, a)
        a_s[...] = a.reshape(t, LRU_W)
        u_s[...] = u.reshape(t, LRU_W)
        h0 = jnp.broadcast_to(h0_ref[0, d:d + 1, :], (SUBLANES, LRU_W))
        edge = SUBLANES - 1 if d == 0 else 0

        def body(i, carry, d=d, edge=edge):
            g = i if d == 0 else ng - 1 - i
            rows = pl.ds(pl.multiple_of(g * SUBLANES, SUBLANES), SUBLANES)
            h = a_s[rows, :] * carry + u_s[rows, :]
            if d == 0:
                h_s[rows, :] = h
            else:
                h_s[rows, :] = h_s[rows, :] + h
            return jnp.broadcast_to(h[edge:edge + 1, :], (SUBLANES, LRU_W))

        last = lax.fori_loop(0, ng, body, h0)
        fin_ref[0, d:d + 1, :] = last[0:1, :]
    o_ref[...] = _rms(h_s[...] * _gelu_tanh(ly), g_ref[...]).astype(bf16)


def _lru_call(pl_, row0, n_seq, t, h0, cw, cb, wg, bg, lam, g):
    sb0 = row0 // t
    assert row0 % t == 0
    const = lambda b: (0, 0)
    in_specs = [pl.BlockSpec((t, PROJ_L), lambda b: (sb0 + b, 0)),
                pl.BlockSpec((1, 2, LRU_W), lambda b: (b, 0, 0)),
                pl.BlockSpec((CONV_W, LRU_W), const), pl.BlockSpec((1, LRU_W), const),
                pl.BlockSpec((LRU_W, 4 * LRU_W), const), pl.BlockSpec((1, 4 * LRU_W), const),
                pl.BlockSpec((2, LRU_W), const), pl.BlockSpec((1, LRU_W), const)]
    args = [pl_, h0, cw, cb, wg, bg, lam, g]
    return pl.pallas_call(
        functools.partial(_lru_kernel, t=t),
        grid=(n_seq,),
        in_specs=in_specs,
        out_specs=[pl.BlockSpec((t, LRU_W), lambda b: (b, 0)),
                   pl.BlockSpec((1, 2, LRU_W), lambda b: (b, 0, 0))],
        out_shape=[jax.ShapeDtypeStruct((n_seq * t, LRU_W), bf16), jax.ShapeDtypeStruct((n_seq, 2, LRU_W), f32)],
        scratch_shapes=[pltpu.VMEM((t + 2 * SUBLANES, LRU_W), f32), pltpu.VMEM((t, LRU_W), f32),
                        pltpu.VMEM((t, LRU_W), f32), pltpu.VMEM((t, LRU_W), f32)],
        compiler_params=_cparams(("parallel",)),
        name="rglru_t%d" % t,
    )(*args)


def _bcast_rows(b, period, off):
    w = b.shape[1]
    return jnp.concatenate(
        [jnp.broadcast_to(b[i * period + off:i * period + off + 1, :], (period, w)) for i in range(b.shape[0] // period)],
        axis=0)


def _gla_block(q, k, v, la, reverse, st_ref):
    n = GLA_BLOCK
    row = lax.broadcasted_iota(i32, (n, n), 0)
    col = lax.broadcasted_iota(i32, (n, n), 1)
    same64 = (row >> 6) == (col >> 6)
    same32 = (row >> 5) == (col >> 5)
    same16 = (row >> 4) == (col >> 4)
    if not reverse:
        cum = (same64 & (col <= row)).astype(bf16)
        m1 = same64 & ((row & 63) >= 32) & ((col & 63) < 32)
        m2 = same32 & ((row & 31) >= 16) & ((col & 31) < 16)
        m3 = same16 & (col <= row)
        offs = (31, 15, 7, 63)
    else:
        cum = (same64 & (col >= row)).astype(bf16)
        m1 = same64 & ((row & 63) < 32) & ((col & 63) >= 32)
        m2 = same32 & ((row & 31) < 16) & ((col & 31) >= 16)
        m3 = same16 & (col >= row)
        offs = (32, 16, 8, 0)
    b = _dot_2x(cum, la)
    r1 = _bcast_rows(b, 64, offs[0])
    r2 = _bcast_rows(b, 32, offs[1])
    r3 = _bcast_rows(b, 16, offs[2])
    bl = _bcast_rows(b, 64, offs[3])
    q1 = q * jnp.exp(jnp.minimum(b - r1, 0.0))
    k1 = (k * jnp.exp(jnp.minimum(r1 - b, 0.0))).astype(bf16)
    q2 = q * jnp.exp(jnp.minimum(b - r2, 0.0))
    k2 = (k * jnp.exp(jnp.minimum(r2 - b, 0.0))).astype(bf16)
    q3 = q * jnp.exp(b - r3)
    k3 = (k * jnp.exp(r3 - b)).astype(bf16)
    qe = (q * jnp.exp(b)).astype(bf16)
    kl = (k * jnp.exp(bl - b)).astype(bf16)
    dec = jnp.exp(bl)
    vb = v.astype(bf16)
    zq = jnp.zeros((n, n), f32)
    intra = jnp.zeros((n, n), f32)
    for h in range(GLA_H):
        hm = (col >> 6) == h
        a1 = _dot_nt(jnp.where(hm, q1, zq).astype(bf16), k1)
        a2 = _dot_nt(jnp.where(hm, q2, zq).astype(bf16), k2)
        a3 = _dot_nt(jnp.where(hm, q3, zq).astype(bf16), k3)
        att = jnp.where(m1, a1, jnp.where(m2, a2, jnp.where(m3, a3, zq)))
        intra = intra + _dot(att.astype(bf16), jnp.where(hm, v, zq).astype(bf16))
    outs = [None] * 4
    for c in (range(4) if not reverse else range(3, -1, -1)):
        rs = slice(64 * c, 64 * c + 64)
        st = st_ref[...]
        inter = _dot_nt(qe[rs], st.astype(bf16))
        kv = _dot_tn(vb[rs], kl[rs])
        drow = dec[64 * c:64 * c + 1, :]
        st_ref[...] = st * drow + jnp.where(same64, kv, zq)
        outs[c] = intra[rs] + inter
    return jnp.concatenate(outs, axis=0)


def _gla_kernel(x_ref, s0_ref, wa_ref, ba_ref, g_ref, bm_ref, o_ref, fin_ref, st_s, o_s, *, t):
    nblk = t // GLA_BLOCK

    def block(j, reverse):
        rows = pl.ds(j * GLA_BLOCK if isinstance(j, int) else pl.multiple_of(j * GLA_BLOCK, GLA_BLOCK), GLA_BLOCK)
        q = x_ref[rows, 0:GLA_W].astype(f32) * GLA_DK ** -0.5
        k = x_ref[rows, GLA_W:2 * GLA_W].astype(f32)
        v = x_ref[rows, 2 * GLA_W:3 * GLA_W].astype(f32)
        ga = x_ref[rows, 4 * GLA_W:4 * GLA_W + LANES]
        d = 1 if reverse else 0
        z = _dot(ga.astype(bf16), wa_ref[:, d * GLA_W:(d + 1) * GLA_W]) + ba_ref[:, d * GLA_W:(d + 1) * GLA_W]
        la = -_softplus(-z) * (1.0 / GLA_TAU)
        o = _gla_block(q, k, v, la, reverse, st_s)
        if reverse:
            o_s[rows, :] = o_s[rows, :] + o
        else:
            o_s[rows, :] = o

    for d in range(2):
        if s0_ref is None:
            st_s[...] = jnp.zeros((GLA_W, GLA_W), f32)
        else:
            st_s[...] = s0_ref[0, d]
        if nblk == 1:
            block(0, d == 1)
        else:
            def body(i, carry, d=d):
                block(i if d == 0 else nblk - 1 - i, d == 1)
                return carry
            lax.fori_loop(0, nblk, body, 0)
        st_t = st_s[...].T
        for h in range(GLA_H):
            fin_ref[0, d, h] = st_t[h * GLA_DK:(h + 1) * GLA_DK, h * GLA_DK:(h + 1) * GLA_DK]
    gg = x_ref[:, 3 * GLA_W:4 * GLA_W].astype(f32)
    o_ref[...] = (_head_rms(o_s[...], g_ref[...], bm_ref[...]) * _silu(gg)).astype(bf16)


def _gla_call(pg, row0, n_seq, t, s0, wa, ba, g, bm):
    sb0 = row0 // t
    assert row0 % t == 0 and t % GLA_BLOCK == 0
    const = lambda b: (0, 0)
    in_specs = [pl.BlockSpec((t, PROJ_G), lambda b: (sb0 + b, 0))]
    args = [pg]
    if s0 is not None:
        in_specs.append(pl.BlockSpec((1, 2, GLA_W, GLA_W), lambda b: (b, 0, 0, 0)))
        args.append(s0)
    in_specs += [pl.BlockSpec((LANES, 2 * GLA_W), const), pl.BlockSpec((1, 2 * GLA_W), const),
                 pl.BlockSpec((1, GLA_W), const), pl.BlockSpec((GLA_W, GLA_W), const)]
    args += [wa, ba, g, bm]

    def kern(*refs):
        refs = list(refs)
        x_ref = refs.pop(0)
        s0_ref = refs.pop(0) if s0 is not None else None
        wa_ref, ba_ref, g_ref, bm_ref = refs[:4]
        _gla_kernel(x_ref, s0_ref, wa_ref, ba_ref, g_ref, bm_ref, *refs[4:], t=t)

    return pl.pallas_call(
        kern,
        grid=(n_seq,),
        in_specs=in_specs,
        out_specs=[pl.BlockSpec((t, GLA_W), lambda b: (b, 0)),
                   pl.BlockSpec((1, 2, GLA_H, GLA_DK, GLA_DK), lambda b: (b, 0, 0, 0, 0))],
        out_shape=[jax.ShapeDtypeStruct((n_seq * t, GLA_W), bf16),
                   jax.ShapeDtypeStruct((n_seq, 2, GLA_H, GLA_DK, GLA_DK), f32)],
        scratch_shapes=[pltpu.VMEM((GLA_W, GLA_W), f32), pltpu.VMEM((t, GLA_W), f32)],
        compiler_params=_cparams(("parallel",)),
        name="gla_t%d" % t,
    )(*args)


def _out_kernel(xc_ref, xl_ref, ac_ref, lc_ref, gc_ref, al_ref, ll_ref, gl_ref, mod_ref, wo_ref, g2_ref,
                rwh_ref, rwl_ref, s1_ref, s3_ref, s2_ref, xb_ref, h2_ref, sc_ref, *, n_ctx_tiles):
    is_ctx = pl.program_id(0) < n_ctx_tiles
    pick = lambda c_ref, l_ref: jnp.where(is_ctx, c_ref[...], l_ref[...]).astype(bf16)
    mod = mod_ref[0]
    g1 = mod[:, 2 * D_MODEL:3 * D_MODEL]
    sh2 = mod[:, 3 * D_MODEL:4 * D_MODEL]
    sc2 = mod[:, 4 * D_MODEL:5 * D_MODEL]
    g2 = mod[:, 5 * D_MODEL:6 * D_MODEL]
    m = (_dot(pick(ac_ref, al_ref), wo_ref[0:ATTN_W, :])
         + _dot(pick(lc_ref, ll_ref), wo_ref[ATTN_W:ATTN_W + LRU_W, :])
         + _dot(pick(gc_ref, gl_ref), wo_ref[ATTN_W + LRU_W:ATTN_W + LRU_W + GLA_W, :]))
    x1 = jnp.where(is_ctx, xc_ref[...], xl_ref[...]) + g1 * m
    h2 = _rms(x1, g2_ref[...]) * (1.0 + sc2) + sh2
    h2_ref[...] = _pack_pairs(h2)
    h_hi, h_lo = _split(h2)
    logits_t = _dot_nt(rwh_ref[...], h_hi) + _dot_nt(rwh_ref[...], h_lo) + _dot_nt(rwl_ref[...], h_hi)
    sc_ref[...] = _sigmoid(logits_t)
    hb = h2.astype(bf16)
    act = (_silu(_dot(hb, s1_ref[...])) * _dot(hb, s3_ref[...])).astype(bf16)
    xb_ref[...] = x1 + g2 * _dot(act, s2_ref[...])


def _out_call(xc, xl, mix_ctx, mix_lat, mod, wo, g2, rwh, rwl, s1, s3, s2, seq_row):
    n = xc.shape[0] + xl.shape[0]
    nct = xc.shape[0] // PROJ_TILE
    const = lambda i: (0, 0)
    rowb = lambda w: pl.BlockSpec((PROJ_TILE, w), lambda i: (i, 0))
    ctxb = lambda w: _row_pair_specs(nct, w)[0]
    latb = lambda w: _row_pair_specs(nct, w)[1]
    return pl.pallas_call(
        functools.partial(_out_kernel, n_ctx_tiles=nct),
        grid=(n // PROJ_TILE,),
        in_specs=[ctxb(D_MODEL), latb(D_MODEL),
                  ctxb(ATTN_W), ctxb(LRU_W), ctxb(GLA_W), latb(ATTN_W), latb(LRU_W), latb(GLA_W),
                  pl.BlockSpec((1, 1, 6 * D_MODEL), lambda i: (seq_row(i), 0, 0)),
                  pl.BlockSpec((D_MODEL, D_MODEL), const), pl.BlockSpec((1, D_MODEL), const),
                  pl.BlockSpec((N_EXPERTS, D_MODEL), const), pl.BlockSpec((N_EXPERTS, D_MODEL), const),
                  pl.BlockSpec((D_MODEL, EXPERT_FF), const), pl.BlockSpec((D_MODEL, EXPERT_FF), const),
                  pl.BlockSpec((EXPERT_FF, D_MODEL), const)],
        out_specs=[rowb(D_MODEL),
                   rowb(D_PACK),
                   pl.BlockSpec((N_EXPERTS, PROJ_TILE), lambda i: (0, i))],
        out_shape=[jax.ShapeDtypeStruct((n, D_MODEL), f32),
                   jax.ShapeDtypeStruct((n, D_PACK), i32),
                   jax.ShapeDtypeStruct((N_EXPERTS, n), f32)],
        compiler_params=_cparams(("parallel",)),
        name="out_proj",
    )(xc, xl, *mix_ctx, *mix_lat, mod, wo, g2, rwh, rwl, s1, s3, s2)


def _set_row(acc, k, row):
    sub = lax.broadcasted_iota(i32, acc.shape, 0)
    return jnp.where(sub == k, jnp.broadcast_to(row, acc.shape), acc)


def _route_kernel(sc_ref, rb_ref, idx_ref, gate_ref, rank_ref, cnt_ref, run_s):
    tm = sc_ref.shape[1]

    @pl.when(pl.program_id(0) == 0)
    def _():
        run_s[...] = jnp.zeros_like(run_s)

    scores = sc_ref[...]
    sel = scores + rb_ref[...]
    erow = lax.broadcasted_iota(i32, (N_EXPERTS, tm), 0).astype(f32)
    neg = jnp.full((N_EXPERTS, tm), -jnp.inf, f32)
    hots = []
    idx_o = jnp.zeros((TOP_K, tm), f32)
    gate_o = jnp.zeros((TOP_K, tm), f32)
    gsum = jnp.zeros((1, tm), f32)
    for k in range(TOP_K):
        m = jnp.max(sel, axis=0, keepdims=True)
        idx_f = jnp.min(jnp.where(sel == m, erow, float(N_EXPERTS)), axis=0, keepdims=True)
        hot = erow == idx_f
        gk = jnp.sum(jnp.where(hot, scores, 0.0), axis=0, keepdims=True)
        sel = jnp.where(hot, neg, sel)
        hots.append(hot)
        gsum = gsum + gk
        idx_o = _set_row(idx_o, k, idx_f)
        gate_o = _set_row(gate_o, k, gk)
    chosen = jnp.where(sel == neg, 1.0, 0.0)
    gate_ref[...] = gate_o / gsum * ROUTED_SCALE
    idx_ref[...] = idx_o.astype(i32)
    r = lax.broadcasted_iota(i32, (tm, tm), 0)
    c = lax.broadcasted_iota(i32, (tm, tm), 1)
    pos = _dot(chosen.astype(bf16), (r < c).astype(bf16)) + run_s[...]
    rank_o = jnp.zeros((TOP_K, tm), f32)
    for k in range(TOP_K):
        rank_o = _set_row(rank_o, k, jnp.sum(jnp.where(hots[k], pos, 0.0), axis=0, keepdims=True))
    rank_ref[...] = rank_o.astype(i32)
    run_s[...] = run_s[...] + jnp.sum(chosen, axis=1, keepdims=True)
    cnt_ref[...] = run_s[...]


def _route_call(scores_t, rb):
    n = scores_t.shape[1]
    colb = lambda r: pl.BlockSpec((r, ROUTE_TILE), lambda i: (0, i))
    cnt_spec = pl.BlockSpec((N_EXPERTS, 1), lambda i: (0, 0))
    return pl.pallas_call(
        _route_kernel,
        grid=(n // ROUTE_TILE,),
        in_specs=[colb(N_EXPERTS), cnt_spec],
        out_specs=[colb(TOP_K), colb(TOP_K), colb(TOP_K), cnt_spec],
        out_shape=[jax.ShapeDtypeStruct((TOP_K, n), i32), jax.ShapeDtypeStruct((TOP_K, n), f32),
                   jax.ShapeDtypeStruct((TOP_K, n), i32), jax.ShapeDtypeStruct((N_EXPERTS, 1), f32)],
        scratch_shapes=[pltpu.VMEM((N_EXPERTS, 1), f32)],
        compiler_params=_cparams(("arbitrary",)),
        name="route",
    )(scores_t, rb)


def _dest_kernel(idx_ref, rank_ref, start_ref, dest_ref):
    tm = idx_ref.shape[1]
    erow = lax.broadcasted_iota(i32, (N_EXPERTS, tm), 0)
    start = jnp.broadcast_to(start_ref[...], (N_EXPERTS, tm))
    idx = idx_ref[...]
    dest = jnp.zeros((TOP_K, tm), f32)
    for k in range(TOP_K):
        hot = erow == idx[k:k + 1, :]
        dest = _set_row(dest, k, jnp.sum(jnp.where(hot, start, 0.0), axis=0, keepdims=True))
    dest_ref[...] = dest.astype(i32) + rank_ref[...]


def _dest_call(idx, rank, start):
    n = idx.shape[1]
    colb = lambda r: pl.BlockSpec((r, ROUTE_TILE), lambda i: (0, i))
    return pl.pallas_call(
        _dest_kernel,
        grid=(n // ROUTE_TILE,),
        in_specs=[colb(TOP_K), colb(TOP_K), pl.BlockSpec((N_EXPERTS, 1), lambda i: (0, 0))],
        out_specs=colb(TOP_K),
        out_shape=jax.ShapeDtypeStruct((TOP_K, n), i32),
        compiler_params=_cparams(("parallel",)),
        name="dest",
    )(idx, rank, start)


def _sc_workers():
    info = plsc.get_sparse_core_info()
    mesh = plsc.VectorSubcoreMesh(core_axis_name="c", subcore_axis_name="s")
    worker_id = lambda: lax.axis_index("s") * info.num_cores + lax.axis_index("c")
    return mesh, info.num_cores * info.num_subcores, worker_id


def _scatter_rows_call(rows, dest_km, n_slots):
    n, d = rows.shape
    ch = SC_SCATTER_CHUNK
    mesh, n_workers, worker_id = _sc_workers()
    chunks_per_worker = n // (n_workers * ch)
    assert n % (n_workers * ch) == 0
    idx_cm = dest_km.reshape(TOP_K, n // ch, ch).transpose(1, 0, 2)

    def body(rows_hbm, idx_hbm, out_hbm, idx_v, rows_v, sem):
        base = worker_id() * chunks_per_worker

        @pl.loop(0, chunks_per_worker)
        def _(i):
            c = base + i
            pltpu.sync_copy(rows_hbm.at[pl.ds(pl.multiple_of(c * ch, ch), ch)], rows_v)
            pltpu.sync_copy(idx_hbm.at[c], idx_v)
            copies = [pltpu.async_copy(rows_v, out_hbm.at[idx_v.at[k]], sem) for k in range(TOP_K)]
            for cp in copies:
                cp.wait()

    return pl.kernel(
        body,
        out_type=jax.ShapeDtypeStruct((n_slots, d), rows.dtype),
        mesh=mesh,
        scratch_types=[pltpu.VMEM((TOP_K, ch), i32), pltpu.VMEM((ch, d), rows.dtype), pltpu.SemaphoreType.DMA],
        name="scatter_rows",
    )(rows, idx_cm)


def _expert_kernel(nu_ref, bv_ref, ord_ref, eseq_ref, nex_ref, x_ref, w1_hbm, w3_hbm, w2_hbm, y_ref,
                   w1_f, w3_f, w2_f, w1_s, w3_s, w2_s, sems, *, layer):
    def weight_copies(j):
        e = eseq_ref[j]
        slot = j % W_RING
        return [pltpu.make_async_copy(w1_hbm.at[layer, e], w1_f.at[slot], sems.at[slot, 0]),
                pltpu.make_async_copy(w3_hbm.at[layer, e], w3_f.at[slot], sems.at[slot, 1]),
                pltpu.make_async_copy(w2_hbm.at[layer, e], w2_f.at[slot], sems.at[slot, 2])]

    def start_if_exists(j):
        @pl.when(j < nex_ref[0])
        def _():
            for cp in weight_copies(j):
                cp.start()

    for u in range(STEP_BLOCKS):
        i = pl.program_id(0) * STEP_BLOCKS + u
        rows = slice(u * SLOT_BLOCK, (u + 1) * SLOT_BLOCK)

        @pl.when(i < nu_ref[0])
        def _(i=i, rows=rows):
            j = ord_ref[i]

            @pl.when(i == 0)
            def _():
                for ahead in range(W_RING - 1):
                    start_if_exists(ahead)

            @pl.when((i == 0) | (j != ord_ref[jnp.maximum(i - 1, 0)]))
            def _():
                start_if_exists(j + W_RING - 1)
                for cp in weight_copies(j):
                    cp.wait()
                slot = j % W_RING
                w1_s[...] = w1_f[slot].astype(bf16)
                w3_s[...] = w3_f[slot].astype(bf16)
                w2_s[...] = w2_f[slot].astype(bf16)

            row = lax.broadcasted_iota(i32, (SLOT_BLOCK, D_PACK), 0)
            x_hi, x_lo = _unpack_pairs(jnp.where(row < bv_ref[i], x_ref[rows, :], 0))
            x = jnp.concatenate([x_hi, x_lo], axis=1).astype(bf16)
            act = (_silu(_dot(x, w1_s[...])) * _dot(x, w3_s[...])).astype(bf16)
            y_ref[rows, :] = _pack_pairs(_dot(act, w2_s[...]))

        @pl.when((i >= nu_ref[0]) & (pl.program_id(0) * STEP_BLOCKS < nu_ref[0]))
        def _(rows=rows):
            y_ref[rows, :] = x_ref[rows, :]


def _expert_call(n_used, blk_valid, blk_ord, expert_seq, n_seq_experts, slots, w1, w3, w2, layer):
    step_rows = STEP_BLOCKS * SLOT_BLOCK
    assert slots.shape[0] % step_rows == 0
    blk = lambda s, nu, *_: (jnp.minimum(s, (nu[0] - 1) // STEP_BLOCKS), 0)
    hbm = pl.BlockSpec(memory_space=pl.ANY)
    grid_spec = pltpu.PrefetchScalarGridSpec(
        num_scalar_prefetch=5,
        grid=(slots.shape[0] // step_rows,),
        in_specs=[pl.BlockSpec((step_rows, D_PACK), blk), hbm, hbm, hbm],
        out_specs=pl.BlockSpec((step_rows, D_PACK), blk),
        scratch_shapes=[pltpu.VMEM((W_RING, D_MODEL, EXPERT_FF), f32), pltpu.VMEM((W_RING, D_MODEL, EXPERT_FF), f32),
                        pltpu.VMEM((W_RING, EXPERT_FF, D_MODEL), f32),
                        pltpu.VMEM((D_MODEL, EXPERT_FF), bf16), pltpu.VMEM((D_MODEL, EXPERT_FF), bf16),
                        pltpu.VMEM((EXPERT_FF, D_MODEL), bf16),
                        pltpu.SemaphoreType.DMA((W_RING, 3))],
    )
    return pl.pallas_call(
        functools.partial(_expert_kernel, layer=layer),
        grid_spec=grid_spec,
        out_shape=jax.ShapeDtypeStruct(slots.shape, slots.dtype),
        input_output_aliases={5: 0},
        compiler_params=_cparams(("arbitrary",)),
        name="experts",
    )(n_used, blk_valid, blk_ord, expert_seq, n_seq_experts, slots, w1, w3, w2)


def _gather_rows_call(table, idx):
    n_idx = idx.shape[0]
    d = table.shape[1]
    ch, nb, ahead = SC_CHUNK, SC_GATHER_BUFS, SC_GATHER_BUFS // 2
    mesh, n_workers, worker_id = _sc_workers()
    per_worker = n_idx // n_workers
    n_chunks = per_worker // ch
    assert n_idx % (n_workers * ch) == 0 and n_chunks % nb == 0 and n_chunks >= nb

    def body(table_hbm, idx_hbm, out_hbm, idx_all, *scratch):
        rows, gsem, wsem = scratch[:nb], scratch[nb:2 * nb], scratch[2 * nb:3 * nb]
        wid = worker_id()
        base = wid * per_worker
        pltpu.sync_copy(idx_hbm.at[wid], idx_all)

        def gather(c, b):
            return pltpu.make_async_copy(table_hbm.at[idx_all.at[c]], rows[b], gsem[b])

        def write(c, b):
            off = pl.multiple_of(base + c * ch, ch)
            return pltpu.make_async_copy(rows[b], out_hbm.at[pl.ds(off, ch)], wsem[b])

        for c in range(ahead):
            gather(c, c).start()

        @pl.loop(0, n_chunks, step=nb)
        def _(c0):
            for b in range(nb):
                c = c0 + b
                gather(c, b).wait()
                write(c, b).start()
                b2 = (b + ahead) % nb

                @pl.when(c + ahead < n_chunks)
                def _():
                    @pl.when(c + ahead >= nb)
                    def _():
                        write(c + ahead - nb, b2).wait()

                    gather(c + ahead, b2).start()

        for b in range(nb):
            write(n_chunks - nb + b, b).wait()

    return pl.kernel(
        body,
        out_type=jax.ShapeDtypeStruct((n_idx, d), table.dtype),
        mesh=mesh,
        scratch_types=([pltpu.VMEM((n_chunks, ch), i32)] + [pltpu.VMEM((ch, d), table.dtype)] * nb
                       + [pltpu.SemaphoreType.DMA] * (2 * nb)),
        name="gather_rows",
    )(table, idx.reshape(n_workers, n_chunks, ch))


def _combine_kernel(gate_ref, xb_ref, mod_ref, y_ref, o_ref):
    g2 = mod_ref[0][:, 5 * D_MODEL:6 * D_MODEL]
    gates = gate_ref[...]
    acc_hi = jnp.zeros((gates.shape[0], D_PACK), f32)
    acc_lo = jnp.zeros((gates.shape[0], D_PACK), f32)
    for k in range(TOP_K):
        y_hi, y_lo = _unpack_pairs(y_ref[k])
        acc_hi = acc_hi + gates[:, k:k + 1] * y_hi
        acc_lo = acc_lo + gates[:, k:k + 1] * y_lo
    o_ref[:, 0:D_PACK] = xb_ref[:, 0:D_PACK] + g2[:, 0:D_PACK] * acc_hi
    o_ref[:, D_PACK:D_MODEL] = xb_ref[:, D_PACK:D_MODEL] + g2[:, D_PACK:D_MODEL] * acc_lo


def _combine_call(gates, xbase, mod, ygath, seq_row_c, row0):
    tm = COMBINE_TILE
    n_rows = ygath.shape[1]
    assert row0 % tm == 0 and n_rows % tm == 0
    b0 = row0 // tm
    return pl.pallas_call(
        _combine_kernel,
        grid=(n_rows // tm,),
        in_specs=[pl.BlockSpec((tm, TOP_K), lambda i: (b0 + i, 0)),
                  pl.BlockSpec((tm, D_MODEL), lambda i: (b0 + i, 0)),
                  pl.BlockSpec((1, 1, 6 * D_MODEL), lambda i: (seq_row_c(b0 + i), 0, 0)),
                  pl.BlockSpec((TOP_K, tm, D_PACK), lambda i: (0, i, 0))],
        out_specs=pl.BlockSpec((tm, D_MODEL), lambda i: (i, 0)),
        out_shape=jax.ShapeDtypeStruct((n_rows, D_MODEL), f32),
        compiler_params=_cparams(("parallel",)),
        name="combine",
    )(gates, xbase, mod, ygath)


def _block_avg(width, group):
    r = np.arange(width)
    return jnp.asarray((r[:, None] // group == r[None, :] // group).astype(np.float32) / group, dtype=bf16)


def _kv_replicate():
    c = np.arange(ATTN_W)
    src = (c // (ATTN_W // N_KV_HEADS)) * HEAD_DIM + c % HEAD_DIM
    return jnp.asarray((np.arange(KV_W)[:, None] == src[None, :]).astype(np.float32), dtype=bf16)


def _rope_lane_tables(n_tok):
    rows = n_tok // GRID_W
    r = jnp.repeat(jnp.arange(rows, dtype=f32), GRID_W)
    col = jnp.tile(jnp.arange(GRID_W, dtype=f32), rows)
    inv = ROPE_THETA ** (-jnp.arange(ROPE_FREQ, dtype=f32) / ROPE_FREQ)
    ar = r[:, None] * inv
    ac = col[:, None] * inv
    cos_h = jnp.concatenate([jnp.cos(ar), jnp.cos(ar), jnp.cos(ac), jnp.cos(ac)], axis=-1)
    sin_h = jnp.concatenate([-jnp.sin(ar), jnp.sin(ar), -jnp.sin(ac), jnp.sin(ac)], axis=-1)
    return jnp.tile(cos_h, (1, N_HEADS)), jnp.tile(sin_h, (1, N_HEADS))


def _block_diag(w):
    nb, bw, _ = w.shape
    eye = jnp.eye(nb, dtype=w.dtype)
    return (w[:, :, None, :] * eye[:, None, :, None]).reshape(nb * bw, nb * bw)


def _gla_state_in(s):
    bsz = s.shape[0]
    eye = jnp.eye(GLA_H, dtype=s.dtype)
    st = jnp.swapaxes(s, -1, -2)
    big = st[:, :, :, :, None, :] * eye[None, None, :, None, :, None]
    return big.reshape(bsz, 2, GLA_W, GLA_W)


def kernel(x_prompt, x_sample, cache_k, cache_v, state_lru, state_gla, c, c_ctx, ada_w, ada_b, norm1_g, norm2_g, w_in, q_norm_g, k_norm_g, attn_out_g, conv_w, conv_b, lru_wa, lru_ba, lru_wi, lru_bi, lru_lambda, lru_out_g, gla_wa2, gla_ba, gla_out_g, w_out, router_w, router_b, exp_w1, exp_w3, exp_w2, sh_w1, sh_w3, sh_w2):
    bc, tc, _ = x_prompt.shape
    bl, tl, _ = x_sample.shape
    depth = w_in.shape[0]
    nc = bc * tc
    n = nc + bl * tl
    past = cache_k.shape[2]
    assert tc == ROW_TILE and tl % ROW_TILE == 0 and nc % tl == 0 and bl + 1 <= SUBLANES
    assert nc % PROJ_TILE == 0 and tl % PROJ_TILE == 0 and nc % COMBINE_TILE == 0 and tl % COMBINE_TILE == 0

    def seq_row_for(tile):
        def seq_row(i):
            return jnp.where(i < nc // tile, 0, 1 + (i - nc // tile) // (tl // tile))
        return seq_row

    seq_row = seq_row_for(PROJ_TILE)
    seq_row_c = seq_row_for(COMBINE_TILE)

    xc = x_prompt.reshape(nc, D_MODEL)
    xl = x_sample.reshape(bl * tl, D_MODEL)
    cond = jnp.zeros((SUBLANES, D_MODEL), f32).at[0].set(c_ctx).at[1:1 + bl].set(c)
    mods = _ada_call(cond, ada_w, ada_b)

    bmq = _block_avg(ATTN_W, HEAD_DIM)
    bmk = _block_avg(KV_W, HEAD_DIM)
    bmg = _block_avg(GLA_W, GLA_DK)
    rep = _kv_replicate()
    cos_t, sin_t = _rope_lane_tables(tl)
    step_rows = STEP_BLOCKS * SLOT_BLOCK
    n_slots = -(-(n * TOP_K + N_EXPERTS * (SLOT_BLOCK - 1)) // step_rows) * step_rows
    tile8 = lambda v: jnp.tile(v, N_HEADS)[None, :]

    ks, vs, lrus, glas = [], [], [], []
    for l in range(depth):
        mod = mods[l].reshape(SUBLANES, 1, 6 * D_MODEL)
        pa, pl_, pg = _proj_call(xc, xl, mod, norm1_g[l][None, :], w_in, l, seq_row)

        qg, kg, og = tile8(q_norm_g[l]), jnp.tile(k_norm_g[l], N_KV_HEADS)[None, :], attn_out_g[l][None, :]
        attn_c, k_new, v_new = _attn_ctx_call(pa, bc, tc, qg, kg, og, bmq, bmk, rep)
        attn_l = _attn_lat_call(pa, nc, bl, tl, cache_k[:, l].reshape(bl, past, KV_W),
                                cache_v[:, l].reshape(bl, past, KV_W), cos_t, sin_t, qg, kg, og, bmq, bmk, rep)
        ks.append(k_new.reshape(bc, tc, N_KV_HEADS, HEAD_DIM))
        vs.append(v_new.reshape(bc, tc, N_KV_HEADS, HEAD_DIM))

        wg = jnp.concatenate([_block_diag(lru_wa[l, 0]), _block_diag(lru_wi[l, 0]),
                              _block_diag(lru_wa[l, 1]), _block_diag(lru_wi[l, 1])], axis=1).astype(bf16)
        bg = jnp.concatenate([lru_ba[l, 0], lru_bi[l, 0], lru_ba[l, 1], lru_bi[l, 1]])[None, :]
        lru_args = (conv_w[l], conv_b[l][None, :], wg, bg, lru_lambda[l], lru_out_g[l][None, :])
        lru_c, lru_fin = _lru_call(pl_, 0, bc, tc, jnp.zeros((bc, 2, LRU_W), f32), *lru_args)
        lru_l, _ = _lru_call(pl_, nc, bl, tl, state_lru[:, l], *lru_args)
        lrus.append(lru_fin)

        wa = jnp.zeros((LANES, 2 * GLA_W), f32)
        wa = wa.at[0:GLA_RANK, 0:GLA_W].set(gla_wa2[l, 0]).at[GLA_RANK:2 * GLA_RANK, GLA_W:].set(gla_wa2[l, 1])
        gla_args = (wa.astype(bf16), gla_ba[l].reshape(1, 2 * GLA_W), gla_out_g[l].reshape(1, GLA_W), bmg)
        gla_c, gla_fin = _gla_call(pg, 0, bc, tc, None, *gla_args)
        gla_l, _ = _gla_call(pg, nc, bl, tl, _gla_state_in(state_gla[:, l]), *gla_args)
        glas.append(gla_fin)

        rw_t = router_w[l].T
        rw_hi = rw_t.astype(bf16)
        rw_lo = (rw_t - rw_hi.astype(f32)).astype(bf16)
        xbase, h2, scores_t = _out_call(xc, xl, (attn_c, lru_c, gla_c), (attn_l, lru_l, gla_l), mod, w_out[l].astype(bf16),
                                        norm2_g[l][None, :], rw_hi, rw_lo, sh_w1[l].astype(bf16),
                                        sh_w3[l].astype(bf16), sh_w2[l].astype(bf16), seq_row)

        idx, gates, rank, counts = _route_call(scores_t, router_b[l][:, None])
        cnt = counts[:, 0].astype(i32)
        padded = (cnt + SLOT_BLOCK - 1) // SLOT_BLOCK * SLOT_BLOCK
        padded_end = jnp.cumsum(padded)
        dest_km = _dest_call(idx, rank, (padded_end - padded).astype(f32)[:, None])
        nb = n_slots // SLOT_BLOCK
        blk_first = jnp.arange(nb, dtype=i32) * SLOT_BLOCK
        blk_e = jnp.minimum(jnp.sum((padded_end[None, :] <= blk_first[:, None]).astype(i32), axis=1), N_EXPERTS - 1)
        blk_valid = jnp.clip((padded_end - padded + cnt)[blk_e] - blk_first, 0, SLOT_BLOCK)
        n_used = padded_end[-1:] // SLOT_BLOCK
        owns = jnp.cumsum((cnt > 0).astype(i32))
        blk_ord = (owns - 1)[blk_e]
        expert_seq = jnp.minimum(jnp.sum((owns[None, :] <= jnp.arange(N_EXPERTS, dtype=i32)[:, None]).astype(i32),
                                         axis=1), N_EXPERTS - 1)
        slots = _scatter_rows_call(h2, dest_km, n_slots)
        slots = _expert_call(n_used, blk_valid, blk_ord, expert_seq, owns[-1:], slots, exp_w1, exp_w3, exp_w2, l)
        gates_t = gates.T
        yg_c = _gather_rows_call(slots, dest_km[:, :nc].reshape(-1)).reshape(TOP_K, nc, D_PACK)
        yg_l = _gather_rows_call(slots, dest_km[:, nc:].reshape(-1)).reshape(TOP_K, n - nc, D_PACK)
        xc = _combine_call(gates_t, xbase, mod, yg_c, seq_row_c, 0)
        xl = _combine_call(gates_t, xbase, mod, yg_l, seq_row_c, nc)

    y_prompt = xc.reshape(bc, tc, D_MODEL)
    y_sample = xl.reshape(bl, tl, D_MODEL)
    return (y_prompt, y_sample, jnp.stack(ks, axis=1), jnp.stack(vs, axis=1),
            jnp.stack(lrus, axis=1), jnp.stack(glas, axis=1))
```

```python
import functools

import jax
import jax.numpy as jnp
import numpy as np
from jax import lax
from jax.experimental import pallas as pl
from jax.experimental.pallas import tpu as pltpu
from jax.experimental.pallas import tpu_sc as plsc

f32 = jnp.float32
bf16 = jnp.bfloat16
i32 = jnp.int32

D_MODEL = 1024
N_HEADS = 8
N_KV_HEADS = 2
HEAD_DIM = 64
ATTN_W = N_HEADS * HEAD_DIM
KV_W = N_KV_HEADS * HEAD_DIM
GRID_W = 64
ROPE_FREQ = HEAD_DIM // 4
ROPE_THETA = 10000.0
LRU_W = 256
LRU_BLOCKS = 4
LRU_C = 8.0
CONV_W = 4
CONV_LEFT = 2
GLA_H = 4
GLA_DK = 64
GLA_W = 256
GLA_RANK = 16
GLA_TAU = 16.0
N_EXPERTS = 256
TOP_K = 8
EXPERT_FF = 256
ROUTED_SCALE = 2.5
EPS = 1e-6

LANES = 128
SUBLANES = 8
ROW_TILE = 256
ROUTE_TILE = 512
PROJ_TILE = 512
GLA_BLOCK = 256
SLOT_BLOCK = 256
STEP_BLOCKS = 4
W_RING = 4
COMBINE_TILE = 512
SC_SCATTER_CHUNK = 128
SC_CHUNK = 64
D_PACK = D_MODEL // 2
VMEM_LIMIT = 56 * 1024 * 1024


def _cparams(sem, vmem=VMEM_LIMIT):
    return pltpu.CompilerParams(dimension_semantics=sem, vmem_limit_bytes=vmem)


def _dot(a, b):
    return jnp.dot(a, b, preferred_element_type=f32)


def _dot_nt(a, b):
    return lax.dot_general(a, b, (((1,), (1,)), ((), ())), preferred_element_type=f32)


def _dot_tn(a, b):
    return lax.dot_general(a, b, (((0,), (0,)), ((), ())), preferred_element_type=f32)


def _split(x):
    hi = x.astype(bf16)
    lo = (x - hi.astype(f32)).astype(bf16)
    return hi, lo


def _dot_x2(x, w):
    hi, lo = _split(x)
    return _dot(hi, w) + _dot(lo, w)


def _dot_2x(m, x):
    hi, lo = _split(x)
    return _dot(m, hi) + _dot(m, lo)


def _dot3(a, b_hi, b_lo):
    a_hi, a_lo = _split(a)
    return _dot(a_hi, b_hi) + _dot(a_lo, b_hi) + _dot(a_hi, b_lo)


def _sigmoid(x):
    return 1.0 / (1.0 + jnp.exp(-x))


def _silu(x):
    return x * _sigmoid(x)


def _softplus(x):
    return jnp.maximum(x, 0.0) + jnp.log(1.0 + jnp.exp(-jnp.abs(x)))


def _gelu_tanh(x):
    return 0.5 * x * (1.0 + jnp.tanh(0.7978845608028654 * (x + 0.044715 * x * x * x)))


def _rms(x, g):
    return x * lax.rsqrt(jnp.mean(x * x, axis=-1, keepdims=True) + EPS) * g


def _pack_pairs(x):
    c = x.shape[1] // 2
    hi = lax.bitcast_convert_type(x[:, :c].astype(bf16).astype(f32), i32)
    lo = lax.bitcast_convert_type(x[:, c:].astype(bf16).astype(f32), i32)
    return hi | lax.shift_right_logical(lo, 16)


def _unpack_pairs(w):
    hi = lax.bitcast_convert_type(w & jnp.int32(-65536), f32)
    lo = lax.bitcast_convert_type(w << 16, f32)
    return hi, lo


def _head_rms(x, g, bm):
    ms = _dot_x2(x * x, bm)
    return x * lax.rsqrt(ms + EPS) * g


def _ada_kernel(c_ref, w_ref, b_ref, o_ref):
    s = _silu(c_ref[...])
    w = w_ref[0]
    w_hi, w_lo = _split(w)
    o_ref[0] = _dot3(s, w_hi, w_lo) + b_ref[0]


def _ada_call(cond, ada_w, ada_b):
    depth = ada_w.shape[0]
    nt = 1536
    return pl.pallas_call(
        _ada_kernel,
        grid=(depth, 6 * D_MODEL // nt),
        in_specs=[pl.BlockSpec((SUBLANES, D_MODEL), lambda l, j: (0, 0)),
                  pl.BlockSpec((1, D_MODEL, nt), lambda l, j: (l, 0, j)),
                  pl.BlockSpec((1, 1, nt), lambda l, j: (l, 0, j))],
        out_specs=pl.BlockSpec((1, SUBLANES, nt), lambda l, j: (l, 0, j)),
        out_shape=jax.ShapeDtypeStruct((depth, SUBLANES, 6 * D_MODEL), f32),
        compiler_params=_cparams(("parallel", "parallel")),
        name="ada_mod",
    )(cond, ada_w, ada_b.reshape(depth, 1, 6 * D_MODEL))


PROJ_A = ATTN_W + 2 * KV_W
PROJ_L = 2 * LRU_W
PROJ_G = 4 * GLA_W + LANES
IN_COLS = PROJ_A + PROJ_L + 4 * GLA_W + 2 * GLA_RANK


def _proj_kernel(xc_ref, xl_ref, mod_ref, g_ref, w_ref, oa_ref, ol_ref, og_ref, w_s, *, n_ctx_tiles):
    @pl.when(pl.program_id(0) == 0)
    def _():
        w_s[:, 0:IN_COLS] = w_ref[0].astype(bf16)
        w_s[:, IN_COLS:] = jnp.zeros((D_MODEL, w_s.shape[1] - IN_COLS), bf16)

    mod = mod_ref[0]
    sh = mod[:, 0:D_MODEL]
    sc = mod[:, D_MODEL:2 * D_MODEL]
    x = jnp.where(pl.program_id(0) < n_ctx_tiles, xc_ref[...], xl_ref[...])
    h = (_rms(x, g_ref[...]) * (1.0 + sc) + sh).astype(bf16)
    p = _dot(h, w_s[...])
    oa_ref[...] = p[:, 0:PROJ_A].astype(bf16)
    ol_ref[...] = p[:, PROJ_A:PROJ_A + PROJ_L].astype(bf16)
    og_ref[...] = p[:, PROJ_A + PROJ_L:PROJ_A + PROJ_L + PROJ_G].astype(bf16)


def _row_pair_specs(nct, width):
    return (pl.BlockSpec((PROJ_TILE, width), lambda i: (jnp.minimum(i, nct - 1), 0)),
            pl.BlockSpec((PROJ_TILE, width), lambda i: (jnp.maximum(i - nct, 0), 0)))


def _proj_call(xc, xl, mod, g, w_all, layer, seq_row):
    n = xc.shape[0] + xl.shape[0]
    nct = xc.shape[0] // PROJ_TILE
    cols = PROJ_A + PROJ_L + PROJ_G
    return pl.pallas_call(
        functools.partial(_proj_kernel, n_ctx_tiles=nct),
        grid=(n // PROJ_TILE,),
        in_specs=[*_row_pair_specs(nct, D_MODEL),
                  pl.BlockSpec((1, 1, 6 * D_MODEL), lambda i: (seq_row(i), 0, 0)),
                  pl.BlockSpec((1, D_MODEL), lambda i: (0, 0)),
                  pl.BlockSpec((1, D_MODEL, IN_COLS), lambda i: (layer, 0, 0))],
        out_specs=[pl.BlockSpec((PROJ_TILE, PROJ_A), lambda i: (i, 0)),
                   pl.BlockSpec((PROJ_TILE, PROJ_L), lambda i: (i, 0)),
                   pl.BlockSpec((PROJ_TILE, PROJ_G), lambda i: (i, 0))],
        out_shape=[jax.ShapeDtypeStruct((n, PROJ_A), bf16),
                   jax.ShapeDtypeStruct((n, PROJ_L), bf16),
                   jax.ShapeDtypeStruct((n, PROJ_G), bf16)],
        scratch_shapes=[pltpu.VMEM((D_MODEL, cols), bf16)],
        compiler_params=_cparams(("arbitrary",)),
        name="in_proj",
    )(xc, xl, mod, g, w_all)


def _rope(x, cos_t, sin_t):
    w = x.shape[1]
    up = pltpu.roll(x, w - ROPE_FREQ, 1)
    dn = pltpu.roll(x, ROPE_FREQ, 1)
    lane = lax.broadcasted_iota(i32, x.shape, 1)
    partner = jnp.where((lane & (2 * ROPE_FREQ - 1)) < ROPE_FREQ, up, dn)
    return x * cos_t + partner * sin_t


def _attend(q, kk_ref, vv_ref, o_ref):
    tq = q.shape[0]
    gw = ATTN_W // N_KV_HEADS
    lane = lax.broadcasted_iota(i32, (tq, gw), 1)
    for g in range(N_KV_HEADS):
        qg = q[:, g * gw:(g + 1) * gw]
        kg = kk_ref[:, g * gw:(g + 1) * gw]
        vg = vv_ref[:, g * gw:(g + 1) * gw]
        acc = jnp.zeros((tq, gw), f32)
        for hh in range(N_HEADS // N_KV_HEADS):
            hm = (lane >> 6) == hh
            s = _dot_nt(jnp.where(hm, qg, 0.0).astype(bf16), kg)
            m = jnp.max(s, axis=-1, keepdims=True)
            p = jnp.exp(s - m)
            l = jnp.sum(p, axis=-1, keepdims=True)
            o = _dot(p.astype(bf16), vg) / l
            acc = jnp.where(hm, o, acc)
        o_ref[:, g * gw:(g + 1) * gw] = acc


def _attn_ctx_kernel(p_ref, qg_ref, kg_ref, og_ref, bmq_ref, bmk_ref, rep_ref,
                     o_ref, ko_ref, vo_ref, kk_s, vv_s, o_s):
    p = p_ref[...].astype(f32)
    q = _head_rms(p[:, 0:ATTN_W], qg_ref[...], bmq_ref[...])
    k = _head_rms(p[:, ATTN_W:ATTN_W + KV_W], kg_ref[...], bmk_ref[...])
    v = p[:, ATTN_W + KV_W:PROJ_A]
    ko_ref[...] = k
    vo_ref[...] = v
    kk_s[...] = _dot(k.astype(bf16), rep_ref[...]).astype(bf16)
    vv_s[...] = _dot(v.astype(bf16), rep_ref[...]).astype(bf16)
    _attend(q * HEAD_DIM ** -0.5, kk_s, vv_s, o_s)
    o_ref[...] = _rms(o_s[...], og_ref[...]).astype(bf16)


def _attn_ctx_call(pa, n_seq, t, qg, kg, og, bmq, bmk, rep):
    assert t == ROW_TILE
    const = lambda i: (0, 0)
    return pl.pallas_call(
        _attn_ctx_kernel,
        grid=(n_seq,),
        in_specs=[pl.BlockSpec((t, PROJ_A), lambda i: (i, 0)),
                  pl.BlockSpec((1, ATTN_W), const), pl.BlockSpec((1, KV_W), const), pl.BlockSpec((1, ATTN_W), const),
                  pl.BlockSpec((ATTN_W, ATTN_W), const), pl.BlockSpec((KV_W, KV_W), const),
                  pl.BlockSpec((KV_W, ATTN_W), const)],
        out_specs=[pl.BlockSpec((t, ATTN_W), lambda i: (i, 0)),
                   pl.BlockSpec((t, KV_W), lambda i: (i, 0)),
                   pl.BlockSpec((t, KV_W), lambda i: (i, 0))],
        out_shape=[jax.ShapeDtypeStruct((n_seq * t, ATTN_W), bf16),
                   jax.ShapeDtypeStruct((n_seq * t, KV_W), f32),
                   jax.ShapeDtypeStruct((n_seq * t, KV_W), f32)],
        scratch_shapes=[pltpu.VMEM((t, ATTN_W), bf16), pltpu.VMEM((t, ATTN_W), bf16), pltpu.VMEM((t, ATTN_W), f32)],
        compiler_params=_cparams(("parallel",)),
        name="attn_ctx",
    )(pa, qg, kg, og, bmq, bmk, rep)


def _attn_lat_kernel(q_ref, kv_ref, ck_ref, cv_ref, cq_ref, sq_ref, ckk_ref, skk_ref,
                     qg_ref, kg_ref, og_ref, bmq_ref, bmk_ref, rep_ref,
                     o_ref, kk_s, vv_s, o_s, *, past):
    @pl.when(pl.program_id(1) == 0)
    def _():
        kv = kv_ref[...].astype(f32)
        k = _head_rms(kv[:, 0:KV_W], kg_ref[...], bmk_ref[...])
        k = _rope(k, ckk_ref[...], skk_ref[...])
        v = kv[:, KV_W:2 * KV_W]
        kk_s[0:past, :] = _dot(ck_ref[0].astype(bf16), rep_ref[...]).astype(bf16)
        vv_s[0:past, :] = _dot(cv_ref[0].astype(bf16), rep_ref[...]).astype(bf16)
        kk_s[past:, :] = _dot(k.astype(bf16), rep_ref[...]).astype(bf16)
        vv_s[past:, :] = _dot(v.astype(bf16), rep_ref[...]).astype(bf16)

    q = _head_rms(q_ref[...].astype(f32), qg_ref[...], bmq_ref[...])
    q = _rope(q, cq_ref[...], sq_ref[...])
    _attend(q * HEAD_DIM ** -0.5, kk_s, vv_s, o_s)
    o_ref[...] = _rms(o_s[...], og_ref[...]).astype(bf16)


def _attn_lat_call(pa, row0, n_seq, t, cache_k, cache_v, cq, sq, qg, kg, og, bmq, bmk, rep):
    past = cache_k.shape[1]
    nq = t // ROW_TILE
    assert row0 % t == 0 and t % ROW_TILE == 0
    qb0 = row0 // ROW_TILE
    sb0 = row0 // t
    const = lambda b, j: (0, 0)
    return pl.pallas_call(
        functools.partial(_attn_lat_kernel, past=past),
        grid=(n_seq, nq),
        in_specs=[pl.BlockSpec((ROW_TILE, ATTN_W), lambda b, j: (qb0 + b * nq + j, 0)),
                  pl.BlockSpec((t, 2 * KV_W), lambda b, j: (sb0 + b, ATTN_W // (2 * KV_W))),
                  pl.BlockSpec((1, past, KV_W), lambda b, j: (b, 0, 0)),
                  pl.BlockSpec((1, past, KV_W), lambda b, j: (b, 0, 0)),
                  pl.BlockSpec((ROW_TILE, ATTN_W), lambda b, j: (j, 0)),
                  pl.BlockSpec((ROW_TILE, ATTN_W), lambda b, j: (j, 0)),
                  pl.BlockSpec((t, KV_W), const), pl.BlockSpec((t, KV_W), const),
                  pl.BlockSpec((1, ATTN_W), const), pl.BlockSpec((1, KV_W), const), pl.BlockSpec((1, ATTN_W), const),
                  pl.BlockSpec((ATTN_W, ATTN_W), const), pl.BlockSpec((KV_W, KV_W), const),
                  pl.BlockSpec((KV_W, ATTN_W), const)],
        out_specs=pl.BlockSpec((ROW_TILE, ATTN_W), lambda b, j: (b * nq + j, 0)),
        out_shape=jax.ShapeDtypeStruct((n_seq * t, ATTN_W), bf16),
        scratch_shapes=[pltpu.VMEM((past + t, ATTN_W), bf16), pltpu.VMEM((past + t, ATTN_W), bf16),
                        pltpu.VMEM((ROW_TILE, ATTN_W), f32)],
        compiler_params=_cparams(("parallel", "arbitrary")),
        name="attn_lat",
    )(pa, pa, cache_k, cache_v, cq, sq, cq, sq, qg, kg, og, bmq, bmk, rep)


def _lru_kernel(x_ref, h0_ref, cw_ref, cb_ref, wg_ref, bg_ref, lam_ref, g_ref,
                o_ref, fin_ref, xs, a_s, u_s, h_s, *, t):
    ng = t // SUBLANES
    lx = x_ref[:, 0:LRU_W].astype(f32)
    ly = x_ref[:, LRU_W:2 * LRU_W].astype(f32)
    xs[0:SUBLANES, :] = jnp.zeros((SUBLANES, LRU_W), f32)
    xs[SUBLANES + t:2 * SUBLANES + t, :] = jnp.zeros((SUBLANES, LRU_W), f32)
    xs[SUBLANES:SUBLANES + t, :] = lx
    xc = jnp.broadcast_to(cb_ref[...], (t, LRU_W))
    for j in range(CONV_W):
        off = SUBLANES + j - CONV_LEFT
        xc = xc + xs[off:off + t, :] * cw_ref[j:j + 1, :]
    gates = _sigmoid(_dot(xc.astype(bf16), wg_ref[...]) + bg_ref[...])
    sp = _softplus(-lam_ref[...])
    sub = lax.broadcasted_iota(i32, (ng, SUBLANES, LRU_W), 1)
    for d in range(2):
        r = gates[:, (2 * d) * LRU_W:(2 * d + 1) * LRU_W]
        gi = gates[:, (2 * d + 1) * LRU_W:(2 * d + 2) * LRU_W]
        log_a = -LRU_C * r * sp[d:d + 1, :]
        a = jnp.exp(log_a)
        th = jnp.tanh(log_a)
        u = jnp.sqrt(-2.0 * th / (1.0 - th)) * (gi * xc)
        a = a.reshape(ng, SUBLANES, LRU_W)
        u = u.reshape(ng, SUBLANES, LRU_W)
        for s in (1, 2, 4):
            if d == 0:
                a_sh = pltpu.roll(a, s, 1)
                u_sh = pltpu.roll(u, s, 1)
                ok = sub >= s
            else:
                a_sh = pltpu.roll(a, SUBLANES - s, 1)
                u_sh = pltpu.roll(u, SUBLANES - s, 1)
                ok = sub < SUBLANES - s
            u = jnp.where(ok, a * u_sh + u, u)
            a = jnp.where(ok, a * a_sh, a)
        a_s[...] = a.reshape(t, LRU_W)
        u_s[...] = u.reshape(t, LRU_W)
        h0 = jnp.broadcast_to(h0_ref[0, d:d + 1, :], (SUBLANES, LRU_W))
        edge = SUBLANES - 1 if d == 0 else 0

        def body(i, carry, d=d, edge=edge):
            g = i if d == 0 else ng - 1 - i
            rows = pl.ds(pl.multiple_of(g * SUBLANES, SUBLANES), SUBLANES)
            h = a_s[rows, :] * carry + u_s[rows, :]
            if d == 0:
                h_s[rows, :] = h
            else:
                h_s[rows, :] = h_s[rows, :] + h
            return jnp.broadcast_to(h[edge:edge + 1, :], (SUBLANES, LRU_W))

        last = lax.fori_loop(0, ng, body, h0)
        fin_ref[0, d:d + 1, :] = last[0:1, :]
    o_ref[...] = _rms(h_s[...] * _gelu_tanh(ly), g_ref[...]).astype(bf16)


def _lru_call(pl_, row0, n_seq, t, h0, cw, cb, wg, bg, lam, g):
    sb0 = row0 // t
    assert row0 % t == 0
    const = lambda b: (0, 0)
    in_specs = [pl.BlockSpec((t, PROJ_L), lambda b: (sb0 + b, 0)),
                pl.BlockSpec((1, 2, LRU_W), lambda b: (b, 0, 0)),
                pl.BlockSpec((CONV_W, LRU_W), const), pl.BlockSpec((1, LRU_W), const),
                pl.BlockSpec((LRU_W, 4 * LRU_W), const), pl.BlockSpec((1, 4 * LRU_W), const),
                pl.BlockSpec((2, LRU_W), const), pl.BlockSpec((1, LRU_W), const)]
    args = [pl_, h0, cw, cb, wg, bg, lam, g]
    return pl.pallas_call(
        functools.partial(_lru_kernel, t=t),
        grid=(n_seq,),
        in_specs=in_specs,
        out_specs=[pl.BlockSpec((t, LRU_W), lambda b: (b, 0)),
                   pl.BlockSpec((1, 2, LRU_W), lambda b: (b, 0, 0))],
        out_shape=[jax.ShapeDtypeStruct((n_seq * t, LRU_W), bf16), jax.ShapeDtypeStruct((n_seq, 2, LRU_W), f32)],
        scratch_shapes=[pltpu.VMEM((t + 2 * SUBLANES, LRU_W), f32), pltpu.VMEM((t, LRU_W), f32),
                        pltpu.VMEM((t, LRU_W), f32), pltpu.VMEM((t, LRU_W), f32)],
        compiler_params=_cparams(("parallel",)),
        name="rglru_t%d" % t,
    )(*args)


def _bcast_rows(b, period, off):
    w = b.shape[1]
    return jnp.concatenate(
        [jnp.broadcast_to(b[i * period + off:i * period + off + 1, :], (period, w)) for i in range(b.shape[0] // period)],
        axis=0)


def _gla_block(q, k, v, la, reverse, st_ref):
    n = GLA_BLOCK
    row = lax.broadcasted_iota(i32, (n, n), 0)
    col = lax.broadcasted_iota(i32, (n, n), 1)
    same64 = (row >> 6) == (col >> 6)
    same32 = (row >> 5) == (col >> 5)
    same16 = (row >> 4) == (col >> 4)
    if not reverse:
        cum = (same64 & (col <= row)).astype(bf16)
        m1 = same64 & ((row & 63) >= 32) & ((col & 63) < 32)
        m2 = same32 & ((row & 31) >= 16) & ((col & 31) < 16)
        m3 = same16 & (col <= row)
        offs = (31, 15, 7, 63)
    else:
        cum = (same64 & (col >= row)).astype(bf16)
        m1 = same64 & ((row & 63) < 32) & ((col & 63) >= 32)
        m2 = same32 & ((row & 31) < 16) & ((col & 31) >= 16)
        m3 = same16 & (col >= row)
        offs = (32, 16, 8, 0)
    b = _dot_2x(cum, la)
    r1 = _bcast_rows(b, 64, offs[0])
    r2 = _bcast_rows(b, 32, offs[1])
    r3 = _bcast_rows(b, 16, offs[2])
    bl = _bcast_rows(b, 64, offs[3])
    q1 = q * jnp.exp(jnp.minimum(b - r1, 0.0))
    k1 = (k * jnp.exp(jnp.minimum(r1 - b, 0.0))).astype(bf16)
    q2 = q * jnp.exp(jnp.minimum(b - r2, 0.0))
    k2 = (k * jnp.exp(jnp.minimum(r2 - b, 0.0))).astype(bf16)
    q3 = q * jnp.exp(b - r3)
    k3 = (k * jnp.exp(r3 - b)).astype(bf16)
    qe = (q * jnp.exp(b)).astype(bf16)
    kl = (k * jnp.exp(bl - b)).astype(bf16)
    dec = jnp.exp(bl)
    vb = v.astype(bf16)
    zq = jnp.zeros((n, n), f32)
    intra = jnp.zeros((n, n), f32)
    for h in range(GLA_H):
        hm = (col >> 6) == h
        a1 = _dot_nt(jnp.where(hm, q1, zq).astype(bf16), k1)
        a2 = _dot_nt(jnp.where(hm, q2, zq).astype(bf16), k2)
        a3 = _dot_nt(jnp.where(hm, q3, zq).astype(bf16), k3)
        att = jnp.where(m1, a1, jnp.where(m2, a2, jnp.where(m3, a3, zq)))
        intra = intra + _dot(att.astype(bf16), jnp.where(hm, v, zq).astype(bf16))
    outs = [None] * 4
    for c in (range(4) if not reverse else range(3, -1, -1)):
        rs = slice(64 * c, 64 * c + 64)
        st = st_ref[...]
        inter = _dot_nt(qe[rs], st.astype(bf16))
        kv = _dot_tn(vb[rs], kl[rs])
        drow = dec[64 * c:64 * c + 1, :]
        st_ref[...] = st * drow + jnp.where(same64, kv, zq)
        outs[c] = intra[rs] + inter
    return jnp.concatenate(outs, axis=0)


def _gla_kernel(x_ref, s0_ref, wa_ref, ba_ref, g_ref, bm_ref, o_ref, fin_ref, st_s, o_s, *, t):
    nblk = t // GLA_BLOCK

    def block(j, reverse):
        rows = pl.ds(j * GLA_BLOCK if isinstance(j, int) else pl.multiple_of(j * GLA_BLOCK, GLA_BLOCK), GLA_BLOCK)
        q = x_ref[rows, 0:GLA_W].astype(f32) * GLA_DK ** -0.5
        k = x_ref[rows, GLA_W:2 * GLA_W].astype(f32)
        v = x_ref[rows, 2 * GLA_W:3 * GLA_W].astype(f32)
        ga = x_ref[rows, 4 * GLA_W:4 * GLA_W + LANES]
        d = 1 if reverse else 0
        z = _dot(ga.astype(bf16), wa_ref[:, d * GLA_W:(d + 1) * GLA_W]) + ba_ref[:, d * GLA_W:(d + 1) * GLA_W]
        la = -_softplus(-z) * (1.0 / GLA_TAU)
        o = _gla_block(q, k, v, la, reverse, st_s)
        if reverse:
            o_s[rows, :] = o_s[rows, :] + o
        else:
            o_s[rows, :] = o

    for d in range(2):
        if s0_ref is None:
            st_s[...] = jnp.zeros((GLA_W, GLA_W), f32)
        else:
            st_s[...] = s0_ref[0, d]
        if nblk == 1:
            block(0, d == 1)
        else:
            def body(i, carry, d=d):
                block(i if d == 0 else nblk - 1 - i, d == 1)
                return carry
            lax.fori_loop(0, nblk, body, 0)
        st_t = st_s[...].T
        for h in range(GLA_H):
            fin_ref[0, d, h] = st_t[h * GLA_DK:(h + 1) * GLA_DK, h * GLA_DK:(h + 1) * GLA_DK]
    gg = x_ref[:, 3 * GLA_W:4 * GLA_W].astype(f32)
    o_ref[...] = (_head_rms(o_s[...], g_ref[...], bm_ref[...]) * _silu(gg)).astype(bf16)


def _gla_call(pg, row0, n_seq, t, s0, wa, ba, g, bm):
    sb0 = row0 // t
    assert row0 % t == 0 and t % GLA_BLOCK == 0
    const = lambda b: (0, 0)
    in_specs = [pl.BlockSpec((t, PROJ_G), lambda b: (sb0 + b, 0))]
    args = [pg]
    if s0 is not None:
        in_specs.append(pl.BlockSpec((1, 2, GLA_W, GLA_W), lambda b: (b, 0, 0, 0)))
        args.append(s0)
    in_specs += [pl.BlockSpec((LANES, 2 * GLA_W), const), pl.BlockSpec((1, 2 * GLA_W), const),
                 pl.BlockSpec((1, GLA_W), const), pl.BlockSpec((GLA_W, GLA_W), const)]
    args += [wa, ba, g, bm]

    def kern(*refs):
        refs = list(refs)
        x_ref = refs.pop(0)
        s0_ref = refs.pop(0) if s0 is not None else None
        wa_ref, ba_ref, g_ref, bm_ref = refs[:4]
        _gla_kernel(x_ref, s0_ref, wa_ref, ba_ref, g_ref, bm_ref, *refs[4:], t=t)

    return pl.pallas_call(
        kern,
        grid=(n_seq,),
        in_specs=in_specs,
        out_specs=[pl.BlockSpec((t, GLA_W), lambda b: (b, 0)),
                   pl.BlockSpec((1, 2, GLA_H, GLA_DK, GLA_DK), lambda b: (b, 0, 0, 0, 0))],
        out_shape=[jax.ShapeDtypeStruct((n_seq * t, GLA_W), bf16),
                   jax.ShapeDtypeStruct((n_seq, 2, GLA_H, GLA_DK, GLA_DK), f32)],
        scratch_shapes=[pltpu.VMEM((GLA_W, GLA_W), f32), pltpu.VMEM((t, GLA_W), f32)],
        compiler_params=_cparams(("parallel",)),
        name="gla_t%d" % t,
    )(*args)


def _out_kernel(xc_ref, xl_ref, ac_ref, lc_ref, gc_ref, al_ref, ll_ref, gl_ref, mod_ref, wo_ref, g2_ref,
                rwh_ref, rwl_ref, xb_ref, h2_ref, sc_ref, *, n_ctx_tiles):
    is_ctx = pl.program_id(0) < n_ctx_tiles
    pick = lambda c_ref, l_ref: jnp.where(is_ctx, c_ref[...], l_ref[...]).astype(bf16)
    mod = mod_ref[0]
    g1 = mod[:, 2 * D_MODEL:3 * D_MODEL]
    sh2 = mod[:, 3 * D_MODEL:4 * D_MODEL]
    sc2 = mod[:, 4 * D_MODEL:5 * D_MODEL]
    g2 = mod[:, 5 * D_MODEL:6 * D_MODEL]
    m = (_dot(pick(ac_ref, al_ref), wo_ref[0:ATTN_W, :])
         + _dot(pick(lc_ref, ll_ref), wo_ref[ATTN_W:ATTN_W + LRU_W, :])
         + _dot(pick(gc_ref, gl_ref), wo_ref[ATTN_W + LRU_W:ATTN_W + LRU_W + GLA_W, :]))
    x1 = jnp.where(is_ctx, xc_ref[...], xl_ref[...]) + g1 * m
    h2 = _rms(x1, g2_ref[...]) * (1.0 + sc2) + sh2
    h2_ref[...] = _pack_pairs(h2)
    h_hi, h_lo = _split(h2)
    logits_t = _dot_nt(rwh_ref[...], h_hi) + _dot_nt(rwh_ref[...], h_lo) + _dot_nt(rwl_ref[...], h_hi)
    sc_ref[...] = _sigmoid(logits_t)
    xb_ref[...] = x1


def _shared_kernel(h_ref, s1_ref, s3_ref, s2_ref, o_ref):
    h_hi, h_lo = _unpack_pairs(h_ref[...])
    hb = jnp.concatenate([h_hi, h_lo], axis=1).astype(bf16)
    act = (_silu(_dot(hb, s1_ref[...])) * _dot(hb, s3_ref[...])).astype(bf16)
    o_ref[...] = _pack_pairs(_dot(act, s2_ref[...]))


def _shared_call(h2p, s1, s3, s2):
    n = h2p.shape[0]
    const = lambda i: (0, 0)
    return pl.pallas_call(
        _shared_kernel,
        grid=(n // PROJ_TILE,),
        in_specs=[pl.BlockSpec((PROJ_TILE, D_PACK), lambda i: (i, 0)),
                  pl.BlockSpec((D_MODEL, EXPERT_FF), const), pl.BlockSpec((D_MODEL, EXPERT_FF), const),
                  pl.BlockSpec((EXPERT_FF, D_MODEL), const)],
        out_specs=pl.BlockSpec((PROJ_TILE, D_PACK), lambda i: (i, 0)),
        out_shape=jax.ShapeDtypeStruct((n, D_PACK), i32),
        compiler_params=_cparams(("parallel",)),
        name="shared_expert",
    )(h2p, s1, s3, s2)


def _out_call(xc, xl, mix_ctx, mix_lat, mod, wo, g2, rwh, rwl, seq_row):
    n = xc.shape[0] + xl.shape[0]
    nct = xc.shape[0] // PROJ_TILE
    const = lambda i: (0, 0)
    rowb = lambda w: pl.BlockSpec((PROJ_TILE, w), lambda i: (i, 0))
    ctxb = lambda w: _row_pair_specs(nct, w)[0]
    latb = lambda w: _row_pair_specs(nct, w)[1]
    return pl.pallas_call(
        functools.partial(_out_kernel, n_ctx_tiles=nct),
        grid=(n // PROJ_TILE,),
        in_specs=[ctxb(D_MODEL), latb(D_MODEL),
                  ctxb(ATTN_W), ctxb(LRU_W), ctxb(GLA_W), latb(ATTN_W), latb(LRU_W), latb(GLA_W),
                  pl.BlockSpec((1, 1, 6 * D_MODEL), lambda i: (seq_row(i), 0, 0)),
                  pl.BlockSpec((D_MODEL, D_MODEL), const), pl.BlockSpec((1, D_MODEL), const),
                  pl.BlockSpec((N_EXPERTS, D_MODEL), const), pl.BlockSpec((N_EXPERTS, D_MODEL), const)],
        out_specs=[rowb(D_MODEL),
                   rowb(D_PACK),
                   pl.BlockSpec((N_EXPERTS, PROJ_TILE), lambda i: (0, i))],
        out_shape=[jax.ShapeDtypeStruct((n, D_MODEL), f32),
                   jax.ShapeDtypeStruct((n, D_PACK), i32),
                   jax.ShapeDtypeStruct((N_EXPERTS, n), f32)],
        compiler_params=_cparams(("parallel",)),
        name="out_proj",
    )(xc, xl, *mix_ctx, *mix_lat, mod, wo, g2, rwh, rwl)


def _set_row(acc, k, row):
    sub = lax.broadcasted_iota(i32, acc.shape, 0)
    return jnp.where(sub == k, jnp.broadcast_to(row, acc.shape), acc)


def _route_kernel(sc_ref, rb_ref, idx_ref, gate_ref, rank_ref, cnt_ref, run_s):
    tm = sc_ref.shape[1]

    @pl.when(pl.program_id(0) == 0)
    def _():
        run_s[...] = jnp.zeros_like(run_s)

    scores = sc_ref[...]
    sel = scores + rb_ref[...]
    erow = lax.broadcasted_iota(i32, (N_EXPERTS, tm), 0).astype(f32)
    neg = jnp.full((N_EXPERTS, tm), -jnp.inf, f32)
    hots = []
    idx_o = jnp.zeros((TOP_K, tm), f32)
    gate_o = jnp.zeros((TOP_K, tm), f32)
    gsum = jnp.zeros((1, tm), f32)
    for k in range(TOP_K):
        m = jnp.max(sel, axis=0, keepdims=True)
        idx_f = jnp.min(jnp.where(sel == m, erow, float(N_EXPERTS)), axis=0, keepdims=True)
        hot = erow == idx_f
        gk = jnp.sum(jnp.where(hot, scores, 0.0), axis=0, keepdims=True)
        sel = jnp.where(hot, neg, sel)
        hots.append(hot)
        gsum = gsum + gk
        idx_o = _set_row(idx_o, k, idx_f)
        gate_o = _set_row(gate_o, k, gk)
    chosen = jnp.where(sel == neg, 1.0, 0.0)
    gate_ref[...] = gate_o / gsum * ROUTED_SCALE
    idx_ref[...] = idx_o.astype(i32)
    r = lax.broadcasted_iota(i32, (tm, tm), 0)
    c = lax.broadcasted_iota(i32, (tm, tm), 1)
    pos = _dot(chosen.astype(bf16), (r < c).astype(bf16)) + run_s[...]
    rank_o = jnp.zeros((TOP_K, tm), f32)
    for k in range(TOP_K):
        rank_o = _set_row(rank_o, k, jnp.sum(jnp.where(hots[k], pos, 0.0), axis=0, keepdims=True))
    rank_ref[...] = rank_o.astype(i32)
    run_s[...] = run_s[...] + jnp.sum(chosen, axis=1, keepdims=True)
    cnt_ref[...] = run_s[...]


def _route_call(scores_t, rb):
    n = scores_t.shape[1]
    colb = lambda r: pl.BlockSpec((r, ROUTE_TILE), lambda i: (0, i))
    cnt_spec = pl.BlockSpec((N_EXPERTS, 1), lambda i: (0, 0))
    return pl.pallas_call(
        _route_kernel,
        grid=(n // ROUTE_TILE,),
        in_specs=[colb(N_EXPERTS), cnt_spec],
        out_specs=[colb(TOP_K), colb(TOP_K), colb(TOP_K), cnt_spec],
        out_shape=[jax.ShapeDtypeStruct((TOP_K, n), i32), jax.ShapeDtypeStruct((TOP_K, n), f32),
                   jax.ShapeDtypeStruct((TOP_K, n), i32), jax.ShapeDtypeStruct((N_EXPERTS, 1), f32)],
        scratch_shapes=[pltpu.VMEM((N_EXPERTS, 1), f32)],
        compiler_params=_cparams(("arbitrary",)),
        name="route",
    )(scores_t, rb)


def _dest_kernel(idx_ref, rank_ref, start_ref, dest_ref):
    tm = idx_ref.shape[1]
    erow = lax.broadcasted_iota(i32, (N_EXPERTS, tm), 0)
    start = jnp.broadcast_to(start_ref[...], (N_EXPERTS, tm))
    idx = idx_ref[...]
    dest = jnp.zeros((TOP_K, tm), f32)
    for k in range(TOP_K):
        hot = erow == idx[k:k + 1, :]
        dest = _set_row(dest, k, jnp.sum(jnp.where(hot, start, 0.0), axis=0, keepdims=True))
    dest_ref[...] = dest.astype(i32) + rank_ref[...]


def _dest_call(idx, rank, start):
    n = idx.shape[1]
    colb = lambda r: pl.BlockSpec((r, ROUTE_TILE), lambda i: (0, i))
    return pl.pallas_call(
        _dest_kernel,
        grid=(n // ROUTE_TILE,),
        in_specs=[colb(TOP_K), colb(TOP_K), pl.BlockSpec((N_EXPERTS, 1), lambda i: (0, 0))],
        out_specs=colb(TOP_K),
        out_shape=jax.ShapeDtypeStruct((TOP_K, n), i32),
        compiler_params=_cparams(("parallel",)),
        name="dest",
    )(idx, rank, start)


def _sc_workers():
    info = plsc.get_sparse_core_info()
    mesh = plsc.VectorSubcoreMesh(core_axis_name="c", subcore_axis_name="s")
    worker_id = lambda: lax.axis_index("s") * info.num_cores + lax.axis_index("c")
    return mesh, info.num_cores * info.num_subcores, worker_id


def _scatter_rows_call(rows, dest_km, n_slots):
    n, d = rows.shape
    ch = SC_SCATTER_CHUNK
    mesh, n_workers, worker_id = _sc_workers()
    chunks_per_worker = n // (n_workers * ch)
    assert n % (n_workers * ch) == 0
    idx_cm = dest_km.reshape(TOP_K, n // ch, ch).transpose(1, 0, 2)

    def body(rows_hbm, idx_hbm, out_hbm, idx_v, rows_v, sem):
        base = worker_id() * chunks_per_worker

        @pl.loop(0, chunks_per_worker)
        def _(i):
            c = base + i
            pltpu.sync_copy(rows_hbm.at[pl.ds(pl.multiple_of(c * ch, ch), ch)], rows_v)
            pltpu.sync_copy(idx_hbm.at[c], idx_v)
            copies = [pltpu.async_copy(rows_v, out_hbm.at[idx_v.at[k]], sem) for k in range(TOP_K)]
            for cp in copies:
                cp.wait()

    return pl.kernel(
        body,
        out_type=jax.ShapeDtypeStruct((n_slots, d), rows.dtype),
        mesh=mesh,
        scratch_types=[pltpu.VMEM((TOP_K, ch), i32), pltpu.VMEM((ch, d), rows.dtype), pltpu.SemaphoreType.DMA],
        name="scatter_rows",
    )(rows, idx_cm)


def _expert_kernel(nu_ref, bv_ref, ord_ref, eseq_ref, nex_ref, x_ref, w1_hbm, w3_hbm, w2_hbm, y_ref,
                   w1_f, w3_f, w2_f, w1_s, w3_s, w2_s, sems, *, layer):
    def weight_copies(j):
        e = eseq_ref[j]
        slot = j % W_RING
        return [pltpu.make_async_copy(w1_hbm.at[layer, e], w1_f.at[slot], sems.at[slot, 0]),
                pltpu.make_async_copy(w3_hbm.at[layer, e], w3_f.at[slot], sems.at[slot, 1]),
                pltpu.make_async_copy(w2_hbm.at[layer, e], w2_f.at[slot], sems.at[slot, 2])]

    def start_if_exists(j):
        @pl.when(j < nex_ref[0])
        def _():
            for cp in weight_copies(j):
                cp.start()

    for u in range(STEP_BLOCKS):
        i = pl.program_id(0) * STEP_BLOCKS + u
        rows = slice(u * SLOT_BLOCK, (u + 1) * SLOT_BLOCK)

        @pl.when(i < nu_ref[0])
        def _(i=i, rows=rows):
            j = ord_ref[i]

            @pl.when(i == 0)
            def _():
                for ahead in range(W_RING - 1):
                    start_if_exists(ahead)

            @pl.when((i == 0) | (j != ord_ref[jnp.maximum(i - 1, 0)]))
            def _():
                start_if_exists(j + W_RING - 1)
                for cp in weight_copies(j):
                    cp.wait()
                slot = j % W_RING
                w1_s[...] = w1_f[slot].astype(bf16)
                w3_s[...] = w3_f[slot].astype(bf16)
                w2_s[...] = w2_f[slot].astype(bf16)

            row = lax.broadcasted_iota(i32, (SLOT_BLOCK, D_PACK), 0)
            x_hi, x_lo = _unpack_pairs(jnp.where(row < bv_ref[i], x_ref[rows, :], 0))
            x = jnp.concatenate([x_hi, x_lo], axis=1).astype(bf16)
            act = (_silu(_dot(x, w1_s[...])) * _dot(x, w3_s[...])).astype(bf16)
            y_ref[rows, :] = _pack_pairs(_dot(act, w2_s[...]))

        @pl.when((i >= nu_ref[0]) & (pl.program_id(0) * STEP_BLOCKS < nu_ref[0]))
        def _(rows=rows):
            y_ref[rows, :] = x_ref[rows, :]


def _expert_call(n_used, blk_valid, blk_ord, expert_seq, n_seq_experts, slots, w1, w3, w2, layer):
    step_rows = STEP_BLOCKS * SLOT_BLOCK
    assert slots.shape[0] % step_rows == 0
    blk = lambda s, nu, *_: (jnp.minimum(s, (nu[0] - 1) // STEP_BLOCKS), 0)
    hbm = pl.BlockSpec(memory_space=pl.ANY)
    grid_spec = pltpu.PrefetchScalarGridSpec(
        num_scalar_prefetch=5,
        grid=(slots.shape[0] // step_rows,),
        in_specs=[pl.BlockSpec((step_rows, D_PACK), blk), hbm, hbm, hbm],
        out_specs=pl.BlockSpec((step_rows, D_PACK), blk),
        scratch_shapes=[pltpu.VMEM((W_RING, D_MODEL, EXPERT_FF), f32), pltpu.VMEM((W_RING, D_MODEL, EXPERT_FF), f32),
                        pltpu.VMEM((W_RING, EXPERT_FF, D_MODEL), f32),
                        pltpu.VMEM((D_MODEL, EXPERT_FF), bf16), pltpu.VMEM((D_MODEL, EXPERT_FF), bf16),
                        pltpu.VMEM((EXPERT_FF, D_MODEL), bf16),
                        pltpu.SemaphoreType.DMA((W_RING, 3))],
    )
    return pl.pallas_call(
        functools.partial(_expert_kernel, layer=layer),
        grid_spec=grid_spec,
        out_shape=jax.ShapeDtypeStruct(slots.shape, slots.dtype),
        input_output_aliases={5: 0},
        compiler_params=_cparams(("arbitrary",)),
        name="experts",
    )(n_used, blk_valid, blk_ord, expert_seq, n_seq_experts, slots, w1, w3, w2)


def _gather_rows_call(table, idx):
    n_idx = idx.shape[0]
    d = table.shape[1]
    mesh, n_workers, worker_id = _sc_workers()
    per_worker = n_idx // n_workers
    assert n_idx % (n_workers * SC_CHUNK) == 0

    n_chunks = per_worker // SC_CHUNK
    assert n_chunks % 2 == 0

    def body(table_hbm, idx_hbm, out_hbm, idx_a, idx_b, rows_a, rows_b, sem_a, sem_b):
        base = worker_id() * per_worker
        bufs = ((idx_a, rows_a, sem_a), (idx_b, rows_b, sem_b))

        def start_gather(c, buf):
            idx_v, rows_v, sem = buf
            off = pl.multiple_of(base + c * SC_CHUNK, SC_CHUNK)
            pltpu.sync_copy(idx_hbm.at[pl.ds(off, SC_CHUNK)], idx_v)
            pltpu.async_copy(table_hbm.at[idx_v], rows_v, sem)

        start_gather(0, bufs[0])

        @pl.loop(0, n_chunks, step=2)
        def _(c0):
            for b in range(2):
                c = c0 + b
                idx_v, rows_v, sem = bufs[b]

                @pl.when(c + 1 < n_chunks)
                def _():
                    start_gather(c + 1, bufs[1 - b])

                pltpu.make_async_copy(table_hbm.at[idx_v], rows_v, sem).wait()
                off = pl.multiple_of(base + c * SC_CHUNK, SC_CHUNK)
                pltpu.sync_copy(rows_v, out_hbm.at[pl.ds(off, SC_CHUNK)])

    return pl.kernel(
        body,
        out_type=jax.ShapeDtypeStruct((n_idx, d), table.dtype),
        mesh=mesh,
        scratch_types=[pltpu.VMEM((SC_CHUNK,), i32), pltpu.VMEM((SC_CHUNK,), i32),
                       pltpu.VMEM((SC_CHUNK, d), table.dtype), pltpu.VMEM((SC_CHUNK, d), table.dtype),
                       pltpu.SemaphoreType.DMA, pltpu.SemaphoreType.DMA],
        name="gather_rows",
    )(table, idx)


def _combine_kernel(gate_ref, xb_ref, mod_ref, y_ref, sh_ref, o_ref):
    g2 = mod_ref[0][:, 5 * D_MODEL:6 * D_MODEL]
    gates = gate_ref[...]
    acc_hi, acc_lo = _unpack_pairs(sh_ref[...])
    for k in range(TOP_K):
        y_hi, y_lo = _unpack_pairs(y_ref[k])
        acc_hi = acc_hi + gates[:, k:k + 1] * y_hi
        acc_lo = acc_lo + gates[:, k:k + 1] * y_lo
    o_ref[:, 0:D_PACK] = xb_ref[:, 0:D_PACK] + g2[:, 0:D_PACK] * acc_hi
    o_ref[:, D_PACK:D_MODEL] = xb_ref[:, D_PACK:D_MODEL] + g2[:, D_PACK:D_MODEL] * acc_lo


def _combine_call(gates, xbase, mod, ygath, shared, seq_row_c, row0):
    tm = COMBINE_TILE
    n_rows = ygath.shape[1]
    assert row0 % tm == 0 and n_rows % tm == 0
    b0 = row0 // tm
    return pl.pallas_call(
        _combine_kernel,
        grid=(n_rows // tm,),
        in_specs=[pl.BlockSpec((tm, TOP_K), lambda i: (b0 + i, 0)),
                  pl.BlockSpec((tm, D_MODEL), lambda i: (b0 + i, 0)),
                  pl.BlockSpec((1, 1, 6 * D_MODEL), lambda i: (seq_row_c(b0 + i), 0, 0)),
                  pl.BlockSpec((TOP_K, tm, D_PACK), lambda i: (0, i, 0)),
                  pl.BlockSpec((tm, D_PACK), lambda i: (b0 + i, 0))],
        out_specs=pl.BlockSpec((tm, D_MODEL), lambda i: (i, 0)),
        out_shape=jax.ShapeDtypeStruct((n_rows, D_MODEL), f32),
        compiler_params=_cparams(("parallel",)),
        name="combine",
    )(gates, xbase, mod, ygath, shared)


def _block_avg(width, group):
    r = np.arange(width)
    return jnp.asarray((r[:, None] // group == r[None, :] // group).astype(np.float32) / group, dtype=bf16)


def _kv_replicate():
    c = np.arange(ATTN_W)
    src = (c // (ATTN_W // N_KV_HEADS)) * HEAD_DIM + c % HEAD_DIM
    return jnp.asarray((np.arange(KV_W)[:, None] == src[None, :]).astype(np.float32), dtype=bf16)


def _rope_lane_tables(n_tok):
    rows = n_tok // GRID_W
    r = jnp.repeat(jnp.arange(rows, dtype=f32), GRID_W)
    col = jnp.tile(jnp.arange(GRID_W, dtype=f32), rows)
    inv = ROPE_THETA ** (-jnp.arange(ROPE_FREQ, dtype=f32) / ROPE_FREQ)
    ar = r[:, None] * inv
    ac = col[:, None] * inv
    cos_h = jnp.concatenate([jnp.cos(ar), jnp.cos(ar), jnp.cos(ac), jnp.cos(ac)], axis=-1)
    sin_h = jnp.concatenate([-jnp.sin(ar), jnp.sin(ar), -jnp.sin(ac), jnp.sin(ac)], axis=-1)
    return jnp.tile(cos_h, (1, N_HEADS)), jnp.tile(sin_h, (1, N_HEADS))


def _block_diag(w):
    nb, bw, _ = w.shape
    eye = jnp.eye(nb, dtype=w.dtype)
    return (w[:, :, None, :] * eye[:, None, :, None]).reshape(nb * bw, nb * bw)


def _gla_state_in(s):
    bsz = s.shape[0]
    eye = jnp.eye(GLA_H, dtype=s.dtype)
    st = jnp.swapaxes(s, -1, -2)
    big = st[:, :, :, :, None, :] * eye[None, None, :, None, :, None]
    return big.reshape(bsz, 2, GLA_W, GLA_W)


def kernel(x_prompt, x_sample, cache_k, cache_v, state_lru, state_gla, c, c_ctx, ada_w, ada_b, norm1_g, norm2_g, w_in, q_norm_g, k_norm_g, attn_out_g, conv_w, conv_b, lru_wa, lru_ba, lru_wi, lru_bi, lru_lambda, lru_out_g, gla_wa2, gla_ba, gla_out_g, w_out, router_w, router_b, exp_w1, exp_w3, exp_w2, sh_w1, sh_w3, sh_w2):
    bc, tc, _ = x_prompt.shape
    bl, tl, _ = x_sample.shape
    depth = w_in.shape[0]
    nc = bc * tc
    n = nc + bl * tl
    past = cache_k.shape[2]
    assert tc == ROW_TILE and tl % ROW_TILE == 0 and nc % tl == 0 and bl + 1 <= SUBLANES
    assert nc % PROJ_TILE == 0 and tl % PROJ_TILE == 0 and nc % COMBINE_TILE == 0 and tl % COMBINE_TILE == 0

    def seq_row_for(tile):
        def seq_row(i):
            return jnp.where(i < nc // tile, 0, 1 + (i - nc // tile) // (tl // tile))
        return seq_row

    seq_row = seq_row_for(PROJ_TILE)
    seq_row_c = seq_row_for(COMBINE_TILE)

    xc = x_prompt.reshape(nc, D_MODEL)
    xl = x_sample.reshape(bl * tl, D_MODEL)
    cond = jnp.zeros((SUBLANES, D_MODEL), f32).at[0].set(c_ctx).at[1:1 + bl].set(c)
    mods = _ada_call(cond, ada_w, ada_b)

    bmq = _block_avg(ATTN_W, HEAD_DIM)
    bmk = _block_avg(KV_W, HEAD_DIM)
    bmg = _block_avg(GLA_W, GLA_DK)
    rep = _kv_replicate()
    cos_t, sin_t = _rope_lane_tables(tl)
    step_rows = STEP_BLOCKS * SLOT_BLOCK
    n_slots = -(-(n * TOP_K + N_EXPERTS * (SLOT_BLOCK - 1)) // step_rows) * step_rows
    tile8 = lambda v: jnp.tile(v, N_HEADS)[None, :]

    ks, vs, lrus, glas = [], [], [], []
    for l in range(depth):
        mod = mods[l].reshape(SUBLANES, 1, 6 * D_MODEL)
        pa, pl_, pg = _proj_call(xc, xl, mod, norm1_g[l][None, :], w_in, l, seq_row)

        qg, kg, og = tile8(q_norm_g[l]), jnp.tile(k_norm_g[l], N_KV_HEADS)[None, :], attn_out_g[l][None, :]
        attn_c, k_new, v_new = _attn_ctx_call(pa, bc, tc, qg, kg, og, bmq, bmk, rep)
        attn_l = _attn_lat_call(pa, nc, bl, tl, cache_k[:, l].reshape(bl, past, KV_W),
                                cache_v[:, l].reshape(bl, past, KV_W), cos_t, sin_t, qg, kg, og, bmq, bmk, rep)
        ks.append(k_new.reshape(bc, tc, N_KV_HEADS, HEAD_DIM))
        vs.append(v_new.reshape(bc, tc, N_KV_HEADS, HEAD_DIM))

        wg = jnp.concatenate([_block_diag(lru_wa[l, 0]), _block_diag(lru_wi[l, 0]),
                              _block_diag(lru_wa[l, 1]), _block_diag(lru_wi[l, 1])], axis=1).astype(bf16)
        bg = jnp.concatenate([lru_ba[l, 0], lru_bi[l, 0], lru_ba[l, 1], lru_bi[l, 1]])[None, :]
        lru_args = (conv_w[l], conv_b[l][None, :], wg, bg, lru_lambda[l], lru_out_g[l][None, :])
        lru_c, lru_fin = _lru_call(pl_, 0, bc, tc, jnp.zeros((bc, 2, LRU_W), f32), *lru_args)
        lru_l, _ = _lru_call(pl_, nc, bl, tl, state_lru[:, l], *lru_args)
        lrus.append(lru_fin)

        wa = jnp.zeros((LANES, 2 * GLA_W), f32)
        wa = wa.at[0:GLA_RANK, 0:GLA_W].set(gla_wa2[l, 0]).at[GLA_RANK:2 * GLA_RANK, GLA_W:].set(gla_wa2[l, 1])
        gla_args = (wa.astype(bf16), gla_ba[l].reshape(1, 2 * GLA_W), gla_out_g[l].reshape(1, GLA_W), bmg)
        gla_c, gla_fin = _gla_call(pg, 0, bc, tc, None, *gla_args)
        gla_l, _ = _gla_call(pg, nc, bl, tl, _gla_state_in(state_gla[:, l]), *gla_args)
        glas.append(gla_fin)

        rw_t = router_w[l].T
        rw_hi = rw_t.astype(bf16)
        rw_lo = (rw_t - rw_hi.astype(f32)).astype(bf16)
        xbase, h2, scores_t = _out_call(xc, xl, (attn_c, lru_c, gla_c), (attn_l, lru_l, gla_l), mod, w_out[l].astype(bf16),
                                        norm2_g[l][None, :], rw_hi, rw_lo, seq_row)

        idx, gates, rank, counts = _route_call(scores_t, router_b[l][:, None])
        cnt = counts[:, 0].astype(i32)
        padded = (cnt + SLOT_BLOCK - 1) // SLOT_BLOCK * SLOT_BLOCK
        padded_end = jnp.cumsum(padded)
        dest_km = _dest_call(idx, rank, (padded_end - padded).astype(f32)[:, None])
        nb = n_slots // SLOT_BLOCK
        blk_first = jnp.arange(nb, dtype=i32) * SLOT_BLOCK
        blk_e = jnp.minimum(jnp.sum((padded_end[None, :] <= blk_first[:, None]).astype(i32), axis=1), N_EXPERTS - 1)
        blk_valid = jnp.clip((padded_end - padded + cnt)[blk_e] - blk_first, 0, SLOT_BLOCK)
        n_used = padded_end[-1:] // SLOT_BLOCK
        owns = jnp.cumsum((cnt > 0).astype(i32))
        blk_ord = (owns - 1)[blk_e]
        expert_seq = jnp.minimum(jnp.sum((owns[None, :] <= jnp.arange(N_EXPERTS, dtype=i32)[:, None]).astype(i32),
                                         axis=1), N_EXPERTS - 1)
        slots = _scatter_rows_call(h2, dest_km, n_slots)
        shared = _shared_call(h2, sh_w1[l].astype(bf16), sh_w3[l].astype(bf16), sh_w2[l].astype(bf16))
        slots = _expert_call(n_used, blk_valid, blk_ord, expert_seq, owns[-1:], slots, exp_w1, exp_w3, exp_w2, l)
        gates_t = gates.T
        yg_c = _gather_rows_call(slots, dest_km[:, :nc].reshape(-1)).reshape(TOP_K, nc, D_PACK)
        yg_l = _gather_rows_call(slots, dest_km[:, nc:].reshape(-1)).reshape(TOP_K, n - nc, D_PACK)
        xc = _combine_call(gates_t, xbase, mod, yg_c, shared, seq_row_c, 0)
        xl = _combine_call(gates_t, xbase, mod, yg_l, shared, seq_row_c, nc)

    y_prompt = xc.reshape(bc, tc, D_MODEL)
    y_sample = xl.reshape(bl, tl, D_MODEL)
    return (y_prompt, y_sample, jnp.stack(ks, axis=1), jnp.stack(vs, axis=1),
            jnp.stack(lrus, axis=1), jnp.stack(glas, axis=1))
```
